```python
import math
import jax
import jax.numpy as jnp
from jax import lax
import numpy as np

D_MODEL = 2048
BATCH = 4
SEQ = 2048
DEPTH = 2
DEC_BATCH = 128
DEC_SEQ = 8
PAST_LEN = 16384
PAGE_SIZE = 128

SSD_HEAD_DIM = 64
SSD_HEADS = D_MODEL // SSD_HEAD_DIM
SSD_WIDTH = SSD_HEADS * SSD_HEAD_DIM
SSD_GROUPS = 4
SSD_HEADS_PER_GROUP = SSD_HEADS // SSD_GROUPS
SSD_STATE = 128
SSD_CONV = 4
SSD_CONV_DIM = SSD_WIDTH + 2 * SSD_GROUPS * SSD_STATE
SSD_CHUNK = 128
S5_WIDTH = D_MODEL // 2
S5_GROUP_SIZE = 16
S5_GROUPS = S5_WIDTH // S5_GROUP_SIZE
S5_STATE = 64
HG_WIDTH = D_MODEL // 2
HG_KEY_DIM = 128
HG_HEADS = HG_WIDTH // HG_KEY_DIM
HG_VAL_DIM = HG_WIDTH // HG_HEADS
HG_CHUNK = 16
MEM_LEN = 256
XA_HEADS = 4
XA_HEAD_DIM = D_MODEL // XA_HEADS
FFN_DIM = 5632
N_BRANCH = 3
IN_SIZES = (SSD_WIDTH, SSD_CONV_DIM, SSD_HEADS, S5_WIDTH, HG_WIDTH, HG_WIDTH, HG_WIDTH, HG_WIDTH, N_BRANCH * D_MODEL)
IN_DIM = SSD_WIDTH + SSD_CONV_DIM + SSD_HEADS + S5_WIDTH + 4 * HG_WIDTH + N_BRANCH * D_MODEL
NORM_EPS = 1e-5

kernel_name = 'hybrid_ssd_s5_hgrn2_gated_macaron_step'


def _rmsnorm(x, g):
    xf = x.astype(jnp.float32)
    y = xf * lax.rsqrt(jnp.mean(xf * xf, axis=-1, keepdims=True) + NORM_EPS)
    return (y * g.astype(jnp.float32)).astype(x.dtype)


def _swiglu(x, w1, w3, w2):
    return (jax.nn.silu(x @ w1) * (x @ w3)) @ w2


def _split_in(proj):
    points = np.cumsum(np.array(IN_SIZES))[:-1].tolist()
    return jnp.split(proj, points, axis=-1)


def _causal_conv(x, buf, w, b):
    L = x.shape[1]
    ext = jnp.concatenate([buf.astype(x.dtype), x], axis=1)
    y = b + ext[:, 0:L] * w[0]
    for k in range(1, SSD_CONV):
        y = y + ext[:, k:k + L] * w[k]
    return y, ext[:, L:]


def _masked_decay(rel, mask):
    return jnp.where(mask, jnp.exp(jnp.where(mask, rel, 0.0)), 0.0)


def _ssd_scan(xdt, dA, Bm, Cm, s0):
    bsz, L, G, R, P = xdt.shape
    c = math.gcd(L, SSD_CHUNK)
    z = L // c
    xdt = xdt.reshape(bsz, z, c, G, R, P)
    a = jnp.cumsum(dA.astype(jnp.float32).reshape(bsz, z, c, G, R), axis=2)
    Bm = Bm.reshape(bsz, z, c, G, -1)
    Cm = Cm.reshape(bsz, z, c, G, -1)
    mask = jnp.tril(jnp.ones((c, c), bool))[None, None, :, :, None, None]
    decay = _masked_decay(a[:, :, :, None] - a[:, :, None, :], mask)
    cb = jnp.einsum('bzign,bzjgn->bzijg', Cm, Bm)
    y = jnp.einsum('bzijg,bzijgr,bzjgrp->bzigrp', cb, decay, xdt)
    states = jnp.einsum('bzjgn,bzjgr,bzjgrp->bzgrpn', Bm, jnp.exp(a[:, :, -1:] - a), xdt)

    def step(S, inp):
        dec, st = inp
        return dec[..., None, None] * S + st, S

    s_fin, s_in = lax.scan(step, s0.astype(jnp.float32),
                           (jnp.moveaxis(jnp.exp(a[:, :, -1]), 1, 0), jnp.moveaxis(states, 1, 0)))
    y = y + jnp.einsum('bzign,zbgrpn,bzigr->bzigrp', Cm, s_in, jnp.exp(a))
    return y.reshape(bsz, L, G, R, P), s_fin


def _ssd_branch(z, xbc, dt_raw, conv_buf, s0, conv_w, conv_b, dt_bias, a_log, d_skip, norm_g, w_out):
    bsz, L, _ = z.shape
    xbc, new_buf = _causal_conv(xbc, conv_buf, conv_w, conv_b)
    xbc = jax.nn.silu(xbc)
    xs, Bm, Cm = jnp.split(xbc, [SSD_WIDTH, SSD_WIDTH + SSD_GROUPS * SSD_STATE], axis=-1)
    xs = xs.reshape(bsz, L, SSD_GROUPS, SSD_HEADS_PER_GROUP, SSD_HEAD_DIM)
    Bm = Bm.reshape(bsz, L, SSD_GROUPS, SSD_STATE)
    Cm = Cm.reshape(bsz, L, SSD_GROUPS, SSD_STATE)
    dt = jax.nn.softplus((dt_raw + dt_bias).astype(jnp.float32)).reshape(bsz, L, SSD_GROUPS, SSD_HEADS_PER_GROUP)
    A = -jnp.exp(a_log.astype(jnp.float32)).reshape(SSD_GROUPS, SSD_HEADS_PER_GROUP)
    s0 = s0.reshape(bsz, SSD_GROUPS, SSD_HEADS_PER_GROUP, SSD_HEAD_DIM, SSD_STATE)
    y, s_fin = _ssd_scan(xs * dt[..., None], dt * A, Bm, Cm, s0)
    y = y + d_skip.reshape(SSD_GROUPS, SSD_HEADS_PER_GROUP)[..., None] * xs
    y = y.reshape(bsz, L, SSD_WIDTH)
    y = _rmsnorm(y * jax.nn.silu(z), norm_g)
    return y @ w_out, new_buf, s_fin.reshape(bsz, SSD_HEADS, SSD_HEAD_DIM, SSD_STATE)


def _complex_affine_combine(e1, e2):
    a1r, a1i, b1r, b1i = e1
    a2r, a2i, b2r, b2i = e2
    return (a2r * a1r - a2i * a1i,
            a2r * a1i + a2i * a1r,
            a2r * b1r - a2i * b1i + b2r,
            a2r * b1i + a2i * b1r + b2i)


def _s5_branch(u, h0_re, h0_im, a_re, a_im, log_dt, b_re, b_im, c_re, c_im, d_skip, w_a, w_b):
    bsz, L, _ = u.shape
    f32 = jnp.float32
    ug = u.reshape(bsz, L, S5_GROUPS, S5_GROUP_SIZE)
    ar = a_re.astype(f32)
    ai = a_im.astype(f32)
    dt = jnp.exp(log_dt.astype(f32))[:, None]
    mag = jnp.exp(ar * dt)
    abar_re = mag * jnp.cos(ai * dt)
    abar_im = mag * jnp.sin(ai * dt)
    den = ar * ar + ai * ai
    nr = abar_re - 1.0
    coef_re = (nr * ar + abar_im * ai) / den
    coef_im = (abar_im * ar - nr * ai) / den
    br = b_re.astype(f32)
    bi = b_im.astype(f32)
    bbar_re = coef_re[..., None] * br - coef_im[..., None] * bi
    bbar_im = coef_re[..., None] * bi + coef_im[..., None] * br
    bu_re = jnp.einsum('blgk,gnk->blgn', ug, bbar_re)
    bu_im = jnp.einsum('blgk,gnk->blgn', ug, bbar_im)
    h0r = h0_re.astype(f32)
    h0i = h0_im.astype(f32)
    bu_re = bu_re.at[:, 0].add(abar_re * h0r - abar_im * h0i)
    bu_im = bu_im.at[:, 0].add(abar_re * h0i + abar_im * h0r)
    elems = (jnp.broadcast_to(abar_re, bu_re.shape), jnp.broadcast_to(abar_im, bu_re.shape), bu_re, bu_im)
    _, _, h_re, h_im = lax.associative_scan(_complex_affine_combine, elems, axis=1)
    y = jnp.einsum('blgn,gkn->blgk', h_re, c_re) - jnp.einsum('blgn,gkn->blgk', h_im, c_im)
    y = y.reshape(bsz, L, S5_WIDTH) + d_skip * u
    y = jax.nn.gelu(y)
    out = (y @ w_a) * jax.nn.sigmoid(y @ w_b)
    return out, h_re[:, -1], h_im[:, -1]


def _hgrn_scan(q, k, v, logf, s0):
    bsz, L, H, K = q.shape
    V = v.shape[-1]
    c = math.gcd(L, HG_CHUNK)
    z = L // c

    def chunks(t):
        return jnp.moveaxis(t.reshape(bsz, z, c, *t.shape[2:]), 1, 0)

    mask = jnp.tril(jnp.ones((c, c), bool))[None, :, :, None, None]

    def step(S, inp):
        qc, kc, vc, lf = inp
        b = jnp.cumsum(lf, axis=1)
        decay = _masked_decay(b[:, :, None] - b[:, None, :], mask)
        scores = jnp.einsum('bthk,bshk,btshk->bhts', qc, kc, decay)
        o = jnp.einsum('bhts,bshv->bthv', scores, vc) + jnp.einsum('bthk,bhkv->bthv', qc * jnp.exp(b), S)
        S = jnp.exp(b[:, -1])[..., None] * S + jnp.einsum('bshk,bshv->bhkv', kc * jnp.exp(b[:, -1:] - b), vc)
        return S, o

    s_fin, o = lax.scan(step, s0.astype(jnp.float32), (chunks(q), chunks(k), chunks(v), chunks(logf)))
    o = jnp.moveaxis(o, 0, 1).reshape(bsz, L, H, V)
    return o, s_fin


def _hgrn_branch(hq, hf, hi, hgate, s0, lb, norm_g, w_out):
    bsz, L, _ = hq.shape
    fr = hf.astype(jnp.float32)
    lb = lb.astype(jnp.float32)
    logf = jax.nn.log_sigmoid(fr) + jnp.log1p(lb * jnp.exp(-fr))
    kk = (1.0 - lb) * jax.nn.sigmoid(-fr)
    q = hq.reshape(bsz, L, HG_HEADS, HG_KEY_DIM)
    kk = kk.reshape(bsz, L, HG_HEADS, HG_KEY_DIM)
    logf = logf.reshape(bsz, L, HG_HEADS, HG_KEY_DIM)
    v = hi.reshape(bsz, L, HG_HEADS, HG_VAL_DIM)
    o, s_fin = _hgrn_scan(q, kk, v, logf, s0)
    o = _rmsnorm(o, norm_g) * jax.nn.silu(hgate.reshape(bsz, L, HG_HEADS, HG_VAL_DIM))
    return o.reshape(bsz, L, HG_WIDTH) @ w_out, s_fin


def _mem_kv(mem, g, wk, wv):
    bsz, n, _ = mem.shape
    m = _rmsnorm(mem, g)
    k = (m @ wk).reshape(bsz, n, XA_HEADS, XA_HEAD_DIM)
    v = (m @ wv).reshape(bsz, n, XA_HEADS, XA_HEAD_DIM)
    return k, v


def _cross_attn(h, k, v, wq, wo):
    bsz, L, _ = h.shape
    q = (h @ wq).reshape(bsz, L, XA_HEADS, XA_HEAD_DIM)
    s = jnp.einsum('blhd,bmhd->bhlm', q, k).astype(jnp.float32) * (XA_HEAD_DIM ** -0.5)
    p = jax.nn.softmax(s, axis=-1).astype(v.dtype)
    o = jnp.einsum('bhlm,bmhd->blhd', p, v).reshape(bsz, L, D_MODEL)
    return o @ wo


def setup_inputs(seed: int = 0) -> dict:
    key = jax.random.key(seed)
    ks = jax.random.split(key, 64)
    counter = [0]
    f32 = jnp.float32

    def nk():
        k = ks[counter[0]]
        counter[0] += 1
        return k

    def nrm(shape, scale=1.0):
        return jax.random.normal(nk(), shape, f32) * scale

    def gain(shape):
        return 1.0 + nrm(shape, 0.01)

    def dense(shape):
        return nrm(shape, shape[-2] ** -0.5)

    L = DEPTH
    d = {}
    d['x_prompt'] = nrm((BATCH, SEQ, D_MODEL))
    d['x_sample'] = nrm((DEC_BATCH, DEC_SEQ, D_MODEL))
    d['cache_mem_k'] = nrm((L, DEC_BATCH, MEM_LEN, XA_HEADS, XA_HEAD_DIM))
    d['cache_mem_v'] = nrm((L, DEC_BATCH, MEM_LEN, XA_HEADS, XA_HEAD_DIM))
    d['state_ssd'] = nrm((L, DEC_BATCH, SSD_HEADS, SSD_HEAD_DIM, SSD_STATE), 0.1)
    d['state_ssd_conv'] = nrm((L, DEC_BATCH, SSD_CONV - 1, SSD_CONV_DIM))
    d['state_s5_re'] = nrm((L, DEC_BATCH, S5_GROUPS, S5_STATE), 0.3)
    d['state_s5_im'] = nrm((L, DEC_BATCH, S5_GROUPS, S5_STATE), 0.3)
    d['state_hgrn'] = nrm((L, DEC_BATCH, HG_HEADS, HG_KEY_DIM, HG_VAL_DIM), 0.5)
    d['mem_prompt'] = nrm((BATCH, MEM_LEN, D_MODEL))
    d['norm_ffn1'] = gain((L, D_MODEL))
    d['ffn1_w1'] = dense((L, D_MODEL, FFN_DIM))
    d['ffn1_w3'] = dense((L, D_MODEL, FFN_DIM))
    d['ffn1_w2'] = dense((L, FFN_DIM, D_MODEL))
    d['norm_mix'] = gain((L, D_MODEL))
    d['w_in'] = dense((L, D_MODEL, IN_DIM))
    d['ssd_conv_w'] = nrm((L, SSD_CONV, SSD_CONV_DIM), 0.5)
    d['ssd_conv_b'] = nrm((L, SSD_CONV_DIM), 0.01)
    dt0 = jnp.exp(jax.random.uniform(nk(), (L, SSD_HEADS), f32, math.log(1e-3), math.log(1e-1)))
    d['ssd_dt_bias'] = dt0 + jnp.log(-jnp.expm1(-dt0))
    d['ssd_a_log'] = jnp.log(jax.random.uniform(nk(), (L, SSD_HEADS), f32, 1.0, 16.0))
    d['ssd_d'] = 1.0 + nrm((L, SSD_HEADS), 0.1)
    d['ssd_norm'] = gain((L, SSD_WIDTH))
    d['ssd_w_out'] = dense((L, SSD_WIDTH, D_MODEL))
    d['s5_a_re'] = -0.5 + nrm((L, S5_GROUPS, S5_STATE), 0.01)
    d['s5_a_im'] = jnp.pi * jnp.arange(S5_STATE, dtype=f32) + nrm((L, S5_GROUPS, S5_STATE), 0.01)
    d['s5_log_dt'] = jax.random.uniform(nk(), (L, S5_GROUPS), f32, math.log(1e-3), math.log(1e-1))
    d['s5_b_re'] = nrm((L, S5_GROUPS, S5_STATE, S5_GROUP_SIZE), (2 * S5_GROUP_SIZE) ** -0.5)
    d['s5_b_im'] = nrm((L, S5_GROUPS, S5_STATE, S5_GROUP_SIZE), (2 * S5_GROUP_SIZE) ** -0.5)
    d['s5_c_re'] = nrm((L, S5_GROUPS, S5_GROUP_SIZE, S5_STATE), S5_STATE ** -0.5)
    d['s5_c_im'] = nrm((L, S5_GROUPS, S5_GROUP_SIZE, S5_STATE), S5_STATE ** -0.5)
    d['s5_d'] = nrm((L, S5_WIDTH))
    d['s5_w_glu_a'] = dense((L, S5_WIDTH, D_MODEL))
    d['s5_w_glu_b'] = dense((L, S5_WIDTH, D_MODEL))
    d['hg_lower_bounds'] = nrm((L, HG_WIDTH), 0.1)
    d['hg_norm'] = gain((L, HG_VAL_DIM))
    d['hg_w_out'] = dense((L, HG_WIDTH, D_MODEL))
    d['w_mix_out'] = dense((L, D_MODEL, D_MODEL))
    d['norm_xa'] = gain((L, D_MODEL))
    d['norm_mem'] = gain((L, D_MODEL))
    d['xa_wq'] = dense((L, D_MODEL, D_MODEL))
    d['xa_wk'] = dense((L, D_MODEL, D_MODEL))
    d['xa_wv'] = dense((L, D_MODEL, D_MODEL))
    d['xa_wo'] = dense((L, D_MODEL, D_MODEL))
    d['norm_ffn2'] = gain((L, D_MODEL))
    d['ffn2_w1'] = dense((L, D_MODEL, FFN_DIM))
    d['ffn2_w3'] = dense((L, D_MODEL, FFN_DIM))
    d['ffn2_w2'] = dense((L, FFN_DIM, D_MODEL))
    d['norm_final'] = gain((D_MODEL,))
    return d


def reference(x_prompt, x_sample, cache_mem_k, cache_mem_v, state_ssd, state_ssd_conv, state_s5_re, state_s5_im,
              state_hgrn, mem_prompt, norm_ffn1, ffn1_w1, ffn1_w3, ffn1_w2, norm_mix, w_in, ssd_conv_w, ssd_conv_b,
              ssd_dt_bias, ssd_a_log, ssd_d, ssd_norm, ssd_w_out, s5_a_re, s5_a_im, s5_log_dt, s5_b_re, s5_b_im,
              s5_c_re, s5_c_im, s5_d, s5_w_glu_a, s5_w_glu_b, hg_lower_bounds, hg_norm, hg_w_out, w_mix_out,
              norm_xa, norm_mem, xa_wq, xa_wk, xa_wv, xa_wo, norm_ffn2, ffn2_w1, ffn2_w3, ffn2_w2, norm_final):
    f32 = jnp.float32
    lb_p = jax.nn.softmax(hg_lower_bounds.astype(f32), axis=0)
    lb_all = jnp.cumsum(lb_p, axis=0) - lb_p[0]

    def layer(x, l, mem_k, mem_v, conv_buf, ssd_s, s5r, s5i, hg_s):
        h = _rmsnorm(x, norm_ffn1[l])
        x = x + 0.5 * _swiglu(h, ffn1_w1[l], ffn1_w3[l], ffn1_w2[l])
        h = _rmsnorm(x, norm_mix[l])
        z, xbc, dt_raw, u, hq, hf, hi, hgate, gates = _split_in(h @ w_in[l])
        y_a, conv_new, ssd_new = _ssd_branch(z, xbc, dt_raw, conv_buf, ssd_s, ssd_conv_w[l], ssd_conv_b[l],
                                             ssd_dt_bias[l], ssd_a_log[l], ssd_d[l], ssd_norm[l], ssd_w_out[l])
        y_b, s5r_new, s5i_new = _s5_branch(u, s5r, s5i, s5_a_re[l], s5_a_im[l], s5_log_dt[l], s5_b_re[l],
                                           s5_b_im[l], s5_c_re[l], s5_c_im[l], s5_d[l], s5_w_glu_a[l],
                                           s5_w_glu_b[l])
        y_c, hg_new = _hgrn_branch(hq, hf, hi, hgate, hg_s, lb_all[l], hg_norm[l], hg_w_out[l])
        g_a, g_b, g_c = jnp.split(jax.nn.sigmoid(gates), N_BRANCH, axis=-1)
        x = x + (g_a * y_a + g_b * y_b + g_c * y_c) @ w_mix_out[l]
        h = _rmsnorm(x, norm_xa[l])
        x = x + _cross_attn(h, mem_k, mem_v, xa_wq[l], xa_wo[l])
        h = _rmsnorm(x, norm_ffn2[l])
        x = x + 0.5 * _swiglu(h, ffn2_w1[l], ffn2_w3[l], ffn2_w2[l])
        return x, conv_new, ssd_new, s5r_new, s5i_new, hg_new

    bp = x_prompt.shape[0]
    x = x_prompt
    pk, pv, pss, pcv, psr, psi, phg = [], [], [], [], [], [], []
    for l in range(DEPTH):
        mk, mv = _mem_kv(mem_prompt, norm_mem[l], xa_wk[l], xa_wv[l])
        x, cb, ss, sr, si, sh = layer(
            x, l, mk, mv,
            jnp.zeros((bp, SSD_CONV - 1, SSD_CONV_DIM), x_prompt.dtype),
            jnp.zeros((bp, SSD_HEADS, SSD_HEAD_DIM, SSD_STATE), f32),
            jnp.zeros((bp, S5_GROUPS, S5_STATE), f32),
            jnp.zeros((bp, S5_GROUPS, S5_STATE), f32),
            jnp.zeros((bp, HG_HEADS, HG_KEY_DIM, HG_VAL_DIM), f32))
        pk.append(mk)
        pv.append(mv)
        pss.append(ss)
        pcv.append(cb)
        psr.append(sr)
        psi.append(si)
        phg.append(sh)
    y_prompt = _rmsnorm(x, norm_final)

    x = x_sample
    sss, scv, ssr, ssi, shg = [], [], [], [], []
    for l in range(DEPTH):
        x, cb, ss, sr, si, sh = layer(x, l, cache_mem_k[l], cache_mem_v[l], state_ssd_conv[l], state_ssd[l],
                                      state_s5_re[l], state_s5_im[l], state_hgrn[l])
        sss.append(ss)
        scv.append(cb)
        ssr.append(sr)
        ssi.append(si)
        shg.append(sh)
    y_sample = _rmsnorm(x, norm_final)

    return (y_prompt, y_sample,
            jnp.stack(pk), jnp.stack(pv), jnp.stack(pss), jnp.stack(pcv), jnp.stack(psr), jnp.stack(psi),
            jnp.stack(phg),
            jnp.stack(sss), jnp.stack(scv), jnp.stack(ssr), jnp.stack(ssi), jnp.stack(shg))
```

```python
import functools
import math

import jax
import jax.numpy as jnp
import numpy as np
from jax import lax
from jax.experimental import pallas as pl
from jax.experimental.pallas import tpu as pltpu

F32 = jnp.float32
BF16 = jnp.bfloat16

D_MODEL = 2048
DEPTH = 2
NORM_EPS = 1e-5
SSD_HEAD_DIM = 64
SSD_HEADS = 32
SSD_GROUPS = 4
SSD_STATE = 128
SSD_CONV = 4
SSD_WIDTH = 2048
SSD_CONV_DIM = 3072
S5_WIDTH = 1024
S5_GROUP_SIZE = 16
S5_GROUPS = 64
S5_STATE = 64
HG_WIDTH = 1024
HG_HEADS = 8
HG_KEY_DIM = 128
HG_VAL_DIM = 128
MEM_LEN = 256
XA_HEADS = 4
XA_HEAD_DIM = 512
FFN_DIM = 5632

P_GATES, P_XBC, P_U, P_HQ, P_HF, P_HI, P_HGATE, P_Z = 0, 6144, 9216, 10240, 11264, 12288, 13312, 14336
P_WIDTH = 16384

V7X_VMEM_LIMIT = 56 * 1024 * 1024


def _cparams(sem, vmem=V7X_VMEM_LIMIT):
    return pltpu.CompilerParams(dimension_semantics=sem, vmem_limit_bytes=vmem)


def _rms_scale(x):
    return lax.rsqrt(jnp.mean(x * x, axis=-1, keepdims=True) + NORM_EPS)


def _silu(x):
    return x * jax.nn.sigmoid(x)


def _dot(a, b):
    return jnp.dot(a, b, preferred_element_type=F32)


def _dot_nt(a, b):
    return lax.dot_general(a, b, (((1,), (1,)), ((), ())), preferred_element_type=F32)


def _dot_tn(a, b):
    return lax.dot_general(a, b, (((0,), (0,)), ((), ())), preferred_element_type=F32)


def _split2(x):
    hi = x.astype(BF16)
    lo = (x - hi.astype(F32)).astype(BF16)
    return hi, lo


def _split3(x):
    hi = x.astype(BF16)
    r = x - hi.astype(F32)
    mid = r.astype(BF16)
    lo = (r - mid.astype(F32)).astype(BF16)
    return hi, mid, lo


def _sel_dot(sel, x, parts=3):
    ps = _split3(x) if parts == 3 else _split2(x)
    out = _dot(sel, ps[0])
    for p in ps[1:]:
        out = out + _dot(sel, p)
    return out


def _dot_sel(x, sel, parts=2):
    ps = _split3(x) if parts == 3 else _split2(x)
    out = _dot(ps[0], sel)
    for p in ps[1:]:
        out = out + _dot(p, sel)
    return out


def _norm_proj_body(x_ref, g_ref, w_ref, o_ref, h_ref):
    @pl.when(pl.program_id(1) == 0)
    def _():
        x = x_ref[...]
        h_ref[...] = (x * _rms_scale(x) * g_ref[...]).astype(BF16)

    o_ref[...] = _dot(h_ref[...], w_ref[...]).astype(o_ref.dtype)


def norm_proj(x, g, w, out_dtype, bm=1024, bn=1024):
    m, k = x.shape
    n = w.shape[1]
    return pl.pallas_call(
        _norm_proj_body,
        grid=(m // bm, n // bn),
        in_specs=[pl.BlockSpec((bm, k), lambda i, j: (i, 0)),
                  pl.BlockSpec((1, k), lambda i, j: (0, 0)),
                  pl.BlockSpec((k, bn), lambda i, j: (0, j))],
        out_specs=pl.BlockSpec((bm, bn), lambda i, j: (i, j)),
        out_shape=jax.ShapeDtypeStruct((m, n), out_dtype),
        scratch_shapes=[pltpu.VMEM((bm, k), BF16)],
        compiler_params=_cparams(("parallel", "arbitrary")),
        name="norm_proj",
    )(x, g, w)


def _in_proj_body(x_ref, g_ref, w_ref, wdt_ref, o_ref, odt_ref, h_ref):
    @pl.when(pl.program_id(1) == 0)
    def _():
        x = x_ref[...]
        h = (x * _rms_scale(x) * g_ref[...]).astype(BF16)
        h_ref[...] = h
        odt_ref[...] = _dot(h, wdt_ref[...])

    o_ref[...] = _dot(h_ref[...], w_ref[...])


def in_proj(x, g, w, wdt, bm=1024, bn=1024):
    m, k = x.shape
    n = w.shape[1]
    ndt = wdt.shape[1]
    return pl.pallas_call(
        _in_proj_body,
        grid=(m // bm, n // bn),
        in_specs=[pl.BlockSpec((bm, k), lambda i, j: (i, 0)),
                  pl.BlockSpec((1, k), lambda i, j: (0, 0)),
                  pl.BlockSpec((k, bn), lambda i, j: (0, j)),
                  pl.BlockSpec((k, ndt), lambda i, j: (0, 0))],
        out_specs=[pl.BlockSpec((bm, bn), lambda i, j: (i, j)),
                   pl.BlockSpec((bm, ndt), lambda i, j: (i, 0))],
        out_shape=[jax.ShapeDtypeStruct((m, n), F32), jax.ShapeDtypeStruct((m, ndt), F32)],
        scratch_shapes=[pltpu.VMEM((bm, k), BF16)],
        compiler_params=_cparams(("parallel", "arbitrary")),
        name="in_proj",
    )(x, g, w, wdt)


def _res_mm_body(x_ref, a_ref, w_ref, o_ref):
    o_ref[...] = x_ref[...] + _dot(a_ref[...], w_ref[...])


def res_mm(x, a, w, bm=1024, bn=1024):
    m, n = x.shape
    k = a.shape[1]
    return pl.pallas_call(
        _res_mm_body,
        grid=(m // bm, n // bn),
        in_specs=[pl.BlockSpec((bm, bn), lambda i, j: (i, j)),
                  pl.BlockSpec((bm, k), lambda i, j: (i, 0)),
                  pl.BlockSpec((k, bn), lambda i, j: (0, j))],
        out_specs=pl.BlockSpec((bm, bn), lambda i, j: (i, j)),
        out_shape=jax.ShapeDtypeStruct((m, n), F32),
        compiler_params=_cparams(("parallel", "parallel")),
        name="res_mm",
    )(x, a, w)


def _ffn_body(x_ref, g_ref, w1_ref, w3_ref, w2_ref, gf_ref, o_ref, h_ref, acc_ref, *, final_norm):
    f = pl.program_id(1)

    @pl.when(f == 0)
    def _():
        x = x_ref[...]
        h_ref[...] = (x * _rms_scale(x) * g_ref[...]).astype(BF16)
        acc_ref[...] = jnp.zeros_like(acc_ref)

    h = h_ref[...]
    a = _silu(_dot(h, w1_ref[...])) * _dot(h, w3_ref[...])
    acc_ref[...] += _dot(a.astype(BF16), w2_ref[...])

    @pl.when(f == pl.num_programs(1) - 1)
    def _():
        y = x_ref[...] + 0.5 * acc_ref[...]
        if final_norm:
            y = y * _rms_scale(y) * gf_ref[...]
        o_ref[...] = y


def ffn(x, g, w1, w3, w2, gf, final_norm, bm=512, bf=512):
    m, d = x.shape
    fdim = w1.shape[1]
    return pl.pallas_call(
        functools.partial(_ffn_body, final_norm=final_norm),
        grid=(m // bm, fdim // bf),
        in_specs=[pl.BlockSpec((bm, d), lambda i, f: (i, 0)),
                  pl.BlockSpec((1, d), lambda i, f: (0, 0)),
                  pl.BlockSpec((d, bf), lambda i, f: (0, f)),
                  pl.BlockSpec((d, bf), lambda i, f: (0, f)),
                  pl.BlockSpec((bf, d), lambda i, f: (f, 0)),
                  pl.BlockSpec((1, d), lambda i, f: (0, 0))],
        out_specs=pl.BlockSpec((bm, d), lambda i, f: (i, 0)),
        out_shape=jax.ShapeDtypeStruct((m, d), F32),
        scratch_shapes=[pltpu.VMEM((bm, d), BF16), pltpu.VMEM((bm, d), F32)],
        compiler_params=_cparams(("parallel", "arbitrary")),
        name="ffn",
    )(x, g, w1, w3, w2, gf)


def _mix_body(x_ref, ys_ref, gy_ref, o_ref, ga_ref, gb_ref, gc_ref, wssd_ref, wa_ref, wb_ref, whg_ref,
              wmix_ref, out_ref, acc_ref):
    j = pl.program_id(1)

    @pl.when(j == 0)
    def _():
        acc_ref[...] = jnp.zeros_like(acc_ref)

    gy = gy_ref[...]
    y_a = _dot(ys_ref[...], wssd_ref[...])
    y_b = _dot(gy, wa_ref[...]) * jax.nn.sigmoid(_dot(gy, wb_ref[...]))
    y_c = _dot(o_ref[...], whg_ref[...])
    mix = (jax.nn.sigmoid(ga_ref[...]) * y_a + jax.nn.sigmoid(gb_ref[...]) * y_b
           + jax.nn.sigmoid(gc_ref[...]) * y_c)
    acc_ref[...] += _dot(mix.astype(BF16), wmix_ref[...])

    @pl.when(j == pl.num_programs(1) - 1)
    def _():
        out_ref[...] = x_ref[...] + acc_ref[...]


def branch_mix(x, ys, gy, o, p, wssd, wa, wb, whg, wmix, bm=512, bn=512):
    m, d = x.shape
    nj = d // bn
    return pl.pallas_call(
        _mix_body,
        grid=(m // bm, nj),
        in_specs=[pl.BlockSpec((bm, d), lambda i, j: (i, 0)),
                  pl.BlockSpec((bm, ys.shape[1]), lambda i, j: (i, 0)),
                  pl.BlockSpec((bm, gy.shape[1]), lambda i, j: (i, 0)),
                  pl.BlockSpec((bm, o.shape[1]), lambda i, j: (i, 0)),
                  pl.BlockSpec((bm, bn), lambda i, j: (i, j)),
                  pl.BlockSpec((bm, bn), lambda i, j: (i, nj + j)),
                  pl.BlockSpec((bm, bn), lambda i, j: (i, 2 * nj + j)),
                  pl.BlockSpec((wssd.shape[0], bn), lambda i, j: (0, j)),
                  pl.BlockSpec((wa.shape[0], bn), lambda i, j: (0, j)),
                  pl.BlockSpec((wb.shape[0], bn), lambda i, j: (0, j)),
                  pl.BlockSpec((whg.shape[0], bn), lambda i, j: (0, j)),
                  pl.BlockSpec((bn, d), lambda i, j: (j, 0))],
        out_specs=pl.BlockSpec((bm, d), lambda i, j: (i, 0)),
        out_shape=jax.ShapeDtypeStruct((m, d), F32),
        scratch_shapes=[pltpu.VMEM((bm, d), F32)],
        compiler_params=_cparams(("parallel", "arbitrary")),
        name="branch_mix",
    )(x, ys, gy, o, p, p, p, wssd, wa, wb, whg, wmix)


def _attend(q, k, v):
    outs = []
    for h in range(XA_HEADS):
        sl = slice(h * XA_HEAD_DIM, (h + 1) * XA_HEAD_DIM)
        s = _dot_nt(q[:, sl], k[:, sl]) * (XA_HEAD_DIM ** -0.5)
        s = s - jnp.max(s, axis=-1, keepdims=True)
        e = jnp.exp(s)
        p = e / jnp.sum(e, axis=-1, keepdims=True)
        outs.append(_dot(p.astype(BF16), v[:, sl]))
    return jnp.concatenate(outs, axis=-1)


def _xattn_prompt_body(q_ref, kv_ref, o_ref):
    k = kv_ref[:, :D_MODEL].astype(BF16)
    v = kv_ref[:, D_MODEL:].astype(BF16)
    o_ref[...] = _attend(q_ref[...], k, v).astype(BF16)


def xattn_prompt(q, kv, n_seq, seq_len, bl=512):
    nl = seq_len // bl
    return pl.pallas_call(
        _xattn_prompt_body,
        grid=(n_seq, nl),
        in_specs=[pl.BlockSpec((bl, D_MODEL), lambda b, i: (b * nl + i, 0)),
                  pl.BlockSpec((MEM_LEN, 2 * D_MODEL), lambda b, i: (b, 0))],
        out_specs=pl.BlockSpec((bl, D_MODEL), lambda b, i: (b * nl + i, 0)),
        out_shape=jax.ShapeDtypeStruct((n_seq * seq_len, D_MODEL), BF16),
        compiler_params=_cparams(("parallel", "arbitrary")),
        name="xattn_prompt",
    )(q, kv)


def _xattn_sample_body(q_ref, k_ref, v_ref, o_ref, *, seq_len):
    q = q_ref[...]
    rows = lax.broadcasted_iota(jnp.int32, (2 * seq_len, D_MODEL), 0)
    o0 = _attend(q, k_ref[0].astype(BF16), v_ref[0].astype(BF16))
    o1 = _attend(q, k_ref[1].astype(BF16), v_ref[1].astype(BF16))
    o_ref[...] = jnp.where(rows < seq_len, o0, o1).astype(BF16)


def xattn_sample(q, k, v, row0, n_seq, seq_len):
    r = 2 * seq_len
    blk0 = row0 // r
    return pl.pallas_call(
        functools.partial(_xattn_sample_body, seq_len=seq_len),
        grid=(n_seq // 2,),
        in_specs=[pl.BlockSpec((r, D_MODEL), lambda i: (blk0 + i, 0)),
                  pl.BlockSpec((2, MEM_LEN, D_MODEL), lambda i: (i, 0, 0)),
                  pl.BlockSpec((2, MEM_LEN, D_MODEL), lambda i: (i, 0, 0))],
        out_specs=pl.BlockSpec((r, D_MODEL), lambda i: (i, 0)),
        out_shape=jax.ShapeDtypeStruct((n_seq * seq_len, D_MODEL), BF16),
        compiler_params=_cparams(("parallel",)),
        name="xattn_sample",
    )(q, k, v)


LANES = 128
HALO = 8


def _softplus(x):
    return jnp.maximum(x, 0.0) + jnp.log1p(jnp.exp(-jnp.abs(x)))


def _pad_rows(x, rows):
    if x.shape[0] == rows:
        return x
    return jnp.concatenate([x, jnp.zeros((rows - x.shape[0], x.shape[1]), x.dtype)], axis=0)


def _tile_consts(r, t):
    i = np.arange(r)[:, None]
    j = np.arange(r)[None, :]
    same = (i // t) == (j // t)
    lt = (same & (j <= i)).astype(np.float32)
    last = (same & (j % t == t - 1)).astype(np.float32)
    return jnp.asarray(lt, BF16), jnp.asarray(last, BF16)


def _head_expand(n_heads, width, rows=LANES):
    e = np.zeros((rows, n_heads * width), np.float32)
    for h in range(n_heads):
        e[h, h * width:(h + 1) * width] = 1.0
    return jnp.asarray(e, BF16)


def _ssd_tile(xc, dt_raw, dtb, a_log, lt, last, e, e128, r, t):
    xs = xc[:, :SSD_WIDTH]
    dt = _softplus(dt_raw + dtb)
    d_a = dt * (-jnp.exp(a_log))
    a = _sel_dot(lt, d_a, 3)
    a_e = _dot_sel(a, e, 3)
    dt_e = _dot_sel(dt, e, 2)
    alast_e = _sel_dot(last, a_e, 3)
    a_col = _dot_sel(a, e128, 3)
    a_t = _pad_rows(a, LANES).T

    row = lax.broadcasted_iota(jnp.int32, (r, LANES), 0)
    col = lax.broadcasted_iota(jnp.int32, (r, LANES), 1)
    valid = (col <= row) & (col >= (row // t) * t)
    lane_lo = col < SSD_HEAD_DIM

    xdt = _pad_rows((xs * dt_e).astype(BF16), LANES)
    hpg = SSD_HEADS // SSD_GROUPS
    ys = []
    for g in range(SSD_GROUPS):
        bg = xc[:, SSD_WIDTH + g * SSD_STATE:SSD_WIDTH + (g + 1) * SSD_STATE].astype(BF16)
        cg = xc[:, SSD_WIDTH + (SSD_GROUPS + g) * SSD_STATE:
                SSD_WIDTH + (SSD_GROUPS + g + 1) * SSD_STATE].astype(BF16)
        cb = _dot_nt(cg, _pad_rows(bg, LANES))
        for hp in range(hpg // 2):
            h0 = g * hpg + 2 * hp
            res = []
            for h in (h0, h0 + 1):
                rel = a_col[:, h * LANES:(h + 1) * LANES] - a_t[h:h + 1, :]
                dec = jnp.where(valid, jnp.exp(jnp.where(valid, rel, 0.0)), 0.0)
                res.append(_dot((cb * dec).astype(BF16), xdt[:, h0 * SSD_HEAD_DIM:(h0 + 2) * SSD_HEAD_DIM]))
            ys.append(jnp.where(lane_lo, res[0], res[1]))
    y_intra = jnp.concatenate(ys, axis=-1)
    return xs, y_intra, a_e, dt_e, alast_e


def _ssd_finish(y, xs, z, dsk, ng):
    y = y + dsk * xs
    y = y * _silu(z)
    return (y * _rms_scale(y) * ng).astype(BF16)


def _conv_silu(ext_ref, cw_ref, cb_ref, base, r):
    acc = cb_ref[...] + cw_ref[SSD_CONV - 1:SSD_CONV, :] * ext_ref[pl.ds(base, r), :]
    for k in range(1, SSD_CONV):
        acc = acc + cw_ref[SSD_CONV - 1 - k:SSD_CONV - k, :] * ext_ref[pl.ds(base - k, r), :]
    return _silu(acc)


def _ssd_prompt_body(xbc_ref, z_ref, dt_ref, cw_ref, cb_ref, dtb_ref, alog_ref, dsk_ref, ng_ref,
                     lt_ref, last_ref, e_ref, e128_ref, y_ref, sout_ref, ext_ref, st_ref, *, r):
    c = pl.program_id(1)

    @pl.when(c == 0)
    def _():
        ext_ref[0:HALO, :] = jnp.zeros((HALO, SSD_CONV_DIM), F32)
        st_ref[...] = jnp.zeros_like(st_ref)

    ext_ref[HALO:HALO + r, :] = xbc_ref[...]
    xc = _conv_silu(ext_ref, cw_ref, cb_ref, HALO, r)
    ext_ref[0:HALO, :] = xbc_ref[r - HALO:r, :]

    xs, y, a_e, dt_e, alast_e = _ssd_tile(xc, dt_ref[...], dtb_ref[...], alog_ref[...], lt_ref[...],
                                          last_ref[...], e_ref[...], e128_ref[...], r, r)
    ea_e = jnp.exp(a_e)
    xw = (xs * (dt_e * jnp.exp(alast_e - a_e))).astype(BF16)
    sdec = jnp.exp(alast_e[0:1, :])
    gw = SSD_WIDTH // SSD_GROUPS
    inter = []
    for g in range(SSD_GROUPS):
        bg = xc[:, SSD_WIDTH + g * SSD_STATE:SSD_WIDTH + (g + 1) * SSD_STATE].astype(BF16)
        cg = xc[:, SSD_WIDTH + (SSD_GROUPS + g) * SSD_STATE:
                SSD_WIDTH + (SSD_GROUPS + g + 1) * SSD_STATE].astype(BF16)
        st = st_ref[:, g * gw:(g + 1) * gw]
        inter.append(_dot(cg, st.astype(BF16)))
        st_ref[:, g * gw:(g + 1) * gw] = st * sdec[:, g * gw:(g + 1) * gw] + _dot_tn(bg, xw[:, g * gw:(g + 1) * gw])
    y = y + jnp.concatenate(inter, axis=-1) * ea_e
    y_ref[...] = _ssd_finish(y, xs, z_ref[...], dsk_ref[...], ng_ref[...])

    @pl.when(c == pl.num_programs(1) - 1)
    def _():
        sout_ref[0] = st_ref[...].T


def _ssd_consts(r, t):
    lt, last = _tile_consts(r, t)
    return lt, last, _head_expand(SSD_HEADS, SSD_HEAD_DIM), _head_expand(SSD_HEADS, LANES)


def ssd_prompt(p, dt, prm, n_seq, seq_len, r=128):
    nc = seq_len // r
    consts = _ssd_consts(r, r)
    full = lambda a: pl.BlockSpec(a.shape, lambda b, c: (0,) * a.ndim)
    small = [prm["conv_w"], prm["conv_b"], prm["dt_bias"], prm["a_log"], prm["d_skip"], prm["norm"], *consts]
    return pl.pallas_call(
        functools.partial(_ssd_prompt_body, r=r),
        grid=(n_seq, nc),
        in_specs=[pl.BlockSpec((r, SSD_CONV_DIM), lambda b, c: (b * nc + c, P_XBC // SSD_CONV_DIM)),
                  pl.BlockSpec((r, SSD_WIDTH), lambda b, c: (b * nc + c, P_Z // SSD_WIDTH)),
                  pl.BlockSpec((r, LANES), lambda b, c: (b * nc + c, 0)),
                  *[full(a) for a in small]],
        out_specs=[pl.BlockSpec((r, SSD_WIDTH), lambda b, c: (b * nc + c, 0)),
                   pl.BlockSpec((1, SSD_WIDTH, SSD_STATE), lambda b, c: (b, 0, 0))],
        out_shape=[jax.ShapeDtypeStruct((n_seq * seq_len, SSD_WIDTH), BF16),
                   jax.ShapeDtypeStruct((n_seq, SSD_WIDTH, SSD_STATE), F32)],
        scratch_shapes=[pltpu.VMEM((HALO + r, SSD_CONV_DIM), F32), pltpu.VMEM((SSD_STATE, SSD_WIDTH), F32)],
        compiler_params=_cparams(("parallel", "arbitrary")),
        name="ssd_prompt",
    )(p, p, dt, *small)


def _ssd_sample_body(xbc_ref, z_ref, dt_ref, buf_ref, s0_ref, cw_ref, cb_ref, dtb_ref, alog_ref, dsk_ref,
                     ng_ref, lt_ref, last_ref, e_ref, e128_ref, y_ref, sout_ref, ext_ref, *, r, t):
    nb = r // t
    pitch = HALO + t
    for b in range(nb):
        ext_ref[b * pitch + HALO - (SSD_CONV - 1):b * pitch + HALO, :] = buf_ref[b]
        ext_ref[b * pitch + HALO:(b + 1) * pitch, :] = xbc_ref[b * t:(b + 1) * t, :]
    xc = jnp.concatenate([_conv_silu(ext_ref, cw_ref, cb_ref, b * pitch + HALO, t) for b in range(nb)], axis=0)

    xs, y, a_e, dt_e, alast_e = _ssd_tile(xc, dt_ref[...], dtb_ref[...], alog_ref[...], lt_ref[...],
                                          last_ref[...], e_ref[...], e128_ref[...], r, t)
    ea_e = jnp.exp(a_e)
    xw = xs * (dt_e * jnp.exp(alast_e - a_e))
    sdec = jnp.exp(alast_e)
    gw = SSD_WIDTH // SSD_GROUPS
    pr = 2 * t
    prow = lax.broadcasted_iota(jnp.int32, (pr, 1), 0)
    ones = jnp.ones((pr, SSD_STATE), BF16)
    inter = []
    for g in range(SSD_GROUPS):
        bg = xc[:, SSD_WIDTH + g * SSD_STATE:SSD_WIDTH + (g + 1) * SSD_STATE].astype(BF16)
        cg = xc[:, SSD_WIDTH + (SSD_GROUPS + g) * SSD_STATE:
                SSD_WIDTH + (SSD_GROUPS + g + 1) * SSD_STATE].astype(BF16)
        cols = slice(g * gw, (g + 1) * gw)
        rows_out = []
        for q in range(nb // 2):
            rs = slice(q * pr, (q + 1) * pr)
            acc = None
            for s in range(2):
                b = 2 * q + s
                mine = (prow >= s * t) & (prow < (s + 1) * t)
                s0 = s0_ref[b, cols, :]
                yi = _dot_nt(cg[rs], s0.astype(BF16))
                acc = jnp.where(mine, yi, 0.0) if acc is None else acc + jnp.where(mine, yi, 0.0)
                upd = _dot_tn(jnp.where(mine, xw[rs, cols], 0.0).astype(BF16), bg[rs])
                lastrow = prow == (s + 1) * t - 1
                dh, dl = _split2(jnp.where(lastrow, sdec[rs, cols], 0.0))
                dcol = _dot_tn(dh, ones) + _dot_tn(dl, ones)
                sout_ref[b, cols, :] = s0 * dcol + upd
            rows_out.append(acc)
        inter.append(jnp.concatenate(rows_out, axis=0))
    y = y + jnp.concatenate(inter, axis=-1) * ea_e
    y_ref[...] = _ssd_finish(y, xs, z_ref[...], dsk_ref[...], ng_ref[...])


def ssd_sample(p, dt, conv_buf, s0, prm, row0, n_seq, seq_len, r=64):
    nb = r // seq_len
    blk0 = row0 // r
    consts = _ssd_consts(r, seq_len)
    full = lambda a: pl.BlockSpec(a.shape, lambda i: (0,) * a.ndim)
    small = [prm["conv_w"], prm["conv_b"], prm["dt_bias"], prm["a_log"], prm["d_skip"], prm["norm"], *consts]
    return pl.pallas_call(
        functools.partial(_ssd_sample_body, r=r, t=seq_len),
        grid=(n_seq // nb,),
        in_specs=[pl.BlockSpec((r, SSD_CONV_DIM), lambda i: (blk0 + i, P_XBC // SSD_CONV_DIM)),
                  pl.BlockSpec((r, SSD_WIDTH), lambda i: (blk0 + i, P_Z // SSD_WIDTH)),
                  pl.BlockSpec((r, LANES), lambda i: (blk0 + i, 0)),
                  pl.BlockSpec((nb, SSD_CONV - 1, SSD_CONV_DIM), lambda i: (i, 0, 0)),
                  pl.BlockSpec((nb, SSD_WIDTH, SSD_STATE), lambda i: (i, 0, 0)),
                  *[full(a) for a in small]],
        out_specs=[pl.BlockSpec((r, SSD_WIDTH), lambda i: (i, 0)),
                   pl.BlockSpec((nb, SSD_WIDTH, SSD_STATE), lambda i: (i, 0, 0))],
        out_shape=[jax.ShapeDtypeStruct((n_seq * seq_len, SSD_WIDTH), BF16),
                   jax.ShapeDtypeStruct((n_seq, SSD_WIDTH, SSD_STATE), F32)],
        scratch_shapes=[pltpu.VMEM((nb * (HALO + seq_len), SSD_CONV_DIM), F32)],
        compiler_params=_cparams(("parallel",)),
        name="ssd_sample",
    )(p, p, dt, conv_buf, s0, *small)


S5_CH = S5_GROUPS * S5_STATE
S5_BLK = 4
S5_KT = S5_CH // LANES


def _s5_in(u, bre_ref, bim_ref):
    ub = u.astype(BF16)
    kin = S5_WIDTH // S5_BLK
    re = [_dot(ub[:, q * kin:(q + 1) * kin], bre_ref[q]) for q in range(S5_BLK)]
    im = [_dot(ub[:, q * kin:(q + 1) * kin], bim_ref[q]) for q in range(S5_BLK)]
    return jnp.concatenate(re, axis=-1), jnp.concatenate(im, axis=-1)


def _s5_out(h_re, h_im, u, cre_ref, cim_ref, dsk):
    kst = S5_CH // S5_BLK
    hr = h_re.astype(BF16)
    hi = h_im.astype(BF16)
    y = [_dot(hr[:, q * kst:(q + 1) * kst], cre_ref[q]) - _dot(hi[:, q * kst:(q + 1) * kst], cim_ref[q])
         for q in range(S5_BLK)]
    y = jnp.concatenate(y, axis=-1) + dsk * u
    return jax.nn.gelu(y).astype(BF16)


def _s5_prompt_body(u_ref, are_ref, aim_ref, bre_ref, bim_ref, cre_ref, cim_ref, dsk_ref,
                    y_ref, hre_ref, him_ref, sre_ref, sim_ref, *, r):
    c = pl.program_id(1)

    @pl.when(c == 0)
    def _():
        hre_ref[0] = jnp.zeros((S5_KT, LANES), F32)
        him_ref[0] = jnp.zeros((S5_KT, LANES), F32)

    u = u_ref[...]
    bu_re, bu_im = _s5_in(u, bre_ref, bim_ref)
    for k in range(S5_KT):
        sre_ref[k * r:(k + 1) * r, :] = bu_re[:, k * LANES:(k + 1) * LANES]
        sim_ref[k * r:(k + 1) * r, :] = bu_im[:, k * LANES:(k + 1) * LANES]
    a_re = are_ref[...]
    a_im = aim_ref[...]

    def step(t, carry):
        h_re, h_im = carry
        idx = pl.ds(t, S5_KT, stride=r)
        n_re = a_re * h_re - a_im * h_im + sre_ref[idx, :]
        n_im = a_re * h_im + a_im * h_re + sim_ref[idx, :]
        sre_ref[idx, :] = n_re
        sim_ref[idx, :] = n_im
        return n_re, n_im

    h_re, h_im = lax.fori_loop(0, r, step, (hre_ref[0], him_ref[0]), unroll=8)
    hre_ref[0] = h_re
    him_ref[0] = h_im
    hs_re = jnp.concatenate([sre_ref[k * r:(k + 1) * r, :] for k in range(S5_KT)], axis=-1)
    hs_im = jnp.concatenate([sim_ref[k * r:(k + 1) * r, :] for k in range(S5_KT)], axis=-1)
    y_ref[...] = _s5_out(hs_re, hs_im, u, cre_ref, cim_ref, dsk_ref[...])


def s5_prompt(p, prm, n_seq, seq_len, r=256):
    nc = seq_len // r
    full = lambda a: pl.BlockSpec(a.shape, lambda b, c: (0,) * a.ndim)
    small = [prm["abar_re"], prm["abar_im"], prm["b_re"], prm["b_im"], prm["c_re"], prm["c_im"], prm["d_skip"]]
    state = jax.ShapeDtypeStruct((n_seq, S5_KT, LANES), F32)
    sspec = pl.BlockSpec((1, S5_KT, LANES), lambda b, c: (b, 0, 0))
    return pl.pallas_call(
        functools.partial(_s5_prompt_body, r=r),
        grid=(n_seq, nc),
        in_specs=[pl.BlockSpec((r, S5_WIDTH), lambda b, c: (b * nc + c, P_U // S5_WIDTH)),
                  *[full(a) for a in small]],
        out_specs=[pl.BlockSpec((r, S5_WIDTH), lambda b, c: (b * nc + c, 0)), sspec, sspec],
        out_shape=[jax.ShapeDtypeStruct((n_seq * seq_len, S5_WIDTH), BF16), state, state],
        scratch_shapes=[pltpu.VMEM((S5_KT * r, LANES), F32), pltpu.VMEM((S5_KT * r, LANES), F32)],
        compiler_params=_cparams(("parallel", "arbitrary")),
        name="s5_prompt",
    )(p, *small)


def _s5_sample_body(u_ref, h0re_ref, h0im_ref, are_ref, aim_ref, bre_ref, bim_ref, cre_ref, cim_ref, dsk_ref,
                    y_ref, hre_ref, him_ref, sre_ref, sim_ref, *, r, t):
    u = u_ref[...]
    bu_re, bu_im = _s5_in(u, bre_ref, bim_ref)
    nb = r // t
    for k in range(S5_KT):
        sre_ref[k * r:(k + 1) * r, :] = bu_re[:, k * LANES:(k + 1) * LANES]
        sim_ref[k * r:(k + 1) * r, :] = bu_im[:, k * LANES:(k + 1) * LANES]
    tiles = lambda x: jnp.concatenate([x[:, k * LANES:(k + 1) * LANES] for k in range(S5_KT)], axis=0)
    rows = lambda a: jnp.concatenate([jnp.broadcast_to(a[k:k + 1, :], (nb, LANES)) for k in range(S5_KT)], axis=0)
    a_re = rows(are_ref[...])
    a_im = rows(aim_ref[...])
    h_re = tiles(h0re_ref[...])
    h_im = tiles(h0im_ref[...])
    for s in range(t):
        idx = pl.ds(s, S5_KT * nb, stride=t)
        n_re = a_re * h_re - a_im * h_im + sre_ref[idx, :]
        n_im = a_re * h_im + a_im * h_re + sim_ref[idx, :]
        sre_ref[idx, :] = n_re
        sim_ref[idx, :] = n_im
        h_re, h_im = n_re, n_im
    untile = lambda x: jnp.concatenate([x[k * nb:(k + 1) * nb, :] for k in range(S5_KT)], axis=-1)
    hre_ref[...] = untile(h_re)
    him_ref[...] = untile(h_im)
    hs_re = jnp.concatenate([sre_ref[k * r:(k + 1) * r, :] for k in range(S5_KT)], axis=-1)
    hs_im = jnp.concatenate([sim_ref[k * r:(k + 1) * r, :] for k in range(S5_KT)], axis=-1)
    y_ref[...] = _s5_out(hs_re, hs_im, u, cre_ref, cim_ref, dsk_ref[...])


def s5_sample(p, h0_re, h0_im, prm, row0, n_seq, seq_len, r=256):
    nb = r // seq_len
    blk0 = row0 // r
    full = lambda a: pl.BlockSpec(a.shape, lambda i: (0,) * a.ndim)
    small = [prm["abar_re"], prm["abar_im"], prm["b_re"], prm["b_im"], prm["c_re"], prm["c_im"],
             prm["d_skip"]]
    state = jax.ShapeDtypeStruct((n_seq, S5_CH), F32)
    sspec = pl.BlockSpec((nb, S5_CH), lambda i: (i, 0))
    return pl.pallas_call(
        functools.partial(_s5_sample_body, r=r, t=seq_len),
        grid=(n_seq // nb,),
        in_specs=[pl.BlockSpec((r, S5_WIDTH), lambda i: (blk0 + i, P_U // S5_WIDTH)), sspec, sspec,
                  *[full(a) for a in small]],
        out_specs=[pl.BlockSpec((r, S5_WIDTH), lambda i: (i, 0)), sspec, sspec],
        out_shape=[jax.ShapeDtypeStruct((n_seq * seq_len, S5_WIDTH), BF16), state, state],
        scratch_shapes=[pltpu.VMEM((S5_KT * r, LANES), F32), pltpu.VMEM((S5_KT * r, LANES), F32)],
        compiler_params=_cparams(("parallel",)),
        name="s5_sample",
    )(p, h0_re, h0_im, *small)


def _hg_consts(r, t):
    i = np.arange(r)[:, None]
    j = np.arange(r)[None, :]
    sums, upper, pair = [], [], []
    s = 1
    while s < t:
        blk_i, blk_j = i // (2 * s), j // (2 * s)
        up_i = (i % (2 * s)) >= s
        mid_i = blk_i * 2 * s + s
        m_up = up_i & (j >= mid_i) & (j <= i)
        m_lo = (~up_i) & (j > i) & (j < mid_i)
        sums.append((m_up | m_lo).astype(np.float32))
        upper.append(np.broadcast_to(up_i, (r, 1)).astype(np.float32))
        pair.append(((blk_i == blk_j) & up_i & ((j % (2 * s)) < s)).astype(np.float32))
        s *= 2
    pair.append((i == j).astype(np.float32))
    return (jnp.asarray(np.stack(sums), BF16), jnp.asarray(np.stack(upper), F32),
            jnp.asarray(np.stack(pair), F32))


def _hg_gates(hf, lb):
    logf = -_softplus(-hf) + jnp.log1p(lb * jnp.exp(-hf))
    kk = (1.0 - lb) * jax.nn.sigmoid(-hf)
    return logf, kk


def _hg_intra(q, kk, v, logf, sums_ref, upper_ref, pair_ref):
    r = q.shape[0]
    nlev = sums_ref.shape[0]
    lf3 = _split3(logf)
    qb = q.astype(BF16)
    kb = kk.astype(BF16)
    vb = _pad_rows(v.astype(BF16), LANES) if r < LANES else v.astype(BF16)
    scores = [None] * HG_HEADS
    for lev in range(nlev + 1):
        if lev < nlev:
            m = sums_ref[lev]
            d = _dot(m, lf3[0]) + _dot(m, lf3[1]) + _dot(m, lf3[2])
            x = (jnp.where(upper_ref[lev] > 0.5, q, kk) * jnp.exp(d)).astype(BF16)
            xq, xk = x, x
        else:
            xq, xk = qb, kb
        mask = pair_ref[lev]
        for h in range(HG_HEADS):
            sl = slice(h * HG_KEY_DIM, (h + 1) * HG_KEY_DIM)
            sc = _dot_nt(xq[:, sl], xk[:, sl]) * mask
            scores[h] = sc if scores[h] is None else scores[h] + sc
    outs = [_dot(scores[h].astype(BF16), vb[:, h * HG_VAL_DIM:(h + 1) * HG_VAL_DIM]) for h in range(HG_HEADS)]
    return jnp.concatenate(outs, axis=-1)


def _hg_finish(o, hgate, ng):
    outs = []
    for h in range(HG_HEADS):
        oh = o[:, h * HG_VAL_DIM:(h + 1) * HG_VAL_DIM]
        outs.append(oh * _rms_scale(oh) * ng)
    return (jnp.concatenate(outs, axis=-1) * _silu(hgate)).astype(BF16)


def _hg_prompt_body(q_ref, f_ref, i_ref, gate_ref, lb_ref, ng_ref, lt_ref, last_ref, sums_ref, upper_ref,
                    pair_ref, o_ref, sout_ref, *, r):
    c = pl.program_id(1)

    @pl.when(c == 0)
    def _():
        sout_ref[...] = jnp.zeros_like(sout_ref)

    q = q_ref[...]
    v = i_ref[...]
    logf, kk = _hg_gates(f_ref[...], lb_ref[...])
    o = _hg_intra(q, kk, v, logf, sums_ref, upper_ref, pair_ref)
    b = _sel_dot(lt_ref[...], logf, 3)
    blast = b[r - 1:r, :]
    qe = (q * jnp.exp(b)).astype(BF16)
    kw = (kk * jnp.exp(blast - b)).astype(BF16)
    vb = v.astype(BF16)
    ones = jnp.ones((r, HG_VAL_DIM), BF16)
    rows = lax.broadcasted_iota(jnp.int32, (r, 1), 0)
    dh, dl = _split2(jnp.where(rows == r - 1, jnp.exp(b), 0.0))
    inter = []
    for h in range(HG_HEADS):
        sl = slice(h * HG_KEY_DIM, (h + 1) * HG_KEY_DIM)
        s = sout_ref[0, h]
        inter.append(_dot(qe[:, sl], s.astype(BF16)))
        dcol = _dot_tn(dh[:, sl], ones) + _dot_tn(dl[:, sl], ones)
        sout_ref[0, h] = s * dcol + _dot_tn(kw[:, sl], vb[:, sl])
    o = o + jnp.concatenate(inter, axis=-1)
    o_ref[...] = _hg_finish(o, gate_ref[...], ng_ref[...])


def hg_prompt(p, prm, n_seq, seq_len, r=128):
    nc = seq_len // r
    lt, last = _tile_consts(r, r)
    consts = [lt, last, *_hg_consts(r, r)]
    full = lambda a: pl.BlockSpec(a.shape, lambda b, c: (0,) * a.ndim)
    small = [prm["lb"], prm["norm"], *consts]
    col = lambda off: pl.BlockSpec((r, HG_WIDTH), lambda b, c: (b * nc + c, off // HG_WIDTH))
    return pl.pallas_call(
        functools.partial(_hg_prompt_body, r=r),
        grid=(n_seq, nc),
        in_specs=[col(P_HQ), col(P_HF), col(P_HI), col(P_HGATE), *[full(a) for a in small]],
        out_specs=[pl.BlockSpec((r, HG_WIDTH), lambda b, c: (b * nc + c, 0)),
                   pl.BlockSpec((1, HG_HEADS, HG_KEY_DIM, HG_VAL_DIM), lambda b, c: (b, 0, 0, 0))],
        out_shape=[jax.ShapeDtypeStruct((n_seq * seq_len, HG_WIDTH), BF16),
                   jax.ShapeDtypeStruct((n_seq, HG_HEADS, HG_KEY_DIM, HG_VAL_DIM), F32)],
        compiler_params=_cparams(("parallel", "arbitrary")),
        name="hg_prompt",
    )(p, p, p, p, *small)


def _hg_sample_body(q_ref, f_ref, i_ref, gate_ref, s0_ref, lb_ref, ng_ref, lt_ref, last_ref, sums_ref,
                    upper_ref, pair_ref, o_ref, sout_ref, *, r, t):
    q = q_ref[...]
    v = i_ref[...]
    logf, kk = _hg_gates(f_ref[...], lb_ref[...])
    o = _hg_intra(q, kk, v, logf, sums_ref, upper_ref, pair_ref)
    b = _sel_dot(lt_ref[...], logf, 3)
    blast = _sel_dot(last_ref[...], b, 3)
    qe = (q * jnp.exp(b)).astype(BF16)
    kw = kk * jnp.exp(blast - b)
    sdec = jnp.exp(blast)
    vb = v.astype(BF16)
    pr = 2 * t
    prow = lax.broadcasted_iota(jnp.int32, (pr, 1), 0)
    ones = jnp.ones((pr, HG_VAL_DIM), BF16)
    rows_out = []
    for p2 in range(r // pr):
        rs = slice(p2 * pr, (p2 + 1) * pr)
        heads = []
        for h in range(HG_HEADS):
            sl = slice(h * HG_KEY_DIM, (h + 1) * HG_KEY_DIM)
            acc = None
            for s in range(2):
                bi = 2 * p2 + s
                mine = (prow >= s * t) & (prow < (s + 1) * t)
                s0 = s0_ref[bi, h]
                oi = jnp.where(mine, _dot(qe[rs, sl], s0.astype(BF16)), 0.0)
                acc = oi if acc is None else acc + oi
                upd = _dot_tn(jnp.where(mine, kw[rs, sl], 0.0).astype(BF16), vb[rs, sl])
                dh, dl = _split2(jnp.where(prow == (s + 1) * t - 1, sdec[rs, sl], 0.0))
                dcol = _dot_tn(dh, ones) + _dot_tn(dl, ones)
                sout_ref[bi, h] = s0 * dcol + upd
            heads.append(acc)
        rows_out.append(jnp.concatenate(heads, axis=-1))
    o = o + jnp.concatenate(rows_out, axis=0)
    o_ref[...] = _hg_finish(o, gate_ref[...], ng_ref[...])


def hg_sample(p, s0, prm, row0, n_seq, seq_len, r=128):
    nb = r // seq_len
    blk0 = row0 // r
    lt, last = _tile_consts(r, seq_len)
    consts = [lt, last, *_hg_consts(r, seq_len)]
    full = lambda a: pl.BlockSpec(a.shape, lambda i: (0,) * a.ndim)
    small = [prm["lb"], prm["norm"], *consts]
    col = lambda off: pl.BlockSpec((r, HG_WIDTH), lambda i: (blk0 + i, off // HG_WIDTH))
    sspec = pl.BlockSpec((nb, HG_HEADS, HG_KEY_DIM, HG_VAL_DIM), lambda i: (i, 0, 0, 0))
    return pl.pallas_call(
        functools.partial(_hg_sample_body, r=r, t=seq_len),
        grid=(n_seq // nb,),
        in_specs=[col(P_HQ), col(P_HF), col(P_HI), col(P_HGATE), sspec, *[full(a) for a in small]],
        out_specs=[pl.BlockSpec((r, HG_WIDTH), lambda i: (i, 0)), sspec],
        out_shape=[jax.ShapeDtypeStruct((n_seq * seq_len, HG_WIDTH), BF16),
                   jax.ShapeDtypeStruct((n_seq, HG_HEADS, HG_KEY_DIM, HG_VAL_DIM), F32)],
        compiler_params=_cparams(("parallel",)),
        name="hg_sample",
    )(p, p, p, p, s0, *small)


def _s5_params(a_re, a_im, log_dt, b_re, b_im, c_re, c_im, d_skip):
    dt = jnp.exp(log_dt)[:, None]
    mag = jnp.exp(a_re * dt)
    abar_re = mag * jnp.cos(a_im * dt)
    abar_im = mag * jnp.sin(a_im * dt)
    den = a_re * a_re + a_im * a_im
    nr = abar_re - 1.0
    coef_re = (nr * a_re + abar_im * a_im) / den
    coef_im = (abar_im * a_re - nr * a_im) / den
    bbar_re = coef_re[..., None] * b_re - coef_im[..., None] * b_im
    bbar_im = coef_re[..., None] * b_im + coef_im[..., None] * b_re
    gpb = S5_GROUPS // S5_BLK
    eye = jnp.eye(gpb, dtype=F32)

    def in_blocks(bbar):
        bb = bbar.reshape(S5_BLK, gpb, S5_STATE, S5_GROUP_SIZE)
        return jnp.einsum("qgnk,gh->qgkhn", bb, eye).reshape(
            S5_BLK, gpb * S5_GROUP_SIZE, gpb * S5_STATE).astype(BF16)

    def out_blocks(c):
        cc = c.reshape(S5_BLK, gpb, S5_GROUP_SIZE, S5_STATE)
        return jnp.einsum("qgkn,gh->qgnhk", cc, eye).reshape(
            S5_BLK, gpb * S5_STATE, gpb * S5_GROUP_SIZE).astype(BF16)

    return {
        "abar_re": abar_re.reshape(S5_KT, LANES), "abar_im": abar_im.reshape(S5_KT, LANES),
        "b_re": in_blocks(bbar_re), "b_im": in_blocks(bbar_im),
        "c_re": out_blocks(c_re), "c_im": out_blocks(c_im),
        "d_skip": d_skip.reshape(1, S5_WIDTH),
    }


def _in_proj_weights(w_in):
    z0, xbc0, dt0, u0, gates0 = 0, 2048, 5120, 5152, 10272
    w = jnp.concatenate([w_in[:, gates0:], w_in[:, xbc0:dt0], w_in[:, u0:gates0], w_in[:, z0:xbc0]], axis=1)
    wdt = jnp.pad(w_in[:, dt0:u0], ((0, 0), (0, LANES - SSD_HEADS)))
    return w.astype(BF16), wdt.astype(BF16)


def _pad_lanes(v):
    return jnp.pad(v.reshape(1, -1), ((0, 0), (0, LANES - v.shape[-1])))


def kernel(x_prompt, x_sample, cache_mem_k, cache_mem_v, state_ssd, state_ssd_conv, state_s5_re, state_s5_im,
           state_hgrn, mem_prompt, norm_ffn1, ffn1_w1, ffn1_w3, ffn1_w2, norm_mix, w_in, ssd_conv_w, ssd_conv_b,
           ssd_dt_bias, ssd_a_log, ssd_d, ssd_norm, ssd_w_out, s5_a_re, s5_a_im, s5_log_dt, s5_b_re, s5_b_im,
           s5_c_re, s5_c_im, s5_d, s5_w_glu_a, s5_w_glu_b, hg_lower_bounds, hg_norm, hg_w_out, w_mix_out,
           norm_xa, norm_mem, xa_wq, xa_wk, xa_wv, xa_wo, norm_ffn2, ffn2_w1, ffn2_w3, ffn2_w2, norm_final):
    bp, lp, d = x_prompt.shape
    bs, ls, _ = x_sample.shape
    mp, ms = bp * lp, bs * ls
    x = jnp.concatenate([x_prompt.reshape(mp, d), x_sample.reshape(ms, d)], axis=0)
    mem = mem_prompt.reshape(bp * MEM_LEN, d)
    row = lambda v: v.reshape(1, -1)
    bf = lambda w: w.astype(BF16)

    lb_p = jax.nn.softmax(hg_lower_bounds, axis=0)
    lb_all = jnp.cumsum(lb_p, axis=0) - lb_p[0]

    outs = {k: [] for k in ("pk", "pv", "pss", "pcv", "psr", "psi", "phg", "sss", "scv", "ssr", "ssi", "shg")}
    for l in range(DEPTH):
        x = ffn(x, row(norm_ffn1[l]), bf(ffn1_w1[l]), bf(ffn1_w3[l]), bf(ffn1_w2[l]), row(norm_final), False)

        w_p, w_dt = _in_proj_weights(w_in[l])
        p, dt = in_proj(x, row(norm_mix[l]), w_p, w_dt)

        ssd_prm = {"conv_w": ssd_conv_w[l], "conv_b": row(ssd_conv_b[l]), "dt_bias": _pad_lanes(ssd_dt_bias[l]),
                   "a_log": _pad_lanes(ssd_a_log[l]), "d_skip": row(jnp.repeat(ssd_d[l], SSD_HEAD_DIM)),
                   "norm": row(ssd_norm[l])}
        ys_p, ss_p = ssd_prompt(p, dt, ssd_prm, bp, lp)
        ys_s, ss_s = ssd_sample(p, dt, state_ssd_conv[l], state_ssd[l].reshape(bs, SSD_WIDTH, SSD_STATE),
                                ssd_prm, mp, bs, ls)

        s5_prm = _s5_params(s5_a_re[l], s5_a_im[l], s5_log_dt[l], s5_b_re[l], s5_b_im[l], s5_c_re[l],
                            s5_c_im[l], s5_d[l])
        gy_p, sr_p, si_p = s5_prompt(p, s5_prm, bp, lp)
        gy_s, sr_s, si_s = s5_sample(p, state_s5_re[l].reshape(bs, S5_CH), state_s5_im[l].reshape(bs, S5_CH),
                                     s5_prm, mp, bs, ls)

        hg_prm = {"lb": row(lb_all[l]), "norm": row(hg_norm[l])}
        o_p, hg_p = hg_prompt(p, hg_prm, bp, lp)
        o_s, hg_s = hg_sample(p, state_hgrn[l], hg_prm, mp, bs, ls)

        cat = lambda a, b: jnp.concatenate([a, b], axis=0)
        x = branch_mix(x, cat(ys_p, ys_s), cat(gy_p, gy_s), cat(o_p, o_s), p, bf(ssd_w_out[l]),
                       bf(s5_w_glu_a[l]), bf(s5_w_glu_b[l]), bf(hg_w_out[l]), bf(w_mix_out[l]))

        q = norm_proj(x, row(norm_xa[l]), bf(xa_wq[l]), BF16)
        kv = norm_proj(mem, row(norm_mem[l]), bf(jnp.concatenate([xa_wk[l], xa_wv[l]], axis=1)), F32)
        at_p = xattn_prompt(q, kv, bp, lp)
        at_s = xattn_sample(q, cache_mem_k[l].reshape(bs, MEM_LEN, d), cache_mem_v[l].reshape(bs, MEM_LEN, d),
                            mp, bs, ls)
        x = res_mm(x, cat(at_p, at_s), bf(xa_wo[l]))

        x = ffn(x, row(norm_ffn2[l]), bf(ffn2_w1[l]), bf(ffn2_w3[l]), bf(ffn2_w2[l]), row(norm_final),
                l == DEPTH - 1)

        tail = SSD_CONV - 1
        outs["pk"].append(kv[:, :d].reshape(bp, MEM_LEN, XA_HEADS, XA_HEAD_DIM))
        outs["pv"].append(kv[:, d:].reshape(bp, MEM_LEN, XA_HEADS, XA_HEAD_DIM))
        outs["pss"].append(ss_p.reshape(bp, SSD_HEADS, SSD_HEAD_DIM, SSD_STATE))
        outs["pcv"].append(p[:mp].reshape(bp, lp, P_WIDTH)[:, lp - tail:, P_XBC:P_XBC + SSD_CONV_DIM])
        outs["psr"].append(sr_p.reshape(bp, S5_GROUPS, S5_STATE))
        outs["psi"].append(si_p.reshape(bp, S5_GROUPS, S5_STATE))
        outs["phg"].append(hg_p)
        outs["sss"].append(ss_s.reshape(bs, SSD_HEADS, SSD_HEAD_DIM, SSD_STATE))
        outs["scv"].append(p[mp:].reshape(bs, ls, P_WIDTH)[:, ls - tail:, P_XBC:P_XBC + SSD_CONV_DIM])
        outs["ssr"].append(sr_s.reshape(bs, S5_GROUPS, S5_STATE))
        outs["ssi"].append(si_s.reshape(bs, S5_GROUPS, S5_STATE))
        outs["shg"].append(hg_s)

    st = lambda k: jnp.stack(outs[k])
    return (x[:mp].reshape(bp, lp, d), x[mp:].reshape(bs, ls, d),
            st("pk"), st("pv"), st("pss"), st("pcv"), st("psr"), st("psi"), st("phg"),
            st("sss"), st("scv"), st("ssr"), st("ssi"), st("shg"))
```

```python
import functools
import math

import jax
import jax.numpy as jnp
import numpy as np
from jax import lax
from jax.experimental import pallas as pl
from jax.experimental.pallas import tpu as pltpu

F32 = jnp.float32
BF16 = jnp.bfloat16

D_MODEL = 2048
DEPTH = 2
NORM_EPS = 1e-5
SSD_HEAD_DIM = 64
SSD_HEADS = 32
SSD_GROUPS = 4
SSD_STATE = 128
SSD_CONV = 4
SSD_WIDTH = 2048
SSD_CONV_DIM = 3072
S5_WIDTH = 1024
S5_GROUP_SIZE = 16
S5_GROUPS = 64
S5_STATE = 64
HG_WIDTH = 1024
HG_HEADS = 8
HG_KEY_DIM = 128
HG_VAL_DIM = 128
MEM_LEN = 256
XA_HEADS = 4
XA_HEAD_DIM = 512
FFN_DIM = 5632

P_GATES, P_XBC, P_U, P_HQ, P_HF, P_HI, P_HGATE, P_Z = 0, 6144, 9216, 10240, 11264, 12288, 13312, 14336
P_WIDTH = 16384

V7X_VMEM_LIMIT = 56 * 1024 * 1024


def _cparams(sem, vmem=V7X_VMEM_LIMIT):
    return pltpu.CompilerParams(dimension_semantics=sem, vmem_limit_bytes=vmem)


def _rms_scale(x):
    return lax.rsqrt(jnp.mean(x * x, axis=-1, keepdims=True) + NORM_EPS)


def _silu(x):
    return x * jax.nn.sigmoid(x)


def _dot(a, b):
    return jnp.dot(a, b, preferred_element_type=F32)


def _dot_nt(a, b):
    return lax.dot_general(a, b, (((1,), (1,)), ((), ())), preferred_element_type=F32)


def _dot_tn(a, b):
    return lax.dot_general(a, b, (((0,), (0,)), ((), ())), preferred_element_type=F32)


def _split2(x):
    hi = x.astype(BF16)
    lo = (x - hi.astype(F32)).astype(BF16)
    return hi, lo


def _split3(x):
    hi = x.astype(BF16)
    r = x - hi.astype(F32)
    mid = r.astype(BF16)
    lo = (r - mid.astype(F32)).astype(BF16)
    return hi, mid, lo


def _sel_dot(sel, x, parts=3):
    ps = _split3(x) if parts == 3 else _split2(x)
    out = _dot(sel, ps[0])
    for p in ps[1:]:
        out = out + _dot(sel, p)
    return out


def _dot_sel(x, sel, parts=2):
    ps = _split3(x) if parts == 3 else _split2(x)
    out = _dot(ps[0], sel)
    for p in ps[1:]:
        out = out + _dot(p, sel)
    return out


def _norm_proj_body(x_ref, g_ref, w_ref, o_ref, h_ref):
    @pl.when(pl.program_id(1) == 0)
    def _():
        x = x_ref[...]
        h_ref[...] = (x * _rms_scale(x) * g_ref[...]).astype(BF16)

    o_ref[...] = _dot(h_ref[...], w_ref[...]).astype(o_ref.dtype)


def norm_proj(x, g, w, out_dtype, bm=1024, bn=1024):
    m, k = x.shape
    n = w.shape[1]
    return pl.pallas_call(
        _norm_proj_body,
        grid=(m // bm, n // bn),
        in_specs=[pl.BlockSpec((bm, k), lambda i, j: (i, 0)),
                  pl.BlockSpec((1, k), lambda i, j: (0, 0)),
                  pl.BlockSpec((k, bn), lambda i, j: (0, j))],
        out_specs=pl.BlockSpec((bm, bn), lambda i, j: (i, j)),
        out_shape=jax.ShapeDtypeStruct((m, n), out_dtype),
        scratch_shapes=[pltpu.VMEM((bm, k), BF16)],
        compiler_params=_cparams(("parallel", "arbitrary")),
        name="norm_proj",
    )(x, g, w)


def _in_proj_body(x_ref, g_ref, w_ref, wdt_ref, o_ref, odt_ref, h_ref):
    @pl.when(pl.program_id(1) == 0)
    def _():
        x = x_ref[...]
        h = (x * _rms_scale(x) * g_ref[...]).astype(BF16)
        h_ref[...] = h
        odt_ref[...] = _dot(h, wdt_ref[...])

    o_ref[...] = _dot(h_ref[...], w_ref[...])


def in_proj(x, g, w, wdt, bm=1024, bn=1024):
    m, k = x.shape
    n = w.shape[1]
    ndt = wdt.shape[1]
    return pl.pallas_call(
        _in_proj_body,
        grid=(m // bm, n // bn),
        in_specs=[pl.BlockSpec((bm, k), lambda i, j: (i, 0)),
                  pl.BlockSpec((1, k), lambda i, j: (0, 0)),
                  pl.BlockSpec((k, bn), lambda i, j: (0, j)),
                  pl.BlockSpec((k, ndt), lambda i, j: (0, 0))],
        out_specs=[pl.BlockSpec((bm, bn), lambda i, j: (i, j)),
                   pl.BlockSpec((bm, ndt), lambda i, j: (i, 0))],
        out_shape=[jax.ShapeDtypeStruct((m, n), F32), jax.ShapeDtypeStruct((m, ndt), F32)],
        scratch_shapes=[pltpu.VMEM((bm, k), BF16)],
        compiler_params=_cparams(("parallel", "arbitrary")),
        name="in_proj",
    )(x, g, w, wdt)


def _res_mm_body(x_ref, a_ref, w_ref, o_ref):
    o_ref[...] = x_ref[...] + _dot(a_ref[...], w_ref[...])


def res_mm(x, a, w, bm=1024, bn=1024):
    m, n = x.shape
    k = a.shape[1]
    return pl.pallas_call(
        _res_mm_body,
        grid=(m // bm, n // bn),
        in_specs=[pl.BlockSpec((bm, bn), lambda i, j: (i, j)),
                  pl.BlockSpec((bm, k), lambda i, j: (i, 0)),
                  pl.BlockSpec((k, bn), lambda i, j: (0, j))],
        out_specs=pl.BlockSpec((bm, bn), lambda i, j: (i, j)),
        out_shape=jax.ShapeDtypeStruct((m, n), F32),
        compiler_params=_cparams(("parallel", "parallel")),
        name="res_mm",
    )(x, a, w)


def _ffn_body(x_ref, g_ref, w1_ref, w3_ref, w2_ref, gf_ref, o_ref, h_ref, acc_ref, *, final_norm):
    f = pl.program_id(1)

    @pl.when(f == 0)
    def _():
        x = x_ref[...]
        h_ref[...] = (x * _rms_scale(x) * g_ref[...]).astype(BF16)
        acc_ref[...] = jnp.zeros_like(acc_ref)

    h = h_ref[...]
    a = _silu(_dot(h, w1_ref[...])) * _dot(h, w3_ref[...])
    acc_ref[...] += _dot(a.astype(BF16), w2_ref[...])

    @pl.when(f == pl.num_programs(1) - 1)
    def _():
        y = x_ref[...] + 0.5 * acc_ref[...]
        if final_norm:
            y = y * _rms_scale(y) * gf_ref[...]
        o_ref[...] = y


def ffn(x, g, w1, w3, w2, gf, final_norm, bm=512, bf=512):
    m, d = x.shape
    fdim = w1.shape[1]
    return pl.pallas_call(
        functools.partial(_ffn_body, final_norm=final_norm),
        grid=(m // bm, fdim // bf),
        in_specs=[pl.BlockSpec((bm, d), lambda i, f: (i, 0)),
                  pl.BlockSpec((1, d), lambda i, f: (0, 0)),
                  pl.BlockSpec((d, bf), lambda i, f: (0, f)),
                  pl.BlockSpec((d, bf), lambda i, f: (0, f)),
                  pl.BlockSpec((bf, d), lambda i, f: (f, 0)),
                  pl.BlockSpec((1, d), lambda i, f: (0, 0))],
        out_specs=pl.BlockSpec((bm, d), lambda i, f: (i, 0)),
        out_shape=jax.ShapeDtypeStruct((m, d), F32),
        scratch_shapes=[pltpu.VMEM((bm, d), BF16), pltpu.VMEM((bm, d), F32)],
        compiler_params=_cparams(("parallel", "arbitrary")),
        name="ffn",
    )(x, g, w1, w3, w2, gf)


def _mix_body(x_ref, ys_ref, gy_ref, o_ref, ga_ref, gb_ref, gc_ref, wssd_ref, wa_ref, wb_ref, whg_ref,
              wmix_ref, out_ref, acc_ref):
    j = pl.program_id(1)

    @pl.when(j == 0)
    def _():
        acc_ref[...] = jnp.zeros_like(acc_ref)

    gy = gy_ref[...]
    y_a = _dot(ys_ref[...], wssd_ref[...])
    y_b = _dot(gy, wa_ref[...]) * jax.nn.sigmoid(_dot(gy, wb_ref[...]))
    y_c = _dot(o_ref[...], whg_ref[...])
    mix = (jax.nn.sigmoid(ga_ref[...]) * y_a + jax.nn.sigmoid(gb_ref[...]) * y_b
           + jax.nn.sigmoid(gc_ref[...]) * y_c)
    acc_ref[...] += _dot(mix.astype(BF16), wmix_ref[...])

    @pl.when(j == pl.num_programs(1) - 1)
    def _():
        out_ref[...] = x_ref[...] + acc_ref[...]


def branch_mix(x, ys, gy, o, p, wssd, wa, wb, whg, wmix, bm=512, bn=512):
    m, d = x.shape
    nj = d // bn
    return pl.pallas_call(
        _mix_body,
        grid=(m // bm, nj),
        in_specs=[pl.BlockSpec((bm, d), lambda i, j: (i, 0)),
                  pl.BlockSpec((bm, ys.shape[1]), lambda i, j: (i, 0)),
                  pl.BlockSpec((bm, gy.shape[1]), lambda i, j: (i, 0)),
                  pl.BlockSpec((bm, o.shape[1]), lambda i, j: (i, 0)),
                  pl.BlockSpec((bm, bn), lambda i, j: (i, j)),
                  pl.BlockSpec((bm, bn), lambda i, j: (i, nj + j)),
                  pl.BlockSpec((bm, bn), lambda i, j: (i, 2 * nj + j)),
                  pl.BlockSpec((wssd.shape[0], bn), lambda i, j: (0, j)),
                  pl.BlockSpec((wa.shape[0], bn), lambda i, j: (0, j)),
                  pl.BlockSpec((wb.shape[0], bn), lambda i, j: (0, j)),
                  pl.BlockSpec((whg.shape[0], bn), lambda i, j: (0, j)),
                  pl.BlockSpec((bn, d), lambda i, j: (j, 0))],
        out_specs=pl.BlockSpec((bm, d), lambda i, j: (i, 0)),
        out_shape=jax.ShapeDtypeStruct((m, d), F32),
        scratch_shapes=[pltpu.VMEM((bm, d), F32)],
        compiler_params=_cparams(("parallel", "arbitrary")),
        name="branch_mix",
    )(x, ys, gy, o, p, p, p, wssd, wa, wb, whg, wmix)


def _attend(q, k, v):
    outs = []
    for h in range(XA_HEADS):
        sl = slice(h * XA_HEAD_DIM, (h + 1) * XA_HEAD_DIM)
        s = _dot_nt(q[:, sl], k[:, sl]) * (XA_HEAD_DIM ** -0.5)
        s = s - jnp.max(s, axis=-1, keepdims=True)
        e = jnp.exp(s)
        p = e / jnp.sum(e, axis=-1, keepdims=True)
        outs.append(_dot(p.astype(BF16), v[:, sl]))
    return jnp.concatenate(outs, axis=-1)


def _xattn_prompt_body(q_ref, kv_ref, o_ref):
    k = kv_ref[:, :D_MODEL].astype(BF16)
    v = kv_ref[:, D_MODEL:].astype(BF16)
    o_ref[...] = _attend(q_ref[...], k, v).astype(BF16)


def xattn_prompt(q, kv, n_seq, seq_len, bl=512):
    nl = seq_len // bl
    return pl.pallas_call(
        _xattn_prompt_body,
        grid=(n_seq, nl),
        in_specs=[pl.BlockSpec((bl, D_MODEL), lambda b, i: (b * nl + i, 0)),
                  pl.BlockSpec((MEM_LEN, 2 * D_MODEL), lambda b, i: (b, 0))],
        out_specs=pl.BlockSpec((bl, D_MODEL), lambda b, i: (b * nl + i, 0)),
        out_shape=jax.ShapeDtypeStruct(q.shape, BF16),
        compiler_params=_cparams(("parallel", "arbitrary")),
        name="xattn_prompt",
    )(q, kv)


def _xattn_sample_body(q_ref, k_ref, v_ref, buf_ref, o_ref, *, seq_len):
    del buf_ref
    q = q_ref[...]
    rows = lax.broadcasted_iota(jnp.int32, (2 * seq_len, D_MODEL), 0)
    o0 = _attend(q, k_ref[0].astype(BF16), v_ref[0].astype(BF16))
    o1 = _attend(q, k_ref[1].astype(BF16), v_ref[1].astype(BF16))
    o_ref[...] = jnp.where(rows < seq_len, o0, o1).astype(BF16)


def xattn_sample(q, k, v, buf, layer, row0, n_seq, seq_len):
    r = 2 * seq_len
    blk0 = row0 // r
    return pl.pallas_call(
        functools.partial(_xattn_sample_body, seq_len=seq_len),
        grid=(n_seq // 2,),
        in_specs=[pl.BlockSpec((r, D_MODEL), lambda i: (blk0 + i, 0)),
                  pl.BlockSpec((None, 2, MEM_LEN, D_MODEL), lambda i: (layer, i, 0, 0)),
                  pl.BlockSpec((None, 2, MEM_LEN, D_MODEL), lambda i: (layer, i, 0, 0)),
                  pl.BlockSpec(memory_space=pl.ANY)],
        out_specs=pl.BlockSpec((r, D_MODEL), lambda i: (blk0 + i, 0)),
        out_shape=jax.ShapeDtypeStruct(buf.shape, BF16),
        input_output_aliases={3: 0},
        compiler_params=_cparams(("parallel",)),
        name="xattn_sample",
    )(q, k, v, buf)


LANES = 128
HALO = 8


def _softplus(x):
    return jnp.maximum(x, 0.0) + jnp.log1p(jnp.exp(-jnp.abs(x)))


def _pad_rows(x, rows):
    if x.shape[0] == rows:
        return x
    return jnp.concatenate([x, jnp.zeros((rows - x.shape[0], x.shape[1]), x.dtype)], axis=0)


def _tile_consts(r, t):
    i = np.arange(r)[:, None]
    j = np.arange(r)[None, :]
    same = (i // t) == (j // t)
    lt = (same & (j <= i)).astype(np.float32)
    last = (same & (j % t == t - 1)).astype(np.float32)
    return jnp.asarray(lt, BF16), jnp.asarray(last, BF16)


def _head_expand(n_heads, width, rows=LANES):
    e = np.zeros((rows, n_heads * width), np.float32)
    for h in range(n_heads):
        e[h, h * width:(h + 1) * width] = 1.0
    return jnp.asarray(e, BF16)


def _ssd_tile(xc, dt_raw, dtb, a_log, lt, last, e, e128, r, t):
    xs = xc[:, :SSD_WIDTH]
    dt = _softplus(dt_raw + dtb)
    d_a = dt * (-jnp.exp(a_log))
    a = _sel_dot(lt, d_a, 3)
    a_e = _dot_sel(a, e, 3)
    dt_e = _dot_sel(dt, e, 2)
    alast_e = _sel_dot(last, a_e, 3)
    a_col = _dot_sel(a, e128, 3)
    a_t = _pad_rows(a, LANES).T

    row = lax.broadcasted_iota(jnp.int32, (r, LANES), 0)
    col = lax.broadcasted_iota(jnp.int32, (r, LANES), 1)
    valid = (col <= row) & (col >= (row // t) * t)
    lane_lo = col < SSD_HEAD_DIM

    xdt = _pad_rows((xs * dt_e).astype(BF16), LANES)
    hpg = SSD_HEADS // SSD_GROUPS
    ys = []
    for g in range(SSD_GROUPS):
        bg = xc[:, SSD_WIDTH + g * SSD_STATE:SSD_WIDTH + (g + 1) * SSD_STATE].astype(BF16)
        cg = xc[:, SSD_WIDTH + (SSD_GROUPS + g) * SSD_STATE:
                SSD_WIDTH + (SSD_GROUPS + g + 1) * SSD_STATE].astype(BF16)
        cb = _dot_nt(cg, _pad_rows(bg, LANES))
        for hp in range(hpg // 2):
            h0 = g * hpg + 2 * hp
            res = []
            for h in (h0, h0 + 1):
                rel = a_col[:, h * LANES:(h + 1) * LANES] - a_t[h:h + 1, :]
                dec = jnp.where(valid, jnp.exp(jnp.where(valid, rel, 0.0)), 0.0)
                res.append(_dot((cb * dec).astype(BF16), xdt[:, h0 * SSD_HEAD_DIM:(h0 + 2) * SSD_HEAD_DIM]))
            ys.append(jnp.where(lane_lo, res[0], res[1]))
    y_intra = jnp.concatenate(ys, axis=-1)
    return xs, y_intra, a_e, dt_e, alast_e


def _ssd_finish(y, xs, z, dsk, ng):
    y = y + dsk * xs
    y = y * _silu(z)
    return (y * _rms_scale(y) * ng).astype(BF16)


def _conv_silu(ext_ref, cw_ref, cb_ref, base, r):
    acc = cb_ref[...] + cw_ref[SSD_CONV - 1:SSD_CONV, :] * ext_ref[pl.ds(base, r), :]
    for k in range(1, SSD_CONV):
        acc = acc + cw_ref[SSD_CONV - 1 - k:SSD_CONV - k, :] * ext_ref[pl.ds(base - k, r), :]
    return _silu(acc)


def _ssd_prompt_body(xbc_ref, z_ref, dt_ref, cw_ref, cb_ref, dtb_ref, alog_ref, dsk_ref, ng_ref,
                     lt_ref, last_ref, e_ref, e128_ref, y_ref, sout_ref, ext_ref, st_ref, *, r):
    c = pl.program_id(1)

    @pl.when(c == 0)
    def _():
        ext_ref[0:HALO, :] = jnp.zeros((HALO, SSD_CONV_DIM), F32)
        st_ref[...] = jnp.zeros_like(st_ref)

    ext_ref[HALO:HALO + r, :] = xbc_ref[...]
    xc = _conv_silu(ext_ref, cw_ref, cb_ref, HALO, r)
    ext_ref[0:HALO, :] = xbc_ref[r - HALO:r, :]

    xs, y, a_e, dt_e, alast_e = _ssd_tile(xc, dt_ref[...], dtb_ref[...], alog_ref[...], lt_ref[...],
                                          last_ref[...], e_ref[...], e128_ref[...], r, r)
    ea_e = jnp.exp(a_e)
    xw = (xs * (dt_e * jnp.exp(alast_e - a_e))).astype(BF16)
    sdec = jnp.exp(alast_e[0:1, :])
    gw = SSD_WIDTH // SSD_GROUPS
    inter = []
    for g in range(SSD_GROUPS):
        bg = xc[:, SSD_WIDTH + g * SSD_STATE:SSD_WIDTH + (g + 1) * SSD_STATE].astype(BF16)
        cg = xc[:, SSD_WIDTH + (SSD_GROUPS + g) * SSD_STATE:
                SSD_WIDTH + (SSD_GROUPS + g + 1) * SSD_STATE].astype(BF16)
        st = st_ref[:, g * gw:(g + 1) * gw]
        inter.append(_dot(cg, st.astype(BF16)))
        st_ref[:, g * gw:(g + 1) * gw] = st * sdec[:, g * gw:(g + 1) * gw] + _dot_tn(bg, xw[:, g * gw:(g + 1) * gw])
    y = y + jnp.concatenate(inter, axis=-1) * ea_e
    y_ref[...] = _ssd_finish(y, xs, z_ref[...], dsk_ref[...], ng_ref[...])

    @pl.when(c == pl.num_programs(1) - 1)
    def _():
        sout_ref[0] = st_ref[...].T


def _ssd_consts(r, t):
    lt, last = _tile_consts(r, t)
    return lt, last, _head_expand(SSD_HEADS, SSD_HEAD_DIM), _head_expand(SSD_HEADS, LANES)


def ssd_prompt(p, dt, prm, n_seq, seq_len, r=128):
    nc = seq_len // r
    consts = _ssd_consts(r, r)
    full = lambda a: pl.BlockSpec(a.shape, lambda b, c: (0,) * a.ndim)
    small = [prm["conv_w"], prm["conv_b"], prm["dt_bias"], prm["a_log"], prm["d_skip"], prm["norm"], *consts]
    return pl.pallas_call(
        functools.partial(_ssd_prompt_body, r=r),
        grid=(n_seq, nc),
        in_specs=[pl.BlockSpec((r, SSD_CONV_DIM), lambda b, c: (b * nc + c, P_XBC // SSD_CONV_DIM)),
                  pl.BlockSpec((r, SSD_WIDTH), lambda b, c: (b * nc + c, P_Z // SSD_WIDTH)),
                  pl.BlockSpec((r, LANES), lambda b, c: (b * nc + c, 0)),
                  *[full(a) for a in small]],
        out_specs=[pl.BlockSpec((r, SSD_WIDTH), lambda b, c: (b * nc + c, 0)),
                   pl.BlockSpec((1, SSD_WIDTH, SSD_STATE), lambda b, c: (b, 0, 0))],
        out_shape=[jax.ShapeDtypeStruct((p.shape[0], SSD_WIDTH), BF16),
                   jax.ShapeDtypeStruct((n_seq, SSD_WIDTH, SSD_STATE), F32)],
        scratch_shapes=[pltpu.VMEM((HALO + r, SSD_CONV_DIM), F32), pltpu.VMEM((SSD_STATE, SSD_WIDTH), F32)],
        compiler_params=_cparams(("parallel", "arbitrary")),
        name="ssd_prompt",
    )(p, p, dt, *small)


def _ssd_sample_body(xbc_ref, z_ref, dt_ref, buf_ref, s0_ref, cw_ref, cb_ref, dtb_ref, alog_ref, dsk_ref,
                     ng_ref, lt_ref, last_ref, e_ref, e128_ref, *rest, r, t):
    y_ref, sout_ref, ext_ref = rest[-3:]
    nb = r // t
    pitch = HALO + t
    for b in range(nb):
        ext_ref[b * pitch + HALO - (SSD_CONV - 1):b * pitch + HALO, :] = buf_ref[b]
        ext_ref[b * pitch + HALO:(b + 1) * pitch, :] = xbc_ref[b * t:(b + 1) * t, :]
    xc = jnp.concatenate([_conv_silu(ext_ref, cw_ref, cb_ref, b * pitch + HALO, t) for b in range(nb)], axis=0)

    xs, y, a_e, dt_e, alast_e = _ssd_tile(xc, dt_ref[...], dtb_ref[...], alog_ref[...], lt_ref[...],
                                          last_ref[...], e_ref[...], e128_ref[...], r, t)
    ea_e = jnp.exp(a_e)
    xw = xs * (dt_e * jnp.exp(alast_e - a_e))
    sdec = jnp.exp(alast_e)
    gw = SSD_WIDTH // SSD_GROUPS
    pr = 2 * t
    prow = lax.broadcasted_iota(jnp.int32, (pr, 1), 0)
    ones = jnp.ones((pr, SSD_STATE), BF16)
    inter = []
    for g in range(SSD_GROUPS):
        bg = xc[:, SSD_WIDTH + g * SSD_STATE:SSD_WIDTH + (g + 1) * SSD_STATE].astype(BF16)
        cg = xc[:, SSD_WIDTH + (SSD_GROUPS + g) * SSD_STATE:
                SSD_WIDTH + (SSD_GROUPS + g + 1) * SSD_STATE].astype(BF16)
        cols = slice(g * gw, (g + 1) * gw)
        rows_out = []
        for q in range(nb // 2):
            rs = slice(q * pr, (q + 1) * pr)
            acc = None
            for s in range(2):
                b = 2 * q + s
                mine = (prow >= s * t) & (prow < (s + 1) * t)
                s0 = s0_ref[b, cols, :]
                yi = _dot_nt(cg[rs], s0.astype(BF16))
                acc = jnp.where(mine, yi, 0.0) if acc is None else acc + jnp.where(mine, yi, 0.0)
                upd = _dot_tn(jnp.where(mine, xw[rs, cols], 0.0).astype(BF16), bg[rs])
                lastrow = prow == (s + 1) * t - 1
                dh, dl = _split2(jnp.where(lastrow, sdec[rs, cols], 0.0))
                dcol = _dot_tn(dh, ones) + _dot_tn(dl, ones)
                sout_ref[b, cols, :] = s0 * dcol + upd
            rows_out.append(acc)
        inter.append(jnp.concatenate(rows_out, axis=0))
    y = y + jnp.concatenate(inter, axis=-1) * ea_e
    y_ref[...] = _ssd_finish(y, xs, z_ref[...], dsk_ref[...], ng_ref[...])


def ssd_sample(p, dt, conv_buf, s0, ybuf, sprev, prm, layer, row0, n_seq, seq_len, r=64):
    nb = r // seq_len
    blk0 = row0 // r
    consts = _ssd_consts(r, seq_len)
    full = lambda a: pl.BlockSpec(a.shape, lambda i: (0,) * a.ndim)
    small = [prm["conv_w"], prm["conv_b"], prm["dt_bias"], prm["a_log"], prm["d_skip"], prm["norm"], *consts]
    inplace = [ybuf] if sprev is None else [ybuf, sprev]
    n_in = 5 + len(small)
    return pl.pallas_call(
        functools.partial(_ssd_sample_body, r=r, t=seq_len),
        grid=(n_seq // nb,),
        in_specs=[pl.BlockSpec((r, SSD_CONV_DIM), lambda i: (blk0 + i, P_XBC // SSD_CONV_DIM)),
                  pl.BlockSpec((r, SSD_WIDTH), lambda i: (blk0 + i, P_Z // SSD_WIDTH)),
                  pl.BlockSpec((r, LANES), lambda i: (blk0 + i, 0)),
                  pl.BlockSpec((None, nb, SSD_CONV - 1, SSD_CONV_DIM), lambda i: (layer, i, 0, 0)),
                  pl.BlockSpec((None, nb, SSD_WIDTH, SSD_STATE), lambda i: (layer, i, 0, 0)),
                  *[full(a) for a in small],
                  *[pl.BlockSpec(memory_space=pl.ANY) for _ in inplace]],
        out_specs=[pl.BlockSpec((r, SSD_WIDTH), lambda i: (blk0 + i, 0)),
                   pl.BlockSpec((None, nb, SSD_WIDTH, SSD_STATE), lambda i: (layer, i, 0, 0))],
        out_shape=[jax.ShapeDtypeStruct(ybuf.shape, BF16),
                   jax.ShapeDtypeStruct(s0.shape, F32)],
        input_output_aliases={n_in + k: k for k in range(len(inplace))},
        scratch_shapes=[pltpu.VMEM((nb * (HALO + seq_len), SSD_CONV_DIM), F32)],
        compiler_params=_cparams(("parallel",)),
        name="ssd_sample",
    )(p, p, dt, conv_buf, s0, *small, *inplace)


S5_CH = S5_GROUPS * S5_STATE
S5_BLK = 4
S5_PSEQ = 8
S5_SSEQ = 32


def _s5_in(u, bre_ref, bim_ref):
    ub = u.astype(BF16)
    kin = S5_WIDTH // S5_BLK
    re = [_dot(ub[:, q * kin:(q + 1) * kin], bre_ref[q]) for q in range(S5_BLK)]
    im = [_dot(ub[:, q * kin:(q + 1) * kin], bim_ref[q]) for q in range(S5_BLK)]
    return jnp.concatenate(re, axis=-1), jnp.concatenate(im, axis=-1)


def _s5_out(h_re, h_im, u, cre_ref, cim_ref, dsk):
    kst = S5_CH // S5_BLK
    hr = h_re.astype(BF16)
    hi = h_im.astype(BF16)
    y = [_dot(hr[:, q * kst:(q + 1) * kst], cre_ref[q]) - _dot(hi[:, q * kst:(q + 1) * kst], cim_ref[q])
         for q in range(S5_BLK)]
    y = jnp.concatenate(y, axis=-1) + dsk * u
    return jax.nn.gelu(y).astype(BF16)


def _s5_scan_body(u_ref, h0re_ref, h0im_ref, are_ref, aim_ref, bre_ref, bim_ref, cre_ref, cim_ref, dsk_ref,
                  y_ref, hre_ref, him_ref, sre_ref, sim_ref, *, s, tc):
    c = pl.program_id(1)

    @pl.when(c == 0)
    def _():
        sre_ref[0:s, :] = h0re_ref[...]
        sim_ref[0:s, :] = h0im_ref[...]

    u = u_ref[...]
    bu_re, bu_im = _s5_in(u, bre_ref, bim_ref)
    sre_ref[s:, :] = bu_re
    sim_ref[s:, :] = bu_im
    a_re = jnp.broadcast_to(are_ref[...], (s, S5_CH))
    a_im = jnp.broadcast_to(aim_ref[...], (s, S5_CH))

    def step(t, carry):
        prev = pl.ds(pl.multiple_of(t * s, s), s)
        cur = pl.ds(pl.multiple_of((t + 1) * s, s), s)
        h_re = sre_ref[prev, :]
        h_im = sim_ref[prev, :]
        sre_ref[cur, :] = a_re * h_re - a_im * h_im + sre_ref[cur, :]
        sim_ref[cur, :] = a_re * h_im + a_im * h_re + sim_ref[cur, :]
        return carry

    lax.fori_loop(0, tc, step, 0, unroll=4)
    h_re = sre_ref[tc * s:, :]
    h_im = sim_ref[tc * s:, :]
    y_ref[...] = _s5_out(sre_ref[s:, :], sim_ref[s:, :], u, cre_ref, cim_ref, dsk_ref[...])
    sre_ref[0:s, :] = h_re
    sim_ref[0:s, :] = h_im
    hre_ref[...] = h_re
    him_ref[...] = h_im


def s5_scan(u, h0_re, h0_im, prm, n_blocks, s, n_steps, tc):
    nc = n_steps // tc
    r = tc * s
    full = lambda a: pl.BlockSpec(a.shape, lambda b, c: (0,) * a.ndim)
    small = [prm["abar_re"], prm["abar_im"], prm["b_re"], prm["b_im"], prm["c_re"], prm["c_im"], prm["d_skip"]]
    state = jax.ShapeDtypeStruct((n_blocks * s, S5_CH), F32)
    sspec = pl.BlockSpec((s, S5_CH), lambda b, c: (b, 0))
    return pl.pallas_call(
        functools.partial(_s5_scan_body, s=s, tc=tc),
        grid=(n_blocks, nc),
        in_specs=[pl.BlockSpec((r, S5_WIDTH), lambda b, c: (b * nc + c, 0)), sspec, sspec,
                  *[full(a) for a in small]],
        out_specs=[pl.BlockSpec((r, S5_WIDTH), lambda b, c: (b * nc + c, 0)), sspec, sspec],
        out_shape=[jax.ShapeDtypeStruct((u.shape[0], S5_WIDTH), BF16), state, state],
        scratch_shapes=[pltpu.VMEM((s + r, S5_CH), F32), pltpu.VMEM((s + r, S5_CH), F32)],
        compiler_params=_cparams(("parallel", "arbitrary")),
        name="s5_scan",
    )(u, h0_re, h0_im, *small)


def _hg_consts(r, t):
    i = np.arange(r)[:, None]
    j = np.arange(r)[None, :]
    sums, upper, pair = [], [], []
    s = 1
    while s < t:
        blk_i, blk_j = i // (2 * s), j // (2 * s)
        up_i = (i % (2 * s)) >= s
        mid_i = blk_i * 2 * s + s
        m_up = up_i & (j >= mid_i) & (j <= i)
        m_lo = (~up_i) & (j > i) & (j < mid_i)
        sums.append((m_up | m_lo).astype(np.float32))
        upper.append(np.broadcast_to(up_i, (r, 1)).astype(np.float32))
        pair.append(((blk_i == blk_j) & up_i & ((j % (2 * s)) < s)).astype(np.float32))
        s *= 2
    pair.append((i == j).astype(np.float32))
    return (jnp.asarray(np.stack(sums), BF16), jnp.asarray(np.stack(upper), F32),
            jnp.asarray(np.stack(pair), F32))


def _hg_gates(hf, lb):
    logf = -_softplus(-hf) + jnp.log1p(lb * jnp.exp(-hf))
    kk = (1.0 - lb) * jax.nn.sigmoid(-hf)
    return logf, kk


def _hg_intra(q, kk, v, logf, sums_ref, upper_ref, pair_ref):
    r = q.shape[0]
    nlev = sums_ref.shape[0]
    lf3 = _split3(logf)
    qb = q.astype(BF16)
    kb = kk.astype(BF16)
    vb = _pad_rows(v.astype(BF16), LANES) if r < LANES else v.astype(BF16)
    scores = [None] * HG_HEADS
    for lev in range(nlev + 1):
        if lev < nlev:
            m = sums_ref[lev]
            d = _dot(m, lf3[0]) + _dot(m, lf3[1]) + _dot(m, lf3[2])
            x = (jnp.where(upper_ref[lev] > 0.5, q, kk) * jnp.exp(d)).astype(BF16)
            xq, xk = x, x
        else:
            xq, xk = qb, kb
        mask = pair_ref[lev]
        for h in range(HG_HEADS):
            sl = slice(h * HG_KEY_DIM, (h + 1) * HG_KEY_DIM)
            sc = _dot_nt(xq[:, sl], xk[:, sl]) * mask
            scores[h] = sc if scores[h] is None else scores[h] + sc
    outs = [_dot(scores[h].astype(BF16), vb[:, h * HG_VAL_DIM:(h + 1) * HG_VAL_DIM]) for h in range(HG_HEADS)]
    return jnp.concatenate(outs, axis=-1)


def _hg_finish(o, hgate, ng):
    outs = []
    for h in range(HG_HEADS):
        oh = o[:, h * HG_VAL_DIM:(h + 1) * HG_VAL_DIM]
        outs.append(oh * _rms_scale(oh) * ng)
    return (jnp.concatenate(outs, axis=-1) * _silu(hgate)).astype(BF16)


def _hg_prompt_body(q_ref, f_ref, i_ref, gate_ref, lb_ref, ng_ref, lt_ref, last_ref, sums_ref, upper_ref,
                    pair_ref, o_ref, sout_ref, *, r):
    c = pl.program_id(1)

    @pl.when(c == 0)
    def _():
        sout_ref[...] = jnp.zeros_like(sout_ref)

    q = q_ref[...]
    v = i_ref[...]
    logf, kk = _hg_gates(f_ref[...], lb_ref[...])
    o = _hg_intra(q, kk, v, logf, sums_ref, upper_ref, pair_ref)
    b = _sel_dot(lt_ref[...], logf, 3)
    blast = b[r - 1:r, :]
    qe = (q * jnp.exp(b)).astype(BF16)
    kw = (kk * jnp.exp(blast - b)).astype(BF16)
    vb = v.astype(BF16)
    ones = jnp.ones((r, HG_VAL_DIM), BF16)
    rows = lax.broadcasted_iota(jnp.int32, (r, 1), 0)
    dh, dl = _split2(jnp.where(rows == r - 1, jnp.exp(b), 0.0))
    inter = []
    for h in range(HG_HEADS):
        sl = slice(h * HG_KEY_DIM, (h + 1) * HG_KEY_DIM)
        s = sout_ref[0, h]
        inter.append(_dot(qe[:, sl], s.astype(BF16)))
        dcol = _dot_tn(dh[:, sl], ones) + _dot_tn(dl[:, sl], ones)
        sout_ref[0, h] = s * dcol + _dot_tn(kw[:, sl], vb[:, sl])
    o = o + jnp.concatenate(inter, axis=-1)
    o_ref[...] = _hg_finish(o, gate_ref[...], ng_ref[...])


def hg_prompt(p, prm, n_seq, seq_len, r=128):
    nc = seq_len // r
    lt, last = _tile_consts(r, r)
    consts = [lt, last, *_hg_consts(r, r)]
    full = lambda a: pl.BlockSpec(a.shape, lambda b, c: (0,) * a.ndim)
    small = [prm["lb"], prm["norm"], *consts]
    col = lambda off: pl.BlockSpec((r, HG_WIDTH), lambda b, c: (b * nc + c, off // HG_WIDTH))
    return pl.pallas_call(
        functools.partial(_hg_prompt_body, r=r),
        grid=(n_seq, nc),
        in_specs=[col(P_HQ), col(P_HF), col(P_HI), col(P_HGATE), *[full(a) for a in small]],
        out_specs=[pl.BlockSpec((r, HG_WIDTH), lambda b, c: (b * nc + c, 0)),
                   pl.BlockSpec((1, HG_HEADS, HG_KEY_DIM, HG_VAL_DIM), lambda b, c: (b, 0, 0, 0))],
        out_shape=[jax.ShapeDtypeStruct((p.shape[0], HG_WIDTH), BF16),
                   jax.ShapeDtypeStruct((n_seq, HG_HEADS, HG_KEY_DIM, HG_VAL_DIM), F32)],
        compiler_params=_cparams(("parallel", "arbitrary")),
        name="hg_prompt",
    )(p, p, p, p, *small)


def _hg_sample_body(q_ref, f_ref, i_ref, gate_ref, s0_ref, lb_ref, ng_ref, lt_ref, last_ref, sums_ref,
                    upper_ref, pair_ref, *rest, r, t):
    o_ref, sout_ref = rest[-2:]
    q = q_ref[...]
    v = i_ref[...]
    logf, kk = _hg_gates(f_ref[...], lb_ref[...])
    o = _hg_intra(q, kk, v, logf, sums_ref, upper_ref, pair_ref)
    b = _sel_dot(lt_ref[...], logf, 3)
    blast = _sel_dot(last_ref[...], b, 3)
    qe = (q * jnp.exp(b)).astype(BF16)
    kw = kk * jnp.exp(blast - b)
    sdec = jnp.exp(blast)
    vb = v.astype(BF16)
    pr = 2 * t
    prow = lax.broadcasted_iota(jnp.int32, (pr, 1), 0)
    ones = jnp.ones((pr, HG_VAL_DIM), BF16)
    rows_out = []
    for p2 in range(r // pr):
        rs = slice(p2 * pr, (p2 + 1) * pr)
        heads = []
        for h in range(HG_HEADS):
            sl = slice(h * HG_KEY_DIM, (h + 1) * HG_KEY_DIM)
            acc = None
            for s in range(2):
                bi = 2 * p2 + s
                mine = (prow >= s * t) & (prow < (s + 1) * t)
                s0 = s0_ref[bi, h]
                oi = jnp.where(mine, _dot(qe[rs, sl], s0.astype(BF16)), 0.0)
                acc = oi if acc is None else acc + oi
                upd = _dot_tn(jnp.where(mine, kw[rs, sl], 0.0).astype(BF16), vb[rs, sl])
                dh, dl = _split2(jnp.where(prow == (s + 1) * t - 1, sdec[rs, sl], 0.0))
                dcol = _dot_tn(dh, ones) + _dot_tn(dl, ones)
                sout_ref[bi, h] = s0 * dcol + upd
            heads.append(acc)
        rows_out.append(jnp.concatenate(heads, axis=-1))
    o = o + jnp.concatenate(rows_out, axis=0)
    o_ref[...] = _hg_finish(o, gate_ref[...], ng_ref[...])


def hg_sample(p, s0, obuf, sprev, prm, layer, row0, n_seq, seq_len, r=128):
    nb = r // seq_len
    blk0 = row0 // r
    lt, last = _tile_consts(r, seq_len)
    consts = [lt, last, *_hg_consts(r, seq_len)]
    full = lambda a: pl.BlockSpec(a.shape, lambda i: (0,) * a.ndim)
    small = [prm["lb"], prm["norm"], *consts]
    col = lambda off: pl.BlockSpec((r, HG_WIDTH), lambda i: (blk0 + i, off // HG_WIDTH))
    sspec = pl.BlockSpec((None, nb, HG_HEADS, HG_KEY_DIM, HG_VAL_DIM), lambda i: (layer, i, 0, 0, 0))
    inplace = [obuf] if sprev is None else [obuf, sprev]
    n_in = 5 + len(small)
    return pl.pallas_call(
        functools.partial(_hg_sample_body, r=r, t=seq_len),
        grid=(n_seq // nb,),
        in_specs=[col(P_HQ), col(P_HF), col(P_HI), col(P_HGATE), sspec, *[full(a) for a in small],
                  *[pl.BlockSpec(memory_space=pl.ANY) for _ in inplace]],
        out_specs=[pl.BlockSpec((r, HG_WIDTH), lambda i: (blk0 + i, 0)), sspec],
        out_shape=[jax.ShapeDtypeStruct(obuf.shape, BF16), jax.ShapeDtypeStruct(s0.shape, F32)],
        input_output_aliases={n_in + k: k for k in range(len(inplace))},
        compiler_params=_cparams(("parallel",)),
        name="hg_sample",
    )(p, p, p, p, s0, *small, *inplace)


def _s5_params(a_re, a_im, log_dt, b_re, b_im, c_re, c_im, d_skip):
    dt = jnp.exp(log_dt)[:, None]
    mag = jnp.exp(a_re * dt)
    abar_re = mag * jnp.cos(a_im * dt)
    abar_im = mag * jnp.sin(a_im * dt)
    den = a_re * a_re + a_im * a_im
    nr = abar_re - 1.0
    coef_re = (nr * a_re + abar_im * a_im) / den
    coef_im = (abar_im * a_re - nr * a_im) / den
    bbar_re = coef_re[..., None] * b_re - coef_im[..., None] * b_im
    bbar_im = coef_re[..., None] * b_im + coef_im[..., None] * b_re
    gpb = S5_GROUPS // S5_BLK
    eye = jnp.eye(gpb, dtype=F32)

    def in_blocks(bbar):
        bb = bbar.reshape(S5_BLK, gpb, S5_STATE, S5_GROUP_SIZE)
        return jnp.einsum("qgnk,gh->qgkhn", bb, eye).reshape(
            S5_BLK, gpb * S5_GROUP_SIZE, gpb * S5_STATE).astype(BF16)

    def out_blocks(c):
        cc = c.reshape(S5_BLK, gpb, S5_GROUP_SIZE, S5_STATE)
        return jnp.einsum("qgkn,gh->qgnhk", cc, eye).reshape(
            S5_BLK, gpb * S5_STATE, gpb * S5_GROUP_SIZE).astype(BF16)

    return {
        "abar_re": abar_re.reshape(1, S5_CH), "abar_im": abar_im.reshape(1, S5_CH),
        "b_re": in_blocks(bbar_re), "b_im": in_blocks(bbar_im),
        "c_re": out_blocks(c_re), "c_im": out_blocks(c_im),
        "d_skip": d_skip.reshape(1, S5_WIDTH),
    }


def _in_proj_weights(w_in):
    z0, xbc0, dt0, u0, gates0 = 0, 2048, 5120, 5152, 10272
    w = jnp.concatenate([w_in[:, gates0:], w_in[:, xbc0:dt0], w_in[:, u0:gates0], w_in[:, z0:xbc0]], axis=1)
    wdt = jnp.pad(w_in[:, dt0:u0], ((0, 0), (0, LANES - SSD_HEADS)))
    return w.astype(BF16), wdt.astype(BF16)


def _pad_lanes(v):
    return jnp.pad(v.reshape(1, -1), ((0, 0), (0, LANES - v.shape[-1])))


def kernel(x_prompt, x_sample, cache_mem_k, cache_mem_v, state_ssd, state_ssd_conv, state_s5_re, state_s5_im,
           state_hgrn, mem_prompt, norm_ffn1, ffn1_w1, ffn1_w3, ffn1_w2, norm_mix, w_in, ssd_conv_w, ssd_conv_b,
           ssd_dt_bias, ssd_a_log, ssd_d, ssd_norm, ssd_w_out, s5_a_re, s5_a_im, s5_log_dt, s5_b_re, s5_b_im,
           s5_c_re, s5_c_im, s5_d, s5_w_glu_a, s5_w_glu_b, hg_lower_bounds, hg_norm, hg_w_out, w_mix_out,
           norm_xa, norm_mem, xa_wq, xa_wk, xa_wv, xa_wo, norm_ffn2, ffn2_w1, ffn2_w3, ffn2_w2, norm_final):
    bp, lp, d = x_prompt.shape
    bs, ls, _ = x_sample.shape
    mp, ms = bp * lp, bs * ls
    x = jnp.concatenate([x_prompt.reshape(mp, d), x_sample.reshape(ms, d)], axis=0)
    mem = mem_prompt.reshape(bp * MEM_LEN, d)
    row = lambda v: v.reshape(1, -1)
    bf = lambda w: w.astype(BF16)

    lb_p = jax.nn.softmax(hg_lower_bounds, axis=0)
    lb_all = jnp.cumsum(lb_p, axis=0) - lb_p[0]

    ssd_s0 = state_ssd.reshape(DEPTH, bs, SSD_WIDTH, SSD_STATE)
    mem_k = cache_mem_k.reshape(DEPTH, bs, MEM_LEN, d)
    mem_v = cache_mem_v.reshape(DEPTH, bs, MEM_LEN, d)
    ssd_states = None
    hg_states = None

    outs = {k: [] for k in ("pk", "pv", "pss", "pcv", "psr", "psi", "phg", "scv", "ssr", "ssi")}
    for l in range(DEPTH):
        x = ffn(x, row(norm_ffn1[l]), bf(ffn1_w1[l]), bf(ffn1_w3[l]), bf(ffn1_w2[l]), row(norm_final), False)

        w_p, w_dt = _in_proj_weights(w_in[l])
        p, dt = in_proj(x, row(norm_mix[l]), w_p, w_dt)

        ssd_prm = {"conv_w": ssd_conv_w[l], "conv_b": row(ssd_conv_b[l]), "dt_bias": _pad_lanes(ssd_dt_bias[l]),
                   "a_log": _pad_lanes(ssd_a_log[l]), "d_skip": row(jnp.repeat(ssd_d[l], SSD_HEAD_DIM)),
                   "norm": row(ssd_norm[l])}
        ys, ss_p = ssd_prompt(p, dt, ssd_prm, bp, lp)
        ys, ssd_states = ssd_sample(p, dt, state_ssd_conv, ssd_s0, ys, ssd_states, ssd_prm, l, mp, bs, ls)

        s5_prm = _s5_params(s5_a_re[l], s5_a_im[l], s5_log_dt[l], s5_b_re[l], s5_b_im[l], s5_c_re[l],
                            s5_c_im[l], s5_d[l])
        u_p = p[:mp, P_U:P_U + S5_WIDTH].reshape(bp, lp, S5_WIDTH).transpose(1, 0, 2)
        u_p = jnp.pad(u_p, ((0, 0), (0, S5_PSEQ - bp), (0, 0))).reshape(lp * S5_PSEQ, S5_WIDTH)
        u_s = p[mp:, P_U:P_U + S5_WIDTH].reshape(bs // S5_SSEQ, S5_SSEQ, ls, S5_WIDTH).transpose(0, 2, 1, 3)
        u_s = u_s.reshape(ms, S5_WIDTH)
        zeros = jnp.zeros((S5_PSEQ, S5_CH), F32)
        gy_p, sr_p, si_p = s5_scan(u_p, zeros, zeros, s5_prm, 1, S5_PSEQ, lp, 32)
        gy_s, sr_s, si_s = s5_scan(u_s, state_s5_re[l].reshape(bs, S5_CH), state_s5_im[l].reshape(bs, S5_CH),
                                   s5_prm, bs // S5_SSEQ, S5_SSEQ, ls, ls)
        gy_p = gy_p.reshape(lp, S5_PSEQ, S5_WIDTH)[:, :bp].transpose(1, 0, 2).reshape(mp, S5_WIDTH)
        gy_s = gy_s.reshape(bs // S5_SSEQ, ls, S5_SSEQ, S5_WIDTH).transpose(0, 2, 1, 3).reshape(ms, S5_WIDTH)
        gy = jnp.concatenate([gy_p, gy_s], axis=0)

        hg_prm = {"lb": row(lb_all[l]), "norm": row(hg_norm[l])}
        o, hg_p = hg_prompt(p, hg_prm, bp, lp)
        o, hg_states = hg_sample(p, state_hgrn, o, hg_states, hg_prm, l, mp, bs, ls)

        x = branch_mix(x, ys, gy, o, p, bf(ssd_w_out[l]),
                       bf(s5_w_glu_a[l]), bf(s5_w_glu_b[l]), bf(hg_w_out[l]), bf(w_mix_out[l]))

        q = norm_proj(x, row(norm_xa[l]), bf(xa_wq[l]), BF16)
        kv = norm_proj(mem, row(norm_mem[l]), bf(jnp.concatenate([xa_wk[l], xa_wv[l]], axis=1)), F32)
        at = xattn_prompt(q, kv, bp, lp)
        at = xattn_sample(q, mem_k, mem_v, at, l, mp, bs, ls)
        x = res_mm(x, at, bf(xa_wo[l]))

        x = ffn(x, row(norm_ffn2[l]), bf(ffn2_w1[l]), bf(ffn2_w3[l]), bf(ffn2_w2[l]), row(norm_final),
                l == DEPTH - 1)

        tail = SSD_CONV - 1
        outs["pk"].append(kv[:, :d].reshape(bp, MEM_LEN, XA_HEADS, XA_HEAD_DIM))
        outs["pv"].append(kv[:, d:].reshape(bp, MEM_LEN, XA_HEADS, XA_HEAD_DIM))
        outs["pss"].append(ss_p.reshape(bp, SSD_HEADS, SSD_HEAD_DIM, SSD_STATE))
        outs["pcv"].append(p[:mp].reshape(bp, lp, P_WIDTH)[:, lp - tail:, P_XBC:P_XBC + SSD_CONV_DIM])
        outs["psr"].append(sr_p[:bp].reshape(bp, S5_GROUPS, S5_STATE))
        outs["psi"].append(si_p[:bp].reshape(bp, S5_GROUPS, S5_STATE))
        outs["phg"].append(hg_p)
        outs["scv"].append(p[mp:].reshape(bs, ls, P_WIDTH)[:, ls - tail:, P_XBC:P_XBC + SSD_CONV_DIM])
        outs["ssr"].append(sr_s.reshape(bs, S5_GROUPS, S5_STATE))
        outs["ssi"].append(si_s.reshape(bs, S5_GROUPS, S5_STATE))

    st = lambda k: jnp.stack(outs[k])
    return (x[:mp].reshape(bp, lp, d), x[mp:].reshape(bs, ls, d),
            st("pk"), st("pv"), st("pss"), st("pcv"), st("psr"), st("psi"), st("phg"),
            ssd_states.reshape(DEPTH, bs, SSD_HEADS, SSD_HEAD_DIM, SSD_STATE), st("scv"), st("ssr"), st("ssi"),
            hg_states)
```

```python
import functools
import math

import jax
import jax.numpy as jnp
import numpy as np
from jax import lax
from jax.experimental import pallas as pl
from jax.experimental.pallas import tpu as pltpu

F32 = jnp.float32
BF16 = jnp.bfloat16

D_MODEL = 2048
DEPTH = 2
NORM_EPS = 1e-5
SSD_HEAD_DIM = 64
SSD_HEADS = 32
SSD_GROUPS = 4
SSD_STATE = 128
SSD_CONV = 4
SSD_WIDTH = 2048
SSD_CONV_DIM = 3072
S5_WIDTH = 1024
S5_GROUP_SIZE = 16
S5_GROUPS = 64
S5_STATE = 64
HG_WIDTH = 1024
HG_HEADS = 8
HG_KEY_DIM = 128
HG_VAL_DIM = 128
MEM_LEN = 256
XA_HEADS = 4
XA_HEAD_DIM = 512
FFN_DIM = 5632

P_GATES, P_XBC, P_U, P_HQ, P_HF, P_HI, P_HGATE, P_Z = 0, 6144, 9216, 10240, 11264, 12288, 13312, 14336
P_WIDTH = 16384

V7X_VMEM_LIMIT = 56 * 1024 * 1024


def _cparams(sem, vmem=V7X_VMEM_LIMIT):
    return pltpu.CompilerParams(dimension_semantics=sem, vmem_limit_bytes=vmem)


def _rms_scale(x):
    return lax.rsqrt(jnp.mean(x * x, axis=-1, keepdims=True) + NORM_EPS)


def _silu(x):
    return x * jax.nn.sigmoid(x)


def _dot(a, b):
    return jnp.dot(a, b, preferred_element_type=F32)


def _dot_nt(a, b):
    return lax.dot_general(a, b, (((1,), (1,)), ((), ())), preferred_element_type=F32)


def _dot_tn(a, b):
    return lax.dot_general(a, b, (((0,), (0,)), ((), ())), preferred_element_type=F32)


def _split2(x):
    hi = x.astype(BF16)
    lo = (x - hi.astype(F32)).astype(BF16)
    return hi, lo


def _split3(x):
    hi = x.astype(BF16)
    r = x - hi.astype(F32)
    mid = r.astype(BF16)
    lo = (r - mid.astype(F32)).astype(BF16)
    return hi, mid, lo


def _sel_dot(sel, x, parts=3):
    ps = _split3(x) if parts == 3 else _split2(x)
    out = _dot(sel, ps[0])
    for p in ps[1:]:
        out = out + _dot(sel, p)
    return out


def _dot_sel(x, sel, parts=2):
    ps = _split3(x) if parts == 3 else _split2(x)
    out = _dot(ps[0], sel)
    for p in ps[1:]:
        out = out + _dot(p, sel)
    return out


def _norm_proj_body(x_ref, g_ref, w_ref, o_ref, h_ref):
    @pl.when(pl.program_id(1) == 0)
    def _():
        x = x_ref[...]
        h_ref[...] = (x * _rms_scale(x) * g_ref[...]).astype(BF16)

    o_ref[...] = _dot(h_ref[...], w_ref[...]).astype(o_ref.dtype)


def norm_proj(x, g, w, out_dtype, bm=1024, bn=1024):
    m, k = x.shape
    n = w.shape[1]
    return pl.pallas_call(
        _norm_proj_body,
        grid=(m // bm, n // bn),
        in_specs=[pl.BlockSpec((bm, k), lambda i, j: (i, 0)),
                  pl.BlockSpec((1, k), lambda i, j: (0, 0)),
                  pl.BlockSpec((k, bn), lambda i, j: (0, j))],
        out_specs=pl.BlockSpec((bm, bn), lambda i, j: (i, j)),
        out_shape=jax.ShapeDtypeStruct((m, n), out_dtype),
        scratch_shapes=[pltpu.VMEM((bm, k), BF16)],
        compiler_params=_cparams(("parallel", "arbitrary")),
        name="norm_proj",
    )(x, g, w)


def _in_proj_body(x_ref, g_ref, w_ref, wdt_ref, o_ref, odt_ref, h_ref):
    @pl.when(pl.program_id(1) == 0)
    def _():
        x = x_ref[...]
        h = (x * _rms_scale(x) * g_ref[...]).astype(BF16)
        h_ref[...] = h
        odt_ref[...] = _dot(h, wdt_ref[...])

    o_ref[...] = _dot(h_ref[...], w_ref[...])


def in_proj(x, g, w, wdt, bm=1024, bn=1024):
    m, k = x.shape
    n = w.shape[1]
    ndt = wdt.shape[1]
    return pl.pallas_call(
        _in_proj_body,
        grid=(m // bm, n // bn),
        in_specs=[pl.BlockSpec((bm, k), lambda i, j: (i, 0)),
                  pl.BlockSpec((1, k), lambda i, j: (0, 0)),
                  pl.BlockSpec((k, bn), lambda i, j: (0, j)),
                  pl.BlockSpec((k, ndt), lambda i, j: (0, 0))],
        out_specs=[pl.BlockSpec((bm, bn), lambda i, j: (i, j)),
                   pl.BlockSpec((bm, ndt), lambda i, j: (i, 0))],
        out_shape=[jax.ShapeDtypeStruct((m, n), F32), jax.ShapeDtypeStruct((m, ndt), F32)],
        scratch_shapes=[pltpu.VMEM((bm, k), BF16)],
        compiler_params=_cparams(("parallel", "arbitrary")),
        name="in_proj",
    )(x, g, w, wdt)


def _res_mm_body(x_ref, a_ref, w_ref, o_ref):
    o_ref[...] = x_ref[...] + _dot(a_ref[...], w_ref[...])


def res_mm(x, a, w, bm=1024, bn=1024):
    m, n = x.shape
    k = a.shape[1]
    return pl.pallas_call(
        _res_mm_body,
        grid=(m // bm, n // bn),
        in_specs=[pl.BlockSpec((bm, bn), lambda i, j: (i, j)),
                  pl.BlockSpec((bm, k), lambda i, j: (i, 0)),
                  pl.BlockSpec((k, bn), lambda i, j: (0, j))],
        out_specs=pl.BlockSpec((bm, bn), lambda i, j: (i, j)),
        out_shape=jax.ShapeDtypeStruct((m, n), F32),
        compiler_params=_cparams(("parallel", "parallel")),
        name="res_mm",
    )(x, a, w)


def _ffn_body(x_ref, g_ref, w1_ref, w3_ref, w2_ref, gf_ref, o_ref, h_ref, acc_ref, *, final_norm):
    f = pl.program_id(1)

    @pl.when(f == 0)
    def _():
        x = x_ref[...]
        h_ref[...] = (x * _rms_scale(x) * g_ref[...]).astype(BF16)
        acc_ref[...] = jnp.zeros_like(acc_ref)

    h = h_ref[...]
    a = _silu(_dot(h, w1_ref[...])) * _dot(h, w3_ref[...])
    acc_ref[...] += _dot(a.astype(BF16), w2_ref[...])

    @pl.when(f == pl.num_programs(1) - 1)
    def _():
        y = x_ref[...] + 0.5 * acc_ref[...]
        if final_norm:
            y = y * _rms_scale(y) * gf_ref[...]
        o_ref[...] = y


def ffn(x, g, w1, w3, w2, gf, final_norm, bm=512, bf=512):
    m, d = x.shape
    fdim = w1.shape[1]
    return pl.pallas_call(
        functools.partial(_ffn_body, final_norm=final_norm),
        grid=(m // bm, fdim // bf),
        in_specs=[pl.BlockSpec((bm, d), lambda i, f: (i, 0)),
                  pl.BlockSpec((1, d), lambda i, f: (0, 0)),
                  pl.BlockSpec((d, bf), lambda i, f: (0, f)),
                  pl.BlockSpec((d, bf), lambda i, f: (0, f)),
                  pl.BlockSpec((bf, d), lambda i, f: (f, 0)),
                  pl.BlockSpec((1, d), lambda i, f: (0, 0))],
        out_specs=pl.BlockSpec((bm, d), lambda i, f: (i, 0)),
        out_shape=jax.ShapeDtypeStruct((m, d), F32),
        scratch_shapes=[pltpu.VMEM((bm, d), BF16), pltpu.VMEM((bm, d), F32)],
        compiler_params=_cparams(("parallel", "arbitrary")),
        name="ffn",
    )(x, g, w1, w3, w2, gf)


def _mix_body(x_ref, ys_ref, gy_ref, o_ref, ga_ref, gb_ref, gc_ref, wssd_ref, wa_ref, wb_ref, whg_ref,
              wmix_ref, out_ref, acc_ref):
    j = pl.program_id(1)

    @pl.when(j == 0)
    def _():
        acc_ref[...] = jnp.zeros_like(acc_ref)

    gy = gy_ref[...]
    y_a = _dot(ys_ref[...], wssd_ref[...])
    y_b = _dot(gy, wa_ref[...]) * jax.nn.sigmoid(_dot(gy, wb_ref[...]))
    y_c = _dot(o_ref[...], whg_ref[...])
    mix = (jax.nn.sigmoid(ga_ref[...]) * y_a + jax.nn.sigmoid(gb_ref[...]) * y_b
           + jax.nn.sigmoid(gc_ref[...]) * y_c)
    acc_ref[...] += _dot(mix.astype(BF16), wmix_ref[...])

    @pl.when(j == pl.num_programs(1) - 1)
    def _():
        out_ref[...] = x_ref[...] + acc_ref[...]


def branch_mix(x, ys, gy, o, p, wssd, wa, wb, whg, wmix, bm=512, bn=512):
    m, d = x.shape
    nj = d // bn
    return pl.pallas_call(
        _mix_body,
        grid=(m // bm, nj),
        in_specs=[pl.BlockSpec((bm, d), lambda i, j: (i, 0)),
                  pl.BlockSpec((bm, ys.shape[1]), lambda i, j: (i, 0)),
                  pl.BlockSpec((bm, gy.shape[1]), lambda i, j: (i, 0)),
                  pl.BlockSpec((bm, o.shape[1]), lambda i, j: (i, 0)),
                  pl.BlockSpec((bm, bn), lambda i, j: (i, j)),
                  pl.BlockSpec((bm, bn), lambda i, j: (i, nj + j)),
                  pl.BlockSpec((bm, bn), lambda i, j: (i, 2 * nj + j)),
                  pl.BlockSpec((wssd.shape[0], bn), lambda i, j: (0, j)),
                  pl.BlockSpec((wa.shape[0], bn), lambda i, j: (0, j)),
                  pl.BlockSpec((wb.shape[0], bn), lambda i, j: (0, j)),
                  pl.BlockSpec((whg.shape[0], bn), lambda i, j: (0, j)),
                  pl.BlockSpec((bn, d), lambda i, j: (j, 0))],
        out_specs=pl.BlockSpec((bm, d), lambda i, j: (i, 0)),
        out_shape=jax.ShapeDtypeStruct((m, d), F32),
        scratch_shapes=[pltpu.VMEM((bm, d), F32)],
        compiler_params=_cparams(("parallel", "arbitrary")),
        name="branch_mix",
    )(x, ys, gy, o, p, p, p, wssd, wa, wb, whg, wmix)


def _attend(q, k_head, v_head):
    outs = []
    for h in range(XA_HEADS):
        sl = slice(h * XA_HEAD_DIM, (h + 1) * XA_HEAD_DIM)
        s = _dot_nt(q[:, sl], k_head(h).astype(BF16)) * (XA_HEAD_DIM ** -0.5)
        s = s - jnp.max(s, axis=-1, keepdims=True)
        e = jnp.exp(s)
        p = e / jnp.sum(e, axis=-1, keepdims=True)
        outs.append(_dot(p.astype(BF16), v_head(h).astype(BF16)))
    return jnp.concatenate(outs, axis=-1)


def _xattn_prompt_body(q_ref, kv_ref, o_ref):
    k_head = lambda h: kv_ref[:, h * XA_HEAD_DIM:(h + 1) * XA_HEAD_DIM]
    v_head = lambda h: kv_ref[:, D_MODEL + h * XA_HEAD_DIM:D_MODEL + (h + 1) * XA_HEAD_DIM]
    o_ref[...] = _attend(q_ref[...], k_head, v_head).astype(BF16)


def xattn_prompt(q, kv, n_seq, seq_len, bl=512):
    nl = seq_len // bl
    return pl.pallas_call(
        _xattn_prompt_body,
        grid=(n_seq, nl),
        in_specs=[pl.BlockSpec((bl, D_MODEL), lambda b, i: (b * nl + i, 0)),
                  pl.BlockSpec((MEM_LEN, 2 * D_MODEL), lambda b, i: (b, 0))],
        out_specs=pl.BlockSpec((bl, D_MODEL), lambda b, i: (b * nl + i, 0)),
        out_shape=jax.ShapeDtypeStruct(q.shape, BF16),
        compiler_params=_cparams(("parallel", "arbitrary")),
        name="xattn_prompt",
    )(q, kv)


def _xattn_sample_body(q_ref, k_ref, v_ref, buf_ref, o_ref, *, seq_len):
    del buf_ref
    q = q_ref[...]
    rows = lax.broadcasted_iota(jnp.int32, (2 * seq_len, D_MODEL), 0)
    head = lambda ref, b: (lambda h: ref[b, :, h * XA_HEAD_DIM:(h + 1) * XA_HEAD_DIM])
    o0 = _attend(q, head(k_ref, 0), head(v_ref, 0))
    o1 = _attend(q, head(k_ref, 1), head(v_ref, 1))
    o_ref[...] = jnp.where(rows < seq_len, o0, o1).astype(BF16)


def xattn_sample(q, k, v, buf, layer, row0, n_seq, seq_len):
    r = 2 * seq_len
    blk0 = row0 // r
    mem_spec = pl.BlockSpec((None, 2, MEM_LEN, D_MODEL), lambda i: (layer, i, 0, 0))
    return pl.pallas_call(
        functools.partial(_xattn_sample_body, seq_len=seq_len),
        grid=(n_seq // 2,),
        in_specs=[pl.BlockSpec((r, D_MODEL), lambda i: (blk0 + i, 0)), mem_spec, mem_spec,
                  pl.BlockSpec(memory_space=pl.ANY)],
        out_specs=pl.BlockSpec((r, D_MODEL), lambda i: (blk0 + i, 0)),
        out_shape=jax.ShapeDtypeStruct(buf.shape, BF16),
        input_output_aliases={3: 0},
        compiler_params=_cparams(("parallel",)),
        name="xattn_sample",
    )(q, k, v, buf)


LANES = 128
HALO = 8


def _softplus(x):
    return jnp.maximum(x, 0.0) + jnp.log1p(jnp.exp(-jnp.abs(x)))


def _pad_rows(x, rows):
    if x.shape[0] == rows:
        return x
    return jnp.concatenate([x, jnp.zeros((rows - x.shape[0], x.shape[1]), x.dtype)], axis=0)


def _tile_consts(r, t):
    i = np.arange(r)[:, None]
    j = np.arange(r)[None, :]
    same = (i // t) == (j // t)
    lt = (same & (j <= i)).astype(np.float32)
    last = (same & (j % t == t - 1)).astype(np.float32)
    return jnp.asarray(lt, BF16), jnp.asarray(last, BF16)


def _head_expand(n_heads, width, rows=LANES):
    e = np.zeros((rows, n_heads * width), np.float32)
    for h in range(n_heads):
        e[h, h * width:(h + 1) * width] = 1.0
    return jnp.asarray(e, BF16)


def _ssd_tile(xc, dt_raw, dtb, a_log, lt, last, e, e128, r, t):
    xs = xc[:, :SSD_WIDTH]
    dt = _softplus(dt_raw + dtb)
    d_a = dt * (-jnp.exp(a_log))
    a = _sel_dot(lt, d_a, 3)
    a_e = _dot_sel(a, e, 3)
    dt_e = _dot_sel(dt, e, 2)
    alast_e = _sel_dot(last, a_e, 3)
    a_col = _dot_sel(a, e128, 3)
    a_t = _pad_rows(a, LANES).T

    row = lax.broadcasted_iota(jnp.int32, (r, LANES), 0)
    col = lax.broadcasted_iota(jnp.int32, (r, LANES), 1)
    valid = (col <= row) & (col >= (row // t) * t)
    lane_lo = col < SSD_HEAD_DIM

    xdt = _pad_rows((xs * dt_e).astype(BF16), LANES)
    hpg = SSD_HEADS // SSD_GROUPS
    ys = []
    for g in range(SSD_GROUPS):
        bg = xc[:, SSD_WIDTH + g * SSD_STATE:SSD_WIDTH + (g + 1) * SSD_STATE].astype(BF16)
        cg = xc[:, SSD_WIDTH + (SSD_GROUPS + g) * SSD_STATE:
                SSD_WIDTH + (SSD_GROUPS + g + 1) * SSD_STATE].astype(BF16)
        cb = _dot_nt(cg, _pad_rows(bg, LANES))
        for hp in range(hpg // 2):
            h0 = g * hpg + 2 * hp
            res = []
            for h in (h0, h0 + 1):
                rel = a_col[:, h * LANES:(h + 1) * LANES] - a_t[h:h + 1, :]
                dec = jnp.where(valid, jnp.exp(jnp.where(valid, rel, 0.0)), 0.0)
                res.append(_dot((cb * dec).astype(BF16), xdt[:, h0 * SSD_HEAD_DIM:(h0 + 2) * SSD_HEAD_DIM]))
            ys.append(jnp.where(lane_lo, res[0], res[1]))
    y_intra = jnp.concatenate(ys, axis=-1)
    return xs, y_intra, a_e, dt_e, alast_e


def _ssd_finish(y, xs, z, dsk, ng):
    y = y + dsk * xs
    y = y * _silu(z)
    return (y * _rms_scale(y) * ng).astype(BF16)


def _conv_silu(ext_ref, cw_ref, cb_ref, base, r):
    acc = cb_ref[...] + cw_ref[SSD_CONV - 1:SSD_CONV, :] * ext_ref[pl.ds(base, r), :]
    for k in range(1, SSD_CONV):
        acc = acc + cw_ref[SSD_CONV - 1 - k:SSD_CONV - k, :] * ext_ref[pl.ds(base - k, r), :]
    return _silu(acc)


def _ssd_prompt_body(xbc_ref, z_ref, dt_ref, cw_ref, cb_ref, dtb_ref, alog_ref, dsk_ref, ng_ref,
                     lt_ref, last_ref, e_ref, e128_ref, y_ref, sout_ref, ext_ref, st_ref, *, r):
    c = pl.program_id(1)

    @pl.when(c == 0)
    def _():
        ext_ref[0:HALO, :] = jnp.zeros((HALO, SSD_CONV_DIM), F32)
        st_ref[...] = jnp.zeros_like(st_ref)

    ext_ref[HALO:HALO + r, :] = xbc_ref[...]
    xc = _conv_silu(ext_ref, cw_ref, cb_ref, HALO, r)
    ext_ref[0:HALO, :] = xbc_ref[r - HALO:r, :]

    xs, y, a_e, dt_e, alast_e = _ssd_tile(xc, dt_ref[...], dtb_ref[...], alog_ref[...], lt_ref[...],
                                          last_ref[...], e_ref[...], e128_ref[...], r, r)
    ea_e = jnp.exp(a_e)
    xw = (xs * (dt_e * jnp.exp(alast_e - a_e))).astype(BF16)
    sdec = jnp.exp(alast_e[0:1, :])
    gw = SSD_WIDTH // SSD_GROUPS
    inter = []
    for g in range(SSD_GROUPS):
        bg = xc[:, SSD_WIDTH + g * SSD_STATE:SSD_WIDTH + (g + 1) * SSD_STATE].astype(BF16)
        cg = xc[:, SSD_WIDTH + (SSD_GROUPS + g) * SSD_STATE:
                SSD_WIDTH + (SSD_GROUPS + g + 1) * SSD_STATE].astype(BF16)
        st = st_ref[:, g * gw:(g + 1) * gw]
        inter.append(_dot(cg, st.astype(BF16)))
        st_ref[:, g * gw:(g + 1) * gw] = st * sdec[:, g * gw:(g + 1) * gw] + _dot_tn(bg, xw[:, g * gw:(g + 1) * gw])
    y = y + jnp.concatenate(inter, axis=-1) * ea_e
    y_ref[...] = _ssd_finish(y, xs, z_ref[...], dsk_ref[...], ng_ref[...])

    @pl.when(c == pl.num_programs(1) - 1)
    def _():
        sout_ref[0] = st_ref[...].T


def _ssd_consts(r, t):
    lt, last = _tile_consts(r, t)
    return lt, last, _head_expand(SSD_HEADS, SSD_HEAD_DIM), _head_expand(SSD_HEADS, LANES)


def ssd_prompt(p, dt, prm, n_seq, seq_len, r=128):
    nc = seq_len // r
    consts = _ssd_consts(r, r)
    full = lambda a: pl.BlockSpec(a.shape, lambda b, c: (0,) * a.ndim)
    small = [prm["conv_w"], prm["conv_b"], prm["dt_bias"], prm["a_log"], prm["d_skip"], prm["norm"], *consts]
    return pl.pallas_call(
        functools.partial(_ssd_prompt_body, r=r),
        grid=(n_seq, nc),
        in_specs=[pl.BlockSpec((r, SSD_CONV_DIM), lambda b, c: (b * nc + c, P_XBC // SSD_CONV_DIM)),
                  pl.BlockSpec((r, SSD_WIDTH), lambda b, c: (b * nc + c, P_Z // SSD_WIDTH)),
                  pl.BlockSpec((r, LANES), lambda b, c: (b * nc + c, 0)),
                  *[full(a) for a in small]],
        out_specs=[pl.BlockSpec((r, SSD_WIDTH), lambda b, c: (b * nc + c, 0)),
                   pl.BlockSpec((1, SSD_WIDTH, SSD_STATE), lambda b, c: (b, 0, 0))],
        out_shape=[jax.ShapeDtypeStruct((p.shape[0], SSD_WIDTH), BF16),
                   jax.ShapeDtypeStruct((n_seq, SSD_WIDTH, SSD_STATE), F32)],
        scratch_shapes=[pltpu.VMEM((HALO + r, SSD_CONV_DIM), F32), pltpu.VMEM((SSD_STATE, SSD_WIDTH), F32)],
        compiler_params=_cparams(("parallel", "arbitrary")),
        name="ssd_prompt",
    )(p, p, dt, *small)


def _ssd_sample_body(xbc_ref, z_ref, dt_ref, buf_ref, s0_ref, cw_ref, cb_ref, dtb_ref, alog_ref, dsk_ref,
                     ng_ref, lt_ref, last_ref, e_ref, e128_ref, *rest, r, t):
    y_ref, sout_ref, ext_ref = rest[-3:]
    nb = r // t
    pitch = HALO + t
    for b in range(nb):
        ext_ref[b * pitch + HALO - (SSD_CONV - 1):b * pitch + HALO, :] = buf_ref[b]
        ext_ref[b * pitch + HALO:(b + 1) * pitch, :] = xbc_ref[b * t:(b + 1) * t, :]
    xc = jnp.concatenate([_conv_silu(ext_ref, cw_ref, cb_ref, b * pitch + HALO, t) for b in range(nb)], axis=0)

    xs, y, a_e, dt_e, alast_e = _ssd_tile(xc, dt_ref[...], dtb_ref[...], alog_ref[...], lt_ref[...],
                                          last_ref[...], e_ref[...], e128_ref[...], r, t)
    ea_e = jnp.exp(a_e)
    xw = xs * (dt_e * jnp.exp(alast_e - a_e))
    sdec = jnp.exp(alast_e)
    gw = SSD_WIDTH // SSD_GROUPS
    pr = 2 * t
    prow = lax.broadcasted_iota(jnp.int32, (pr, 1), 0)
    ones = jnp.ones((pr, SSD_STATE), BF16)
    inter = []
    for g in range(SSD_GROUPS):
        bg = xc[:, SSD_WIDTH + g * SSD_STATE:SSD_WIDTH + (g + 1) * SSD_STATE].astype(BF16)
        cg = xc[:, SSD_WIDTH + (SSD_GROUPS + g) * SSD_STATE:
                SSD_WIDTH + (SSD_GROUPS + g + 1) * SSD_STATE].astype(BF16)
        cols = slice(g * gw, (g + 1) * gw)
        rows_out = []
        for q in range(nb // 2):
            rs = slice(q * pr, (q + 1) * pr)
            acc = None
            for s in range(2):
                b = 2 * q + s
                mine = (prow >= s * t) & (prow < (s + 1) * t)
                s0 = s0_ref[b, cols, :]
                yi = _dot_nt(cg[rs], s0.astype(BF16))
                acc = jnp.where(mine, yi, 0.0) if acc is None else acc + jnp.where(mine, yi, 0.0)
                upd = _dot_tn(jnp.where(mine, xw[rs, cols], 0.0).astype(BF16), bg[rs])
                lastrow = prow == (s + 1) * t - 1
                dh, dl = _split2(jnp.where(lastrow, sdec[rs, cols], 0.0))
                dcol = _dot_tn(dh, ones) + _dot_tn(dl, ones)
                sout_ref[b, cols, :] = s0 * dcol + upd
            rows_out.append(acc)
        inter.append(jnp.concatenate(rows_out, axis=0))
    y = y + jnp.concatenate(inter, axis=-1) * ea_e
    y_ref[...] = _ssd_finish(y, xs, z_ref[...], dsk_ref[...], ng_ref[...])


def ssd_sample(p, dt, conv_buf, s0, ybuf, sprev, prm, layer, row0, n_seq, seq_len, r=64):
    nb = r // seq_len
    blk0 = row0 // r
    consts = _ssd_consts(r, seq_len)
    full = lambda a: pl.BlockSpec(a.shape, lambda i: (0,) * a.ndim)
    small = [prm["conv_w"], prm["conv_b"], prm["dt_bias"], prm["a_log"], prm["d_skip"], prm["norm"], *consts]
    inplace = [ybuf] if sprev is None else [ybuf, sprev]
    n_in = 5 + len(small)
    return pl.pallas_call(
        functools.partial(_ssd_sample_body, r=r, t=seq_len),
        grid=(n_seq // nb,),
        in_specs=[pl.BlockSpec((r, SSD_CONV_DIM), lambda i: (blk0 + i, P_XBC // SSD_CONV_DIM)),
                  pl.BlockSpec((r, SSD_WIDTH), lambda i: (blk0 + i, P_Z // SSD_WIDTH)),
                  pl.BlockSpec((r, LANES), lambda i: (blk0 + i, 0)),
                  pl.BlockSpec((None, nb, SSD_CONV - 1, SSD_CONV_DIM), lambda i: (layer, i, 0, 0)),
                  pl.BlockSpec((None, nb, SSD_WIDTH, SSD_STATE), lambda i: (layer, i, 0, 0)),
                  *[full(a) for a in small],
                  *[pl.BlockSpec(memory_space=pl.ANY) for _ in inplace]],
        out_specs=[pl.BlockSpec((r, SSD_WIDTH), lambda i: (blk0 + i, 0)),
                   pl.BlockSpec((None, nb, SSD_WIDTH, SSD_STATE), lambda i: (layer, i, 0, 0))],
        out_shape=[jax.ShapeDtypeStruct(ybuf.shape, BF16),
                   jax.ShapeDtypeStruct(s0.shape, F32)],
        input_output_aliases={n_in + k: k for k in range(len(inplace))},
        scratch_shapes=[pltpu.VMEM((nb * (HALO + seq_len), SSD_CONV_DIM), F32)],
        compiler_params=_cparams(("parallel",)),
        name="ssd_sample",
    )(p, p, dt, conv_buf, s0, *small, *inplace)


S5_CH = S5_GROUPS * S5_STATE
S5_BLK = 4
S5_PSEQ = 8
S5_SSEQ = 32


def _s5_in(u, bre_ref, bim_ref):
    ub = u.astype(BF16)
    kin = S5_WIDTH // S5_BLK
    re = [_dot(ub[:, q * kin:(q + 1) * kin], bre_ref[q]) for q in range(S5_BLK)]
    im = [_dot(ub[:, q * kin:(q + 1) * kin], bim_ref[q]) for q in range(S5_BLK)]
    return jnp.concatenate(re, axis=-1), jnp.concatenate(im, axis=-1)


def _s5_out(h_re, h_im, u, cre_ref, cim_ref, dsk):
    kst = S5_CH // S5_BLK
    hr = h_re.astype(BF16)
    hi = h_im.astype(BF16)
    y = [_dot(hr[:, q * kst:(q + 1) * kst], cre_ref[q]) - _dot(hi[:, q * kst:(q + 1) * kst], cim_ref[q])
         for q in range(S5_BLK)]
    y = jnp.concatenate(y, axis=-1) + dsk * u
    return jax.nn.gelu(y).astype(BF16)


def _s5_scan_body(u_ref, h0re_ref, h0im_ref, are_ref, aim_ref, bre_ref, bim_ref, cre_ref, cim_ref, dsk_ref,
                  y_ref, hre_ref, him_ref, sre_ref, sim_ref, *, s, tc):
    c = pl.program_id(1)

    @pl.when(c == 0)
    def _():
        sre_ref[0:s, :] = h0re_ref[...]
        sim_ref[0:s, :] = h0im_ref[...]

    u = u_ref[...]
    bu_re, bu_im = _s5_in(u, bre_ref, bim_ref)
    sre_ref[s:, :] = bu_re
    sim_ref[s:, :] = bu_im
    a_re = jnp.broadcast_to(are_ref[...], (s, S5_CH))
    a_im = jnp.broadcast_to(aim_ref[...], (s, S5_CH))

    def step(t, carry):
        prev = pl.ds(pl.multiple_of(t * s, s), s)
        cur = pl.ds(pl.multiple_of((t + 1) * s, s), s)
        h_re = sre_ref[prev, :]
        h_im = sim_ref[prev, :]
        sre_ref[cur, :] = a_re * h_re - a_im * h_im + sre_ref[cur, :]
        sim_ref[cur, :] = a_re * h_im + a_im * h_re + sim_ref[cur, :]
        return carry

    lax.fori_loop(0, tc, step, 0, unroll=4)
    h_re = sre_ref[tc * s:, :]
    h_im = sim_ref[tc * s:, :]
    y_ref[...] = _s5_out(sre_ref[s:, :], sim_ref[s:, :], u, cre_ref, cim_ref, dsk_ref[...])
    sre_ref[0:s, :] = h_re
    sim_ref[0:s, :] = h_im
    hre_ref[...] = h_re
    him_ref[...] = h_im


def s5_scan(u, h0_re, h0_im, prm, n_blocks, s, n_steps, tc):
    nc = n_steps // tc
    r = tc * s
    full = lambda a: pl.BlockSpec(a.shape, lambda b, c: (0,) * a.ndim)
    small = [prm["abar_re"], prm["abar_im"], prm["b_re"], prm["b_im"], prm["c_re"], prm["c_im"], prm["d_skip"]]
    state = jax.ShapeDtypeStruct((n_blocks * s, S5_CH), F32)
    sspec = pl.BlockSpec((s, S5_CH), lambda b, c: (b, 0))
    return pl.pallas_call(
        functools.partial(_s5_scan_body, s=s, tc=tc),
        grid=(n_blocks, nc),
        in_specs=[pl.BlockSpec((r, S5_WIDTH), lambda b, c: (b * nc + c, 0)), sspec, sspec,
                  *[full(a) for a in small]],
        out_specs=[pl.BlockSpec((r, S5_WIDTH), lambda b, c: (b * nc + c, 0)), sspec, sspec],
        out_shape=[jax.ShapeDtypeStruct((u.shape[0], S5_WIDTH), BF16), state, state],
        scratch_shapes=[pltpu.VMEM((s + r, S5_CH), F32), pltpu.VMEM((s + r, S5_CH), F32)],
        compiler_params=_cparams(("parallel", "arbitrary")),
        name="s5_scan",
    )(u, h0_re, h0_im, *small)


def _hg_consts(r, t):
    i = np.arange(r)[:, None]
    j = np.arange(r)[None, :]
    sums, upper, pair = [], [], []
    s = 1
    while s < t:
        blk_i, blk_j = i // (2 * s), j // (2 * s)
        up_i = (i % (2 * s)) >= s
        mid_i = blk_i * 2 * s + s
        m_up = up_i & (j >= mid_i) & (j <= i)
        m_lo = (~up_i) & (j > i) & (j < mid_i)
        sums.append((m_up | m_lo).astype(np.float32))
        upper.append(np.broadcast_to(up_i, (r, 1)).astype(np.float32))
        pair.append(((blk_i == blk_j) & up_i & ((j % (2 * s)) < s)).astype(np.float32))
        s *= 2
    pair.append((i == j).astype(np.float32))
    return (jnp.asarray(np.stack(sums), BF16), jnp.asarray(np.stack(upper), F32),
            jnp.asarray(np.stack(pair), F32))


def _hg_gates(hf, lb):
    logf = -_softplus(-hf) + jnp.log1p(lb * jnp.exp(-hf))
    kk = (1.0 - lb) * jax.nn.sigmoid(-hf)
    return logf, kk


def _hg_intra(q, kk, v, logf, sums_ref, upper_ref, pair_ref):
    r = q.shape[0]
    nlev = sums_ref.shape[0]
    lf3 = _split3(logf)
    qb = q.astype(BF16)
    kb = kk.astype(BF16)
    vb = _pad_rows(v.astype(BF16), LANES) if r < LANES else v.astype(BF16)
    scores = [None] * HG_HEADS
    for lev in range(nlev + 1):
        if lev < nlev:
            m = sums_ref[lev]
            d = _dot(m, lf3[0]) + _dot(m, lf3[1]) + _dot(m, lf3[2])
            x = (jnp.where(upper_ref[lev] > 0.5, q, kk) * jnp.exp(d)).astype(BF16)
            xq, xk = x, x
        else:
            xq, xk = qb, kb
        mask = pair_ref[lev]
        for h in range(HG_HEADS):
            sl = slice(h * HG_KEY_DIM, (h + 1) * HG_KEY_DIM)
            sc = _dot_nt(xq[:, sl], xk[:, sl]) * mask
            scores[h] = sc if scores[h] is None else scores[h] + sc
    outs = [_dot(scores[h].astype(BF16), vb[:, h * HG_VAL_DIM:(h + 1) * HG_VAL_DIM]) for h in range(HG_HEADS)]
    return jnp.concatenate(outs, axis=-1)


def _hg_finish(o, hgate, ng):
    outs = []
    for h in range(HG_HEADS):
        oh = o[:, h * HG_VAL_DIM:(h + 1) * HG_VAL_DIM]
        outs.append(oh * _rms_scale(oh) * ng)
    return (jnp.concatenate(outs, axis=-1) * _silu(hgate)).astype(BF16)


def _hg_prompt_body(q_ref, f_ref, i_ref, gate_ref, lb_ref, ng_ref, lt_ref, last_ref, sums_ref, upper_ref,
                    pair_ref, o_ref, sout_ref, *, r):
    c = pl.program_id(1)

    @pl.when(c == 0)
    def _():
        sout_ref[...] = jnp.zeros_like(sout_ref)

    q = q_ref[...]
    v = i_ref[...]
    logf, kk = _hg_gates(f_ref[...], lb_ref[...])
    o = _hg_intra(q, kk, v, logf, sums_ref, upper_ref, pair_ref)
    b = _sel_dot(lt_ref[...], logf, 3)
    blast = b[r - 1:r, :]
    qe = (q * jnp.exp(b)).astype(BF16)
    kw = (kk * jnp.exp(blast - b)).astype(BF16)
    vb = v.astype(BF16)
    ones = jnp.ones((r, HG_VAL_DIM), BF16)
    rows = lax.broadcasted_iota(jnp.int32, (r, 1), 0)
    dh, dl = _split2(jnp.where(rows == r - 1, jnp.exp(b), 0.0))
    inter = []
    for h in range(HG_HEADS):
        sl = slice(h * HG_KEY_DIM, (h + 1) * HG_KEY_DIM)
        s = sout_ref[0, h]
        inter.append(_dot(qe[:, sl], s.astype(BF16)))
        dcol = _dot_tn(dh[:, sl], ones) + _dot_tn(dl[:, sl], ones)
        sout_ref[0, h] = s * dcol + _dot_tn(kw[:, sl], vb[:, sl])
    o = o + jnp.concatenate(inter, axis=-1)
    o_ref[...] = _hg_finish(o, gate_ref[...], ng_ref[...])


def hg_prompt(p, prm, n_seq, seq_len, r=128):
    nc = seq_len // r
    lt, last = _tile_consts(r, r)
    consts = [lt, last, *_hg_consts(r, r)]
    full = lambda a: pl.BlockSpec(a.shape, lambda b, c: (0,) * a.ndim)
    small = [prm["lb"], prm["norm"], *consts]
    col = lambda off: pl.BlockSpec((r, HG_WIDTH), lambda b, c: (b * nc + c, off // HG_WIDTH))
    return pl.pallas_call(
        functools.partial(_hg_prompt_body, r=r),
        grid=(n_seq, nc),
        in_specs=[col(P_HQ), col(P_HF), col(P_HI), col(P_HGATE), *[full(a) for a in small]],
        out_specs=[pl.BlockSpec((r, HG_WIDTH), lambda b, c: (b * nc + c, 0)),
                   pl.BlockSpec((1, HG_HEADS, HG_KEY_DIM, HG_VAL_DIM), lambda b, c: (b, 0, 0, 0))],
        out_shape=[jax.ShapeDtypeStruct((p.shape[0], HG_WIDTH), BF16),
                   jax.ShapeDtypeStruct((n_seq, HG_HEADS, HG_KEY_DIM, HG_VAL_DIM), F32)],
        compiler_params=_cparams(("parallel", "arbitrary")),
        name="hg_prompt",
    )(p, p, p, p, *small)


def _hg_sample_body(q_ref, f_ref, i_ref, gate_ref, s0_ref, lb_ref, ng_ref, lt_ref, last_ref, sums_ref,
                    upper_ref, pair_ref, *rest, r, t):
    o_ref, sout_ref = rest[-2:]
    q = q_ref[...]
    v = i_ref[...]
    logf, kk = _hg_gates(f_ref[...], lb_ref[...])
    o = _hg_intra(q, kk, v, logf, sums_ref, upper_ref, pair_ref)
    b = _sel_dot(lt_ref[...], logf, 3)
    blast = _sel_dot(last_ref[...], b, 3)
    qe = (q * jnp.exp(b)).astype(BF16)
    kw = kk * jnp.exp(blast - b)
    sdec = jnp.exp(blast)
    vb = v.astype(BF16)
    pr = 2 * t
    prow = lax.broadcasted_iota(jnp.int32, (pr, 1), 0)
    ones = jnp.ones((pr, HG_VAL_DIM), BF16)
    rows_out = []
    for p2 in range(r // pr):
        rs = slice(p2 * pr, (p2 + 1) * pr)
        heads = []
        for h in range(HG_HEADS):
            sl = slice(h * HG_KEY_DIM, (h + 1) * HG_KEY_DIM)
            acc = None
            for s in range(2):
                bi = 2 * p2 + s
                mine = (prow >= s * t) & (prow < (s + 1) * t)
                s0 = s0_ref[bi, h]
                oi = jnp.where(mine, _dot(qe[rs, sl], s0.astype(BF16)), 0.0)
                acc = oi if acc is None else acc + oi
                upd = _dot_tn(jnp.where(mine, kw[rs, sl], 0.0).astype(BF16), vb[rs, sl])
                dh, dl = _split2(jnp.where(prow == (s + 1) * t - 1, sdec[rs, sl], 0.0))
                dcol = _dot_tn(dh, ones) + _dot_tn(dl, ones)
                sout_ref[bi, h] = s0 * dcol + upd
            heads.append(acc)
        rows_out.append(jnp.concatenate(heads, axis=-1))
    o = o + jnp.concatenate(rows_out, axis=0)
    o_ref[...] = _hg_finish(o, gate_ref[...], ng_ref[...])


def hg_sample(p, s0, obuf, sprev, prm, layer, row0, n_seq, seq_len, r=128):
    nb = r // seq_len
    blk0 = row0 // r
    lt, last = _tile_consts(r, seq_len)
    consts = [lt, last, *_hg_consts(r, seq_len)]
    full = lambda a: pl.BlockSpec(a.shape, lambda i: (0,) * a.ndim)
    small = [prm["lb"], prm["norm"], *consts]
    col = lambda off: pl.BlockSpec((r, HG_WIDTH), lambda i: (blk0 + i, off // HG_WIDTH))
    sspec = pl.BlockSpec((None, nb, HG_HEADS, HG_KEY_DIM, HG_VAL_DIM), lambda i: (layer, i, 0, 0, 0))
    inplace = [obuf] if sprev is None else [obuf, sprev]
    n_in = 5 + len(small)
    return pl.pallas_call(
        functools.partial(_hg_sample_body, r=r, t=seq_len),
        grid=(n_seq // nb,),
        in_specs=[col(P_HQ), col(P_HF), col(P_HI), col(P_HGATE), sspec, *[full(a) for a in small],
                  *[pl.BlockSpec(memory_space=pl.ANY) for _ in inplace]],
        out_specs=[pl.BlockSpec((r, HG_WIDTH), lambda i: (blk0 + i, 0)), sspec],
        out_shape=[jax.ShapeDtypeStruct(obuf.shape, BF16), jax.ShapeDtypeStruct(s0.shape, F32)],
        input_output_aliases={n_in + k: k for k in range(len(inplace))},
        compiler_params=_cparams(("parallel",)),
        name="hg_sample",
    )(p, p, p, p, s0, *small, *inplace)


def _s5_params(a_re, a_im, log_dt, b_re, b_im, c_re, c_im, d_skip):
    dt = jnp.exp(log_dt)[:, None]
    mag = jnp.exp(a_re * dt)
    abar_re = mag * jnp.cos(a_im * dt)
    abar_im = mag * jnp.sin(a_im * dt)
    den = a_re * a_re + a_im * a_im
    nr = abar_re - 1.0
    coef_re = (nr * a_re + abar_im * a_im) / den
    coef_im = (abar_im * a_re - nr * a_im) / den
    bbar_re = coef_re[..., None] * b_re - coef_im[..., None] * b_im
    bbar_im = coef_re[..., None] * b_im + coef_im[..., None] * b_re
    gpb = S5_GROUPS // S5_BLK
    eye = jnp.eye(gpb, dtype=F32)

    def in_blocks(bbar):
        bb = bbar.reshape(S5_BLK, gpb, S5_STATE, S5_GROUP_SIZE)
        return jnp.einsum("qgnk,gh->qgkhn", bb, eye).reshape(
            S5_BLK, gpb * S5_GROUP_SIZE, gpb * S5_STATE).astype(BF16)

    def out_blocks(c):
        cc = c.reshape(S5_BLK, gpb, S5_GROUP_SIZE, S5_STATE)
        return jnp.einsum("qgkn,gh->qgnhk", cc, eye).reshape(
            S5_BLK, gpb * S5_STATE, gpb * S5_GROUP_SIZE).astype(BF16)

    return {
        "abar_re": abar_re.reshape(1, S5_CH), "abar_im": abar_im.reshape(1, S5_CH),
        "b_re": in_blocks(bbar_re), "b_im": in_blocks(bbar_im),
        "c_re": out_blocks(c_re), "c_im": out_blocks(c_im),
        "d_skip": d_skip.reshape(1, S5_WIDTH),
    }


def _in_proj_weights(w_in):
    z0, xbc0, dt0, u0, gates0 = 0, 2048, 5120, 5152, 10272
    w = jnp.concatenate([w_in[:, gates0:], w_in[:, xbc0:dt0], w_in[:, u0:gates0], w_in[:, z0:xbc0]], axis=1)
    wdt = jnp.pad(w_in[:, dt0:u0], ((0, 0), (0, LANES - SSD_HEADS)))
    return w.astype(BF16), wdt.astype(BF16)


def _pad_lanes(v):
    return jnp.pad(v.reshape(1, -1), ((0, 0), (0, LANES - v.shape[-1])))


def kernel(x_prompt, x_sample, cache_mem_k, cache_mem_v, state_ssd, state_ssd_conv, state_s5_re, state_s5_im,
           state_hgrn, mem_prompt, norm_ffn1, ffn1_w1, ffn1_w3, ffn1_w2, norm_mix, w_in, ssd_conv_w, ssd_conv_b,
           ssd_dt_bias, ssd_a_log, ssd_d, ssd_norm, ssd_w_out, s5_a_re, s5_a_im, s5_log_dt, s5_b_re, s5_b_im,
           s5_c_re, s5_c_im, s5_d, s5_w_glu_a, s5_w_glu_b, hg_lower_bounds, hg_norm, hg_w_out, w_mix_out,
           norm_xa, norm_mem, xa_wq, xa_wk, xa_wv, xa_wo, norm_ffn2, ffn2_w1, ffn2_w3, ffn2_w2, norm_final):
    bp, lp, d = x_prompt.shape
    bs, ls, _ = x_sample.shape
    mp, ms = bp * lp, bs * ls
    x = jnp.concatenate([x_prompt.reshape(mp, d), x_sample.reshape(ms, d)], axis=0)
    mem = mem_prompt.reshape(bp * MEM_LEN, d)
    row = lambda v: v.reshape(1, -1)
    bf = lambda w: w.astype(BF16)

    lb_p = jax.nn.softmax(hg_lower_bounds, axis=0)
    lb_all = jnp.cumsum(lb_p, axis=0) - lb_p[0]

    ssd_s0 = state_ssd.reshape(DEPTH, bs, SSD_WIDTH, SSD_STATE)
    mem_k = cache_mem_k.astype(BF16).reshape(DEPTH, bs, MEM_LEN, d)
    mem_v = cache_mem_v.astype(BF16).reshape(DEPTH, bs, MEM_LEN, d)
    ssd_states = None
    hg_states = None

    outs = {k: [] for k in ("pk", "pv", "pss", "pcv", "psr", "psi", "phg", "scv", "ssr", "ssi")}
    for l in range(DEPTH):
        x = ffn(x, row(norm_ffn1[l]), bf(ffn1_w1[l]), bf(ffn1_w3[l]), bf(ffn1_w2[l]), row(norm_final), False)

        w_p, w_dt = _in_proj_weights(w_in[l])
        p, dt = in_proj(x, row(norm_mix[l]), w_p, w_dt)

        ssd_prm = {"conv_w": ssd_conv_w[l], "conv_b": row(ssd_conv_b[l]), "dt_bias": _pad_lanes(ssd_dt_bias[l]),
                   "a_log": _pad_lanes(ssd_a_log[l]), "d_skip": row(jnp.repeat(ssd_d[l], SSD_HEAD_DIM)),
                   "norm": row(ssd_norm[l])}
        ys, ss_p = ssd_prompt(p, dt, ssd_prm, bp, lp)
        ys, ssd_states = ssd_sample(p, dt, state_ssd_conv, ssd_s0, ys, ssd_states, ssd_prm, l, mp, bs, ls)

        s5_prm = _s5_params(s5_a_re[l], s5_a_im[l], s5_log_dt[l], s5_b_re[l], s5_b_im[l], s5_c_re[l],
                            s5_c_im[l], s5_d[l])
        u_p = p[:mp, P_U:P_U + S5_WIDTH].reshape(bp, lp, S5_WIDTH).transpose(1, 0, 2)
        u_p = jnp.pad(u_p, ((0, 0), (0, S5_PSEQ - bp), (0, 0))).reshape(lp * S5_PSEQ, S5_WIDTH)
        u_s = p[mp:, P_U:P_U + S5_WIDTH].reshape(bs // S5_SSEQ, S5_SSEQ, ls, S5_WIDTH).transpose(0, 2, 1, 3)
        u_s = u_s.reshape(ms, S5_WIDTH)
        zeros = jnp.zeros((S5_PSEQ, S5_CH), F32)
        gy_p, sr_p, si_p = s5_scan(u_p, zeros, zeros, s5_prm, 1, S5_PSEQ, lp, 32)
        gy_s, sr_s, si_s = s5_scan(u_s, state_s5_re[l].reshape(bs, S5_CH), state_s5_im[l].reshape(bs, S5_CH),
                                   s5_prm, bs // S5_SSEQ, S5_SSEQ, ls, ls)
        gy_p = gy_p.reshape(lp, S5_PSEQ, S5_WIDTH)[:, :bp].transpose(1, 0, 2).reshape(mp, S5_WIDTH)
        gy_s = gy_s.reshape(bs // S5_SSEQ, ls, S5_SSEQ, S5_WIDTH).transpose(0, 2, 1, 3).reshape(ms, S5_WIDTH)
        gy = jnp.concatenate([gy_p, gy_s], axis=0)

        hg_prm = {"lb": row(lb_all[l]), "norm": row(hg_norm[l])}
        o, hg_p = hg_prompt(p, hg_prm, bp, lp)
        o, hg_states = hg_sample(p, state_hgrn, o, hg_states, hg_prm, l, mp, bs, ls)

        x = branch_mix(x, ys, gy, o, p, bf(ssd_w_out[l]),
                       bf(s5_w_glu_a[l]), bf(s5_w_glu_b[l]), bf(hg_w_out[l]), bf(w_mix_out[l]))

        q = norm_proj(x, row(norm_xa[l]), bf(xa_wq[l]), BF16)
        kv = norm_proj(mem, row(norm_mem[l]), bf(jnp.concatenate([xa_wk[l], xa_wv[l]], axis=1)), F32)
        at = xattn_prompt(q, kv, bp, lp)
        at = xattn_sample(q, mem_k, mem_v, at, l, mp, bs, ls)
        x = res_mm(x, at, bf(xa_wo[l]))

        x = ffn(x, row(norm_ffn2[l]), bf(ffn2_w1[l]), bf(ffn2_w3[l]), bf(ffn2_w2[l]), row(norm_final),
                l == DEPTH - 1)

        tail = SSD_CONV - 1
        outs["pk"].append(kv[:, :d].reshape(bp, MEM_LEN, XA_HEADS, XA_HEAD_DIM))
        outs["pv"].append(kv[:, d:].reshape(bp, MEM_LEN, XA_HEADS, XA_HEAD_DIM))
        outs["pss"].append(ss_p.reshape(bp, SSD_HEADS, SSD_HEAD_DIM, SSD_STATE))
        outs["pcv"].append(jnp.stack([p[(b + 1) * lp - tail:(b + 1) * lp, P_XBC:P_XBC + SSD_CONV_DIM]
                                      for b in range(bp)]))
        outs["psr"].append(sr_p[:bp].reshape(bp, S5_GROUPS, S5_STATE))
        outs["psi"].append(si_p[:bp].reshape(bp, S5_GROUPS, S5_STATE))
        outs["phg"].append(hg_p)
        outs["scv"].append(p[mp:, P_XBC:P_XBC + SSD_CONV_DIM].reshape(bs, ls, SSD_CONV_DIM)[:, ls - tail:])
        outs["ssr"].append(sr_s.reshape(bs, S5_GROUPS, S5_STATE))
        outs["ssi"].append(si_s.reshape(bs, S5_GROUPS, S5_STATE))

    st = lambda k: jnp.stack(outs[k])
    return (x[:mp].reshape(bp, lp, d), x[mp:].reshape(bs, ls, d),
            st("pk"), st("pv"), st("pss"), st("pcv"), st("psr"), st("psi"), st("phg"),
            ssd_states.reshape(DEPTH, bs, SSD_HEADS, SSD_HEAD_DIM, SSD_STATE), st("scv"), st("ssr"), st("ssi"),
            hg_states)
```

```python
import functools
import math

import jax
import jax.numpy as jnp
import numpy as np
from jax import lax
from jax.experimental import pallas as pl
from jax.experimental.pallas import tpu as pltpu

F32 = jnp.float32
BF16 = jnp.bfloat16

D_MODEL = 2048
DEPTH = 2
NORM_EPS = 1e-5
SSD_HEAD_DIM = 64
SSD_HEADS = 32
SSD_GROUPS = 4
SSD_STATE = 128
SSD_CONV = 4
SSD_WIDTH = 2048
SSD_CONV_DIM = 3072
S5_WIDTH = 1024
S5_GROUP_SIZE = 16
S5_GROUPS = 64
S5_STATE = 64
HG_WIDTH = 1024
HG_HEADS = 8
HG_KEY_DIM = 128
HG_VAL_DIM = 128
MEM_LEN = 256
XA_HEADS = 4
XA_HEAD_DIM = 512
FFN_DIM = 5632

P_GATES, P_XBC, P_U, P_HQ, P_HF, P_HI, P_HGATE, P_Z = 0, 6144, 9216, 10240, 11264, 12288, 13312, 14336
P_WIDTH = 16384

V7X_VMEM_LIMIT = 56 * 1024 * 1024


def _cparams(sem, vmem=V7X_VMEM_LIMIT):
    return pltpu.CompilerParams(dimension_semantics=sem, vmem_limit_bytes=vmem)


def _rms_scale(x):
    return lax.rsqrt(jnp.mean(x * x, axis=-1, keepdims=True) + NORM_EPS)


def _silu(x):
    return x * jax.nn.sigmoid(x)


def _dot(a, b):
    return jnp.dot(a, b, preferred_element_type=F32)


def _dot_nt(a, b):
    return lax.dot_general(a, b, (((1,), (1,)), ((), ())), preferred_element_type=F32)


def _dot_tn(a, b):
    return lax.dot_general(a, b, (((0,), (0,)), ((), ())), preferred_element_type=F32)


def _split2(x):
    hi = x.astype(BF16)
    lo = (x - hi.astype(F32)).astype(BF16)
    return hi, lo


def _split3(x):
    hi = x.astype(BF16)
    r = x - hi.astype(F32)
    mid = r.astype(BF16)
    lo = (r - mid.astype(F32)).astype(BF16)
    return hi, mid, lo


def _sel_dot(sel, x, parts=3):
    ps = _split3(x) if parts == 3 else _split2(x)
    out = _dot(sel, ps[0])
    for p in ps[1:]:
        out = out + _dot(sel, p)
    return out


def _dot_sel(x, sel, parts=2):
    ps = _split3(x) if parts == 3 else _split2(x)
    out = _dot(ps[0], sel)
    for p in ps[1:]:
        out = out + _dot(p, sel)
    return out


def _norm_proj_body(x_ref, g_ref, w_ref, o_ref, h_ref):
    @pl.when(pl.program_id(1) == 0)
    def _():
        x = x_ref[...]
        h_ref[...] = (x * _rms_scale(x) * g_ref[...]).astype(BF16)

    o_ref[...] = _dot(h_ref[...], w_ref[...]).astype(o_ref.dtype)


def norm_proj(x, g, w, out_dtype, bm=1024, bn=1024):
    m, k = x.shape
    n = w.shape[1]
    return pl.pallas_call(
        _norm_proj_body,
        grid=(m // bm, n // bn),
        in_specs=[pl.BlockSpec((bm, k), lambda i, j: (i, 0)),
                  pl.BlockSpec((1, k), lambda i, j: (0, 0)),
                  pl.BlockSpec((k, bn), lambda i, j: (0, j))],
        out_specs=pl.BlockSpec((bm, bn), lambda i, j: (i, j)),
        out_shape=jax.ShapeDtypeStruct((m, n), out_dtype),
        scratch_shapes=[pltpu.VMEM((bm, k), BF16)],
        compiler_params=_cparams(("parallel", "arbitrary")),
        name="norm_proj",
    )(x, g, w)


def _in_proj_body(x_ref, g_ref, w_ref, wdt_ref, o_ref, odt_ref, h_ref):
    @pl.when(pl.program_id(1) == 0)
    def _():
        x = x_ref[...]
        h = (x * _rms_scale(x) * g_ref[...]).astype(BF16)
        h_ref[...] = h
        odt_ref[...] = _dot(h, wdt_ref[...])

    o_ref[...] = _dot(h_ref[...], w_ref[...])


def in_proj(x, g, w, wdt, bm=1024, bn=1024):
    m, k = x.shape
    n = w.shape[1]
    ndt = wdt.shape[1]
    return pl.pallas_call(
        _in_proj_body,
        grid=(m // bm, n // bn),
        in_specs=[pl.BlockSpec((bm, k), lambda i, j: (i, 0)),
                  pl.BlockSpec((1, k), lambda i, j: (0, 0)),
                  pl.BlockSpec((k, bn), lambda i, j: (0, j)),
                  pl.BlockSpec((k, ndt), lambda i, j: (0, 0))],
        out_specs=[pl.BlockSpec((bm, bn), lambda i, j: (i, j)),
                   pl.BlockSpec((bm, ndt), lambda i, j: (i, 0))],
        out_shape=[jax.ShapeDtypeStruct((m, n), F32), jax.ShapeDtypeStruct((m, ndt), F32)],
        scratch_shapes=[pltpu.VMEM((bm, k), BF16)],
        compiler_params=_cparams(("parallel", "arbitrary")),
        name="in_proj",
    )(x, g, w, wdt)


def _res_mm_body(x_ref, a_ref, w_ref, o_ref):
    o_ref[...] = x_ref[...] + _dot(a_ref[...], w_ref[...])


def res_mm(x, a, w, bm=1024, bn=1024):
    m, n = x.shape
    k = a.shape[1]
    return pl.pallas_call(
        _res_mm_body,
        grid=(m // bm, n // bn),
        in_specs=[pl.BlockSpec((bm, bn), lambda i, j: (i, j)),
                  pl.BlockSpec((bm, k), lambda i, j: (i, 0)),
                  pl.BlockSpec((k, bn), lambda i, j: (0, j))],
        out_specs=pl.BlockSpec((bm, bn), lambda i, j: (i, j)),
        out_shape=jax.ShapeDtypeStruct((m, n), F32),
        compiler_params=_cparams(("parallel", "parallel")),
        name="res_mm",
    )(x, a, w)


def _ffn_body(x_ref, g_ref, w1_ref, w3_ref, w2_ref, gf_ref, o_ref, h_ref, acc_ref, *, final_norm):
    f = pl.program_id(1)

    @pl.when(f == 0)
    def _():
        x = x_ref[...]
        h_ref[...] = (x * _rms_scale(x) * g_ref[...]).astype(BF16)
        acc_ref[...] = jnp.zeros_like(acc_ref)

    h = h_ref[...]
    a = _silu(_dot(h, w1_ref[...])) * _dot(h, w3_ref[...])
    acc_ref[...] += _dot(a.astype(BF16), w2_ref[...])

    @pl.when(f == pl.num_programs(1) - 1)
    def _():
        y = x_ref[...] + 0.5 * acc_ref[...]
        if final_norm:
            y = y * _rms_scale(y) * gf_ref[...]
        o_ref[...] = y


def ffn(x, g, w1, w3, w2, gf, final_norm, bm=512, bf=512):
    m, d = x.shape
    fdim = w1.shape[1]
    return pl.pallas_call(
        functools.partial(_ffn_body, final_norm=final_norm),
        grid=(m // bm, fdim // bf),
        in_specs=[pl.BlockSpec((bm, d), lambda i, f: (i, 0)),
                  pl.BlockSpec((1, d), lambda i, f: (0, 0)),
                  pl.BlockSpec((d, bf), lambda i, f: (0, f)),
                  pl.BlockSpec((d, bf), lambda i, f: (0, f)),
                  pl.BlockSpec((bf, d), lambda i, f: (f, 0)),
                  pl.BlockSpec((1, d), lambda i, f: (0, 0))],
        out_specs=pl.BlockSpec((bm, d), lambda i, f: (i, 0)),
        out_shape=jax.ShapeDtypeStruct((m, d), F32),
        scratch_shapes=[pltpu.VMEM((bm, d), BF16), pltpu.VMEM((bm, d), F32)],
        compiler_params=_cparams(("parallel", "arbitrary")),
        name="ffn",
    )(x, g, w1, w3, w2, gf)


def _mix_body(x_ref, ys_ref, gy_ref, o_ref, ga_ref, gb_ref, gc_ref, wssd_ref, wa_ref, wb_ref, whg_ref,
              wmix_ref, out_ref, acc_ref):
    j = pl.program_id(1)

    @pl.when(j == 0)
    def _():
        acc_ref[...] = jnp.zeros_like(acc_ref)

    gy = gy_ref[...]
    y_a = _dot(ys_ref[...], wssd_ref[...])
    y_b = _dot(gy, wa_ref[...]) * jax.nn.sigmoid(_dot(gy, wb_ref[...]))
    y_c = _dot(o_ref[...], whg_ref[...])
    mix = (jax.nn.sigmoid(ga_ref[...]) * y_a + jax.nn.sigmoid(gb_ref[...]) * y_b
           + jax.nn.sigmoid(gc_ref[...]) * y_c)
    acc_ref[...] += _dot(mix.astype(BF16), wmix_ref[...])

    @pl.when(j == pl.num_programs(1) - 1)
    def _():
        out_ref[...] = x_ref[...] + acc_ref[...]


def branch_mix(x, ys, gy, o, p, wssd, wa, wb, whg, wmix, bm=512, bn=512):
    m, d = x.shape
    nj = d // bn
    return pl.pallas_call(
        _mix_body,
        grid=(m // bm, nj),
        in_specs=[pl.BlockSpec((bm, d), lambda i, j: (i, 0)),
                  pl.BlockSpec((bm, ys.shape[1]), lambda i, j: (i, 0)),
                  pl.BlockSpec((bm, gy.shape[1]), lambda i, j: (i, 0)),
                  pl.BlockSpec((bm, o.shape[1]), lambda i, j: (i, 0)),
                  pl.BlockSpec((bm, bn), lambda i, j: (i, j)),
                  pl.BlockSpec((bm, bn), lambda i, j: (i, nj + j)),
                  pl.BlockSpec((bm, bn), lambda i, j: (i, 2 * nj + j)),
                  pl.BlockSpec((wssd.shape[0], bn), lambda i, j: (0, j)),
                  pl.BlockSpec((wa.shape[0], bn), lambda i, j: (0, j)),
                  pl.BlockSpec((wb.shape[0], bn), lambda i, j: (0, j)),
                  pl.BlockSpec((whg.shape[0], bn), lambda i, j: (0, j)),
                  pl.BlockSpec((bn, d), lambda i, j: (j, 0))],
        out_specs=pl.BlockSpec((bm, d), lambda i, j: (i, 0)),
        out_shape=jax.ShapeDtypeStruct((m, d), F32),
        scratch_shapes=[pltpu.VMEM((bm, d), F32)],
        compiler_params=_cparams(("parallel", "arbitrary")),
        name="branch_mix",
    )(x, ys, gy, o, p, p, p, wssd, wa, wb, whg, wmix)


def _attend(q, k_head, v_head):
    outs = []
    for h in range(XA_HEADS):
        sl = slice(h * XA_HEAD_DIM, (h + 1) * XA_HEAD_DIM)
        s = _dot_nt(q[:, sl], k_head(h).astype(BF16)) * (XA_HEAD_DIM ** -0.5)
        s = s - jnp.max(s, axis=-1, keepdims=True)
        e = jnp.exp(s)
        p = e / jnp.sum(e, axis=-1, keepdims=True)
        outs.append(_dot(p.astype(BF16), v_head(h).astype(BF16)))
    return jnp.concatenate(outs, axis=-1)


def _xattn_prompt_body(q_ref, kv_ref, o_ref):
    k_head = lambda h: kv_ref[:, h * XA_HEAD_DIM:(h + 1) * XA_HEAD_DIM]
    v_head = lambda h: kv_ref[:, D_MODEL + h * XA_HEAD_DIM:D_MODEL + (h + 1) * XA_HEAD_DIM]
    o_ref[...] = _attend(q_ref[...], k_head, v_head).astype(BF16)


def xattn_prompt(q, kv, n_seq, seq_len, bl=512):
    nl = seq_len // bl
    return pl.pallas_call(
        _xattn_prompt_body,
        grid=(n_seq, nl),
        in_specs=[pl.BlockSpec((bl, D_MODEL), lambda b, i: (b * nl + i, 0)),
                  pl.BlockSpec((MEM_LEN, 2 * D_MODEL), lambda b, i: (b, 0))],
        out_specs=pl.BlockSpec((bl, D_MODEL), lambda b, i: (b * nl + i, 0)),
        out_shape=jax.ShapeDtypeStruct(q.shape, BF16),
        compiler_params=_cparams(("parallel", "arbitrary")),
        name="xattn_prompt",
    )(q, kv)


def _xattn_sample_body(q_ref, k_ref, v_ref, buf_ref, o_ref, *, seq_len):
    del buf_ref
    q = q_ref[...]
    r = 2 * seq_len
    nc = XA_HEAD_DIM // LANES
    piece = lambda ref, b, h, c: ref[b, pl.ds(c * XA_HEADS + h, MEM_LEN, stride=nc * XA_HEADS), :].astype(BF16)
    scores = []
    for b in range(2):
        for h in range(XA_HEADS):
            s = None
            for c in range(nc):
                lo = h * XA_HEAD_DIM + c * LANES
                d = _dot_nt(q[:, lo:lo + LANES], piece(k_ref, b, h, c))
                s = d if s is None else s + d
            scores.append(s)
    s = jnp.concatenate(scores, axis=0) * (XA_HEAD_DIM ** -0.5)
    s = s - jnp.max(s, axis=-1, keepdims=True)
    e = jnp.exp(s)
    p = (e / jnp.sum(e, axis=-1, keepdims=True)).astype(BF16)
    rows = lax.broadcasted_iota(jnp.int32, (r, LANES), 0)
    for h in range(XA_HEADS):
        for c in range(nc):
            lo = h * XA_HEAD_DIM + c * LANES
            o0 = _dot(p[h * r:(h + 1) * r], piece(v_ref, 0, h, c))
            o1 = _dot(p[(XA_HEADS + h) * r:(XA_HEADS + h + 1) * r], piece(v_ref, 1, h, c))
            o_ref[:, lo:lo + LANES] = jnp.where(rows < seq_len, o0, o1).astype(BF16)


def xattn_sample(q, k, v, buf, layer, row0, n_seq, seq_len):
    r = 2 * seq_len
    blk0 = row0 // r
    mem_spec = pl.BlockSpec((None, 2, MEM_LEN * D_MODEL // LANES, LANES), lambda i: (layer, i, 0, 0))
    return pl.pallas_call(
        functools.partial(_xattn_sample_body, seq_len=seq_len),
        grid=(n_seq // 2,),
        in_specs=[pl.BlockSpec((r, D_MODEL), lambda i: (blk0 + i, 0)), mem_spec, mem_spec,
                  pl.BlockSpec(memory_space=pl.ANY)],
        out_specs=pl.BlockSpec((r, D_MODEL), lambda i: (blk0 + i, 0)),
        out_shape=jax.ShapeDtypeStruct(buf.shape, BF16),
        input_output_aliases={3: 0},
        compiler_params=_cparams(("parallel",)),
        name="xattn_sample",
    )(q, k, v, buf)


LANES = 128
HALO = 8


def _softplus(x):
    return jnp.maximum(x, 0.0) + jnp.log1p(jnp.exp(-jnp.abs(x)))


def _pad_rows(x, rows):
    if x.shape[0] == rows:
        return x
    return jnp.concatenate([x, jnp.zeros((rows - x.shape[0], x.shape[1]), x.dtype)], axis=0)


def _tile_consts(r, t):
    i = np.arange(r)[:, None]
    j = np.arange(r)[None, :]
    same = (i // t) == (j // t)
    lt = (same & (j <= i)).astype(np.float32)
    last = (same & (j % t == t - 1)).astype(np.float32)
    return jnp.asarray(lt, BF16), jnp.asarray(last, BF16)


def _head_expand(n_heads, width, rows=LANES):
    e = np.zeros((rows, n_heads * width), np.float32)
    for h in range(n_heads):
        e[h, h * width:(h + 1) * width] = 1.0
    return jnp.asarray(e, BF16)


def _ssd_tile(xc, dt_raw, dtb, a_log, lt, last, e, e128, r, t):
    xs = xc[:, :SSD_WIDTH]
    dt = _softplus(dt_raw + dtb)
    d_a = dt * (-jnp.exp(a_log))
    a = _sel_dot(lt, d_a, 3)
    a_e = _dot_sel(a, e, 3)
    dt_e = _dot_sel(dt, e, 2)
    alast_e = _sel_dot(last, a_e, 3)
    a_col = _dot_sel(a, e128, 3)
    a_t = _pad_rows(a, LANES).T

    row = lax.broadcasted_iota(jnp.int32, (r, LANES), 0)
    col = lax.broadcasted_iota(jnp.int32, (r, LANES), 1)
    valid = (col <= row) & (col >= (row // t) * t)
    lane_lo = col < SSD_HEAD_DIM

    xdt = _pad_rows((xs * dt_e).astype(BF16), LANES)
    hpg = SSD_HEADS // SSD_GROUPS
    ys = []
    for g in range(SSD_GROUPS):
        bg = xc[:, SSD_WIDTH + g * SSD_STATE:SSD_WIDTH + (g + 1) * SSD_STATE].astype(BF16)
        cg = xc[:, SSD_WIDTH + (SSD_GROUPS + g) * SSD_STATE:
                SSD_WIDTH + (SSD_GROUPS + g + 1) * SSD_STATE].astype(BF16)
        cb = _dot_nt(cg, _pad_rows(bg, LANES))
        for hp in range(hpg // 2):
            h0 = g * hpg + 2 * hp
            res = []
            for h in (h0, h0 + 1):
                rel = a_col[:, h * LANES:(h + 1) * LANES] - a_t[h:h + 1, :]
                dec = jnp.where(valid, jnp.exp(jnp.where(valid, rel, 0.0)), 0.0)
                res.append(_dot((cb * dec).astype(BF16), xdt[:, h0 * SSD_HEAD_DIM:(h0 + 2) * SSD_HEAD_DIM]))
            ys.append(jnp.where(lane_lo, res[0], res[1]))
    y_intra = jnp.concatenate(ys, axis=-1)
    return xs, y_intra, a_e, dt_e, alast_e


def _ssd_finish(y, xs, z, dsk, ng):
    y = y + dsk * xs
    y = y * _silu(z)
    return (y * _rms_scale(y) * ng).astype(BF16)


def _conv_silu(ext_ref, cw_ref, cb_ref, base, r):
    acc = cb_ref[...] + cw_ref[SSD_CONV - 1:SSD_CONV, :] * ext_ref[pl.ds(base, r), :]
    for k in range(1, SSD_CONV):
        acc = acc + cw_ref[SSD_CONV - 1 - k:SSD_CONV - k, :] * ext_ref[pl.ds(base - k, r), :]
    return _silu(acc)


def _ssd_prompt_body(xbc_ref, z_ref, dt_ref, cw_ref, cb_ref, dtb_ref, alog_ref, dsk_ref, ng_ref,
                     lt_ref, last_ref, e_ref, e128_ref, y_ref, sout_ref, ext_ref, st_ref, *, r):
    c = pl.program_id(1)

    @pl.when(c == 0)
    def _():
        ext_ref[0:HALO, :] = jnp.zeros((HALO, SSD_CONV_DIM), F32)
        st_ref[...] = jnp.zeros_like(st_ref)

    ext_ref[HALO:HALO + r, :] = xbc_ref[...]
    xc = _conv_silu(ext_ref, cw_ref, cb_ref, HALO, r)
    ext_ref[0:HALO, :] = xbc_ref[r - HALO:r, :]

    xs, y, a_e, dt_e, alast_e = _ssd_tile(xc, dt_ref[...], dtb_ref[...], alog_ref[...], lt_ref[...],
                                          last_ref[...], e_ref[...], e128_ref[...], r, r)
    ea_e = jnp.exp(a_e)
    xw = (xs * (dt_e * jnp.exp(alast_e - a_e))).astype(BF16)
    sdec = jnp.exp(alast_e[0:1, :])
    gw = SSD_WIDTH // SSD_GROUPS
    inter = []
    for g in range(SSD_GROUPS):
        bg = xc[:, SSD_WIDTH + g * SSD_STATE:SSD_WIDTH + (g + 1) * SSD_STATE].astype(BF16)
        cg = xc[:, SSD_WIDTH + (SSD_GROUPS + g) * SSD_STATE:
                SSD_WIDTH + (SSD_GROUPS + g + 1) * SSD_STATE].astype(BF16)
        st = st_ref[:, g * gw:(g + 1) * gw]
        inter.append(_dot(cg, st.astype(BF16)))
        st_ref[:, g * gw:(g + 1) * gw] = st * sdec[:, g * gw:(g + 1) * gw] + _dot_tn(bg, xw[:, g * gw:(g + 1) * gw])
    y = y + jnp.concatenate(inter, axis=-1) * ea_e
    y_ref[...] = _ssd_finish(y, xs, z_ref[...], dsk_ref[...], ng_ref[...])

    @pl.when(c == pl.num_programs(1) - 1)
    def _():
        sout_ref[0] = st_ref[...].T


def _ssd_consts(r, t):
    lt, last = _tile_consts(r, t)
    return lt, last, _head_expand(SSD_HEADS, SSD_HEAD_DIM), _head_expand(SSD_HEADS, LANES)


def ssd_prompt(p, dt, prm, n_seq, seq_len, r=128):
    nc = seq_len // r
    consts = _ssd_consts(r, r)
    full = lambda a: pl.BlockSpec(a.shape, lambda b, c: (0,) * a.ndim)
    small = [prm["conv_w"], prm["conv_b"], prm["dt_bias"], prm["a_log"], prm["d_skip"], prm["norm"], *consts]
    return pl.pallas_call(
        functools.partial(_ssd_prompt_body, r=r),
        grid=(n_seq, nc),
        in_specs=[pl.BlockSpec((r, SSD_CONV_DIM), lambda b, c: (b * nc + c, P_XBC // SSD_CONV_DIM)),
                  pl.BlockSpec((r, SSD_WIDTH), lambda b, c: (b * nc + c, P_Z // SSD_WIDTH)),
                  pl.BlockSpec((r, LANES), lambda b, c: (b * nc + c, 0)),
                  *[full(a) for a in small]],
        out_specs=[pl.BlockSpec((r, SSD_WIDTH), lambda b, c: (b * nc + c, 0)),
                   pl.BlockSpec((1, SSD_WIDTH, SSD_STATE), lambda b, c: (b, 0, 0))],
        out_shape=[jax.ShapeDtypeStruct((p.shape[0], SSD_WIDTH), BF16),
                   jax.ShapeDtypeStruct((n_seq, SSD_WIDTH, SSD_STATE), F32)],
        scratch_shapes=[pltpu.VMEM((HALO + r, SSD_CONV_DIM), F32), pltpu.VMEM((SSD_STATE, SSD_WIDTH), F32)],
        compiler_params=_cparams(("parallel", "arbitrary")),
        name="ssd_prompt",
    )(p, p, dt, *small)


def _ssd_sample_body(xbc_ref, z_ref, dt_ref, buf_ref, s0_ref, cw_ref, cb_ref, dtb_ref, alog_ref, dsk_ref,
                     ng_ref, lt_ref, last_ref, e_ref, e128_ref, *rest, r, t):
    y_ref, sout_ref, ext_ref = rest[-3:]
    nb = r // t
    pitch = HALO + t
    for b in range(nb):
        ext_ref[b * pitch + HALO - (SSD_CONV - 1):b * pitch + HALO, :] = buf_ref[b]
        ext_ref[b * pitch + HALO:(b + 1) * pitch, :] = xbc_ref[b * t:(b + 1) * t, :]
    xc = jnp.concatenate([_conv_silu(ext_ref, cw_ref, cb_ref, b * pitch + HALO, t) for b in range(nb)], axis=0)

    xs, y, a_e, dt_e, alast_e = _ssd_tile(xc, dt_ref[...], dtb_ref[...], alog_ref[...], lt_ref[...],
                                          last_ref[...], e_ref[...], e128_ref[...], r, t)
    ea_e = jnp.exp(a_e)
    xw = xs * (dt_e * jnp.exp(alast_e - a_e))
    sdec = jnp.exp(alast_e)
    gw = SSD_WIDTH // SSD_GROUPS
    pr = 2 * t
    prow = lax.broadcasted_iota(jnp.int32, (pr, 1), 0)
    ones = jnp.ones((pr, SSD_STATE), BF16)
    inter = []
    for g in range(SSD_GROUPS):
        bg = xc[:, SSD_WIDTH + g * SSD_STATE:SSD_WIDTH + (g + 1) * SSD_STATE].astype(BF16)
        cg = xc[:, SSD_WIDTH + (SSD_GROUPS + g) * SSD_STATE:
                SSD_WIDTH + (SSD_GROUPS + g + 1) * SSD_STATE].astype(BF16)
        cols = slice(g * gw, (g + 1) * gw)
        rows_out = []
        for q in range(nb // 2):
            rs = slice(q * pr, (q + 1) * pr)
            acc = None
            for s in range(2):
                b = 2 * q + s
                mine = (prow >= s * t) & (prow < (s + 1) * t)
                s0 = s0_ref[b, cols, :]
                yi = _dot_nt(cg[rs], s0.astype(BF16))
                acc = jnp.where(mine, yi, 0.0) if acc is None else acc + jnp.where(mine, yi, 0.0)
                upd = _dot_tn(jnp.where(mine, xw[rs, cols], 0.0).astype(BF16), bg[rs])
                lastrow = prow == (s + 1) * t - 1
                dh, dl = _split2(jnp.where(lastrow, sdec[rs, cols], 0.0))
                dcol = _dot_tn(dh, ones) + _dot_tn(dl, ones)
                sout_ref[b, cols, :] = s0 * dcol + upd
            rows_out.append(acc)
        inter.append(jnp.concatenate(rows_out, axis=0))
    y = y + jnp.concatenate(inter, axis=-1) * ea_e
    y_ref[...] = _ssd_finish(y, xs, z_ref[...], dsk_ref[...], ng_ref[...])


def ssd_sample(p, dt, conv_buf, s0, ybuf, sprev, prm, layer, row0, n_seq, seq_len, r=64):
    nb = r // seq_len
    blk0 = row0 // r
    consts = _ssd_consts(r, seq_len)
    full = lambda a: pl.BlockSpec(a.shape, lambda i: (0,) * a.ndim)
    small = [prm["conv_w"], prm["conv_b"], prm["dt_bias"], prm["a_log"], prm["d_skip"], prm["norm"], *consts]
    inplace = [ybuf] if sprev is None else [ybuf, sprev]
    n_in = 5 + len(small)
    return pl.pallas_call(
        functools.partial(_ssd_sample_body, r=r, t=seq_len),
        grid=(n_seq // nb,),
        in_specs=[pl.BlockSpec((r, SSD_CONV_DIM), lambda i: (blk0 + i, P_XBC // SSD_CONV_DIM)),
                  pl.BlockSpec((r, SSD_WIDTH), lambda i: (blk0 + i, P_Z // SSD_WIDTH)),
                  pl.BlockSpec((r, LANES), lambda i: (blk0 + i, 0)),
                  pl.BlockSpec((None, nb, SSD_CONV - 1, SSD_CONV_DIM), lambda i: (layer, i, 0, 0)),
                  pl.BlockSpec((None, nb, SSD_WIDTH, SSD_STATE), lambda i: (layer, i, 0, 0)),
                  *[full(a) for a in small],
                  *[pl.BlockSpec(memory_space=pl.ANY) for _ in inplace]],
        out_specs=[pl.BlockSpec((r, SSD_WIDTH), lambda i: (blk0 + i, 0)),
                   pl.BlockSpec((None, nb, SSD_WIDTH, SSD_STATE), lambda i: (layer, i, 0, 0))],
        out_shape=[jax.ShapeDtypeStruct(ybuf.shape, BF16),
                   jax.ShapeDtypeStruct(s0.shape, F32)],
        input_output_aliases={n_in + k: k for k in range(len(inplace))},
        scratch_shapes=[pltpu.VMEM((nb * (HALO + seq_len), SSD_CONV_DIM), F32)],
        compiler_params=_cparams(("parallel",)),
        name="ssd_sample",
    )(p, p, dt, conv_buf, s0, *small, *inplace)


S5_CH = S5_GROUPS * S5_STATE
S5_BLK = 4
S5_PSEQ = 8
S5_SSEQ = 32


def _s5_in(u, bre_ref, bim_ref):
    ub = u.astype(BF16)
    kin = S5_WIDTH // S5_BLK
    re = [_dot(ub[:, q * kin:(q + 1) * kin], bre_ref[q]) for q in range(S5_BLK)]
    im = [_dot(ub[:, q * kin:(q + 1) * kin], bim_ref[q]) for q in range(S5_BLK)]
    return jnp.concatenate(re, axis=-1), jnp.concatenate(im, axis=-1)


def _s5_out(h_re, h_im, u, cre_ref, cim_ref, dsk):
    kst = S5_CH // S5_BLK
    hr = h_re.astype(BF16)
    hi = h_im.astype(BF16)
    y = [_dot(hr[:, q * kst:(q + 1) * kst], cre_ref[q]) - _dot(hi[:, q * kst:(q + 1) * kst], cim_ref[q])
         for q in range(S5_BLK)]
    y = jnp.concatenate(y, axis=-1) + dsk * u
    return jax.nn.gelu(y).astype(BF16)


def _s5_scan_body(u_ref, h0re_ref, h0im_ref, are_ref, aim_ref, bre_ref, bim_ref, cre_ref, cim_ref, dsk_ref,
                  y_ref, hre_ref, him_ref, sre_ref, sim_ref, *, s, tc):
    c = pl.program_id(1)

    @pl.when(c == 0)
    def _():
        sre_ref[0:s, :] = h0re_ref[...]
        sim_ref[0:s, :] = h0im_ref[...]

    u = u_ref[...]
    bu_re, bu_im = _s5_in(u, bre_ref, bim_ref)
    sre_ref[s:, :] = bu_re
    sim_ref[s:, :] = bu_im
    a_re = jnp.broadcast_to(are_ref[...], (s, S5_CH))
    a_im = jnp.broadcast_to(aim_ref[...], (s, S5_CH))

    def step(t, carry):
        prev = pl.ds(pl.multiple_of(t * s, s), s)
        cur = pl.ds(pl.multiple_of((t + 1) * s, s), s)
        h_re = sre_ref[prev, :]
        h_im = sim_ref[prev, :]
        sre_ref[cur, :] = a_re * h_re - a_im * h_im + sre_ref[cur, :]
        sim_ref[cur, :] = a_re * h_im + a_im * h_re + sim_ref[cur, :]
        return carry

    lax.fori_loop(0, tc, step, 0, unroll=4)
    h_re = sre_ref[tc * s:, :]
    h_im = sim_ref[tc * s:, :]
    y_ref[...] = _s5_out(sre_ref[s:, :], sim_ref[s:, :], u, cre_ref, cim_ref, dsk_ref[...])
    sre_ref[0:s, :] = h_re
    sim_ref[0:s, :] = h_im
    hre_ref[...] = h_re
    him_ref[...] = h_im


def s5_scan(u, h0_re, h0_im, prm, n_blocks, s, n_steps, tc):
    nc = n_steps // tc
    r = tc * s
    full = lambda a: pl.BlockSpec(a.shape, lambda b, c: (0,) * a.ndim)
    small = [prm["abar_re"], prm["abar_im"], prm["b_re"], prm["b_im"], prm["c_re"], prm["c_im"], prm["d_skip"]]
    state = jax.ShapeDtypeStruct((n_blocks * s, S5_CH), F32)
    sspec = pl.BlockSpec((s, S5_CH), lambda b, c: (b, 0))
    return pl.pallas_call(
        functools.partial(_s5_scan_body, s=s, tc=tc),
        grid=(n_blocks, nc),
        in_specs=[pl.BlockSpec((r, S5_WIDTH), lambda b, c: (b * nc + c, 0)), sspec, sspec,
                  *[full(a) for a in small]],
        out_specs=[pl.BlockSpec((r, S5_WIDTH), lambda b, c: (b * nc + c, 0)), sspec, sspec],
        out_shape=[jax.ShapeDtypeStruct((u.shape[0], S5_WIDTH), BF16), state, state],
        scratch_shapes=[pltpu.VMEM((s + r, S5_CH), F32), pltpu.VMEM((s + r, S5_CH), F32)],
        compiler_params=_cparams(("parallel", "arbitrary")),
        name="s5_scan",
    )(u, h0_re, h0_im, *small)


def _hg_consts(r, t):
    i = np.arange(r)[:, None]
    j = np.arange(r)[None, :]
    sums, upper, pair = [], [], []
    s = 1
    while s < t:
        blk_i, blk_j = i // (2 * s), j // (2 * s)
        up_i = (i % (2 * s)) >= s
        mid_i = blk_i * 2 * s + s
        m_up = up_i & (j >= mid_i) & (j <= i)
        m_lo = (~up_i) & (j > i) & (j < mid_i)
        sums.append((m_up | m_lo).astype(np.float32))
        upper.append(np.broadcast_to(up_i, (r, 1)).astype(np.float32))
        pair.append(((blk_i == blk_j) & up_i & ((j % (2 * s)) < s)).astype(np.float32))
        s *= 2
    pair.append((i == j).astype(np.float32))
    return (jnp.asarray(np.stack(sums), BF16), jnp.asarray(np.stack(upper), F32),
            jnp.asarray(np.stack(pair), F32))


def _hg_gates(hf, lb):
    logf = -_softplus(-hf) + jnp.log1p(lb * jnp.exp(-hf))
    kk = (1.0 - lb) * jax.nn.sigmoid(-hf)
    return logf, kk


def _hg_intra(q, kk, v, logf, sums_ref, upper_ref, pair_ref):
    r = q.shape[0]
    nlev = sums_ref.shape[0]
    lf3 = _split3(logf)
    qb = q.astype(BF16)
    kb = kk.astype(BF16)
    vb = _pad_rows(v.astype(BF16), LANES) if r < LANES else v.astype(BF16)
    scores = [None] * HG_HEADS
    for lev in range(nlev + 1):
        if lev < nlev:
            m = sums_ref[lev]
            d = _dot(m, lf3[0]) + _dot(m, lf3[1]) + _dot(m, lf3[2])
            x = (jnp.where(upper_ref[lev] > 0.5, q, kk) * jnp.exp(d)).astype(BF16)
            xq, xk = x, x
        else:
            xq, xk = qb, kb
        mask = pair_ref[lev]
        for h in range(HG_HEADS):
            sl = slice(h * HG_KEY_DIM, (h + 1) * HG_KEY_DIM)
            sc = _dot_nt(xq[:, sl], xk[:, sl]) * mask
            scores[h] = sc if scores[h] is None else scores[h] + sc
    outs = [_dot(scores[h].astype(BF16), vb[:, h * HG_VAL_DIM:(h + 1) * HG_VAL_DIM]) for h in range(HG_HEADS)]
    return jnp.concatenate(outs, axis=-1)


def _hg_finish(o, hgate, ng):
    outs = []
    for h in range(HG_HEADS):
        oh = o[:, h * HG_VAL_DIM:(h + 1) * HG_VAL_DIM]
        outs.append(oh * _rms_scale(oh) * ng)
    return (jnp.concatenate(outs, axis=-1) * _silu(hgate)).astype(BF16)


def _hg_prompt_body(q_ref, f_ref, i_ref, gate_ref, lb_ref, ng_ref, lt_ref, last_ref, sums_ref, upper_ref,
                    pair_ref, o_ref, sout_ref, *, r):
    c = pl.program_id(1)

    @pl.when(c == 0)
    def _():
        sout_ref[...] = jnp.zeros_like(sout_ref)

    q = q_ref[...]
    v = i_ref[...]
    logf, kk = _hg_gates(f_ref[...], lb_ref[...])
    o = _hg_intra(q, kk, v, logf, sums_ref, upper_ref, pair_ref)
    b = _sel_dot(lt_ref[...], logf, 3)
    blast = b[r - 1:r, :]
    qe = (q * jnp.exp(b)).astype(BF16)
    kw = (kk * jnp.exp(blast - b)).astype(BF16)
    vb = v.astype(BF16)
    ones = jnp.ones((r, HG_VAL_DIM), BF16)
    rows = lax.broadcasted_iota(jnp.int32, (r, 1), 0)
    dh, dl = _split2(jnp.where(rows == r - 1, jnp.exp(b), 0.0))
    inter = []
    for h in range(HG_HEADS):
        sl = slice(h * HG_KEY_DIM, (h + 1) * HG_KEY_DIM)
        s = sout_ref[0, h]
        inter.append(_dot(qe[:, sl], s.astype(BF16)))
        dcol = _dot_tn(dh[:, sl], ones) + _dot_tn(dl[:, sl], ones)
        sout_ref[0, h] = s * dcol + _dot_tn(kw[:, sl], vb[:, sl])
    o = o + jnp.concatenate(inter, axis=-1)
    o_ref[...] = _hg_finish(o, gate_ref[...], ng_ref[...])


def hg_prompt(p, prm, n_seq, seq_len, r=128):
    nc = seq_len // r
    lt, last = _tile_consts(r, r)
    consts = [lt, last, *_hg_consts(r, r)]
    full = lambda a: pl.BlockSpec(a.shape, lambda b, c: (0,) * a.ndim)
    small = [prm["lb"], prm["norm"], *consts]
    col = lambda off: pl.BlockSpec((r, HG_WIDTH), lambda b, c: (b * nc + c, off // HG_WIDTH))
    return pl.pallas_call(
        functools.partial(_hg_prompt_body, r=r),
        grid=(n_seq, nc),
        in_specs=[col(P_HQ), col(P_HF), col(P_HI), col(P_HGATE), *[full(a) for a in small]],
        out_specs=[pl.BlockSpec((r, HG_WIDTH), lambda b, c: (b * nc + c, 0)),
                   pl.BlockSpec((1, HG_HEADS, HG_KEY_DIM, HG_VAL_DIM), lambda b, c: (b, 0, 0, 0))],
        out_shape=[jax.ShapeDtypeStruct((p.shape[0], HG_WIDTH), BF16),
                   jax.ShapeDtypeStruct((n_seq, HG_HEADS, HG_KEY_DIM, HG_VAL_DIM), F32)],
        compiler_params=_cparams(("parallel", "arbitrary")),
        name="hg_prompt",
    )(p, p, p, p, *small)


def _hg_sample_body(q_ref, f_ref, i_ref, gate_ref, s0_ref, lb_ref, ng_ref, lt_ref, last_ref, sums_ref,
                    upper_ref, pair_ref, *rest, r, t):
    o_ref, sout_ref = rest[-2:]
    q = q_ref[...]
    v = i_ref[...]
    logf, kk = _hg_gates(f_ref[...], lb_ref[...])
    o = _hg_intra(q, kk, v, logf, sums_ref, upper_ref, pair_ref)
    b = _sel_dot(lt_ref[...], logf, 3)
    blast = _sel_dot(last_ref[...], b, 3)
    qe = (q * jnp.exp(b)).astype(BF16)
    kw = kk * jnp.exp(blast - b)
    sdec = jnp.exp(blast)
    vb = v.astype(BF16)
    pr = 2 * t
    prow = lax.broadcasted_iota(jnp.int32, (pr, 1), 0)
    ones = jnp.ones((pr, HG_VAL_DIM), BF16)
    rows_out = []
    for p2 in range(r // pr):
        rs = slice(p2 * pr, (p2 + 1) * pr)
        heads = []
        for h in range(HG_HEADS):
            sl = slice(h * HG_KEY_DIM, (h + 1) * HG_KEY_DIM)
            acc = None
            for s in range(2):
                bi = 2 * p2 + s
                mine = (prow >= s * t) & (prow < (s + 1) * t)
                s0 = s0_ref[bi, h]
                oi = jnp.where(mine, _dot(qe[rs, sl], s0.astype(BF16)), 0.0)
                acc = oi if acc is None else acc + oi
                upd = _dot_tn(jnp.where(mine, kw[rs, sl], 0.0).astype(BF16), vb[rs, sl])
                dh, dl = _split2(jnp.where(prow == (s + 1) * t - 1, sdec[rs, sl], 0.0))
                dcol = _dot_tn(dh, ones) + _dot_tn(dl, ones)
                sout_ref[bi, h] = s0 * dcol + upd
            heads.append(acc)
        rows_out.append(jnp.concatenate(heads, axis=-1))
    o = o + jnp.concatenate(rows_out, axis=0)
    o_ref[...] = _hg_finish(o, gate_ref[...], ng_ref[...])


def hg_sample(p, s0, obuf, sprev, prm, layer, row0, n_seq, seq_len, r=128):
    nb = r // seq_len
    blk0 = row0 // r
    lt, last = _tile_consts(r, seq_len)
    consts = [lt, last, *_hg_consts(r, seq_len)]
    full = lambda a: pl.BlockSpec(a.shape, lambda i: (0,) * a.ndim)
    small = [prm["lb"], prm["norm"], *consts]
    col = lambda off: pl.BlockSpec((r, HG_WIDTH), lambda i: (blk0 + i, off // HG_WIDTH))
    sspec = pl.BlockSpec((None, nb, HG_HEADS, HG_KEY_DIM, HG_VAL_DIM), lambda i: (layer, i, 0, 0, 0))
    inplace = [obuf] if sprev is None else [obuf, sprev]
    n_in = 5 + len(small)
    return pl.pallas_call(
        functools.partial(_hg_sample_body, r=r, t=seq_len),
        grid=(n_seq // nb,),
        in_specs=[col(P_HQ), col(P_HF), col(P_HI), col(P_HGATE), sspec, *[full(a) for a in small],
                  *[pl.BlockSpec(memory_space=pl.ANY) for _ in inplace]],
        out_specs=[pl.BlockSpec((r, HG_WIDTH), lambda i: (blk0 + i, 0)), sspec],
        out_shape=[jax.ShapeDtypeStruct(obuf.shape, BF16), jax.ShapeDtypeStruct(s0.shape, F32)],
        input_output_aliases={n_in + k: k for k in range(len(inplace))},
        compiler_params=_cparams(("parallel",)),
        name="hg_sample",
    )(p, p, p, p, s0, *small, *inplace)


def _s5_params(a_re, a_im, log_dt, b_re, b_im, c_re, c_im, d_skip):
    dt = jnp.exp(log_dt)[:, None]
    mag = jnp.exp(a_re * dt)
    abar_re = mag * jnp.cos(a_im * dt)
    abar_im = mag * jnp.sin(a_im * dt)
    den = a_re * a_re + a_im * a_im
    nr = abar_re - 1.0
    coef_re = (nr * a_re + abar_im * a_im) / den
    coef_im = (abar_im * a_re - nr * a_im) / den
    bbar_re = coef_re[..., None] * b_re - coef_im[..., None] * b_im
    bbar_im = coef_re[..., None] * b_im + coef_im[..., None] * b_re
    gpb = S5_GROUPS // S5_BLK
    eye = jnp.eye(gpb, dtype=F32)

    def in_blocks(bbar):
        bb = bbar.reshape(S5_BLK, gpb, S5_STATE, S5_GROUP_SIZE)
        return jnp.einsum("qgnk,gh->qgkhn", bb, eye).reshape(
            S5_BLK, gpb * S5_GROUP_SIZE, gpb * S5_STATE).astype(BF16)

    def out_blocks(c):
        cc = c.reshape(S5_BLK, gpb, S5_GROUP_SIZE, S5_STATE)
        return jnp.einsum("qgkn,gh->qgnhk", cc, eye).reshape(
            S5_BLK, gpb * S5_STATE, gpb * S5_GROUP_SIZE).astype(BF16)

    return {
        "abar_re": abar_re.reshape(1, S5_CH), "abar_im": abar_im.reshape(1, S5_CH),
        "b_re": in_blocks(bbar_re), "b_im": in_blocks(bbar_im),
        "c_re": out_blocks(c_re), "c_im": out_blocks(c_im),
        "d_skip": d_skip.reshape(1, S5_WIDTH),
    }


def _in_proj_weights(w_in):
    z0, xbc0, dt0, u0, gates0 = 0, 2048, 5120, 5152, 10272
    w = jnp.concatenate([w_in[:, gates0:], w_in[:, xbc0:dt0], w_in[:, u0:gates0], w_in[:, z0:xbc0]], axis=1)
    wdt = jnp.pad(w_in[:, dt0:u0], ((0, 0), (0, LANES - SSD_HEADS)))
    return w.astype(BF16), wdt.astype(BF16)


def _mem_rows(cache):
    dd, b, m, h, hd = cache.shape
    c = cache.reshape(dd, b, m, h, hd // LANES, LANES).transpose(0, 1, 2, 4, 3, 5)
    return c.reshape(dd, b, m * h * (hd // LANES), LANES)


def _pad_lanes(v):
    return jnp.pad(v.reshape(1, -1), ((0, 0), (0, LANES - v.shape[-1])))


def kernel(x_prompt, x_sample, cache_mem_k, cache_mem_v, state_ssd, state_ssd_conv, state_s5_re, state_s5_im,
           state_hgrn, mem_prompt, norm_ffn1, ffn1_w1, ffn1_w3, ffn1_w2, norm_mix, w_in, ssd_conv_w, ssd_conv_b,
           ssd_dt_bias, ssd_a_log, ssd_d, ssd_norm, ssd_w_out, s5_a_re, s5_a_im, s5_log_dt, s5_b_re, s5_b_im,
           s5_c_re, s5_c_im, s5_d, s5_w_glu_a, s5_w_glu_b, hg_lower_bounds, hg_norm, hg_w_out, w_mix_out,
           norm_xa, norm_mem, xa_wq, xa_wk, xa_wv, xa_wo, norm_ffn2, ffn2_w1, ffn2_w3, ffn2_w2, norm_final):
    bp, lp, d = x_prompt.shape
    bs, ls, _ = x_sample.shape
    mp, ms = bp * lp, bs * ls
    x = jnp.concatenate([x_prompt.reshape(mp, d), x_sample.reshape(ms, d)], axis=0)
    mem = mem_prompt.reshape(bp * MEM_LEN, d)
    row = lambda v: v.reshape(1, -1)
    bf = lambda w: w.astype(BF16)

    lb_p = jax.nn.softmax(hg_lower_bounds, axis=0)
    lb_all = jnp.cumsum(lb_p, axis=0) - lb_p[0]

    ssd_s0 = state_ssd.reshape(DEPTH, bs, SSD_WIDTH, SSD_STATE)
    mem_k = _mem_rows(cache_mem_k)
    mem_v = _mem_rows(cache_mem_v)
    ssd_states = None
    hg_states = None

    outs = {k: [] for k in ("pk", "pv", "pss", "pcv", "psr", "psi", "phg", "scv", "ssr", "ssi")}
    for l in range(DEPTH):
        x = ffn(x, row(norm_ffn1[l]), bf(ffn1_w1[l]), bf(ffn1_w3[l]), bf(ffn1_w2[l]), row(norm_final), False)

        w_p, w_dt = _in_proj_weights(w_in[l])
        p, dt = in_proj(x, row(norm_mix[l]), w_p, w_dt)

        ssd_prm = {"conv_w": ssd_conv_w[l], "conv_b": row(ssd_conv_b[l]), "dt_bias": _pad_lanes(ssd_dt_bias[l]),
                   "a_log": _pad_lanes(ssd_a_log[l]), "d_skip": row(jnp.repeat(ssd_d[l], SSD_HEAD_DIM)),
                   "norm": row(ssd_norm[l])}
        ys, ss_p = ssd_prompt(p, dt, ssd_prm, bp, lp)
        ys, ssd_states = ssd_sample(p, dt, state_ssd_conv, ssd_s0, ys, ssd_states, ssd_prm, l, mp, bs, ls)

        s5_prm = _s5_params(s5_a_re[l], s5_a_im[l], s5_log_dt[l], s5_b_re[l], s5_b_im[l], s5_c_re[l],
                            s5_c_im[l], s5_d[l])
        u_p = p[:mp, P_U:P_U + S5_WIDTH].reshape(bp, lp, S5_WIDTH).transpose(1, 0, 2)
        u_p = jnp.pad(u_p, ((0, 0), (0, S5_PSEQ - bp), (0, 0))).reshape(lp * S5_PSEQ, S5_WIDTH)
        u_s = p[mp:, P_U:P_U + S5_WIDTH].reshape(bs // S5_SSEQ, S5_SSEQ, ls, S5_WIDTH).transpose(0, 2, 1, 3)
        u_s = u_s.reshape(ms, S5_WIDTH)
        zeros = jnp.zeros((S5_PSEQ, S5_CH), F32)
        gy_p, sr_p, si_p = s5_scan(u_p, zeros, zeros, s5_prm, 1, S5_PSEQ, lp, 32)
        gy_s, sr_s, si_s = s5_scan(u_s, state_s5_re[l].reshape(bs, S5_CH), state_s5_im[l].reshape(bs, S5_CH),
                                   s5_prm, bs // S5_SSEQ, S5_SSEQ, ls, ls)
        gy_p = gy_p.reshape(lp, S5_PSEQ, S5_WIDTH)[:, :bp].transpose(1, 0, 2).reshape(mp, S5_WIDTH)
        gy_s = gy_s.reshape(bs // S5_SSEQ, ls, S5_SSEQ, S5_WIDTH).transpose(0, 2, 1, 3).reshape(ms, S5_WIDTH)
        gy = jnp.concatenate([gy_p, gy_s], axis=0)

        hg_prm = {"lb": row(lb_all[l]), "norm": row(hg_norm[l])}
        o, hg_p = hg_prompt(p, hg_prm, bp, lp)
        o, hg_states = hg_sample(p, state_hgrn, o, hg_states, hg_prm, l, mp, bs, ls)

        x = branch_mix(x, ys, gy, o, p, bf(ssd_w_out[l]),
                       bf(s5_w_glu_a[l]), bf(s5_w_glu_b[l]), bf(hg_w_out[l]), bf(w_mix_out[l]))

        q = norm_proj(x, row(norm_xa[l]), bf(xa_wq[l]), BF16)
        kv = norm_proj(mem, row(norm_mem[l]), bf(jnp.concatenate([xa_wk[l], xa_wv[l]], axis=1)), F32)
        at = xattn_prompt(q, kv, bp, lp)
        at = xattn_sample(q, mem_k, mem_v, at, l, mp, bs, ls)
        x = res_mm(x, at, bf(xa_wo[l]))

        x = ffn(x, row(norm_ffn2[l]), bf(ffn2_w1[l]), bf(ffn2_w3[l]), bf(ffn2_w2[l]), row(norm_final),
                l == DEPTH - 1)

        tail = SSD_CONV - 1
        outs["pk"].append(kv[:, :d].reshape(bp, MEM_LEN, XA_HEADS, XA_HEAD_DIM))
        outs["pv"].append(kv[:, d:].reshape(bp, MEM_LEN, XA_HEADS, XA_HEAD_DIM))
        outs["pss"].append(ss_p.reshape(bp, SSD_HEADS, SSD_HEAD_DIM, SSD_STATE))
        outs["pcv"].append(jnp.stack([p[(b + 1) * lp - tail:(b + 1) * lp, P_XBC:P_XBC + SSD_CONV_DIM]
                                      for b in range(bp)]))
        outs["psr"].append(sr_p[:bp].reshape(bp, S5_GROUPS, S5_STATE))
        outs["psi"].append(si_p[:bp].reshape(bp, S5_GROUPS, S5_STATE))
        outs["phg"].append(hg_p)
        outs["scv"].append(p[mp:, P_XBC:P_XBC + SSD_CONV_DIM].reshape(bs, ls, SSD_CONV_DIM)[:, ls - tail:])
        outs["ssr"].append(sr_s.reshape(bs, S5_GROUPS, S5_STATE))
        outs["ssi"].append(si_s.reshape(bs, S5_GROUPS, S5_STATE))

    st = lambda k: jnp.stack(outs[k])
    return (x[:mp].reshape(bp, lp, d), x[mp:].reshape(bs, ls, d),
            st("pk"), st("pv"), st("pss"), st("pcv"), st("psr"), st("psi"), st("phg"),
            ssd_states.reshape(DEPTH, bs, SSD_HEADS, SSD_HEAD_DIM, SSD_STATE), st("scv"), st("ssr"), st("ssi"),
            hg_states)
```

```python
import functools
import math

import jax
import jax.numpy as jnp
import numpy as np
from jax import lax
from jax.experimental import pallas as pl
from jax.experimental.pallas import tpu as pltpu

F32 = jnp.float32
BF16 = jnp.bfloat16

D_MODEL = 2048
DEPTH = 2
NORM_EPS = 1e-5
SSD_HEAD_DIM = 64
SSD_HEADS = 32
SSD_GROUPS = 4
SSD_STATE = 128
SSD_CONV = 4
SSD_WIDTH = 2048
SSD_CONV_DIM = 3072
S5_WIDTH = 1024
S5_GROUP_SIZE = 16
S5_GROUPS = 64
S5_STATE = 64
HG_WIDTH = 1024
HG_HEADS = 8
HG_KEY_DIM = 128
HG_VAL_DIM = 128
MEM_LEN = 256
XA_HEADS = 4
XA_HEAD_DIM = 512
FFN_DIM = 5632

P_GATES, P_XBC, P_U, P_HQ, P_HF, P_HI, P_HGATE, P_Z = 0, 6144, 9216, 10240, 11264, 12288, 13312, 14336
P_WIDTH = 16384

V7X_VMEM_LIMIT = 56 * 1024 * 1024


def _cparams(sem, vmem=V7X_VMEM_LIMIT):
    return pltpu.CompilerParams(dimension_semantics=sem, vmem_limit_bytes=vmem)


def _rms_scale(x):
    return lax.rsqrt(jnp.mean(x * x, axis=-1, keepdims=True) + NORM_EPS)


def _silu(x):
    return x * jax.nn.sigmoid(x)


def _dot(a, b):
    return jnp.dot(a, b, preferred_element_type=F32)


def _dot_nt(a, b):
    return lax.dot_general(a, b, (((1,), (1,)), ((), ())), preferred_element_type=F32)


def _dot_tn(a, b):
    return lax.dot_general(a, b, (((0,), (0,)), ((), ())), preferred_element_type=F32)


def _split2(x):
    hi = x.astype(BF16)
    lo = (x - hi.astype(F32)).astype(BF16)
    return hi, lo


def _split3(x):
    hi = x.astype(BF16)
    r = x - hi.astype(F32)
    mid = r.astype(BF16)
    lo = (r - mid.astype(F32)).astype(BF16)
    return hi, mid, lo


def _sel_dot(sel, x, parts=3):
    ps = _split3(x) if parts == 3 else _split2(x)
    out = _dot(sel, ps[0])
    for p in ps[1:]:
        out = out + _dot(sel, p)
    return out


def _dot_sel(x, sel, parts=2):
    ps = _split3(x) if parts == 3 else _split2(x)
    out = _dot(ps[0], sel)
    for p in ps[1:]:
        out = out + _dot(p, sel)
    return out


def _norm_proj_body(x_ref, g_ref, w_ref, o_ref, h_ref):
    @pl.when(pl.program_id(1) == 0)
    def _():
        x = x_ref[...]
        h_ref[...] = (x * _rms_scale(x) * g_ref[...]).astype(BF16)

    o_ref[...] = _dot(h_ref[...], w_ref[...]).astype(o_ref.dtype)


def norm_proj(x, g, w, layer, out_dtype, bm=1024, bn=1024):
    m, k = x.shape
    n = w.shape[2]
    return pl.pallas_call(
        _norm_proj_body,
        grid=(m // bm, n // bn),
        in_specs=[pl.BlockSpec((bm, k), lambda i, j: (i, 0)),
                  pl.BlockSpec((1, k), lambda i, j: (0, 0)),
                  pl.BlockSpec((None, k, bn), lambda i, j: (layer, 0, j))],
        out_specs=pl.BlockSpec((bm, bn), lambda i, j: (i, j)),
        out_shape=jax.ShapeDtypeStruct((m, n), out_dtype),
        scratch_shapes=[pltpu.VMEM((bm, k), BF16)],
        compiler_params=_cparams(("parallel", "arbitrary")),
        name="norm_proj",
    )(x, g, w)


def _in_proj_body(x_ref, g_ref, w_ref, wdt_ref, o_ref, odt_ref, h_ref):
    @pl.when(pl.program_id(1) == 0)
    def _():
        x = x_ref[...]
        h = (x * _rms_scale(x) * g_ref[...]).astype(BF16)
        h_ref[...] = h
        odt_ref[...] = _dot(h, wdt_ref[...])

    o_ref[...] = _dot(h_ref[...], w_ref[...])


def in_proj(x, g, w, wdt, layer, bm=1024, bn=1024):
    m, k = x.shape
    n = w.shape[2]
    ndt = wdt.shape[2]
    return pl.pallas_call(
        _in_proj_body,
        grid=(m // bm, n // bn),
        in_specs=[pl.BlockSpec((bm, k), lambda i, j: (i, 0)),
                  pl.BlockSpec((1, k), lambda i, j: (0, 0)),
                  pl.BlockSpec((None, k, bn), lambda i, j: (layer, 0, j)),
                  pl.BlockSpec((None, k, ndt), lambda i, j: (layer, 0, 0))],
        out_specs=[pl.BlockSpec((bm, bn), lambda i, j: (i, j)),
                   pl.BlockSpec((bm, ndt), lambda i, j: (i, 0))],
        out_shape=[jax.ShapeDtypeStruct((m, n), F32), jax.ShapeDtypeStruct((m, ndt), F32)],
        scratch_shapes=[pltpu.VMEM((bm, k), BF16)],
        compiler_params=_cparams(("parallel", "arbitrary")),
        name="in_proj",
    )(x, g, w, wdt)


def _res_mm_body(x_ref, a_ref, w_ref, o_ref):
    o_ref[...] = x_ref[...] + _dot(a_ref[...], w_ref[...])


def res_mm(x, a, w, layer, bm=1024, bn=1024):
    m, n = x.shape
    k = a.shape[1]
    return pl.pallas_call(
        _res_mm_body,
        grid=(m // bm, n // bn),
        in_specs=[pl.BlockSpec((bm, bn), lambda i, j: (i, j)),
                  pl.BlockSpec((bm, k), lambda i, j: (i, 0)),
                  pl.BlockSpec((None, k, bn), lambda i, j: (layer, 0, j))],
        out_specs=pl.BlockSpec((bm, bn), lambda i, j: (i, j)),
        out_shape=jax.ShapeDtypeStruct((m, n), F32),
        compiler_params=_cparams(("parallel", "parallel")),
        name="res_mm",
    )(x, a, w)


def _ffn_body(x_ref, g_ref, w1_ref, w3_ref, w2_ref, gf_ref, o_ref, h_ref, acc_ref, *, final_norm):
    f = pl.program_id(1)

    @pl.when(f == 0)
    def _():
        x = x_ref[...]
        h_ref[...] = (x * _rms_scale(x) * g_ref[...]).astype(BF16)
        acc_ref[...] = jnp.zeros_like(acc_ref)

    h = h_ref[...]
    a = _silu(_dot(h, w1_ref[...])) * _dot(h, w3_ref[...])
    acc_ref[...] += _dot(a.astype(BF16), w2_ref[...])

    @pl.when(f == pl.num_programs(1) - 1)
    def _():
        y = x_ref[...] + 0.5 * acc_ref[...]
        if final_norm:
            y = y * _rms_scale(y) * gf_ref[...]
        o_ref[...] = y


def ffn(x, g, w1, w3, w2, gf, layer, final_norm, bm=512, bf=512):
    m, d = x.shape
    fdim = w1.shape[2]
    return pl.pallas_call(
        functools.partial(_ffn_body, final_norm=final_norm),
        grid=(m // bm, fdim // bf),
        in_specs=[pl.BlockSpec((bm, d), lambda i, f: (i, 0)),
                  pl.BlockSpec((1, d), lambda i, f: (0, 0)),
                  pl.BlockSpec((None, d, bf), lambda i, f: (layer, 0, f)),
                  pl.BlockSpec((None, d, bf), lambda i, f: (layer, 0, f)),
                  pl.BlockSpec((None, bf, d), lambda i, f: (layer, f, 0)),
                  pl.BlockSpec((1, d), lambda i, f: (0, 0))],
        out_specs=pl.BlockSpec((bm, d), lambda i, f: (i, 0)),
        out_shape=jax.ShapeDtypeStruct((m, d), F32),
        scratch_shapes=[pltpu.VMEM((bm, d), BF16), pltpu.VMEM((bm, d), F32)],
        compiler_params=_cparams(("parallel", "arbitrary")),
        name="ffn",
    )(x, g, w1, w3, w2, gf)


def _mix_body(x_ref, ys_ref, gy_ref, o_ref, ga_ref, gb_ref, gc_ref, wssd_ref, wa_ref, wb_ref, whg_ref,
              wmix_ref, out_ref, acc_ref):
    j = pl.program_id(1)

    @pl.when(j == 0)
    def _():
        acc_ref[...] = jnp.zeros_like(acc_ref)

    gy = gy_ref[...]
    y_a = _dot(ys_ref[...], wssd_ref[...])
    y_b = _dot(gy, wa_ref[...]) * jax.nn.sigmoid(_dot(gy, wb_ref[...]))
    y_c = _dot(o_ref[...], whg_ref[...])
    mix = (jax.nn.sigmoid(ga_ref[...]) * y_a + jax.nn.sigmoid(gb_ref[...]) * y_b
           + jax.nn.sigmoid(gc_ref[...]) * y_c)
    acc_ref[...] += _dot(mix.astype(BF16), wmix_ref[...])

    @pl.when(j == pl.num_programs(1) - 1)
    def _():
        out_ref[...] = x_ref[...] + acc_ref[...]


def branch_mix(x, ys, gy, o, p, wssd, wa, wb, whg, wmix, layer, bm=512, bn=512):
    m, d = x.shape
    nj = d // bn
    return pl.pallas_call(
        _mix_body,
        grid=(m // bm, nj),
        in_specs=[pl.BlockSpec((bm, d), lambda i, j: (i, 0)),
                  pl.BlockSpec((bm, ys.shape[1]), lambda i, j: (i, 0)),
                  pl.BlockSpec((bm, gy.shape[1]), lambda i, j: (i, 0)),
                  pl.BlockSpec((bm, o.shape[1]), lambda i, j: (i, 0)),
                  pl.BlockSpec((bm, bn), lambda i, j: (i, j)),
                  pl.BlockSpec((bm, bn), lambda i, j: (i, nj + j)),
                  pl.BlockSpec((bm, bn), lambda i, j: (i, 2 * nj + j)),
                  pl.BlockSpec((None, wssd.shape[1], bn), lambda i, j: (layer, 0, j)),
                  pl.BlockSpec((None, wa.shape[1], bn), lambda i, j: (layer, 0, j)),
                  pl.BlockSpec((None, wb.shape[1], bn), lambda i, j: (layer, 0, j)),
                  pl.BlockSpec((None, whg.shape[1], bn), lambda i, j: (layer, 0, j)),
                  pl.BlockSpec((None, bn, d), lambda i, j: (layer, j, 0))],
        out_specs=pl.BlockSpec((bm, d), lambda i, j: (i, 0)),
        out_shape=jax.ShapeDtypeStruct((m, d), F32),
        scratch_shapes=[pltpu.VMEM((bm, d), F32)],
        compiler_params=_cparams(("parallel", "arbitrary")),
        name="branch_mix",
    )(x, ys, gy, o, p, p, p, wssd, wa, wb, whg, wmix)


def _attend(q, k_head, v_head):
    outs = []
    for h in range(XA_HEADS):
        sl = slice(h * XA_HEAD_DIM, (h + 1) * XA_HEAD_DIM)
        s = _dot_nt(q[:, sl], k_head(h).astype(BF16)) * (XA_HEAD_DIM ** -0.5)
        s = s - jnp.max(s, axis=-1, keepdims=True)
        e = jnp.exp(s)
        p = e / jnp.sum(e, axis=-1, keepdims=True)
        outs.append(_dot(p.astype(BF16), v_head(h).astype(BF16)))
    return jnp.concatenate(outs, axis=-1)


def _xattn_prompt_body(q_ref, kv_ref, o_ref):
    k_head = lambda h: kv_ref[:, h * XA_HEAD_DIM:(h + 1) * XA_HEAD_DIM]
    v_head = lambda h: kv_ref[:, D_MODEL + h * XA_HEAD_DIM:D_MODEL + (h + 1) * XA_HEAD_DIM]
    o_ref[...] = _attend(q_ref[...], k_head, v_head).astype(BF16)


def xattn_prompt(q, kv, n_seq, seq_len, bl=512):
    nl = seq_len // bl
    return pl.pallas_call(
        _xattn_prompt_body,
        grid=(n_seq, nl),
        in_specs=[pl.BlockSpec((bl, D_MODEL), lambda b, i: (b * nl + i, 0)),
                  pl.BlockSpec((MEM_LEN, 2 * D_MODEL), lambda b, i: (b, 0))],
        out_specs=pl.BlockSpec((bl, D_MODEL), lambda b, i: (b * nl + i, 0)),
        out_shape=jax.ShapeDtypeStruct(q.shape, BF16),
        compiler_params=_cparams(("parallel", "arbitrary")),
        name="xattn_prompt",
    )(q, kv)


def _xattn_sample_body(q_ref, k_ref, v_ref, buf_ref, o_ref, *, seq_len):
    del buf_ref
    q = q_ref[...]
    r = 2 * seq_len
    nc = XA_HEAD_DIM // LANES
    piece = lambda ref, b, h, c: ref[b, pl.ds(c * XA_HEADS + h, MEM_LEN, stride=nc * XA_HEADS), :].astype(BF16)
    scores = []
    for b in range(2):
        for h in range(XA_HEADS):
            s = None
            for c in range(nc):
                lo = h * XA_HEAD_DIM + c * LANES
                d = _dot_nt(q[:, lo:lo + LANES], piece(k_ref, b, h, c))
                s = d if s is None else s + d
            scores.append(s)
    s = jnp.concatenate(scores, axis=0) * (XA_HEAD_DIM ** -0.5)
    s = s - jnp.max(s, axis=-1, keepdims=True)
    e = jnp.exp(s)
    p = (e / jnp.sum(e, axis=-1, keepdims=True)).astype(BF16)
    rows = lax.broadcasted_iota(jnp.int32, (r, LANES), 0)
    for h in range(XA_HEADS):
        for c in range(nc):
            lo = h * XA_HEAD_DIM + c * LANES
            o0 = _dot(p[h * r:(h + 1) * r], piece(v_ref, 0, h, c))
            o1 = _dot(p[(XA_HEADS + h) * r:(XA_HEADS + h + 1) * r], piece(v_ref, 1, h, c))
            o_ref[:, lo:lo + LANES] = jnp.where(rows < seq_len, o0, o1).astype(BF16)


def xattn_sample(q, k, v, buf, layer, row0, n_seq, seq_len):
    r = 2 * seq_len
    blk0 = row0 // r
    mem_spec = pl.BlockSpec((None, 2, MEM_LEN * D_MODEL // LANES, LANES), lambda i: (layer, i, 0, 0))
    return pl.pallas_call(
        functools.partial(_xattn_sample_body, seq_len=seq_len),
        grid=(n_seq // 2,),
        in_specs=[pl.BlockSpec((r, D_MODEL), lambda i: (blk0 + i, 0)), mem_spec, mem_spec,
                  pl.BlockSpec(memory_space=pl.ANY)],
        out_specs=pl.BlockSpec((r, D_MODEL), lambda i: (blk0 + i, 0)),
        out_shape=jax.ShapeDtypeStruct(buf.shape, BF16),
        input_output_aliases={3: 0},
        compiler_params=_cparams(("parallel",)),
        name="xattn_sample",
    )(q, k, v, buf)


LANES = 128
HALO = 8


def _softplus(x):
    return jnp.maximum(x, 0.0) + jnp.log1p(jnp.exp(-jnp.abs(x)))


def _pad_rows(x, rows):
    if x.shape[0] == rows:
        return x
    return jnp.concatenate([x, jnp.zeros((rows - x.shape[0], x.shape[1]), x.dtype)], axis=0)


def _tile_consts(r, t):
    i = np.arange(r)[:, None]
    j = np.arange(r)[None, :]
    same = (i // t) == (j // t)
    lt = (same & (j <= i)).astype(np.float32)
    last = (same & (j % t == t - 1)).astype(np.float32)
    return jnp.asarray(lt, BF16), jnp.asarray(last, BF16)


def _head_expand(n_heads, width, rows=LANES):
    e = np.zeros((rows, n_heads * width), np.float32)
    for h in range(n_heads):
        e[h, h * width:(h + 1) * width] = 1.0
    return jnp.asarray(e, BF16)


def _ssd_tile(xc, dt_raw, dtb, a_log, lt, last, e, e128, r, t):
    xs = xc[:, :SSD_WIDTH]
    dt = _softplus(dt_raw + dtb)
    d_a = dt * (-jnp.exp(a_log))
    a = _sel_dot(lt, d_a, 3)
    a_e = _dot_sel(a, e, 3)
    dt_e = _dot_sel(dt, e, 2)
    alast_e = _sel_dot(last, a_e, 3)
    a_col = _dot_sel(a, e128, 3)
    a_t = _pad_rows(a, LANES).T

    row = lax.broadcasted_iota(jnp.int32, (r, LANES), 0)
    col = lax.broadcasted_iota(jnp.int32, (r, LANES), 1)
    valid = (col <= row) & (col >= (row // t) * t)
    lane_lo = col < SSD_HEAD_DIM

    xdt = _pad_rows((xs * dt_e).astype(BF16), LANES)
    hpg = SSD_HEADS // SSD_GROUPS
    ys = []
    for g in range(SSD_GROUPS):
        bg = xc[:, SSD_WIDTH + g * SSD_STATE:SSD_WIDTH + (g + 1) * SSD_STATE].astype(BF16)
        cg = xc[:, SSD_WIDTH + (SSD_GROUPS + g) * SSD_STATE:
                SSD_WIDTH + (SSD_GROUPS + g + 1) * SSD_STATE].astype(BF16)
        cb = _dot_nt(cg, _pad_rows(bg, LANES))
        for hp in range(hpg // 2):
            h0 = g * hpg + 2 * hp
            res = []
            for h in (h0, h0 + 1):
                rel = a_col[:, h * LANES:(h + 1) * LANES] - a_t[h:h + 1, :]
                dec = jnp.where(valid, jnp.exp(jnp.where(valid, rel, 0.0)), 0.0)
                res.append(_dot((cb * dec).astype(BF16), xdt[:, h0 * SSD_HEAD_DIM:(h0 + 2) * SSD_HEAD_DIM]))
            ys.append(jnp.where(lane_lo, res[0], res[1]))
    y_intra = jnp.concatenate(ys, axis=-1)
    return xs, y_intra, a_e, dt_e, alast_e


def _ssd_finish(y, xs, z, dsk, ng):
    y = y + dsk * xs
    y = y * _silu(z)
    return (y * _rms_scale(y) * ng).astype(BF16)


def _conv_silu(ext_ref, cw_ref, cb_ref, base, r):
    acc = cb_ref[...] + cw_ref[SSD_CONV - 1:SSD_CONV, :] * ext_ref[pl.ds(base, r), :]
    for k in range(1, SSD_CONV):
        acc = acc + cw_ref[SSD_CONV - 1 - k:SSD_CONV - k, :] * ext_ref[pl.ds(base - k, r), :]
    return _silu(acc)


def _ssd_prompt_body(xbc_ref, z_ref, dt_ref, cw_ref, cb_ref, dtb_ref, alog_ref, dsk_ref, ng_ref,
                     lt_ref, last_ref, e_ref, e128_ref, y_ref, sout_ref, ext_ref, st_ref, *, r):
    c = pl.program_id(1)

    @pl.when(c == 0)
    def _():
        ext_ref[0:HALO, :] = jnp.zeros((HALO, SSD_CONV_DIM), F32)
        st_ref[...] = jnp.zeros_like(st_ref)

    ext_ref[HALO:HALO + r, :] = xbc_ref[...]
    xc = _conv_silu(ext_ref, cw_ref, cb_ref, HALO, r)
    ext_ref[0:HALO, :] = xbc_ref[r - HALO:r, :]

    xs, y, a_e, dt_e, alast_e = _ssd_tile(xc, dt_ref[...], dtb_ref[...], alog_ref[...], lt_ref[...],
                                          last_ref[...], e_ref[...], e128_ref[...], r, r)
    ea_e = jnp.exp(a_e)
    xw = (xs * (dt_e * jnp.exp(alast_e - a_e))).astype(BF16)
    sdec = jnp.exp(alast_e[0:1, :])
    gw = SSD_WIDTH // SSD_GROUPS
    inter = []
    for g in range(SSD_GROUPS):
        bg = xc[:, SSD_WIDTH + g * SSD_STATE:SSD_WIDTH + (g + 1) * SSD_STATE].astype(BF16)
        cg = xc[:, SSD_WIDTH + (SSD_GROUPS + g) * SSD_STATE:
                SSD_WIDTH + (SSD_GROUPS + g + 1) * SSD_STATE].astype(BF16)
        st = st_ref[:, g * gw:(g + 1) * gw]
        inter.append(_dot(cg, st.astype(BF16)))
        st_ref[:, g * gw:(g + 1) * gw] = st * sdec[:, g * gw:(g + 1) * gw] + _dot_tn(bg, xw[:, g * gw:(g + 1) * gw])
    y = y + jnp.concatenate(inter, axis=-1) * ea_e
    y_ref[...] = _ssd_finish(y, xs, z_ref[...], dsk_ref[...], ng_ref[...])

    @pl.when(c == pl.num_programs(1) - 1)
    def _():
        sout_ref[0] = st_ref[...].T


def _ssd_consts(r, t):
    lt, last = _tile_consts(r, t)
    return lt, last, _head_expand(SSD_HEADS, SSD_HEAD_DIM), _head_expand(SSD_HEADS, LANES)


def ssd_prompt(p, dt, prm, n_seq, seq_len, r=128):
    nc = seq_len // r
    consts = _ssd_consts(r, r)
    full = lambda a: pl.BlockSpec(a.shape, lambda b, c: (0,) * a.ndim)
    small = [prm["conv_w"], prm["conv_b"], prm["dt_bias"], prm["a_log"], prm["d_skip"], prm["norm"], *consts]
    return pl.pallas_call(
        functools.partial(_ssd_prompt_body, r=r),
        grid=(n_seq, nc),
        in_specs=[pl.BlockSpec((r, SSD_CONV_DIM), lambda b, c: (b * nc + c, P_XBC // SSD_CONV_DIM)),
                  pl.BlockSpec((r, SSD_WIDTH), lambda b, c: (b * nc + c, P_Z // SSD_WIDTH)),
                  pl.BlockSpec((r, LANES), lambda b, c: (b * nc + c, 0)),
                  *[full(a) for a in small]],
        out_specs=[pl.BlockSpec((r, SSD_WIDTH), lambda b, c: (b * nc + c, 0)),
                   pl.BlockSpec((1, SSD_WIDTH, SSD_STATE), lambda b, c: (b, 0, 0))],
        out_shape=[jax.ShapeDtypeStruct((p.shape[0], SSD_WIDTH), BF16),
                   jax.ShapeDtypeStruct((n_seq, SSD_WIDTH, SSD_STATE), F32)],
        scratch_shapes=[pltpu.VMEM((HALO + r, SSD_CONV_DIM), F32), pltpu.VMEM((SSD_STATE, SSD_WIDTH), F32)],
        compiler_params=_cparams(("parallel", "arbitrary")),
        name="ssd_prompt",
    )(p, p, dt, *small)


def _ssd_sample_body(xbc_ref, z_ref, dt_ref, buf_ref, s0_ref, cw_ref, cb_ref, dtb_ref, alog_ref, dsk_ref,
                     ng_ref, lt_ref, last_ref, e_ref, e128_ref, *rest, r, t):
    y_ref, sout_ref, ext_ref = rest[-3:]
    nb = r // t
    pitch = HALO + t
    for b in range(nb):
        ext_ref[b * pitch + HALO - (SSD_CONV - 1):b * pitch + HALO, :] = buf_ref[b]
        ext_ref[b * pitch + HALO:(b + 1) * pitch, :] = xbc_ref[b * t:(b + 1) * t, :]
    xc = jnp.concatenate([_conv_silu(ext_ref, cw_ref, cb_ref, b * pitch + HALO, t) for b in range(nb)], axis=0)

    xs, y, a_e, dt_e, alast_e = _ssd_tile(xc, dt_ref[...], dtb_ref[...], alog_ref[...], lt_ref[...],
                                          last_ref[...], e_ref[...], e128_ref[...], r, t)
    ea_e = jnp.exp(a_e)
    xw = xs * (dt_e * jnp.exp(alast_e - a_e))
    sdec = jnp.exp(alast_e)
    gw = SSD_WIDTH // SSD_GROUPS
    pr = 2 * t
    prow = lax.broadcasted_iota(jnp.int32, (pr, 1), 0)
    ones = jnp.ones((pr, SSD_STATE), BF16)
    inter = []
    for g in range(SSD_GROUPS):
        bg = xc[:, SSD_WIDTH + g * SSD_STATE:SSD_WIDTH + (g + 1) * SSD_STATE].astype(BF16)
        cg = xc[:, SSD_WIDTH + (SSD_GROUPS + g) * SSD_STATE:
                SSD_WIDTH + (SSD_GROUPS + g + 1) * SSD_STATE].astype(BF16)
        cols = slice(g * gw, (g + 1) * gw)
        rows_out = []
        for q in range(nb // 2):
            rs = slice(q * pr, (q + 1) * pr)
            acc = None
            for s in range(2):
                b = 2 * q + s
                mine = (prow >= s * t) & (prow < (s + 1) * t)
                s0 = s0_ref[b, cols, :]
                yi = _dot_nt(cg[rs], s0.astype(BF16))
                acc = jnp.where(mine, yi, 0.0) if acc is None else acc + jnp.where(mine, yi, 0.0)
                upd = _dot_tn(jnp.where(mine, xw[rs, cols], 0.0).astype(BF16), bg[rs])
                lastrow = prow == (s + 1) * t - 1
                dh, dl = _split2(jnp.where(lastrow, sdec[rs, cols], 0.0))
                dcol = _dot_tn(dh, ones) + _dot_tn(dl, ones)
                sout_ref[b, cols, :] = s0 * dcol + upd
            rows_out.append(acc)
        inter.append(jnp.concatenate(rows_out, axis=0))
    y = y + jnp.concatenate(inter, axis=-1) * ea_e
    y_ref[...] = _ssd_finish(y, xs, z_ref[...], dsk_ref[...], ng_ref[...])


def ssd_sample(p, dt, conv_buf, s0, ybuf, sprev, prm, layer, row0, n_seq, seq_len, r=64):
    nb = r // seq_len
    blk0 = row0 // r
    consts = _ssd_consts(r, seq_len)
    full = lambda a: pl.BlockSpec(a.shape, lambda i: (0,) * a.ndim)
    small = [prm["conv_w"], prm["conv_b"], prm["dt_bias"], prm["a_log"], prm["d_skip"], prm["norm"], *consts]
    inplace = [ybuf] if sprev is None else [ybuf, sprev]
    n_in = 5 + len(small)
    return pl.pallas_call(
        functools.partial(_ssd_sample_body, r=r, t=seq_len),
        grid=(n_seq // nb,),
        in_specs=[pl.BlockSpec((r, SSD_CONV_DIM), lambda i: (blk0 + i, P_XBC // SSD_CONV_DIM)),
                  pl.BlockSpec((r, SSD_WIDTH), lambda i: (blk0 + i, P_Z // SSD_WIDTH)),
                  pl.BlockSpec((r, LANES), lambda i: (blk0 + i, 0)),
                  pl.BlockSpec((None, nb, SSD_CONV - 1, SSD_CONV_DIM), lambda i: (layer, i, 0, 0)),
                  pl.BlockSpec((None, nb, SSD_WIDTH, SSD_STATE), lambda i: (layer, i, 0, 0)),
                  *[full(a) for a in small],
                  *[pl.BlockSpec(memory_space=pl.ANY) for _ in inplace]],
        out_specs=[pl.BlockSpec((r, SSD_WIDTH), lambda i: (blk0 + i, 0)),
                   pl.BlockSpec((None, nb, SSD_WIDTH, SSD_STATE), lambda i: (layer, i, 0, 0))],
        out_shape=[jax.ShapeDtypeStruct(ybuf.shape, BF16),
                   jax.ShapeDtypeStruct(s0.shape, F32)],
        input_output_aliases={n_in + k: k for k in range(len(inplace))},
        scratch_shapes=[pltpu.VMEM((nb * (HALO + seq_len), SSD_CONV_DIM), F32)],
        compiler_params=_cparams(("parallel",)),
        name="ssd_sample",
    )(p, p, dt, conv_buf, s0, *small, *inplace)


S5_CH = S5_GROUPS * S5_STATE
S5_BLK = 4
S5_PSEQ = 8
S5_SSEQ = 32


def _s5_in(u, bre_ref, bim_ref):
    ub = u.astype(BF16)
    kin = S5_WIDTH // S5_BLK
    re = [_dot(ub[:, q * kin:(q + 1) * kin], bre_ref[q]) for q in range(S5_BLK)]
    im = [_dot(ub[:, q * kin:(q + 1) * kin], bim_ref[q]) for q in range(S5_BLK)]
    return jnp.concatenate(re, axis=-1), jnp.concatenate(im, axis=-1)


def _s5_out(h_re, h_im, u, cre_ref, cim_ref, dsk):
    kst = S5_CH // S5_BLK
    hr = h_re.astype(BF16)
    hi = h_im.astype(BF16)
    y = [_dot(hr[:, q * kst:(q + 1) * kst], cre_ref[q]) - _dot(hi[:, q * kst:(q + 1) * kst], cim_ref[q])
         for q in range(S5_BLK)]
    y = jnp.concatenate(y, axis=-1) + dsk * u
    return jax.nn.gelu(y).astype(BF16)


def _s5_scan_body(u_ref, h0re_ref, h0im_ref, are_ref, aim_ref, bre_ref, bim_ref, cre_ref, cim_ref, dsk_ref,
                  y_ref, hre_ref, him_ref, sre_ref, sim_ref, *, s, tc):
    c = pl.program_id(1)

    @pl.when(c == 0)
    def _():
        sre_ref[0:s, :] = h0re_ref[...]
        sim_ref[0:s, :] = h0im_ref[...]

    u = u_ref[...]
    bu_re, bu_im = _s5_in(u, bre_ref, bim_ref)
    sre_ref[s:, :] = bu_re
    sim_ref[s:, :] = bu_im
    a_re = jnp.broadcast_to(are_ref[...], (s, S5_CH))
    a_im = jnp.broadcast_to(aim_ref[...], (s, S5_CH))

    def step(t, carry):
        prev = pl.ds(pl.multiple_of(t * s, s), s)
        cur = pl.ds(pl.multiple_of((t + 1) * s, s), s)
        h_re = sre_ref[prev, :]
        h_im = sim_ref[prev, :]
        sre_ref[cur, :] = a_re * h_re - a_im * h_im + sre_ref[cur, :]
        sim_ref[cur, :] = a_re * h_im + a_im * h_re + sim_ref[cur, :]
        return carry

    lax.fori_loop(0, tc, step, 0, unroll=4)
    h_re = sre_ref[tc * s:, :]
    h_im = sim_ref[tc * s:, :]
    y_ref[...] = _s5_out(sre_ref[s:, :], sim_ref[s:, :], u, cre_ref, cim_ref, dsk_ref[...])
    sre_ref[0:s, :] = h_re
    sim_ref[0:s, :] = h_im
    hre_ref[...] = h_re
    him_ref[...] = h_im


def s5_scan(u, h0_re, h0_im, prm, n_blocks, s, n_steps, tc):
    nc = n_steps // tc
    r = tc * s
    full = lambda a: pl.BlockSpec(a.shape, lambda b, c: (0,) * a.ndim)
    small = [prm["abar_re"], prm["abar_im"], prm["b_re"], prm["b_im"], prm["c_re"], prm["c_im"], prm["d_skip"]]
    state = jax.ShapeDtypeStruct((n_blocks * s, S5_CH), F32)
    sspec = pl.BlockSpec((s, S5_CH), lambda b, c: (b, 0))
    return pl.pallas_call(
        functools.partial(_s5_scan_body, s=s, tc=tc),
        grid=(n_blocks, nc),
        in_specs=[pl.BlockSpec((r, S5_WIDTH), lambda b, c: (b * nc + c, 0)), sspec, sspec,
                  *[full(a) for a in small]],
        out_specs=[pl.BlockSpec((r, S5_WIDTH), lambda b, c: (b * nc + c, 0)), sspec, sspec],
        out_shape=[jax.ShapeDtypeStruct((u.shape[0], S5_WIDTH), BF16), state, state],
        scratch_shapes=[pltpu.VMEM((s + r, S5_CH), F32), pltpu.VMEM((s + r, S5_CH), F32)],
        compiler_params=_cparams(("parallel", "arbitrary")),
        name="s5_scan",
    )(u, h0_re, h0_im, *small)


S5_T = 16
S5_GB = 4
S5_CW = S5_T * S5_GROUP_SIZE


def _s5_chunk_body(u_ref, tp_ref, bre_ref, bim_ref, cre_ref, cim_ref, are_ref, aim_ref, dsk_ref,
                   y_ref, hre_ref, him_ref, sre_ref, sim_ref, *, n_chunks, slots):
    u = u_ref[...]
    ub = u.astype(BF16)
    sre_ref[0:slots, :] = jnp.zeros((slots, sre_ref.shape[1]), F32)
    sim_ref[0:slots, :] = jnp.zeros((slots, sim_ref.shape[1]), F32)
    sre_ref[slots:, :] = _dot(ub, bre_ref[0])
    sim_ref[slots:, :] = _dot(ub, bim_ref[0])
    a_re = jnp.broadcast_to(are_ref[0], (slots, are_ref.shape[2]))
    a_im = jnp.broadcast_to(aim_ref[0], (slots, aim_ref.shape[2]))

    def step(c, carry):
        prev = pl.ds(pl.multiple_of(c * slots, slots), slots)
        cur = pl.ds(pl.multiple_of((c + 1) * slots, slots), slots)
        h_re = sre_ref[prev, :]
        h_im = sim_ref[prev, :]
        sre_ref[cur, :] = a_re * h_re - a_im * h_im + sre_ref[cur, :]
        sim_ref[cur, :] = a_re * h_im + a_im * h_re + sim_ref[cur, :]
        return carry

    lax.fori_loop(0, n_chunks, step, 0, unroll=4)
    rows = n_chunks * slots
    y = jnp.concatenate([_dot(ub[:, j * S5_CW:(j + 1) * S5_CW], tp_ref[0, j]) for j in range(S5_GB)], axis=-1)
    y = y + _dot(sre_ref[0:rows, :].astype(BF16), cre_ref[0]) - _dot(sim_ref[0:rows, :].astype(BF16), cim_ref[0])
    y = y + dsk_ref[0] * u
    y_ref[...] = jax.nn.gelu(y).astype(BF16)
    hre_ref[...] = sre_ref[rows:, :]
    him_ref[...] = sim_ref[rows:, :]


def s5_chunked(u, prm, n_chunks, slots):
    rows = n_chunks * slots
    nblk = S5_GROUPS // S5_GB
    wl = S5_GB * S5_CW
    ws = S5_GB * S5_STATE
    blk = lambda a: pl.BlockSpec((1,) + a.shape[1:], lambda i: (i,) + (0,) * (a.ndim - 1))
    small = [prm["toep"], prm["bst_re"], prm["bst_im"], prm["cst_re"], prm["cst_im"], prm["apow_re"],
             prm["apow_im"], prm["d_tiled"]]
    state = jax.ShapeDtypeStruct((slots, S5_CH), F32)
    sspec = pl.BlockSpec((slots, ws), lambda i: (0, i))
    return pl.pallas_call(
        functools.partial(_s5_chunk_body, n_chunks=n_chunks, slots=slots),
        grid=(nblk,),
        in_specs=[pl.BlockSpec((rows, wl), lambda i: (0, i)), *[blk(a) for a in small]],
        out_specs=[pl.BlockSpec((rows, wl), lambda i: (0, i)), sspec, sspec],
        out_shape=[jax.ShapeDtypeStruct(u.shape, BF16), state, state],
        scratch_shapes=[pltpu.VMEM((rows + slots, ws), F32), pltpu.VMEM((rows + slots, ws), F32)],
        compiler_params=_cparams(("parallel",)),
        name="s5_chunked",
    )(u, *small)


def _hg_consts(r, t):
    i = np.arange(r)[:, None]
    j = np.arange(r)[None, :]
    sums, upper, pair = [], [], []
    s = 1
    while s < t:
        blk_i, blk_j = i // (2 * s), j // (2 * s)
        up_i = (i % (2 * s)) >= s
        mid_i = blk_i * 2 * s + s
        m_up = up_i & (j >= mid_i) & (j <= i)
        m_lo = (~up_i) & (j > i) & (j < mid_i)
        sums.append((m_up | m_lo).astype(np.float32))
        upper.append(np.broadcast_to(up_i, (r, 1)).astype(np.float32))
        pair.append(((blk_i == blk_j) & up_i & ((j % (2 * s)) < s)).astype(np.float32))
        s *= 2
    pair.append((i == j).astype(np.float32))
    return (jnp.asarray(np.stack(sums), BF16), jnp.asarray(np.stack(upper), F32),
            jnp.asarray(np.stack(pair), F32))


def _hg_gates(hf, lb):
    logf = -_softplus(-hf) + jnp.log1p(lb * jnp.exp(-hf))
    kk = (1.0 - lb) * jax.nn.sigmoid(-hf)
    return logf, kk


def _hg_intra(q, kk, v, logf, sums_ref, upper_ref, pair_ref):
    r = q.shape[0]
    nlev = sums_ref.shape[0]
    lf3 = _split3(logf)
    qb = q.astype(BF16)
    kb = kk.astype(BF16)
    vb = _pad_rows(v.astype(BF16), LANES) if r < LANES else v.astype(BF16)
    scores = [None] * HG_HEADS
    for lev in range(nlev + 1):
        if lev < nlev:
            m = sums_ref[lev]
            d = _dot(m, lf3[0]) + _dot(m, lf3[1]) + _dot(m, lf3[2])
            x = (jnp.where(upper_ref[lev] > 0.5, q, kk) * jnp.exp(d)).astype(BF16)
            xq, xk = x, x
        else:
            xq, xk = qb, kb
        mask = pair_ref[lev]
        for h in range(HG_HEADS):
            sl = slice(h * HG_KEY_DIM, (h + 1) * HG_KEY_DIM)
            sc = _dot_nt(xq[:, sl], xk[:, sl]) * mask
            scores[h] = sc if scores[h] is None else scores[h] + sc
    outs = [_dot(scores[h].astype(BF16), vb[:, h * HG_VAL_DIM:(h + 1) * HG_VAL_DIM]) for h in range(HG_HEADS)]
    return jnp.concatenate(outs, axis=-1)


def _hg_finish(o, hgate, ng):
    outs = []
    for h in range(HG_HEADS):
        oh = o[:, h * HG_VAL_DIM:(h + 1) * HG_VAL_DIM]
        outs.append(oh * _rms_scale(oh) * ng)
    return (jnp.concatenate(outs, axis=-1) * _silu(hgate)).astype(BF16)


def _hg_prompt_body(q_ref, f_ref, i_ref, gate_ref, lb_ref, ng_ref, lt_ref, last_ref, sums_ref, upper_ref,
                    pair_ref, o_ref, sout_ref, *, r):
    c = pl.program_id(1)

    @pl.when(c == 0)
    def _():
        sout_ref[...] = jnp.zeros_like(sout_ref)

    q = q_ref[...]
    v = i_ref[...]
    logf, kk = _hg_gates(f_ref[...], lb_ref[...])
    o = _hg_intra(q, kk, v, logf, sums_ref, upper_ref, pair_ref)
    b = _sel_dot(lt_ref[...], logf, 3)
    blast = b[r - 1:r, :]
    qe = (q * jnp.exp(b)).astype(BF16)
    kw = (kk * jnp.exp(blast - b)).astype(BF16)
    vb = v.astype(BF16)
    ones = jnp.ones((r, HG_VAL_DIM), BF16)
    rows = lax.broadcasted_iota(jnp.int32, (r, 1), 0)
    dh, dl = _split2(jnp.where(rows == r - 1, jnp.exp(b), 0.0))
    inter = []
    for h in range(HG_HEADS):
        sl = slice(h * HG_KEY_DIM, (h + 1) * HG_KEY_DIM)
        s = sout_ref[0, h]
        inter.append(_dot(qe[:, sl], s.astype(BF16)))
        dcol = _dot_tn(dh[:, sl], ones) + _dot_tn(dl[:, sl], ones)
        sout_ref[0, h] = s * dcol + _dot_tn(kw[:, sl], vb[:, sl])
    o = o + jnp.concatenate(inter, axis=-1)
    o_ref[...] = _hg_finish(o, gate_ref[...], ng_ref[...])


def hg_prompt(p, prm, n_seq, seq_len, r=128):
    nc = seq_len // r
    lt, last = _tile_consts(r, r)
    consts = [lt, last, *_hg_consts(r, r)]
    full = lambda a: pl.BlockSpec(a.shape, lambda b, c: (0,) * a.ndim)
    small = [prm["lb"], prm["norm"], *consts]
    col = lambda off: pl.BlockSpec((r, HG_WIDTH), lambda b, c: (b * nc + c, off // HG_WIDTH))
    return pl.pallas_call(
        functools.partial(_hg_prompt_body, r=r),
        grid=(n_seq, nc),
        in_specs=[col(P_HQ), col(P_HF), col(P_HI), col(P_HGATE), *[full(a) for a in small]],
        out_specs=[pl.BlockSpec((r, HG_WIDTH), lambda b, c: (b * nc + c, 0)),
                   pl.BlockSpec((1, HG_HEADS, HG_KEY_DIM, HG_VAL_DIM), lambda b, c: (b, 0, 0, 0))],
        out_shape=[jax.ShapeDtypeStruct((p.shape[0], HG_WIDTH), BF16),
                   jax.ShapeDtypeStruct((n_seq, HG_HEADS, HG_KEY_DIM, HG_VAL_DIM), F32)],
        compiler_params=_cparams(("parallel", "arbitrary")),
        name="hg_prompt",
    )(p, p, p, p, *small)


def _hg_sample_body(q_ref, f_ref, i_ref, gate_ref, s0_ref, lb_ref, ng_ref, lt_ref, last_ref, sums_ref,
                    upper_ref, pair_ref, *rest, r, t):
    o_ref, sout_ref = rest[-2:]
    q = q_ref[...]
    v = i_ref[...]
    logf, kk = _hg_gates(f_ref[...], lb_ref[...])
    o = _hg_intra(q, kk, v, logf, sums_ref, upper_ref, pair_ref)
    b = _sel_dot(lt_ref[...], logf, 3)
    blast = _sel_dot(last_ref[...], b, 3)
    qe = (q * jnp.exp(b)).astype(BF16)
    kw = kk * jnp.exp(blast - b)
    sdec = jnp.exp(blast)
    vb = v.astype(BF16)
    pr = 2 * t
    prow = lax.broadcasted_iota(jnp.int32, (pr, 1), 0)
    ones = jnp.ones((pr, HG_VAL_DIM), BF16)
    rows_out = []
    for p2 in range(r // pr):
        rs = slice(p2 * pr, (p2 + 1) * pr)
        heads = []
        for h in range(HG_HEADS):
            sl = slice(h * HG_KEY_DIM, (h + 1) * HG_KEY_DIM)
            acc = None
            for s in range(2):
                bi = 2 * p2 + s
                mine = (prow >= s * t) & (prow < (s + 1) * t)
                s0 = s0_ref[bi, h]
                oi = jnp.where(mine, _dot(qe[rs, sl], s0.astype(BF16)), 0.0)
                acc = oi if acc is None else acc + oi
                upd = _dot_tn(jnp.where(mine, kw[rs, sl], 0.0).astype(BF16), vb[rs, sl])
                dh, dl = _split2(jnp.where(prow == (s + 1) * t - 1, sdec[rs, sl], 0.0))
                dcol = _dot_tn(dh, ones) + _dot_tn(dl, ones)
                sout_ref[bi, h] = s0 * dcol + upd
            heads.append(acc)
        rows_out.append(jnp.concatenate(heads, axis=-1))
    o = o + jnp.concatenate(rows_out, axis=0)
    o_ref[...] = _hg_finish(o, gate_ref[...], ng_ref[...])


def hg_sample(p, s0, obuf, sprev, prm, layer, row0, n_seq, seq_len, r=128):
    nb = r // seq_len
    blk0 = row0 // r
    lt, last = _tile_consts(r, seq_len)
    consts = [lt, last, *_hg_consts(r, seq_len)]
    full = lambda a: pl.BlockSpec(a.shape, lambda i: (0,) * a.ndim)
    small = [prm["lb"], prm["norm"], *consts]
    col = lambda off: pl.BlockSpec((r, HG_WIDTH), lambda i: (blk0 + i, off // HG_WIDTH))
    sspec = pl.BlockSpec((None, nb, HG_HEADS, HG_KEY_DIM, HG_VAL_DIM), lambda i: (layer, i, 0, 0, 0))
    inplace = [obuf] if sprev is None else [obuf, sprev]
    n_in = 5 + len(small)
    return pl.pallas_call(
        functools.partial(_hg_sample_body, r=r, t=seq_len),
        grid=(n_seq // nb,),
        in_specs=[col(P_HQ), col(P_HF), col(P_HI), col(P_HGATE), sspec, *[full(a) for a in small],
                  *[pl.BlockSpec(memory_space=pl.ANY) for _ in inplace]],
        out_specs=[pl.BlockSpec((r, HG_WIDTH), lambda i: (blk0 + i, 0)), sspec],
        out_shape=[jax.ShapeDtypeStruct(obuf.shape, BF16), jax.ShapeDtypeStruct(s0.shape, F32)],
        input_output_aliases={n_in + k: k for k in range(len(inplace))},
        compiler_params=_cparams(("parallel",)),
        name="hg_sample",
    )(p, p, p, p, s0, *small, *inplace)


def _s5_params(a_re, a_im, log_dt, b_re, b_im, c_re, c_im, d_skip):
    dt = jnp.exp(log_dt)[:, None]
    mag = jnp.exp(a_re * dt)
    abar_re = mag * jnp.cos(a_im * dt)
    abar_im = mag * jnp.sin(a_im * dt)
    den = a_re * a_re + a_im * a_im
    nr = abar_re - 1.0
    coef_re = (nr * a_re + abar_im * a_im) / den
    coef_im = (abar_im * a_re - nr * a_im) / den
    bbar_re = coef_re[..., None] * b_re - coef_im[..., None] * b_im
    bbar_im = coef_re[..., None] * b_im + coef_im[..., None] * b_re
    gpb = S5_GROUPS // S5_BLK
    eye = jnp.eye(gpb, dtype=F32)

    def in_blocks(bbar):
        bb = bbar.reshape(S5_BLK, gpb, S5_STATE, S5_GROUP_SIZE)
        return jnp.einsum("qgnk,gh->qgkhn", bb, eye).reshape(
            S5_BLK, gpb * S5_GROUP_SIZE, gpb * S5_STATE).astype(BF16)

    def out_blocks(c):
        cc = c.reshape(S5_BLK, gpb, S5_GROUP_SIZE, S5_STATE)
        return jnp.einsum("qgkn,gh->qgnhk", cc, eye).reshape(
            S5_BLK, gpb * S5_STATE, gpb * S5_GROUP_SIZE).astype(BF16)

    prm = {
        "abar_re": abar_re.reshape(1, S5_CH), "abar_im": abar_im.reshape(1, S5_CH),
        "b_re": in_blocks(bbar_re), "b_im": in_blocks(bbar_im),
        "c_re": out_blocks(c_re), "c_im": out_blocks(c_im),
        "d_skip": d_skip.reshape(1, S5_WIDTH),
    }

    hi = lax.Precision.HIGHEST
    tau = jnp.arange(S5_T + 1, dtype=F32)[:, None, None]
    pmag = jnp.exp(tau * (a_re * dt))
    pw_re = pmag * jnp.cos(tau * (a_im * dt))
    pw_im = pmag * jnp.sin(tau * (a_im * dt))
    cp_re = c_re[None] * pw_re[:, :, None, :] - c_im[None] * pw_im[:, :, None, :]
    cp_im = c_re[None] * pw_im[:, :, None, :] + c_im[None] * pw_re[:, :, None, :]
    kern = (jnp.einsum("tgkn,gnj->tgkj", cp_re, bbar_re, precision=hi)
            - jnp.einsum("tgkn,gnj->tgkj", cp_im, bbar_im, precision=hi))
    s_idx = jnp.arange(S5_T)[:, None]
    t_idx = jnp.arange(S5_T)[None, :]
    lag = jnp.clip(t_idx - s_idx, 0, S5_T)
    toep = jnp.where((t_idx >= s_idx)[:, :, None, None, None], kern[lag], 0.0)
    toep = toep.transpose(2, 0, 4, 1, 3).reshape(S5_GROUPS // S5_GB, S5_GB, S5_CW, S5_CW)
    rev = pw_re[S5_T - 1 - jnp.arange(S5_T)], pw_im[S5_T - 1 - jnp.arange(S5_T)]
    bst_re = rev[0][..., None] * bbar_re[None] - rev[1][..., None] * bbar_im[None]
    bst_im = rev[0][..., None] * bbar_im[None] + rev[1][..., None] * bbar_re[None]
    eye_gb = jnp.eye(S5_GB, dtype=F32)

    def bst_blocks(b):
        bb = b.transpose(1, 0, 3, 2).reshape(S5_GROUPS // S5_GB, S5_GB, S5_CW, S5_STATE)
        return jnp.einsum("qgrn,gh->qgrhn", bb, eye_gb).reshape(-1, S5_GB * S5_CW, S5_GB * S5_STATE).astype(BF16)

    def cst_blocks(c):
        cc = c.transpose(1, 3, 0, 2).reshape(S5_GROUPS // S5_GB, S5_GB, S5_STATE, S5_CW)
        return jnp.einsum("qgnr,gh->qgnhr", cc, eye_gb).reshape(-1, S5_GB * S5_STATE, S5_GB * S5_CW).astype(BF16)

    prm.update({
        "toep": toep.astype(BF16),
        "bst_re": bst_blocks(bst_re), "bst_im": bst_blocks(bst_im),
        "cst_re": cst_blocks(cp_re[1:]), "cst_im": cst_blocks(cp_im[1:]),
        "apow_re": pw_re[S5_T].reshape(-1, 1, S5_GB * S5_STATE), "apow_im": pw_im[S5_T].reshape(-1, 1, S5_GB * S5_STATE),
        "d_tiled": jnp.broadcast_to(d_skip.reshape(S5_GROUPS, 1, S5_GROUP_SIZE),
                                    (S5_GROUPS, S5_T, S5_GROUP_SIZE)).reshape(-1, 1, S5_GB * S5_CW),
    })
    return prm


def _in_proj_weights(w_in):
    z0, xbc0, dt0, u0, gates0 = 0, 2048, 5120, 5152, 10272
    w = jnp.concatenate([w_in[..., gates0:], w_in[..., xbc0:dt0], w_in[..., u0:gates0], w_in[..., z0:xbc0]],
                        axis=-1)
    wdt = jnp.pad(w_in[..., dt0:u0], ((0, 0), (0, 0), (0, LANES - SSD_HEADS)))
    return w.astype(BF16), wdt.astype(BF16)


def _mem_rows(cache):
    dd, b, m, h, hd = cache.shape
    c = cache.reshape(dd, b, m, h, hd // LANES, LANES).transpose(0, 1, 2, 4, 3, 5)
    return c.reshape(dd, b, m * h * (hd // LANES), LANES)


def _pad_lanes(v):
    return jnp.pad(v.reshape(1, -1), ((0, 0), (0, LANES - v.shape[-1])))


def kernel(x_prompt, x_sample, cache_mem_k, cache_mem_v, state_ssd, state_ssd_conv, state_s5_re, state_s5_im,
           state_hgrn, mem_prompt, norm_ffn1, ffn1_w1, ffn1_w3, ffn1_w2, norm_mix, w_in, ssd_conv_w, ssd_conv_b,
           ssd_dt_bias, ssd_a_log, ssd_d, ssd_norm, ssd_w_out, s5_a_re, s5_a_im, s5_log_dt, s5_b_re, s5_b_im,
           s5_c_re, s5_c_im, s5_d, s5_w_glu_a, s5_w_glu_b, hg_lower_bounds, hg_norm, hg_w_out, w_mix_out,
           norm_xa, norm_mem, xa_wq, xa_wk, xa_wv, xa_wo, norm_ffn2, ffn2_w1, ffn2_w3, ffn2_w2, norm_final):
    bp, lp, d = x_prompt.shape
    bs, ls, _ = x_sample.shape
    mp, ms = bp * lp, bs * ls
    x = jnp.concatenate([x_prompt.reshape(mp, d), x_sample.reshape(ms, d)], axis=0)
    mem = mem_prompt.reshape(bp * MEM_LEN, d)
    row = lambda v: v.reshape(1, -1)
    bf = lambda w: w.astype(BF16)

    lb_p = jax.nn.softmax(hg_lower_bounds, axis=0)
    lb_all = jnp.cumsum(lb_p, axis=0) - lb_p[0]

    ssd_s0 = state_ssd.reshape(DEPTH, bs, SSD_WIDTH, SSD_STATE)
    mem_k = _mem_rows(cache_mem_k)
    mem_v = _mem_rows(cache_mem_v)
    ssd_states = None
    hg_states = None

    f1_w1, f1_w3, f1_w2 = bf(ffn1_w1), bf(ffn1_w3), bf(ffn1_w2)
    f2_w1, f2_w3, f2_w2 = bf(ffn2_w1), bf(ffn2_w3), bf(ffn2_w2)
    w_p, w_dt = _in_proj_weights(w_in)
    b_ssd, b_glu_a, b_glu_b, b_hg, b_mix = bf(ssd_w_out), bf(s5_w_glu_a), bf(s5_w_glu_b), bf(hg_w_out), bf(w_mix_out)
    b_wq, b_wo = bf(xa_wq), bf(xa_wo)
    b_wkv = bf(jnp.concatenate([xa_wk, xa_wv], axis=-1))

    outs = {k: [] for k in ("pk", "pv", "pss", "pcv", "psr", "psi", "phg", "scv", "ssr", "ssi")}
    for l in range(DEPTH):
        x = ffn(x, row(norm_ffn1[l]), f1_w1, f1_w3, f1_w2, row(norm_final), l, False)

        p, dt = in_proj(x, row(norm_mix[l]), w_p, w_dt, l)

        ssd_prm = {"conv_w": ssd_conv_w[l], "conv_b": row(ssd_conv_b[l]), "dt_bias": _pad_lanes(ssd_dt_bias[l]),
                   "a_log": _pad_lanes(ssd_a_log[l]), "d_skip": row(jnp.repeat(ssd_d[l], SSD_HEAD_DIM)),
                   "norm": row(ssd_norm[l])}
        ys, ss_p = ssd_prompt(p, dt, ssd_prm, bp, lp)
        ys, ssd_states = ssd_sample(p, dt, state_ssd_conv, ssd_s0, ys, ssd_states, ssd_prm, l, mp, bs, ls)

        s5_prm = _s5_params(s5_a_re[l], s5_a_im[l], s5_log_dt[l], s5_b_re[l], s5_b_im[l], s5_c_re[l],
                            s5_c_im[l], s5_d[l])
        nch = lp // S5_T
        u_p = p[:mp, P_U:P_U + S5_WIDTH].reshape(bp, nch, S5_T, S5_GROUPS, S5_GROUP_SIZE).transpose(1, 0, 3, 2, 4)
        u_p = jnp.pad(u_p, ((0, 0), (0, S5_PSEQ - bp), (0, 0), (0, 0), (0, 0))).reshape(nch * S5_PSEQ, -1)
        u_s = p[mp:, P_U:P_U + S5_WIDTH].reshape(bs // S5_SSEQ, S5_SSEQ, ls, S5_WIDTH).transpose(0, 2, 1, 3)
        u_s = u_s.reshape(ms, S5_WIDTH)
        gy_p, sr_p, si_p = s5_chunked(u_p, s5_prm, nch, S5_PSEQ)
        gy_s, sr_s, si_s = s5_scan(u_s, state_s5_re[l].reshape(bs, S5_CH), state_s5_im[l].reshape(bs, S5_CH),
                                   s5_prm, bs // S5_SSEQ, S5_SSEQ, ls, ls)
        gy_p = gy_p.reshape(nch, S5_PSEQ, S5_GROUPS, S5_T, S5_GROUP_SIZE)[:, :bp].transpose(1, 0, 3, 2, 4)
        gy_p = gy_p.reshape(mp, S5_WIDTH)
        gy_s = gy_s.reshape(bs // S5_SSEQ, ls, S5_SSEQ, S5_WIDTH).transpose(0, 2, 1, 3).reshape(ms, S5_WIDTH)
        gy = jnp.concatenate([gy_p, gy_s], axis=0)

        hg_prm = {"lb": row(lb_all[l]), "norm": row(hg_norm[l])}
        o, hg_p = hg_prompt(p, hg_prm, bp, lp)
        o, hg_states = hg_sample(p, state_hgrn, o, hg_states, hg_prm, l, mp, bs, ls)

        x = branch_mix(x, ys, gy, o, p, b_ssd, b_glu_a, b_glu_b, b_hg, b_mix, l)

        q = norm_proj(x, row(norm_xa[l]), b_wq, l, BF16)
        kv = norm_proj(mem, row(norm_mem[l]), b_wkv, l, F32)
        at = xattn_prompt(q, kv, bp, lp)
        at = xattn_sample(q, mem_k, mem_v, at, l, mp, bs, ls)
        x = res_mm(x, at, b_wo, l)

        x = ffn(x, row(norm_ffn2[l]), f2_w1, f2_w3, f2_w2, row(norm_final), l, l == DEPTH - 1)

        tail = SSD_CONV - 1
        outs["pk"].append(kv[:, :d].reshape(bp, MEM_LEN, XA_HEADS, XA_HEAD_DIM))
        outs["pv"].append(kv[:, d:].reshape(bp, MEM_LEN, XA_HEADS, XA_HEAD_DIM))
        outs["pss"].append(ss_p.reshape(bp, SSD_HEADS, SSD_HEAD_DIM, SSD_STATE))
        outs["pcv"].append(jnp.stack([p[(b + 1) * lp - tail:(b + 1) * lp, P_XBC:P_XBC + SSD_CONV_DIM]
                                      for b in range(bp)]))
        outs["psr"].append(sr_p[:bp].reshape(bp, S5_GROUPS, S5_STATE))
        outs["psi"].append(si_p[:bp].reshape(bp, S5_GROUPS, S5_STATE))
        outs["phg"].append(hg_p)
        outs["scv"].append(p[mp:, P_XBC:P_XBC + SSD_CONV_DIM].reshape(bs, ls, SSD_CONV_DIM)[:, ls - tail:])
        outs["ssr"].append(sr_s.reshape(bs, S5_GROUPS, S5_STATE))
        outs["ssi"].append(si_s.reshape(bs, S5_GROUPS, S5_STATE))

    st = lambda k: jnp.stack(outs[k])
    return (x[:mp].reshape(bp, lp, d), x[mp:].reshape(bs, ls, d),
            st("pk"), st("pv"), st("pss"), st("pcv"), st("psr"), st("psi"), st("phg"),
            ssd_states.reshape(DEPTH, bs, SSD_HEADS, SSD_HEAD_DIM, SSD_STATE), st("scv"), st("ssr"), st("ssi"),
            hg_states)
```

```python
import functools
import math

import jax
import jax.numpy as jnp
import numpy as np
from jax import lax
from jax.experimental import pallas as pl
from jax.experimental.pallas import tpu as pltpu

F32 = jnp.float32
BF16 = jnp.bfloat16

D_MODEL = 2048
DEPTH = 2
NORM_EPS = 1e-5
SSD_HEAD_DIM = 64
SSD_HEADS = 32
SSD_GROUPS = 4
SSD_STATE = 128
SSD_CONV = 4
SSD_WIDTH = 2048
SSD_CONV_DIM = 3072
S5_WIDTH = 1024
S5_GROUP_SIZE = 16
S5_GROUPS = 64
S5_STATE = 64
HG_WIDTH = 1024
HG_HEADS = 8
HG_KEY_DIM = 128
HG_VAL_DIM = 128
MEM_LEN = 256
XA_HEADS = 4
XA_HEAD_DIM = 512
FFN_DIM = 5632

P_GATES, P_XBC, P_U, P_HQ, P_HF, P_HI, P_HGATE, P_Z = 0, 6144, 9216, 10240, 11264, 12288, 13312, 14336
P_WIDTH = 16384

V7X_VMEM_LIMIT = 56 * 1024 * 1024


def _cparams(sem, vmem=V7X_VMEM_LIMIT):
    return pltpu.CompilerParams(dimension_semantics=sem, vmem_limit_bytes=vmem)


def _rms_scale(x):
    return lax.rsqrt(jnp.mean(x * x, axis=-1, keepdims=True) + NORM_EPS)


def _silu(x):
    return x * jax.nn.sigmoid(x)


def _dot(a, b):
    return jnp.dot(a, b, preferred_element_type=F32)


def _dot_nt(a, b):
    return lax.dot_general(a, b, (((1,), (1,)), ((), ())), preferred_element_type=F32)


def _dot_tn(a, b):
    return lax.dot_general(a, b, (((0,), (0,)), ((), ())), preferred_element_type=F32)


def _split2(x):
    hi = x.astype(BF16)
    lo = (x - hi.astype(F32)).astype(BF16)
    return hi, lo


def _split3(x):
    hi = x.astype(BF16)
    r = x - hi.astype(F32)
    mid = r.astype(BF16)
    lo = (r - mid.astype(F32)).astype(BF16)
    return hi, mid, lo


def _sel_dot(sel, x, parts=3):
    ps = _split3(x) if parts == 3 else _split2(x)
    out = _dot(sel, ps[0])
    for p in ps[1:]:
        out = out + _dot(sel, p)
    return out


def _dot_sel(x, sel, parts=2):
    ps = _split3(x) if parts == 3 else _split2(x)
    out = _dot(ps[0], sel)
    for p in ps[1:]:
        out = out + _dot(p, sel)
    return out


def _norm_proj_body(x_ref, g_ref, w_ref, o_ref, h_ref):
    @pl.when(pl.program_id(1) == 0)
    def _():
        x = x_ref[...]
        h_ref[...] = (x * _rms_scale(x) * g_ref[...]).astype(BF16)

    o_ref[...] = _dot(h_ref[...], w_ref[...]).astype(o_ref.dtype)


def norm_proj(x, g, w, layer, out_dtype, bm=1024, bn=1024):
    m, k = x.shape
    n = w.shape[2]
    return pl.pallas_call(
        _norm_proj_body,
        grid=(m // bm, n // bn),
        in_specs=[pl.BlockSpec((bm, k), lambda i, j: (i, 0)),
                  pl.BlockSpec((1, k), lambda i, j: (0, 0)),
                  pl.BlockSpec((None, k, bn), lambda i, j: (layer, 0, j))],
        out_specs=pl.BlockSpec((bm, bn), lambda i, j: (i, j)),
        out_shape=jax.ShapeDtypeStruct((m, n), out_dtype),
        scratch_shapes=[pltpu.VMEM((bm, k), BF16)],
        compiler_params=_cparams(("parallel", "arbitrary")),
        name="norm_proj",
    )(x, g, w)


def _in_proj_body(x_ref, g_ref, w_ref, wdt_ref, o_ref, odt_ref, h_ref):
    @pl.when(pl.program_id(1) == 0)
    def _():
        x = x_ref[...]
        h = (x * _rms_scale(x) * g_ref[...]).astype(BF16)
        h_ref[...] = h
        odt_ref[...] = _dot(h, wdt_ref[...])

    o_ref[...] = _dot(h_ref[...], w_ref[...])


def in_proj(x, g, w, wdt, layer, bm=1024, bn=1024):
    m, k = x.shape
    n = w.shape[2]
    ndt = wdt.shape[2]
    return pl.pallas_call(
        _in_proj_body,
        grid=(m // bm, n // bn),
        in_specs=[pl.BlockSpec((bm, k), lambda i, j: (i, 0)),
                  pl.BlockSpec((1, k), lambda i, j: (0, 0)),
                  pl.BlockSpec((None, k, bn), lambda i, j: (layer, 0, j)),
                  pl.BlockSpec((None, k, ndt), lambda i, j: (layer, 0, 0))],
        out_specs=[pl.BlockSpec((bm, bn), lambda i, j: (i, j)),
                   pl.BlockSpec((bm, ndt), lambda i, j: (i, 0))],
        out_shape=[jax.ShapeDtypeStruct((m, n), F32), jax.ShapeDtypeStruct((m, ndt), F32)],
        scratch_shapes=[pltpu.VMEM((bm, k), BF16)],
        compiler_params=_cparams(("parallel", "arbitrary")),
        name="in_proj",
    )(x, g, w, wdt)


def _res_mm_body(x_ref, a_ref, w_ref, o_ref):
    o_ref[...] = x_ref[...] + _dot(a_ref[...], w_ref[...])


def res_mm(x, a, w, layer, bm=1024, bn=1024):
    m, n = x.shape
    k = a.shape[1]
    return pl.pallas_call(
        _res_mm_body,
        grid=(m // bm, n // bn),
        in_specs=[pl.BlockSpec((bm, bn), lambda i, j: (i, j)),
                  pl.BlockSpec((bm, k), lambda i, j: (i, 0)),
                  pl.BlockSpec((None, k, bn), lambda i, j: (layer, 0, j))],
        out_specs=pl.BlockSpec((bm, bn), lambda i, j: (i, j)),
        out_shape=jax.ShapeDtypeStruct((m, n), F32),
        compiler_params=_cparams(("parallel", "parallel")),
        name="res_mm",
    )(x, a, w)


def _ffn_body(x_ref, g_ref, w1_ref, w3_ref, w2_ref, gf_ref, o_ref, h_ref, acc_ref, *, final_norm):
    f = pl.program_id(1)

    @pl.when(f == 0)
    def _():
        x = x_ref[...]
        h_ref[...] = (x * _rms_scale(x) * g_ref[...]).astype(BF16)
        acc_ref[...] = jnp.zeros_like(acc_ref)

    h = h_ref[...]
    a = _silu(_dot(h, w1_ref[...])) * _dot(h, w3_ref[...])
    acc_ref[...] += _dot(a.astype(BF16), w2_ref[...])

    @pl.when(f == pl.num_programs(1) - 1)
    def _():
        y = x_ref[...] + 0.5 * acc_ref[...]
        if final_norm:
            y = y * _rms_scale(y) * gf_ref[...]
        o_ref[...] = y


def ffn(x, g, w1, w3, w2, gf, layer, final_norm, bm=512, bf=512):
    m, d = x.shape
    fdim = w1.shape[2]
    return pl.pallas_call(
        functools.partial(_ffn_body, final_norm=final_norm),
        grid=(m // bm, fdim // bf),
        in_specs=[pl.BlockSpec((bm, d), lambda i, f: (i, 0)),
                  pl.BlockSpec((1, d), lambda i, f: (0, 0)),
                  pl.BlockSpec((None, d, bf), lambda i, f: (layer, 0, f)),
                  pl.BlockSpec((None, d, bf), lambda i, f: (layer, 0, f)),
                  pl.BlockSpec((None, bf, d), lambda i, f: (layer, f, 0)),
                  pl.BlockSpec((1, d), lambda i, f: (0, 0))],
        out_specs=pl.BlockSpec((bm, d), lambda i, f: (i, 0)),
        out_shape=jax.ShapeDtypeStruct((m, d), F32),
        scratch_shapes=[pltpu.VMEM((bm, d), BF16), pltpu.VMEM((bm, d), F32)],
        compiler_params=_cparams(("parallel", "arbitrary")),
        name="ffn",
    )(x, g, w1, w3, w2, gf)


def _mix_body(x_ref, ys_ref, gy_ref, o_ref, ga_ref, gb_ref, gc_ref, wssd_ref, wa_ref, wb_ref, whg_ref,
              wmix_ref, out_ref, acc_ref):
    j = pl.program_id(1)

    @pl.when(j == 0)
    def _():
        acc_ref[...] = jnp.zeros_like(acc_ref)

    gy = gy_ref[...].astype(BF16)
    y_a = _dot(ys_ref[...], wssd_ref[...])
    y_b = _dot(gy, wa_ref[...]) * jax.nn.sigmoid(_dot(gy, wb_ref[...]))
    y_c = _dot(o_ref[...], whg_ref[...])
    mix = (jax.nn.sigmoid(ga_ref[...]) * y_a + jax.nn.sigmoid(gb_ref[...]) * y_b
           + jax.nn.sigmoid(gc_ref[...]) * y_c)
    acc_ref[...] += _dot(mix.astype(BF16), wmix_ref[...])

    @pl.when(j == pl.num_programs(1) - 1)
    def _():
        out_ref[...] = x_ref[...] + acc_ref[...]


def branch_mix(x, ys, gy, o, p, wssd, wa, wb, whg, wmix, layer, bm=512, bn=512):
    m, d = x.shape
    nj = d // bn
    return pl.pallas_call(
        _mix_body,
        grid=(m // bm, nj),
        in_specs=[pl.BlockSpec((bm, d), lambda i, j: (i, 0)),
                  pl.BlockSpec((bm, ys.shape[1]), lambda i, j: (i, 0)),
                  pl.BlockSpec((bm, gy.shape[1]), lambda i, j: (i, 0)),
                  pl.BlockSpec((bm, o.shape[1]), lambda i, j: (i, 0)),
                  pl.BlockSpec((bm, bn), lambda i, j: (i, j)),
                  pl.BlockSpec((bm, bn), lambda i, j: (i, nj + j)),
                  pl.BlockSpec((bm, bn), lambda i, j: (i, 2 * nj + j)),
                  pl.BlockSpec((None, wssd.shape[1], bn), lambda i, j: (layer, 0, j)),
                  pl.BlockSpec((None, wa.shape[1], bn), lambda i, j: (layer, 0, j)),
                  pl.BlockSpec((None, wb.shape[1], bn), lambda i, j: (layer, 0, j)),
                  pl.BlockSpec((None, whg.shape[1], bn), lambda i, j: (layer, 0, j)),
                  pl.BlockSpec((None, bn, d), lambda i, j: (layer, j, 0))],
        out_specs=pl.BlockSpec((bm, d), lambda i, j: (i, 0)),
        out_shape=jax.ShapeDtypeStruct((m, d), F32),
        scratch_shapes=[pltpu.VMEM((bm, d), F32)],
        compiler_params=_cparams(("parallel", "arbitrary")),
        name="branch_mix",
    )(x, ys, gy, o, p, p, p, wssd, wa, wb, whg, wmix)


def _attend(q, k_head, v_head):
    outs = []
    for h in range(XA_HEADS):
        sl = slice(h * XA_HEAD_DIM, (h + 1) * XA_HEAD_DIM)
        s = _dot_nt(q[:, sl], k_head(h).astype(BF16)) * (XA_HEAD_DIM ** -0.5)
        s = s - jnp.max(s, axis=-1, keepdims=True)
        e = jnp.exp(s)
        p = e / jnp.sum(e, axis=-1, keepdims=True)
        outs.append(_dot(p.astype(BF16), v_head(h).astype(BF16)))
    return jnp.concatenate(outs, axis=-1)


def _xattn_prompt_body(q_ref, kv_ref, o_ref):
    k_head = lambda h: kv_ref[:, h * XA_HEAD_DIM:(h + 1) * XA_HEAD_DIM]
    v_head = lambda h: kv_ref[:, D_MODEL + h * XA_HEAD_DIM:D_MODEL + (h + 1) * XA_HEAD_DIM]
    o_ref[...] = _attend(q_ref[...], k_head, v_head).astype(BF16)


def xattn_prompt(q, kv, n_seq, seq_len, bl=512):
    nl = seq_len // bl
    return pl.pallas_call(
        _xattn_prompt_body,
        grid=(n_seq, nl),
        in_specs=[pl.BlockSpec((bl, D_MODEL), lambda b, i: (b * nl + i, 0)),
                  pl.BlockSpec((MEM_LEN, 2 * D_MODEL), lambda b, i: (b, 0))],
        out_specs=pl.BlockSpec((bl, D_MODEL), lambda b, i: (b * nl + i, 0)),
        out_shape=jax.ShapeDtypeStruct(q.shape, BF16),
        compiler_params=_cparams(("parallel", "arbitrary")),
        name="xattn_prompt",
    )(q, kv)


def _xattn_sample_body(q_ref, k_ref, v_ref, buf_ref, o_ref, *, seq_len):
    del buf_ref
    q = q_ref[...]
    r = 2 * seq_len
    nc = XA_HEAD_DIM // LANES
    piece = lambda ref, b, h, c: ref[b, pl.ds(c * XA_HEADS + h, MEM_LEN, stride=nc * XA_HEADS), :].astype(BF16)
    scores = []
    for b in range(2):
        for h in range(XA_HEADS):
            s = None
            for c in range(nc):
                lo = h * XA_HEAD_DIM + c * LANES
                d = _dot_nt(q[:, lo:lo + LANES], piece(k_ref, b, h, c))
                s = d if s is None else s + d
            scores.append(s)
    s = jnp.concatenate(scores, axis=0) * (XA_HEAD_DIM ** -0.5)
    s = s - jnp.max(s, axis=-1, keepdims=True)
    e = jnp.exp(s)
    p = (e / jnp.sum(e, axis=-1, keepdims=True)).astype(BF16)
    rows = lax.broadcasted_iota(jnp.int32, (r, LANES), 0)
    for h in range(XA_HEADS):
        for c in range(nc):
            lo = h * XA_HEAD_DIM + c * LANES
            o0 = _dot(p[h * r:(h + 1) * r], piece(v_ref, 0, h, c))
            o1 = _dot(p[(XA_HEADS + h) * r:(XA_HEADS + h + 1) * r], piece(v_ref, 1, h, c))
            o_ref[:, lo:lo + LANES] = jnp.where(rows < seq_len, o0, o1).astype(BF16)


def xattn_sample(q, k, v, buf, layer, row0, n_seq, seq_len):
    r = 2 * seq_len
    blk0 = row0 // r
    mem_spec = pl.BlockSpec((None, 2, MEM_LEN * D_MODEL // LANES, LANES), lambda i: (layer, i, 0, 0))
    return pl.pallas_call(
        functools.partial(_xattn_sample_body, seq_len=seq_len),
        grid=(n_seq // 2,),
        in_specs=[pl.BlockSpec((r, D_MODEL), lambda i: (blk0 + i, 0)), mem_spec, mem_spec,
                  pl.BlockSpec(memory_space=pl.ANY)],
        out_specs=pl.BlockSpec((r, D_MODEL), lambda i: (blk0 + i, 0)),
        out_shape=jax.ShapeDtypeStruct(buf.shape, BF16),
        input_output_aliases={3: 0},
        compiler_params=_cparams(("parallel",)),
        name="xattn_sample",
    )(q, k, v, buf)


LANES = 128
HALO = 8


def _softplus(x):
    return jnp.maximum(x, 0.0) + jnp.log1p(jnp.exp(-jnp.abs(x)))


def _pad_rows(x, rows):
    if x.shape[0] == rows:
        return x
    return jnp.concatenate([x, jnp.zeros((rows - x.shape[0], x.shape[1]), x.dtype)], axis=0)


def _tile_consts(r, t):
    i = np.arange(r)[:, None]
    j = np.arange(r)[None, :]
    same = (i // t) == (j // t)
    lt = (same & (j <= i)).astype(np.float32)
    last = (same & (j % t == t - 1)).astype(np.float32)
    return jnp.asarray(lt, BF16), jnp.asarray(last, BF16)


def _head_expand(n_heads, width, rows=LANES):
    e = np.zeros((rows, n_heads * width), np.float32)
    for h in range(n_heads):
        e[h, h * width:(h + 1) * width] = 1.0
    return jnp.asarray(e, BF16)


def _ssd_tile(xc, dt_raw, dtb, a_log, lt, last, e, e128, r, t):
    xs = xc[:, :SSD_WIDTH]
    dt = _softplus(dt_raw + dtb)
    d_a = dt * (-jnp.exp(a_log))
    a = _sel_dot(lt, d_a, 3)
    a_e = _dot_sel(a, e, 3)
    dt_e = _dot_sel(dt, e, 2)
    alast_e = _sel_dot(last, a_e, 3)
    a_col = _dot_sel(a, e128, 3)
    a_t = _pad_rows(a, LANES).T

    row = lax.broadcasted_iota(jnp.int32, (r, LANES), 0)
    col = lax.broadcasted_iota(jnp.int32, (r, LANES), 1)
    valid = (col <= row) & (col >= (row // t) * t)
    lane_lo = col < SSD_HEAD_DIM

    xdt = _pad_rows((xs * dt_e).astype(BF16), LANES)
    hpg = SSD_HEADS // SSD_GROUPS
    ys = []
    for g in range(SSD_GROUPS):
        bg = xc[:, SSD_WIDTH + g * SSD_STATE:SSD_WIDTH + (g + 1) * SSD_STATE].astype(BF16)
        cg = xc[:, SSD_WIDTH + (SSD_GROUPS + g) * SSD_STATE:
                SSD_WIDTH + (SSD_GROUPS + g + 1) * SSD_STATE].astype(BF16)
        cb = _dot_nt(cg, _pad_rows(bg, LANES))
        for hp in range(hpg // 2):
            h0 = g * hpg + 2 * hp
            res = []
            for h in (h0, h0 + 1):
                rel = a_col[:, h * LANES:(h + 1) * LANES] - a_t[h:h + 1, :]
                dec = jnp.where(valid, jnp.exp(jnp.where(valid, rel, 0.0)), 0.0)
                res.append(_dot((cb * dec).astype(BF16), xdt[:, h0 * SSD_HEAD_DIM:(h0 + 2) * SSD_HEAD_DIM]))
            ys.append(jnp.where(lane_lo, res[0], res[1]))
    y_intra = jnp.concatenate(ys, axis=-1)
    return xs, y_intra, a_e, dt_e, alast_e


def _ssd_finish(y, xs, z, dsk, ng):
    y = y + dsk * xs
    y = y * _silu(z)
    return (y * _rms_scale(y) * ng).astype(BF16)


def _conv_silu(ext_ref, cw_ref, cb_ref, base, r):
    acc = cb_ref[...] + cw_ref[SSD_CONV - 1:SSD_CONV, :] * ext_ref[pl.ds(base, r), :]
    for k in range(1, SSD_CONV):
        acc = acc + cw_ref[SSD_CONV - 1 - k:SSD_CONV - k, :] * ext_ref[pl.ds(base - k, r), :]
    return _silu(acc)


def _ssd_prompt_body(xbc_ref, z_ref, dt_ref, cw_ref, cb_ref, dtb_ref, alog_ref, dsk_ref, ng_ref,
                     lt_ref, last_ref, e_ref, e128_ref, y_ref, sout_ref, ext_ref, st_ref, *, r):
    c = pl.program_id(1)

    @pl.when(c == 0)
    def _():
        ext_ref[0:HALO, :] = jnp.zeros((HALO, SSD_CONV_DIM), F32)
        st_ref[...] = jnp.zeros_like(st_ref)

    ext_ref[HALO:HALO + r, :] = xbc_ref[...]
    xc = _conv_silu(ext_ref, cw_ref, cb_ref, HALO, r)
    ext_ref[0:HALO, :] = xbc_ref[r - HALO:r, :]

    xs, y, a_e, dt_e, alast_e = _ssd_tile(xc, dt_ref[...], dtb_ref[...], alog_ref[...], lt_ref[...],
                                          last_ref[...], e_ref[...], e128_ref[...], r, r)
    ea_e = jnp.exp(a_e)
    xw = (xs * (dt_e * jnp.exp(alast_e - a_e))).astype(BF16)
    sdec = jnp.exp(alast_e[0:1, :])
    gw = SSD_WIDTH // SSD_GROUPS
    inter = []
    for g in range(SSD_GROUPS):
        bg = xc[:, SSD_WIDTH + g * SSD_STATE:SSD_WIDTH + (g + 1) * SSD_STATE].astype(BF16)
        cg = xc[:, SSD_WIDTH + (SSD_GROUPS + g) * SSD_STATE:
                SSD_WIDTH + (SSD_GROUPS + g + 1) * SSD_STATE].astype(BF16)
        st = st_ref[:, g * gw:(g + 1) * gw]
        inter.append(_dot(cg, st.astype(BF16)))
        st_ref[:, g * gw:(g + 1) * gw] = st * sdec[:, g * gw:(g + 1) * gw] + _dot_tn(bg, xw[:, g * gw:(g + 1) * gw])
    y = y + jnp.concatenate(inter, axis=-1) * ea_e
    y_ref[...] = _ssd_finish(y, xs, z_ref[...], dsk_ref[...], ng_ref[...])

    @pl.when(c == pl.num_programs(1) - 1)
    def _():
        sout_ref[0] = st_ref[...].T


def _ssd_consts(r, t):
    lt, last = _tile_consts(r, t)
    return lt, last, _head_expand(SSD_HEADS, SSD_HEAD_DIM), _head_expand(SSD_HEADS, LANES)


def ssd_prompt(p, dt, prm, n_seq, seq_len, r=128):
    nc = seq_len // r
    consts = _ssd_consts(r, r)
    full = lambda a: pl.BlockSpec(a.shape, lambda b, c: (0,) * a.ndim)
    small = [prm["conv_w"], prm["conv_b"], prm["dt_bias"], prm["a_log"], prm["d_skip"], prm["norm"], *consts]
    return pl.pallas_call(
        functools.partial(_ssd_prompt_body, r=r),
        grid=(n_seq, nc),
        in_specs=[pl.BlockSpec((r, SSD_CONV_DIM), lambda b, c: (b * nc + c, P_XBC // SSD_CONV_DIM)),
                  pl.BlockSpec((r, SSD_WIDTH), lambda b, c: (b * nc + c, P_Z // SSD_WIDTH)),
                  pl.BlockSpec((r, LANES), lambda b, c: (b * nc + c, 0)),
                  *[full(a) for a in small]],
        out_specs=[pl.BlockSpec((r, SSD_WIDTH), lambda b, c: (b * nc + c, 0)),
                   pl.BlockSpec((1, SSD_WIDTH, SSD_STATE), lambda b, c: (b, 0, 0))],
        out_shape=[jax.ShapeDtypeStruct((p.shape[0], SSD_WIDTH), BF16),
                   jax.ShapeDtypeStruct((n_seq, SSD_WIDTH, SSD_STATE), F32)],
        scratch_shapes=[pltpu.VMEM((HALO + r, SSD_CONV_DIM), F32), pltpu.VMEM((SSD_STATE, SSD_WIDTH), F32)],
        compiler_params=_cparams(("parallel", "arbitrary")),
        name="ssd_prompt",
    )(p, p, dt, *small)


def _ssd_sample_body(xbc_ref, z_ref, dt_ref, buf_ref, s0_ref, cw_ref, cb_ref, dtb_ref, alog_ref, dsk_ref,
                     ng_ref, lt_ref, last_ref, e_ref, e128_ref, *rest, r, t):
    y_ref, sout_ref, ext_ref = rest[-3:]
    nb = r // t
    pitch = HALO + t
    for b in range(nb):
        ext_ref[b * pitch + HALO - (SSD_CONV - 1):b * pitch + HALO, :] = buf_ref[b]
        ext_ref[b * pitch + HALO:(b + 1) * pitch, :] = xbc_ref[b * t:(b + 1) * t, :]
    xc = jnp.concatenate([_conv_silu(ext_ref, cw_ref, cb_ref, b * pitch + HALO, t) for b in range(nb)], axis=0)

    xs, y, a_e, dt_e, alast_e = _ssd_tile(xc, dt_ref[...], dtb_ref[...], alog_ref[...], lt_ref[...],
                                          last_ref[...], e_ref[...], e128_ref[...], r, t)
    ea_e = jnp.exp(a_e)
    xw = xs * (dt_e * jnp.exp(alast_e - a_e))
    sdec = jnp.exp(alast_e)
    gw = SSD_WIDTH // SSD_GROUPS
    pr = 2 * t
    prow = lax.broadcasted_iota(jnp.int32, (pr, 1), 0)
    ones = jnp.ones((pr, SSD_STATE), BF16)
    inter = []
    for g in range(SSD_GROUPS):
        bg = xc[:, SSD_WIDTH + g * SSD_STATE:SSD_WIDTH + (g + 1) * SSD_STATE].astype(BF16)
        cg = xc[:, SSD_WIDTH + (SSD_GROUPS + g) * SSD_STATE:
                SSD_WIDTH + (SSD_GROUPS + g + 1) * SSD_STATE].astype(BF16)
        cols = slice(g * gw, (g + 1) * gw)
        rows_out = []
        for q in range(nb // 2):
            rs = slice(q * pr, (q + 1) * pr)
            acc = None
            for s in range(2):
                b = 2 * q + s
                mine = (prow >= s * t) & (prow < (s + 1) * t)
                s0 = s0_ref[b, cols, :]
                yi = _dot_nt(cg[rs], s0.astype(BF16))
                acc = jnp.where(mine, yi, 0.0) if acc is None else acc + jnp.where(mine, yi, 0.0)
                upd = _dot_tn(jnp.where(mine, xw[rs, cols], 0.0).astype(BF16), bg[rs])
                lastrow = prow == (s + 1) * t - 1
                dh, dl = _split2(jnp.where(lastrow, sdec[rs, cols], 0.0))
                dcol = _dot_tn(dh, ones) + _dot_tn(dl, ones)
                sout_ref[b, cols, :] = s0 * dcol + upd
            rows_out.append(acc)
        inter.append(jnp.concatenate(rows_out, axis=0))
    y = y + jnp.concatenate(inter, axis=-1) * ea_e
    y_ref[...] = _ssd_finish(y, xs, z_ref[...], dsk_ref[...], ng_ref[...])


def ssd_sample(p, dt, conv_buf, s0, ybuf, sprev, prm, layer, row0, n_seq, seq_len, r=64):
    nb = r // seq_len
    blk0 = row0 // r
    consts = _ssd_consts(r, seq_len)
    full = lambda a: pl.BlockSpec(a.shape, lambda i: (0,) * a.ndim)
    small = [prm["conv_w"], prm["conv_b"], prm["dt_bias"], prm["a_log"], prm["d_skip"], prm["norm"], *consts]
    inplace = [ybuf] if sprev is None else [ybuf, sprev]
    n_in = 5 + len(small)
    return pl.pallas_call(
        functools.partial(_ssd_sample_body, r=r, t=seq_len),
        grid=(n_seq // nb,),
        in_specs=[pl.BlockSpec((r, SSD_CONV_DIM), lambda i: (blk0 + i, P_XBC // SSD_CONV_DIM)),
                  pl.BlockSpec((r, SSD_WIDTH), lambda i: (blk0 + i, P_Z // SSD_WIDTH)),
                  pl.BlockSpec((r, LANES), lambda i: (blk0 + i, 0)),
                  pl.BlockSpec((None, nb, SSD_CONV - 1, SSD_CONV_DIM), lambda i: (layer, i, 0, 0)),
                  pl.BlockSpec((None, nb, SSD_WIDTH, SSD_STATE), lambda i: (layer, i, 0, 0)),
                  *[full(a) for a in small],
                  *[pl.BlockSpec(memory_space=pl.ANY) for _ in inplace]],
        out_specs=[pl.BlockSpec((r, SSD_WIDTH), lambda i: (blk0 + i, 0)),
                   pl.BlockSpec((None, nb, SSD_WIDTH, SSD_STATE), lambda i: (layer, i, 0, 0))],
        out_shape=[jax.ShapeDtypeStruct(ybuf.shape, BF16),
                   jax.ShapeDtypeStruct(s0.shape, F32)],
        input_output_aliases={n_in + k: k for k in range(len(inplace))},
        scratch_shapes=[pltpu.VMEM((nb * (HALO + seq_len), SSD_CONV_DIM), F32)],
        compiler_params=_cparams(("parallel",)),
        name="ssd_sample",
    )(p, p, dt, conv_buf, s0, *small, *inplace)


S5_CH = S5_GROUPS * S5_STATE
S5_BLK = 4
S5_SSEQ = 32


def _s5_in(u, bre_ref, bim_ref):
    ub = u.astype(BF16)
    kin = S5_WIDTH // S5_BLK
    re = [_dot(ub[:, q * kin:(q + 1) * kin], bre_ref[q]) for q in range(S5_BLK)]
    im = [_dot(ub[:, q * kin:(q + 1) * kin], bim_ref[q]) for q in range(S5_BLK)]
    return jnp.concatenate(re, axis=-1), jnp.concatenate(im, axis=-1)


def _s5_out(h_re, h_im, u, cre_ref, cim_ref, dsk):
    kst = S5_CH // S5_BLK
    hr = h_re.astype(BF16)
    hi = h_im.astype(BF16)
    y = [_dot(hr[:, q * kst:(q + 1) * kst], cre_ref[q]) - _dot(hi[:, q * kst:(q + 1) * kst], cim_ref[q])
         for q in range(S5_BLK)]
    y = jnp.concatenate(y, axis=-1) + dsk * u
    return jax.nn.gelu(y).astype(BF16)


def _s5_scan_body(u_ref, h0re_ref, h0im_ref, are_ref, aim_ref, bre_ref, bim_ref, cre_ref, cim_ref, dsk_ref,
                  y_ref, hre_ref, him_ref, sre_ref, sim_ref, *, s, tc):
    c = pl.program_id(1)

    @pl.when(c == 0)
    def _():
        sre_ref[0:s, :] = h0re_ref[...]
        sim_ref[0:s, :] = h0im_ref[...]

    u = u_ref[...]
    bu_re, bu_im = _s5_in(u, bre_ref, bim_ref)
    sre_ref[s:, :] = bu_re
    sim_ref[s:, :] = bu_im
    a_re = jnp.broadcast_to(are_ref[...], (s, S5_CH))
    a_im = jnp.broadcast_to(aim_ref[...], (s, S5_CH))

    def step(t, carry):
        prev = pl.ds(pl.multiple_of(t * s, s), s)
        cur = pl.ds(pl.multiple_of((t + 1) * s, s), s)
        h_re = sre_ref[prev, :]
        h_im = sim_ref[prev, :]
        sre_ref[cur, :] = a_re * h_re - a_im * h_im + sre_ref[cur, :]
        sim_ref[cur, :] = a_re * h_im + a_im * h_re + sim_ref[cur, :]
        return carry

    lax.fori_loop(0, tc, step, 0, unroll=4)
    h_re = sre_ref[tc * s:, :]
    h_im = sim_ref[tc * s:, :]
    y_ref[...] = _s5_out(sre_ref[s:, :], sim_ref[s:, :], u, cre_ref, cim_ref, dsk_ref[...])
    sre_ref[0:s, :] = h_re
    sim_ref[0:s, :] = h_im
    hre_ref[...] = h_re
    him_ref[...] = h_im


def s5_scan(u, h0_re, h0_im, prm, n_blocks, s, n_steps, tc):
    nc = n_steps // tc
    r = tc * s
    full = lambda a: pl.BlockSpec(a.shape, lambda b, c: (0,) * a.ndim)
    small = [prm["abar_re"], prm["abar_im"], prm["b_re"], prm["b_im"], prm["c_re"], prm["c_im"], prm["d_skip"]]
    state = jax.ShapeDtypeStruct((n_blocks * s, S5_CH), F32)
    sspec = pl.BlockSpec((s, S5_CH), lambda b, c: (b, 0))
    return pl.pallas_call(
        functools.partial(_s5_scan_body, s=s, tc=tc),
        grid=(n_blocks, nc),
        in_specs=[pl.BlockSpec((r, S5_WIDTH), lambda b, c: (b * nc + c, 0)), sspec, sspec,
                  *[full(a) for a in small]],
        out_specs=[pl.BlockSpec((r, S5_WIDTH), lambda b, c: (b * nc + c, 0)), sspec, sspec],
        out_shape=[jax.ShapeDtypeStruct((u.shape[0], S5_WIDTH), BF16), state, state],
        scratch_shapes=[pltpu.VMEM((s + r, S5_CH), F32), pltpu.VMEM((s + r, S5_CH), F32)],
        compiler_params=_cparams(("parallel", "arbitrary")),
        name="s5_scan",
    )(u, h0_re, h0_im, *small)


S5_T = 16
S5_GB = LANES // S5_GROUP_SIZE
S5_CL = S5_T * LANES
S5_SL = S5_GB * S5_STATE
S5_NB = 2
S5_PITCH_PAD = 8


def _s5_chunk_body(u_ref, tp_ref, bre_ref, bim_ref, cre_ref, cim_ref, are_ref, aim_ref, dsk_ref,
                   y_ref, hre_ref, him_ref, vre_ref, vim_ref, sre_ref, sim_ref, *, n_chunks):
    rows = S5_NB * n_chunks
    pitch = n_chunks + S5_PITCH_PAD
    nk = S5_SL // LANES
    u = jnp.concatenate([u_ref[pl.ds(t, rows, stride=S5_T), :] for t in range(S5_T)], axis=-1)
    ub = u.astype(BF16)
    v_re = _dot(ub, bre_ref[0])
    v_im = _dot(ub, bim_ref[0])
    for k in range(nk):
        vre_ref[k] = v_re[:, k * LANES:(k + 1) * LANES]
        vim_ref[k] = v_im[:, k * LANES:(k + 1) * LANES]
        sre_ref[k] = jnp.zeros((S5_NB * pitch, LANES), F32)
        sim_ref[k] = jnp.zeros((S5_NB * pitch, LANES), F32)
    a_re = [jnp.broadcast_to(are_ref[0, :, k * LANES:(k + 1) * LANES], (S5_NB, LANES)) for k in range(nk)]
    a_im = [jnp.broadcast_to(aim_ref[0, :, k * LANES:(k + 1) * LANES], (S5_NB, LANES)) for k in range(nk)]

    def step(c, carry):
        for k in range(nk):
            h_re = sre_ref[k, pl.ds(c, S5_NB, stride=pitch), :]
            h_im = sim_ref[k, pl.ds(c, S5_NB, stride=pitch), :]
            x_re = vre_ref[k, pl.ds(c, S5_NB, stride=n_chunks), :]
            x_im = vim_ref[k, pl.ds(c, S5_NB, stride=n_chunks), :]
            sre_ref[k, pl.ds(c + 1, S5_NB, stride=pitch), :] = a_re[k] * h_re - a_im[k] * h_im + x_re
            sim_ref[k, pl.ds(c + 1, S5_NB, stride=pitch), :] = a_re[k] * h_im + a_im[k] * h_re + x_im
        return carry

    lax.fori_loop(0, n_chunks, step, 0, unroll=2)
    starts = lambda ref: jnp.concatenate(
        [jnp.concatenate([ref[k, b * pitch:b * pitch + n_chunks, :] for b in range(S5_NB)], axis=0)
         for k in range(nk)], axis=-1)
    y = _dot(ub, tp_ref[0])
    y = y + _dot(starts(sre_ref).astype(BF16), cre_ref[0]) - _dot(starts(sim_ref).astype(BF16), cim_ref[0])
    y = jax.nn.gelu(y + dsk_ref[0] * u)
    for t in range(S5_T):
        y_ref[pl.ds(t, rows, stride=S5_T), :] = y[:, t * LANES:(t + 1) * LANES]
    final = lambda ref: jnp.concatenate(
        [jnp.concatenate([ref[k, b * pitch + n_chunks:b * pitch + n_chunks + 1, :] for b in range(S5_NB)], axis=0)
         for k in range(nk)], axis=-1)
    hre_ref[...] = final(sre_ref)
    him_ref[...] = final(sim_ref)


def s5_chunked(p, prm, n_seq, seq_len):
    n_chunks = seq_len // S5_T
    rows = S5_NB * n_chunks
    nblk = S5_GROUPS // S5_GB
    nhalf = n_seq // S5_NB
    pitch = n_chunks + S5_PITCH_PAD
    nk = S5_SL // LANES
    blk = lambda a: pl.BlockSpec((1,) + a.shape[1:], lambda j, i: (j,) + (0,) * (a.ndim - 1))
    small = [prm["toep"], prm["bst_re"], prm["bst_im"], prm["cst_re"], prm["cst_im"], prm["apow_re"],
             prm["apow_im"], prm["d_tiled"]]
    state = jax.ShapeDtypeStruct((nhalf, S5_NB, S5_CH), F32)
    sspec = pl.BlockSpec((None, S5_NB, S5_SL), lambda j, i: (i, 0, j))
    return pl.pallas_call(
        functools.partial(_s5_chunk_body, n_chunks=n_chunks),
        grid=(nblk, nhalf),
        in_specs=[pl.BlockSpec((S5_NB * seq_len, LANES), lambda j, i: (i, P_U // LANES + j)),
                  *[blk(a) for a in small]],
        out_specs=[pl.BlockSpec((S5_NB * seq_len, LANES), lambda j, i: (i, j)), sspec, sspec],
        out_shape=[jax.ShapeDtypeStruct((p.shape[0], S5_WIDTH), F32), state, state],
        scratch_shapes=[pltpu.VMEM((nk, rows, LANES), F32), pltpu.VMEM((nk, rows, LANES), F32),
                        pltpu.VMEM((nk, S5_NB * pitch, LANES), F32), pltpu.VMEM((nk, S5_NB * pitch, LANES), F32)],
        compiler_params=_cparams(("parallel", "arbitrary")),
        name="s5_chunked",
    )(p, *small)


def _hg_consts(r, t):
    i = np.arange(r)[:, None]
    j = np.arange(r)[None, :]
    sums, upper, pair = [], [], []
    s = 1
    while s < t:
        blk_i, blk_j = i // (2 * s), j // (2 * s)
        up_i = (i % (2 * s)) >= s
        mid_i = blk_i * 2 * s + s
        m_up = up_i & (j >= mid_i) & (j <= i)
        m_lo = (~up_i) & (j > i) & (j < mid_i)
        sums.append((m_up | m_lo).astype(np.float32))
        upper.append(np.broadcast_to(up_i, (r, 1)).astype(np.float32))
        pair.append(((blk_i == blk_j) & up_i & ((j % (2 * s)) < s)).astype(np.float32))
        s *= 2
    pair.append((i == j).astype(np.float32))
    return (jnp.asarray(np.stack(sums), BF16), jnp.asarray(np.stack(upper), F32),
            jnp.asarray(np.stack(pair), F32))


def _hg_gates(hf, lb):
    logf = -_softplus(-hf) + jnp.log1p(lb * jnp.exp(-hf))
    kk = (1.0 - lb) * jax.nn.sigmoid(-hf)
    return logf, kk


def _hg_intra(q, kk, v, logf, sums_ref, upper_ref, pair_ref):
    r = q.shape[0]
    nlev = sums_ref.shape[0]
    lf3 = _split3(logf)
    qb = q.astype(BF16)
    kb = kk.astype(BF16)
    vb = _pad_rows(v.astype(BF16), LANES) if r < LANES else v.astype(BF16)
    scores = [None] * HG_HEADS
    for lev in range(nlev + 1):
        if lev < nlev:
            m = sums_ref[lev]
            d = _dot(m, lf3[0]) + _dot(m, lf3[1]) + _dot(m, lf3[2])
            x = (jnp.where(upper_ref[lev] > 0.5, q, kk) * jnp.exp(d)).astype(BF16)
            xq, xk = x, x
        else:
            xq, xk = qb, kb
        mask = pair_ref[lev]
        for h in range(HG_HEADS):
            sl = slice(h * HG_KEY_DIM, (h + 1) * HG_KEY_DIM)
            sc = _dot_nt(xq[:, sl], xk[:, sl]) * mask
            scores[h] = sc if scores[h] is None else scores[h] + sc
    outs = [_dot(scores[h].astype(BF16), vb[:, h * HG_VAL_DIM:(h + 1) * HG_VAL_DIM]) for h in range(HG_HEADS)]
    return jnp.concatenate(outs, axis=-1)


def _hg_finish(o, hgate, ng):
    outs = []
    for h in range(HG_HEADS):
        oh = o[:, h * HG_VAL_DIM:(h + 1) * HG_VAL_DIM]
        outs.append(oh * _rms_scale(oh) * ng)
    return (jnp.concatenate(outs, axis=-1) * _silu(hgate)).astype(BF16)


def _hg_prompt_body(q_ref, f_ref, i_ref, gate_ref, lb_ref, ng_ref, lt_ref, last_ref, sums_ref, upper_ref,
                    pair_ref, o_ref, sout_ref, *, r):
    c = pl.program_id(1)

    @pl.when(c == 0)
    def _():
        sout_ref[...] = jnp.zeros_like(sout_ref)

    q = q_ref[...]
    v = i_ref[...]
    logf, kk = _hg_gates(f_ref[...], lb_ref[...])
    o = _hg_intra(q, kk, v, logf, sums_ref, upper_ref, pair_ref)
    b = _sel_dot(lt_ref[...], logf, 3)
    blast = b[r - 1:r, :]
    qe = (q * jnp.exp(b)).astype(BF16)
    kw = (kk * jnp.exp(blast - b)).astype(BF16)
    vb = v.astype(BF16)
    ones = jnp.ones((r, HG_VAL_DIM), BF16)
    rows = lax.broadcasted_iota(jnp.int32, (r, 1), 0)
    dh, dl = _split2(jnp.where(rows == r - 1, jnp.exp(b), 0.0))
    inter = []
    for h in range(HG_HEADS):
        sl = slice(h * HG_KEY_DIM, (h + 1) * HG_KEY_DIM)
        s = sout_ref[0, h]
        inter.append(_dot(qe[:, sl], s.astype(BF16)))
        dcol = _dot_tn(dh[:, sl], ones) + _dot_tn(dl[:, sl], ones)
        sout_ref[0, h] = s * dcol + _dot_tn(kw[:, sl], vb[:, sl])
    o = o + jnp.concatenate(inter, axis=-1)
    o_ref[...] = _hg_finish(o, gate_ref[...], ng_ref[...])


def hg_prompt(p, prm, n_seq, seq_len, r=128):
    nc = seq_len // r
    lt, last = _tile_consts(r, r)
    consts = [lt, last, *_hg_consts(r, r)]
    full = lambda a: pl.BlockSpec(a.shape, lambda b, c: (0,) * a.ndim)
    small = [prm["lb"], prm["norm"], *consts]
    col = lambda off: pl.BlockSpec((r, HG_WIDTH), lambda b, c: (b * nc + c, off // HG_WIDTH))
    return pl.pallas_call(
        functools.partial(_hg_prompt_body, r=r),
        grid=(n_seq, nc),
        in_specs=[col(P_HQ), col(P_HF), col(P_HI), col(P_HGATE), *[full(a) for a in small]],
        out_specs=[pl.BlockSpec((r, HG_WIDTH), lambda b, c: (b * nc + c, 0)),
                   pl.BlockSpec((1, HG_HEADS, HG_KEY_DIM, HG_VAL_DIM), lambda b, c: (b, 0, 0, 0))],
        out_shape=[jax.ShapeDtypeStruct((p.shape[0], HG_WIDTH), BF16),
                   jax.ShapeDtypeStruct((n_seq, HG_HEADS, HG_KEY_DIM, HG_VAL_DIM), F32)],
        compiler_params=_cparams(("parallel", "arbitrary")),
        name="hg_prompt",
    )(p, p, p, p, *small)


def _hg_sample_body(q_ref, f_ref, i_ref, gate_ref, s0_ref, lb_ref, ng_ref, lt_ref, last_ref, sums_ref,
                    upper_ref, pair_ref, *rest, r, t):
    o_ref, sout_ref = rest[-2:]
    q = q_ref[...]
    v = i_ref[...]
    logf, kk = _hg_gates(f_ref[...], lb_ref[...])
    o = _hg_intra(q, kk, v, logf, sums_ref, upper_ref, pair_ref)
    b = _sel_dot(lt_ref[...], logf, 3)
    blast = _sel_dot(last_ref[...], b, 3)
    qe = (q * jnp.exp(b)).astype(BF16)
    kw = kk * jnp.exp(blast - b)
    sdec = jnp.exp(blast)
    vb = v.astype(BF16)
    pr = 2 * t
    prow = lax.broadcasted_iota(jnp.int32, (pr, 1), 0)
    ones = jnp.ones((pr, HG_VAL_DIM), BF16)
    rows_out = []
    for p2 in range(r // pr):
        rs = slice(p2 * pr, (p2 + 1) * pr)
        heads = []
        for h in range(HG_HEADS):
            sl = slice(h * HG_KEY_DIM, (h + 1) * HG_KEY_DIM)
            acc = None
            for s in range(2):
                bi = 2 * p2 + s
                mine = (prow >= s * t) & (prow < (s + 1) * t)
                s0 = s0_ref[bi, h]
                oi = jnp.where(mine, _dot(qe[rs, sl], s0.astype(BF16)), 0.0)
                acc = oi if acc is None else acc + oi
                upd = _dot_tn(jnp.where(mine, kw[rs, sl], 0.0).astype(BF16), vb[rs, sl])
                dh, dl = _split2(jnp.where(prow == (s + 1) * t - 1, sdec[rs, sl], 0.0))
                dcol = _dot_tn(dh, ones) + _dot_tn(dl, ones)
                sout_ref[bi, h] = s0 * dcol + upd
            heads.append(acc)
        rows_out.append(jnp.concatenate(heads, axis=-1))
    o = o + jnp.concatenate(rows_out, axis=0)
    o_ref[...] = _hg_finish(o, gate_ref[...], ng_ref[...])


def hg_sample(p, s0, obuf, sprev, prm, layer, row0, n_seq, seq_len, r=128):
    nb = r // seq_len
    blk0 = row0 // r
    lt, last = _tile_consts(r, seq_len)
    consts = [lt, last, *_hg_consts(r, seq_len)]
    full = lambda a: pl.BlockSpec(a.shape, lambda i: (0,) * a.ndim)
    small = [prm["lb"], prm["norm"], *consts]
    col = lambda off: pl.BlockSpec((r, HG_WIDTH), lambda i: (blk0 + i, off // HG_WIDTH))
    sspec = pl.BlockSpec((None, nb, HG_HEADS, HG_KEY_DIM, HG_VAL_DIM), lambda i: (layer, i, 0, 0, 0))
    inplace = [obuf] if sprev is None else [obuf, sprev]
    n_in = 5 + len(small)
    return pl.pallas_call(
        functools.partial(_hg_sample_body, r=r, t=seq_len),
        grid=(n_seq // nb,),
        in_specs=[col(P_HQ), col(P_HF), col(P_HI), col(P_HGATE), sspec, *[full(a) for a in small],
                  *[pl.BlockSpec(memory_space=pl.ANY) for _ in inplace]],
        out_specs=[pl.BlockSpec((r, HG_WIDTH), lambda i: (blk0 + i, 0)), sspec],
        out_shape=[jax.ShapeDtypeStruct(obuf.shape, BF16), jax.ShapeDtypeStruct(s0.shape, F32)],
        input_output_aliases={n_in + k: k for k in range(len(inplace))},
        compiler_params=_cparams(("parallel",)),
        name="hg_sample",
    )(p, p, p, p, s0, *small, *inplace)


def _s5_params(a_re, a_im, log_dt, b_re, b_im, c_re, c_im, d_skip):
    dt = jnp.exp(log_dt)[:, None]
    mag = jnp.exp(a_re * dt)
    abar_re = mag * jnp.cos(a_im * dt)
    abar_im = mag * jnp.sin(a_im * dt)
    den = a_re * a_re + a_im * a_im
    nr = abar_re - 1.0
    coef_re = (nr * a_re + abar_im * a_im) / den
    coef_im = (abar_im * a_re - nr * a_im) / den
    bbar_re = coef_re[..., None] * b_re - coef_im[..., None] * b_im
    bbar_im = coef_re[..., None] * b_im + coef_im[..., None] * b_re
    gpb = S5_GROUPS // S5_BLK
    eye = jnp.eye(gpb, dtype=F32)

    def in_blocks(bbar):
        bb = bbar.reshape(S5_BLK, gpb, S5_STATE, S5_GROUP_SIZE)
        return jnp.einsum("qgnk,gh->qgkhn", bb, eye).reshape(
            S5_BLK, gpb * S5_GROUP_SIZE, gpb * S5_STATE).astype(BF16)

    def out_blocks(c):
        cc = c.reshape(S5_BLK, gpb, S5_GROUP_SIZE, S5_STATE)
        return jnp.einsum("qgkn,gh->qgnhk", cc, eye).reshape(
            S5_BLK, gpb * S5_STATE, gpb * S5_GROUP_SIZE).astype(BF16)

    prm = {
        "abar_re": abar_re.reshape(1, S5_CH), "abar_im": abar_im.reshape(1, S5_CH),
        "b_re": in_blocks(bbar_re), "b_im": in_blocks(bbar_im),
        "c_re": out_blocks(c_re), "c_im": out_blocks(c_im),
        "d_skip": d_skip.reshape(1, S5_WIDTH),
    }

    hi = lax.Precision.HIGHEST
    tau = jnp.arange(S5_T + 1, dtype=F32)[:, None, None]
    pmag = jnp.exp(tau * (a_re * dt))
    pw_re = pmag * jnp.cos(tau * (a_im * dt))
    pw_im = pmag * jnp.sin(tau * (a_im * dt))
    cp_re = c_re[None] * pw_re[:, :, None, :] - c_im[None] * pw_im[:, :, None, :]
    cp_im = c_re[None] * pw_im[:, :, None, :] + c_im[None] * pw_re[:, :, None, :]
    kern = (jnp.einsum("tgkn,gnj->tgkj", cp_re, bbar_re, precision=hi)
            - jnp.einsum("tgkn,gnj->tgkj", cp_im, bbar_im, precision=hi))
    s_idx = jnp.arange(S5_T)[:, None]
    t_idx = jnp.arange(S5_T)[None, :]
    lag = jnp.clip(t_idx - s_idx, 0, S5_T)
    nq = S5_GROUPS // S5_GB
    ks = S5_GROUP_SIZE
    same = jnp.eye(S5_GB, dtype=F32)
    toep = jnp.where((t_idx >= s_idx)[:, :, None, None, None], kern[lag], 0.0)
    toep = toep.reshape(S5_T, S5_T, nq, S5_GB, ks, ks).transpose(2, 0, 3, 5, 1, 4)
    toep = toep[:, :, :, :, :, None, :] * same[None, None, :, None, None, :, None]
    toep = toep.reshape(nq, S5_CL, S5_CL).astype(BF16)
    rev = pw_re[S5_T - 1 - jnp.arange(S5_T)], pw_im[S5_T - 1 - jnp.arange(S5_T)]
    bst_re = rev[0][..., None] * bbar_re[None] - rev[1][..., None] * bbar_im[None]
    bst_im = rev[0][..., None] * bbar_im[None] + rev[1][..., None] * bbar_re[None]

    def bst_blocks(b):
        bb = b.reshape(S5_T, nq, S5_GB, S5_STATE, ks).transpose(1, 0, 2, 4, 3)
        bb = bb[:, :, :, :, None, :] * same[None, None, :, None, :, None]
        return bb.reshape(nq, S5_CL, S5_SL).astype(BF16)

    def cst_blocks(c):
        cc = c.reshape(S5_T, nq, S5_GB, ks, S5_STATE).transpose(1, 2, 4, 0, 3)
        cc = cc[:, :, :, :, None, :] * same[None, :, None, None, :, None]
        return cc.reshape(nq, S5_SL, S5_CL).astype(BF16)

    prm.update({
        "toep": toep,
        "bst_re": bst_blocks(bst_re), "bst_im": bst_blocks(bst_im),
        "cst_re": cst_blocks(cp_re[1:]), "cst_im": cst_blocks(cp_im[1:]),
        "apow_re": pw_re[S5_T].reshape(nq, 1, S5_SL), "apow_im": pw_im[S5_T].reshape(nq, 1, S5_SL),
        "d_tiled": jnp.broadcast_to(d_skip.reshape(nq, 1, 1, LANES), (nq, 1, S5_T, LANES)).reshape(nq, 1, S5_CL),
    })
    return prm


def _in_proj_weights(w_in):
    z0, xbc0, dt0, u0, gates0 = 0, 2048, 5120, 5152, 10272
    w = jnp.concatenate([w_in[..., gates0:], w_in[..., xbc0:dt0], w_in[..., u0:gates0], w_in[..., z0:xbc0]],
                        axis=-1)
    wdt = jnp.pad(w_in[..., dt0:u0], ((0, 0), (0, 0), (0, LANES - SSD_HEADS)))
    return w.astype(BF16), wdt.astype(BF16)


def _mem_rows(cache):
    dd, b, m, h, hd = cache.shape
    c = cache.reshape(dd, b, m, h, hd // LANES, LANES).transpose(0, 1, 2, 4, 3, 5)
    return c.reshape(dd, b, m * h * (hd // LANES), LANES)


def _pad_lanes(v):
    return jnp.pad(v.reshape(1, -1), ((0, 0), (0, LANES - v.shape[-1])))


def kernel(x_prompt, x_sample, cache_mem_k, cache_mem_v, state_ssd, state_ssd_conv, state_s5_re, state_s5_im,
           state_hgrn, mem_prompt, norm_ffn1, ffn1_w1, ffn1_w3, ffn1_w2, norm_mix, w_in, ssd_conv_w, ssd_conv_b,
           ssd_dt_bias, ssd_a_log, ssd_d, ssd_norm, ssd_w_out, s5_a_re, s5_a_im, s5_log_dt, s5_b_re, s5_b_im,
           s5_c_re, s5_c_im, s5_d, s5_w_glu_a, s5_w_glu_b, hg_lower_bounds, hg_norm, hg_w_out, w_mix_out,
           norm_xa, norm_mem, xa_wq, xa_wk, xa_wv, xa_wo, norm_ffn2, ffn2_w1, ffn2_w3, ffn2_w2, norm_final):
    bp, lp, d = x_prompt.shape
    bs, ls, _ = x_sample.shape
    mp, ms = bp * lp, bs * ls
    x = jnp.concatenate([x_prompt.reshape(mp, d), x_sample.reshape(ms, d)], axis=0)
    mem = mem_prompt.reshape(bp * MEM_LEN, d)
    row = lambda v: v.reshape(1, -1)
    bf = lambda w: w.astype(BF16)

    lb_p = jax.nn.softmax(hg_lower_bounds, axis=0)
    lb_all = jnp.cumsum(lb_p, axis=0) - lb_p[0]

    ssd_s0 = state_ssd.reshape(DEPTH, bs, SSD_WIDTH, SSD_STATE)
    mem_k = _mem_rows(cache_mem_k)
    mem_v = _mem_rows(cache_mem_v)
    ssd_states = None
    hg_states = None

    f1_w1, f1_w3, f1_w2 = bf(ffn1_w1), bf(ffn1_w3), bf(ffn1_w2)
    f2_w1, f2_w3, f2_w2 = bf(ffn2_w1), bf(ffn2_w3), bf(ffn2_w2)
    w_p, w_dt = _in_proj_weights(w_in)
    b_ssd, b_glu_a, b_glu_b, b_hg, b_mix = bf(ssd_w_out), bf(s5_w_glu_a), bf(s5_w_glu_b), bf(hg_w_out), bf(w_mix_out)
    b_wq, b_wo = bf(xa_wq), bf(xa_wo)
    b_wkv = bf(jnp.concatenate([xa_wk, xa_wv], axis=-1))

    outs = {k: [] for k in ("pk", "pv", "pss", "pcv", "psr", "psi", "phg", "scv", "ssr", "ssi")}
    for l in range(DEPTH):
        x = ffn(x, row(norm_ffn1[l]), f1_w1, f1_w3, f1_w2, row(norm_final), l, False)

        p, dt = in_proj(x, row(norm_mix[l]), w_p, w_dt, l)

        ssd_prm = {"conv_w": ssd_conv_w[l], "conv_b": row(ssd_conv_b[l]), "dt_bias": _pad_lanes(ssd_dt_bias[l]),
                   "a_log": _pad_lanes(ssd_a_log[l]), "d_skip": row(jnp.repeat(ssd_d[l], SSD_HEAD_DIM)),
                   "norm": row(ssd_norm[l])}
        ys, ss_p = ssd_prompt(p, dt, ssd_prm, bp, lp)
        ys, ssd_states = ssd_sample(p, dt, state_ssd_conv, ssd_s0, ys, ssd_states, ssd_prm, l, mp, bs, ls)

        s5_prm = _s5_params(s5_a_re[l], s5_a_im[l], s5_log_dt[l], s5_b_re[l], s5_b_im[l], s5_c_re[l],
                            s5_c_im[l], s5_d[l])
        gy, sr_p, si_p = s5_chunked(p, s5_prm, bp, lp)
        u_s = p[mp:, P_U:P_U + S5_WIDTH].reshape(bs // S5_SSEQ, S5_SSEQ, ls, S5_WIDTH).transpose(0, 2, 1, 3)
        u_s = u_s.reshape(ms, S5_WIDTH)
        gy_s, sr_s, si_s = s5_scan(u_s, state_s5_re[l].reshape(bs, S5_CH), state_s5_im[l].reshape(bs, S5_CH),
                                   s5_prm, bs // S5_SSEQ, S5_SSEQ, ls, ls)
        gy_s = gy_s.reshape(bs // S5_SSEQ, ls, S5_SSEQ, S5_WIDTH).transpose(0, 2, 1, 3).reshape(ms, S5_WIDTH)
        gy = lax.dynamic_update_slice(gy, gy_s.astype(F32), (mp, 0))

        hg_prm = {"lb": row(lb_all[l]), "norm": row(hg_norm[l])}
        o, hg_p = hg_prompt(p, hg_prm, bp, lp)
        o, hg_states = hg_sample(p, state_hgrn, o, hg_states, hg_prm, l, mp, bs, ls)

        x = branch_mix(x, ys, gy, o, p, b_ssd, b_glu_a, b_glu_b, b_hg, b_mix, l)

        q = norm_proj(x, row(norm_xa[l]), b_wq, l, BF16)
        kv = norm_proj(mem, row(norm_mem[l]), b_wkv, l, F32)
        at = xattn_prompt(q, kv, bp, lp)
        at = xattn_sample(q, mem_k, mem_v, at, l, mp, bs, ls)
        x = res_mm(x, at, b_wo, l)

        x = ffn(x, row(norm_ffn2[l]), f2_w1, f2_w3, f2_w2, row(norm_final), l, l == DEPTH - 1)

        tail = SSD_CONV - 1
        outs["pk"].append(kv[:, :d].reshape(bp, MEM_LEN, XA_HEADS, XA_HEAD_DIM))
        outs["pv"].append(kv[:, d:].reshape(bp, MEM_LEN, XA_HEADS, XA_HEAD_DIM))
        outs["pss"].append(ss_p.reshape(bp, SSD_HEADS, SSD_HEAD_DIM, SSD_STATE))
        outs["pcv"].append(jnp.stack([p[(b + 1) * lp - tail:(b + 1) * lp, P_XBC:P_XBC + SSD_CONV_DIM]
                                      for b in range(bp)]))
        outs["psr"].append(sr_p.reshape(bp, S5_GROUPS, S5_STATE))
        outs["psi"].append(si_p.reshape(bp, S5_GROUPS, S5_STATE))
        outs["phg"].append(hg_p)
        outs["scv"].append(p[mp:, P_XBC:P_XBC + SSD_CONV_DIM].reshape(bs, ls, SSD_CONV_DIM)[:, ls - tail:])
        outs["ssr"].append(sr_s.reshape(bs, S5_GROUPS, S5_STATE))
        outs["ssi"].append(si_s.reshape(bs, S5_GROUPS, S5_STATE))

    st = lambda k: jnp.stack(outs[k])
    return (x[:mp].reshape(bp, lp, d), x[mp:].reshape(bs, ls, d),
            st("pk"), st("pv"), st("pss"), st("pcv"), st("psr"), st("psi"), st("phg"),
            ssd_states.reshape(DEPTH, bs, SSD_HEADS, SSD_HEAD_DIM, SSD_STATE), st("scv"), st("ssr"), st("ssi"),
            hg_states)
```

```python
import functools
import math

import jax
import jax.numpy as jnp
import numpy as np
from jax import lax
from jax.experimental import pallas as pl
from jax.experimental.pallas import tpu as pltpu

F32 = jnp.float32
BF16 = jnp.bfloat16

D_MODEL = 2048
DEPTH = 2
NORM_EPS = 1e-5
SSD_HEAD_DIM = 64
SSD_HEADS = 32
SSD_GROUPS = 4
SSD_STATE = 128
SSD_CONV = 4
SSD_WIDTH = 2048
SSD_CONV_DIM = 3072
S5_WIDTH = 1024
S5_GROUP_SIZE = 16
S5_GROUPS = 64
S5_STATE = 64
HG_WIDTH = 1024
HG_HEADS = 8
HG_KEY_DIM = 128
HG_VAL_DIM = 128
MEM_LEN = 256
XA_HEADS = 4
XA_HEAD_DIM = 512
FFN_DIM = 5632

P_GATES, P_XBC, P_U, P_HQ, P_HF, P_HI, P_HGATE, P_Z = 0, 6144, 9216, 10240, 11264, 12288, 13312, 14336
P_WIDTH = 16384

V7X_VMEM_LIMIT = 56 * 1024 * 1024


def _cparams(sem, vmem=V7X_VMEM_LIMIT):
    return pltpu.CompilerParams(dimension_semantics=sem, vmem_limit_bytes=vmem)


def _rms_scale(x):
    return lax.rsqrt(jnp.mean(x * x, axis=-1, keepdims=True) + NORM_EPS)


def _silu(x):
    return x * jax.nn.sigmoid(x)


def _dot(a, b):
    return jnp.dot(a, b, preferred_element_type=F32)


def _dot_nt(a, b):
    return lax.dot_general(a, b, (((1,), (1,)), ((), ())), preferred_element_type=F32)


def _dot_tn(a, b):
    return lax.dot_general(a, b, (((0,), (0,)), ((), ())), preferred_element_type=F32)


def _split2(x):
    hi = x.astype(BF16)
    lo = (x - hi.astype(F32)).astype(BF16)
    return hi, lo


def _split3(x):
    hi = x.astype(BF16)
    r = x - hi.astype(F32)
    mid = r.astype(BF16)
    lo = (r - mid.astype(F32)).astype(BF16)
    return hi, mid, lo


def _sel_dot(sel, x, parts=3):
    ps = _split3(x) if parts == 3 else _split2(x)
    out = _dot(sel, ps[0])
    for p in ps[1:]:
        out = out + _dot(sel, p)
    return out


def _dot_sel(x, sel, parts=2):
    ps = _split3(x) if parts == 3 else _split2(x)
    out = _dot(ps[0], sel)
    for p in ps[1:]:
        out = out + _dot(p, sel)
    return out


def _norm_proj_body(x_ref, g_ref, w_ref, o_ref, h_ref):
    @pl.when(pl.program_id(1) == 0)
    def _():
        x = x_ref[...]
        h_ref[...] = (x * _rms_scale(x) * g_ref[...]).astype(BF16)

    o_ref[...] = _dot(h_ref[...], w_ref[...]).astype(o_ref.dtype)


def norm_proj(x, g, w, layer, out_dtype, bm=1024, bn=1024):
    m, k = x.shape
    n = w.shape[2]
    return pl.pallas_call(
        _norm_proj_body,
        grid=(m // bm, n // bn),
        in_specs=[pl.BlockSpec((bm, k), lambda i, j: (i, 0)),
                  pl.BlockSpec((1, k), lambda i, j: (0, 0)),
                  pl.BlockSpec((None, k, bn), lambda i, j: (layer, 0, j))],
        out_specs=pl.BlockSpec((bm, bn), lambda i, j: (i, j)),
        out_shape=jax.ShapeDtypeStruct((m, n), out_dtype),
        scratch_shapes=[pltpu.VMEM((bm, k), BF16)],
        compiler_params=_cparams(("parallel", "arbitrary")),
        name="norm_proj",
    )(x, g, w)


def _in_proj_body(x_ref, g_ref, w_ref, wdt_ref, o_ref, odt_ref, h_ref):
    @pl.when(pl.program_id(1) == 0)
    def _():
        x = x_ref[...]
        h = (x * _rms_scale(x) * g_ref[...]).astype(BF16)
        h_ref[...] = h
        odt_ref[...] = _dot(h, wdt_ref[...])

    o_ref[...] = _dot(h_ref[...], w_ref[...])


def in_proj(x, g, w, wdt, layer, bm=1024, bn=1024):
    m, k = x.shape
    n = w.shape[2]
    ndt = wdt.shape[2]
    return pl.pallas_call(
        _in_proj_body,
        grid=(m // bm, n // bn),
        in_specs=[pl.BlockSpec((bm, k), lambda i, j: (i, 0)),
                  pl.BlockSpec((1, k), lambda i, j: (0, 0)),
                  pl.BlockSpec((None, k, bn), lambda i, j: (layer, 0, j)),
                  pl.BlockSpec((None, k, ndt), lambda i, j: (layer, 0, 0))],
        out_specs=[pl.BlockSpec((bm, bn), lambda i, j: (i, j)),
                   pl.BlockSpec((bm, ndt), lambda i, j: (i, 0))],
        out_shape=[jax.ShapeDtypeStruct((m, n), F32), jax.ShapeDtypeStruct((m, ndt), F32)],
        scratch_shapes=[pltpu.VMEM((bm, k), BF16)],
        compiler_params=_cparams(("parallel", "arbitrary")),
        name="in_proj",
    )(x, g, w, wdt)


def _res_mm_body(x_ref, a_ref, w_ref, o_ref):
    o_ref[...] = x_ref[...] + _dot(a_ref[...], w_ref[...])


def res_mm(x, a, w, layer, bm=1024, bn=1024):
    m, n = x.shape
    k = a.shape[1]
    return pl.pallas_call(
        _res_mm_body,
        grid=(m // bm, n // bn),
        in_specs=[pl.BlockSpec((bm, bn), lambda i, j: (i, j)),
                  pl.BlockSpec((bm, k), lambda i, j: (i, 0)),
                  pl.BlockSpec((None, k, bn), lambda i, j: (layer, 0, j))],
        out_specs=pl.BlockSpec((bm, bn), lambda i, j: (i, j)),
        out_shape=jax.ShapeDtypeStruct((m, n), F32),
        compiler_params=_cparams(("parallel", "parallel")),
        name="res_mm",
    )(x, a, w)


def _ffn_body(x_ref, g_ref, w1_ref, w3_ref, w2_ref, gf_ref, o_ref, h_ref, acc_ref, *, final_norm):
    f = pl.program_id(1)

    @pl.when(f == 0)
    def _():
        x = x_ref[...]
        h_ref[...] = (x * _rms_scale(x) * g_ref[...]).astype(BF16)
        acc_ref[...] = jnp.zeros_like(acc_ref)

    h = h_ref[...]
    a = _silu(_dot(h, w1_ref[...])) * _dot(h, w3_ref[...])
    acc_ref[...] += _dot(a.astype(BF16), w2_ref[...])

    @pl.when(f == pl.num_programs(1) - 1)
    def _():
        y = x_ref[...] + 0.5 * acc_ref[...]
        if final_norm:
            y = y * _rms_scale(y) * gf_ref[...]
        o_ref[...] = y


def ffn(x, g, w1, w3, w2, gf, layer, final_norm, bm=512, bf=512):
    m, d = x.shape
    fdim = w1.shape[2]
    return pl.pallas_call(
        functools.partial(_ffn_body, final_norm=final_norm),
        grid=(m // bm, fdim // bf),
        in_specs=[pl.BlockSpec((bm, d), lambda i, f: (i, 0)),
                  pl.BlockSpec((1, d), lambda i, f: (0, 0)),
                  pl.BlockSpec((None, d, bf), lambda i, f: (layer, 0, f)),
                  pl.BlockSpec((None, d, bf), lambda i, f: (layer, 0, f)),
                  pl.BlockSpec((None, bf, d), lambda i, f: (layer, f, 0)),
                  pl.BlockSpec((1, d), lambda i, f: (0, 0))],
        out_specs=pl.BlockSpec((bm, d), lambda i, f: (i, 0)),
        out_shape=jax.ShapeDtypeStruct((m, d), F32),
        scratch_shapes=[pltpu.VMEM((bm, d), BF16), pltpu.VMEM((bm, d), F32)],
        compiler_params=_cparams(("parallel", "arbitrary")),
        name="ffn",
    )(x, g, w1, w3, w2, gf)


def _mix_body(x_ref, ys_ref, gy_ref, o_ref, ga_ref, gb_ref, gc_ref, wssd_ref, wa_ref, wb_ref, whg_ref,
              wmix_ref, out_ref, acc_ref):
    j = pl.program_id(1)

    @pl.when(j == 0)
    def _():
        acc_ref[...] = jnp.zeros_like(acc_ref)

    gy = gy_ref[...].astype(BF16)
    y_a = _dot(ys_ref[...], wssd_ref[...])
    y_b = _dot(gy, wa_ref[...]) * jax.nn.sigmoid(_dot(gy, wb_ref[...]))
    y_c = _dot(o_ref[...], whg_ref[...])
    mix = (jax.nn.sigmoid(ga_ref[...]) * y_a + jax.nn.sigmoid(gb_ref[...]) * y_b
           + jax.nn.sigmoid(gc_ref[...]) * y_c)
    acc_ref[...] += _dot(mix.astype(BF16), wmix_ref[...])

    @pl.when(j == pl.num_programs(1) - 1)
    def _():
        out_ref[...] = x_ref[...] + acc_ref[...]


def branch_mix(x, ys, gy, o, p, wssd, wa, wb, whg, wmix, layer, bm=512, bn=512):
    m, d = x.shape
    nj = d // bn
    return pl.pallas_call(
        _mix_body,
        grid=(m // bm, nj),
        in_specs=[pl.BlockSpec((bm, d), lambda i, j: (i, 0)),
                  pl.BlockSpec((bm, ys.shape[1]), lambda i, j: (i, 0)),
                  pl.BlockSpec((bm, gy.shape[1]), lambda i, j: (i, 0)),
                  pl.BlockSpec((bm, o.shape[1]), lambda i, j: (i, 0)),
                  pl.BlockSpec((bm, bn), lambda i, j: (i, j)),
                  pl.BlockSpec((bm, bn), lambda i, j: (i, nj + j)),
                  pl.BlockSpec((bm, bn), lambda i, j: (i, 2 * nj + j)),
                  pl.BlockSpec((None, wssd.shape[1], bn), lambda i, j: (layer, 0, j)),
                  pl.BlockSpec((None, wa.shape[1], bn), lambda i, j: (layer, 0, j)),
                  pl.BlockSpec((None, wb.shape[1], bn), lambda i, j: (layer, 0, j)),
                  pl.BlockSpec((None, whg.shape[1], bn), lambda i, j: (layer, 0, j)),
                  pl.BlockSpec((None, bn, d), lambda i, j: (layer, j, 0))],
        out_specs=pl.BlockSpec((bm, d), lambda i, j: (i, 0)),
        out_shape=jax.ShapeDtypeStruct((m, d), F32),
        scratch_shapes=[pltpu.VMEM((bm, d), F32)],
        compiler_params=_cparams(("parallel", "arbitrary")),
        name="branch_mix",
    )(x, ys, gy, o, p, p, p, wssd, wa, wb, whg, wmix)


def _attend(q, k_head, v_head):
    outs = []
    for h in range(XA_HEADS):
        sl = slice(h * XA_HEAD_DIM, (h + 1) * XA_HEAD_DIM)
        s = _dot_nt(q[:, sl], k_head(h).astype(BF16)) * (XA_HEAD_DIM ** -0.5)
        s = s - jnp.max(s, axis=-1, keepdims=True)
        e = jnp.exp(s)
        p = e / jnp.sum(e, axis=-1, keepdims=True)
        outs.append(_dot(p.astype(BF16), v_head(h).astype(BF16)))
    return jnp.concatenate(outs, axis=-1)


def _xattn_prompt_body(q_ref, kv_ref, o_ref):
    k_head = lambda h: kv_ref[:, h * XA_HEAD_DIM:(h + 1) * XA_HEAD_DIM]
    v_head = lambda h: kv_ref[:, D_MODEL + h * XA_HEAD_DIM:D_MODEL + (h + 1) * XA_HEAD_DIM]
    o_ref[...] = _attend(q_ref[...], k_head, v_head).astype(BF16)


def xattn_prompt(q, kv, n_seq, seq_len, bl=512):
    nl = seq_len // bl
    return pl.pallas_call(
        _xattn_prompt_body,
        grid=(n_seq, nl),
        in_specs=[pl.BlockSpec((bl, D_MODEL), lambda b, i: (b * nl + i, 0)),
                  pl.BlockSpec((MEM_LEN, 2 * D_MODEL), lambda b, i: (b, 0))],
        out_specs=pl.BlockSpec((bl, D_MODEL), lambda b, i: (b * nl + i, 0)),
        out_shape=jax.ShapeDtypeStruct(q.shape, BF16),
        compiler_params=_cparams(("parallel", "arbitrary")),
        name="xattn_prompt",
    )(q, kv)


def _xattn_sample_body(q_ref, k_ref, v_ref, buf_ref, o_ref, *, seq_len):
    del buf_ref
    q = q_ref[...]
    r = 2 * seq_len
    nc = XA_HEAD_DIM // LANES
    piece = lambda ref, b, h, c: ref[b, pl.ds(c * XA_HEADS + h, MEM_LEN, stride=nc * XA_HEADS), :].astype(BF16)
    scores = []
    for b in range(2):
        for h in range(XA_HEADS):
            s = None
            for c in range(nc):
                lo = h * XA_HEAD_DIM + c * LANES
                d = _dot_nt(q[:, lo:lo + LANES], piece(k_ref, b, h, c))
                s = d if s is None else s + d
            scores.append(s)
    s = jnp.concatenate(scores, axis=0) * (XA_HEAD_DIM ** -0.5)
    s = s - jnp.max(s, axis=-1, keepdims=True)
    e = jnp.exp(s)
    p = (e / jnp.sum(e, axis=-1, keepdims=True)).astype(BF16)
    rows = lax.broadcasted_iota(jnp.int32, (r, LANES), 0)
    for h in range(XA_HEADS):
        for c in range(nc):
            lo = h * XA_HEAD_DIM + c * LANES
            o0 = _dot(p[h * r:(h + 1) * r], piece(v_ref, 0, h, c))
            o1 = _dot(p[(XA_HEADS + h) * r:(XA_HEADS + h + 1) * r], piece(v_ref, 1, h, c))
            o_ref[:, lo:lo + LANES] = jnp.where(rows < seq_len, o0, o1).astype(BF16)


def xattn_sample(q, k, v, buf, layer, row0, n_seq, seq_len):
    r = 2 * seq_len
    blk0 = row0 // r
    mem_spec = pl.BlockSpec((None, 2, MEM_LEN * D_MODEL // LANES, LANES), lambda i: (layer, i, 0, 0))
    return pl.pallas_call(
        functools.partial(_xattn_sample_body, seq_len=seq_len),
        grid=(n_seq // 2,),
        in_specs=[pl.BlockSpec((r, D_MODEL), lambda i: (blk0 + i, 0)), mem_spec, mem_spec,
                  pl.BlockSpec(memory_space=pl.ANY)],
        out_specs=pl.BlockSpec((r, D_MODEL), lambda i: (blk0 + i, 0)),
        out_shape=jax.ShapeDtypeStruct(buf.shape, BF16),
        input_output_aliases={3: 0},
        compiler_params=_cparams(("parallel",)),
        name="xattn_sample",
    )(q, k, v, buf)


LANES = 128
HALO = 8


def _softplus(x):
    return jnp.maximum(x, 0.0) + jnp.log1p(jnp.exp(-jnp.abs(x)))


def _pad_rows(x, rows):
    if x.shape[0] == rows:
        return x
    return jnp.concatenate([x, jnp.zeros((rows - x.shape[0], x.shape[1]), x.dtype)], axis=0)


def _tile_consts(r, t):
    i = np.arange(r)[:, None]
    j = np.arange(r)[None, :]
    same = (i // t) == (j // t)
    lt = (same & (j <= i)).astype(np.float32)
    last = (same & (j % t == t - 1)).astype(np.float32)
    return jnp.asarray(lt, BF16), jnp.asarray(last, BF16)


def _head_expand(n_heads, width, rows=LANES):
    e = np.zeros((rows, n_heads * width), np.float32)
    for h in range(n_heads):
        e[h, h * width:(h + 1) * width] = 1.0
    return jnp.asarray(e, BF16)


def _ssd_tile(xc, dt_raw, dtb, a_log, lt, last, e, e128, r, t):
    xs = xc[:, :SSD_WIDTH]
    dt = _softplus(dt_raw + dtb)
    d_a = dt * (-jnp.exp(a_log))
    a = _sel_dot(lt, d_a, 3)
    a_e = _dot_sel(a, e, 3)
    dt_e = _dot_sel(dt, e, 2)
    alast_e = _sel_dot(last, a_e, 3)
    a_col = _dot_sel(a, e128, 3)
    a_t = _pad_rows(a, LANES).T

    row = lax.broadcasted_iota(jnp.int32, (r, LANES), 0)
    col = lax.broadcasted_iota(jnp.int32, (r, LANES), 1)
    valid = (col <= row) & (col >= (row // t) * t)
    lane_lo = col < SSD_HEAD_DIM

    xdt = _pad_rows((xs * dt_e).astype(BF16), LANES)
    hpg = SSD_HEADS // SSD_GROUPS
    ys = []
    for g in range(SSD_GROUPS):
        bg = xc[:, SSD_WIDTH + g * SSD_STATE:SSD_WIDTH + (g + 1) * SSD_STATE].astype(BF16)
        cg = xc[:, SSD_WIDTH + (SSD_GROUPS + g) * SSD_STATE:
                SSD_WIDTH + (SSD_GROUPS + g + 1) * SSD_STATE].astype(BF16)
        cb = _dot_nt(cg, _pad_rows(bg, LANES))
        for hp in range(hpg // 2):
            h0 = g * hpg + 2 * hp
            res = []
            for h in (h0, h0 + 1):
                rel = a_col[:, h * LANES:(h + 1) * LANES] - a_t[h:h + 1, :]
                dec = jnp.where(valid, jnp.exp(jnp.where(valid, rel, 0.0)), 0.0)
                res.append(_dot((cb * dec).astype(BF16), xdt[:, h0 * SSD_HEAD_DIM:(h0 + 2) * SSD_HEAD_DIM]))
            ys.append(jnp.where(lane_lo, res[0], res[1]))
    y_intra = jnp.concatenate(ys, axis=-1)
    return xs, y_intra, a_e, dt_e, alast_e


def _ssd_finish(y, xs, z, dsk, ng):
    y = y + dsk * xs
    y = y * _silu(z)
    return (y * _rms_scale(y) * ng).astype(BF16)


def _conv_silu(ext_ref, cw_ref, cb_ref, base, r):
    acc = cb_ref[...] + cw_ref[SSD_CONV - 1:SSD_CONV, :] * ext_ref[pl.ds(base, r), :]
    for k in range(1, SSD_CONV):
        acc = acc + cw_ref[SSD_CONV - 1 - k:SSD_CONV - k, :] * ext_ref[pl.ds(base - k, r), :]
    return _silu(acc)


def _ssd_prompt_body(xbc_ref, z_ref, dt_ref, cw_ref, cb_ref, dtb_ref, alog_ref, dsk_ref, ng_ref,
                     lt_ref, last_ref, e_ref, e128_ref, y_ref, sout_ref, ext_ref, st_ref, *, r):
    c = pl.program_id(1)

    @pl.when(c == 0)
    def _():
        ext_ref[0:HALO, :] = jnp.zeros((HALO, SSD_CONV_DIM), F32)
        st_ref[...] = jnp.zeros_like(st_ref)

    ext_ref[HALO:HALO + r, :] = xbc_ref[...]
    xc = _conv_silu(ext_ref, cw_ref, cb_ref, HALO, r)
    ext_ref[0:HALO, :] = xbc_ref[r - HALO:r, :]

    xs, y, a_e, dt_e, alast_e = _ssd_tile(xc, dt_ref[...], dtb_ref[...], alog_ref[...], lt_ref[...],
                                          last_ref[...], e_ref[...], e128_ref[...], r, r)
    ea_e = jnp.exp(a_e)
    xw = (xs * (dt_e * jnp.exp(alast_e - a_e))).astype(BF16)
    sdec = jnp.exp(alast_e[0:1, :])
    gw = SSD_WIDTH // SSD_GROUPS
    inter = []
    for g in range(SSD_GROUPS):
        bg = xc[:, SSD_WIDTH + g * SSD_STATE:SSD_WIDTH + (g + 1) * SSD_STATE].astype(BF16)
        cg = xc[:, SSD_WIDTH + (SSD_GROUPS + g) * SSD_STATE:
                SSD_WIDTH + (SSD_GROUPS + g + 1) * SSD_STATE].astype(BF16)
        st = st_ref[:, g * gw:(g + 1) * gw]
        inter.append(_dot(cg, st.astype(BF16)))
        st_ref[:, g * gw:(g + 1) * gw] = st * sdec[:, g * gw:(g + 1) * gw] + _dot_tn(bg, xw[:, g * gw:(g + 1) * gw])
    y = y + jnp.concatenate(inter, axis=-1) * ea_e
    y_ref[...] = _ssd_finish(y, xs, z_ref[...], dsk_ref[...], ng_ref[...])

    @pl.when(c == pl.num_programs(1) - 1)
    def _():
        sout_ref[0] = st_ref[...].T


def _ssd_consts(r, t):
    lt, last = _tile_consts(r, t)
    return lt, last, _head_expand(SSD_HEADS, SSD_HEAD_DIM), _head_expand(SSD_HEADS, LANES)


def ssd_prompt(p, dt, prm, n_seq, seq_len, r=128):
    nc = seq_len // r
    consts = _ssd_consts(r, r)
    full = lambda a: pl.BlockSpec(a.shape, lambda b, c: (0,) * a.ndim)
    small = [prm["conv_w"], prm["conv_b"], prm["dt_bias"], prm["a_log"], prm["d_skip"], prm["norm"], *consts]
    return pl.pallas_call(
        functools.partial(_ssd_prompt_body, r=r),
        grid=(n_seq, nc),
        in_specs=[pl.BlockSpec((r, SSD_CONV_DIM), lambda b, c: (b * nc + c, P_XBC // SSD_CONV_DIM)),
                  pl.BlockSpec((r, SSD_WIDTH), lambda b, c: (b * nc + c, P_Z // SSD_WIDTH)),
                  pl.BlockSpec((r, LANES), lambda b, c: (b * nc + c, 0)),
                  *[full(a) for a in small]],
        out_specs=[pl.BlockSpec((r, SSD_WIDTH), lambda b, c: (b * nc + c, 0)),
                   pl.BlockSpec((1, SSD_WIDTH, SSD_STATE), lambda b, c: (b, 0, 0))],
        out_shape=[jax.ShapeDtypeStruct((p.shape[0], SSD_WIDTH), BF16),
                   jax.ShapeDtypeStruct((n_seq, SSD_WIDTH, SSD_STATE), F32)],
        scratch_shapes=[pltpu.VMEM((HALO + r, SSD_CONV_DIM), F32), pltpu.VMEM((SSD_STATE, SSD_WIDTH), F32)],
        compiler_params=_cparams(("parallel", "arbitrary")),
        name="ssd_prompt",
    )(p, p, dt, *small)


def _ssd_sample_body(xbc_ref, z_ref, dt_ref, buf_ref, s0_ref, cw_ref, cb_ref, dtb_ref, alog_ref, dsk_ref,
                     ng_ref, lt_ref, last_ref, e_ref, e128_ref, *rest, r, t):
    y_ref, sout_ref, ext_ref = rest[-3:]
    nb = r // t
    pitch = HALO + t
    for b in range(nb):
        ext_ref[b * pitch + HALO - (SSD_CONV - 1):b * pitch + HALO, :] = buf_ref[b]
        ext_ref[b * pitch + HALO:(b + 1) * pitch, :] = xbc_ref[b * t:(b + 1) * t, :]
    xc = jnp.concatenate([_conv_silu(ext_ref, cw_ref, cb_ref, b * pitch + HALO, t) for b in range(nb)], axis=0)

    xs, y, a_e, dt_e, alast_e = _ssd_tile(xc, dt_ref[...], dtb_ref[...], alog_ref[...], lt_ref[...],
                                          last_ref[...], e_ref[...], e128_ref[...], r, t)
    ea_e = jnp.exp(a_e)
    xw = xs * (dt_e * jnp.exp(alast_e - a_e))
    sdec = jnp.exp(alast_e)
    gw = SSD_WIDTH // SSD_GROUPS
    pr = 2 * t
    prow = lax.broadcasted_iota(jnp.int32, (pr, 1), 0)
    ones = jnp.ones((pr, SSD_STATE), BF16)
    inter = []
    for g in range(SSD_GROUPS):
        bg = xc[:, SSD_WIDTH + g * SSD_STATE:SSD_WIDTH + (g + 1) * SSD_STATE].astype(BF16)
        cg = xc[:, SSD_WIDTH + (SSD_GROUPS + g) * SSD_STATE:
                SSD_WIDTH + (SSD_GROUPS + g + 1) * SSD_STATE].astype(BF16)
        cols = slice(g * gw, (g + 1) * gw)
        rows_out = []
        for q in range(nb // 2):
            rs = slice(q * pr, (q + 1) * pr)
            acc = None
            for s in range(2):
                b = 2 * q + s
                mine = (prow >= s * t) & (prow < (s + 1) * t)
                s0 = s0_ref[b, cols, :]
                yi = _dot_nt(cg[rs], s0.astype(BF16))
                acc = jnp.where(mine, yi, 0.0) if acc is None else acc + jnp.where(mine, yi, 0.0)
                upd = _dot_tn(jnp.where(mine, xw[rs, cols], 0.0).astype(BF16), bg[rs])
                lastrow = prow == (s + 1) * t - 1
                dh, dl = _split2(jnp.where(lastrow, sdec[rs, cols], 0.0))
                dcol = _dot_tn(dh, ones) + _dot_tn(dl, ones)
                sout_ref[b, cols, :] = s0 * dcol + upd
            rows_out.append(acc)
        inter.append(jnp.concatenate(rows_out, axis=0))
    y = y + jnp.concatenate(inter, axis=-1) * ea_e
    y_ref[...] = _ssd_finish(y, xs, z_ref[...], dsk_ref[...], ng_ref[...])


def ssd_sample(p, dt, conv_buf, s0, ybuf, sprev, prm, layer, row0, n_seq, seq_len, r=64):
    nb = r // seq_len
    blk0 = row0 // r
    consts = _ssd_consts(r, seq_len)
    full = lambda a: pl.BlockSpec(a.shape, lambda i: (0,) * a.ndim)
    small = [prm["conv_w"], prm["conv_b"], prm["dt_bias"], prm["a_log"], prm["d_skip"], prm["norm"], *consts]
    inplace = [ybuf] if sprev is None else [ybuf, sprev]
    n_in = 5 + len(small)
    return pl.pallas_call(
        functools.partial(_ssd_sample_body, r=r, t=seq_len),
        grid=(n_seq // nb,),
        in_specs=[pl.BlockSpec((r, SSD_CONV_DIM), lambda i: (blk0 + i, P_XBC // SSD_CONV_DIM)),
                  pl.BlockSpec((r, SSD_WIDTH), lambda i: (blk0 + i, P_Z // SSD_WIDTH)),
                  pl.BlockSpec((r, LANES), lambda i: (blk0 + i, 0)),
                  pl.BlockSpec((None, nb, SSD_CONV - 1, SSD_CONV_DIM), lambda i: (layer, i, 0, 0)),
                  pl.BlockSpec((None, nb, SSD_WIDTH, SSD_STATE), lambda i: (layer, i, 0, 0)),
                  *[full(a) for a in small],
                  *[pl.BlockSpec(memory_space=pl.ANY) for _ in inplace]],
        out_specs=[pl.BlockSpec((r, SSD_WIDTH), lambda i: (blk0 + i, 0)),
                   pl.BlockSpec((None, nb, SSD_WIDTH, SSD_STATE), lambda i: (layer, i, 0, 0))],
        out_shape=[jax.ShapeDtypeStruct(ybuf.shape, BF16),
                   jax.ShapeDtypeStruct(s0.shape, F32)],
        input_output_aliases={n_in + k: k for k in range(len(inplace))},
        scratch_shapes=[pltpu.VMEM((nb * (HALO + seq_len), SSD_CONV_DIM), F32)],
        compiler_params=_cparams(("parallel",)),
        name="ssd_sample",
    )(p, p, dt, conv_buf, s0, *small, *inplace)


S5_CH = S5_GROUPS * S5_STATE
S5_BLK = 4
S5_SSEQ = 32


def _s5_in(u, bre_ref, bim_ref):
    ub = u.astype(BF16)
    kin = S5_WIDTH // S5_BLK
    re = [_dot(ub[:, q * kin:(q + 1) * kin], bre_ref[q]) for q in range(S5_BLK)]
    im = [_dot(ub[:, q * kin:(q + 1) * kin], bim_ref[q]) for q in range(S5_BLK)]
    return jnp.concatenate(re, axis=-1), jnp.concatenate(im, axis=-1)


def _s5_out(h_re, h_im, u, cre_ref, cim_ref, dsk):
    kst = S5_CH // S5_BLK
    hr = h_re.astype(BF16)
    hi = h_im.astype(BF16)
    y = [_dot(hr[:, q * kst:(q + 1) * kst], cre_ref[q]) - _dot(hi[:, q * kst:(q + 1) * kst], cim_ref[q])
         for q in range(S5_BLK)]
    y = jnp.concatenate(y, axis=-1) + dsk * u
    return jax.nn.gelu(y).astype(BF16)


def _s5_scan_body(u_ref, h0re_ref, h0im_ref, are_ref, aim_ref, bre_ref, bim_ref, cre_ref, cim_ref, dsk_ref,
                  y_ref, hre_ref, him_ref, sre_ref, sim_ref, *, s, tc):
    c = pl.program_id(1)

    @pl.when(c == 0)
    def _():
        sre_ref[0:s, :] = h0re_ref[...]
        sim_ref[0:s, :] = h0im_ref[...]

    u = u_ref[...]
    bu_re, bu_im = _s5_in(u, bre_ref, bim_ref)
    sre_ref[s:, :] = bu_re
    sim_ref[s:, :] = bu_im
    a_re = jnp.broadcast_to(are_ref[...], (s, S5_CH))
    a_im = jnp.broadcast_to(aim_ref[...], (s, S5_CH))

    def step(t, carry):
        prev = pl.ds(pl.multiple_of(t * s, s), s)
        cur = pl.ds(pl.multiple_of((t + 1) * s, s), s)
        h_re = sre_ref[prev, :]
        h_im = sim_ref[prev, :]
        sre_ref[cur, :] = a_re * h_re - a_im * h_im + sre_ref[cur, :]
        sim_ref[cur, :] = a_re * h_im + a_im * h_re + sim_ref[cur, :]
        return carry

    lax.fori_loop(0, tc, step, 0, unroll=4)
    h_re = sre_ref[tc * s:, :]
    h_im = sim_ref[tc * s:, :]
    y_ref[...] = _s5_out(sre_ref[s:, :], sim_ref[s:, :], u, cre_ref, cim_ref, dsk_ref[...])
    sre_ref[0:s, :] = h_re
    sim_ref[0:s, :] = h_im
    hre_ref[...] = h_re
    him_ref[...] = h_im


def s5_scan(u, h0_re, h0_im, prm, n_blocks, s, n_steps, tc):
    nc = n_steps // tc
    r = tc * s
    full = lambda a: pl.BlockSpec(a.shape, lambda b, c: (0,) * a.ndim)
    small = [prm["abar_re"], prm["abar_im"], prm["b_re"], prm["b_im"], prm["c_re"], prm["c_im"], prm["d_skip"]]
    state = jax.ShapeDtypeStruct((n_blocks * s, S5_CH), F32)
    sspec = pl.BlockSpec((s, S5_CH), lambda b, c: (b, 0))
    return pl.pallas_call(
        functools.partial(_s5_scan_body, s=s, tc=tc),
        grid=(n_blocks, nc),
        in_specs=[pl.BlockSpec((r, S5_WIDTH), lambda b, c: (b * nc + c, 0)), sspec, sspec,
                  *[full(a) for a in small]],
        out_specs=[pl.BlockSpec((r, S5_WIDTH), lambda b, c: (b * nc + c, 0)), sspec, sspec],
        out_shape=[jax.ShapeDtypeStruct((u.shape[0], S5_WIDTH), BF16), state, state],
        scratch_shapes=[pltpu.VMEM((s + r, S5_CH), F32), pltpu.VMEM((s + r, S5_CH), F32)],
        compiler_params=_cparams(("parallel", "arbitrary")),
        name="s5_scan",
    )(u, h0_re, h0_im, *small)


S5_T = 16
S5_GB = LANES // S5_GROUP_SIZE
S5_CL = S5_T * LANES
S5_SL = S5_GB * S5_STATE
S5_NB = 2
S5_PITCH_PAD = 8


def _s5_chunk_body(u_ref, tp_ref, bre_ref, bim_ref, cre_ref, cim_ref, are_ref, aim_ref, dsk_ref,
                   y_ref, hre_ref, him_ref, vre_ref, vim_ref, sre_ref, sim_ref, *, n_chunks):
    rows = S5_NB * n_chunks
    pitch = n_chunks + S5_PITCH_PAD
    nk = S5_SL // LANES
    u = jnp.concatenate([u_ref[pl.ds(t, rows, stride=S5_T), :] for t in range(S5_T)], axis=-1)
    ub = u.astype(BF16)
    v_re = _dot(ub, bre_ref[0])
    v_im = _dot(ub, bim_ref[0])
    for k in range(nk):
        vre_ref[k] = v_re[:, k * LANES:(k + 1) * LANES]
        vim_ref[k] = v_im[:, k * LANES:(k + 1) * LANES]
        sre_ref[k] = jnp.zeros((S5_NB * pitch, LANES), F32)
        sim_ref[k] = jnp.zeros((S5_NB * pitch, LANES), F32)
    a_re = [jnp.broadcast_to(are_ref[0, :, k * LANES:(k + 1) * LANES], (S5_NB, LANES)) for k in range(nk)]
    a_im = [jnp.broadcast_to(aim_ref[0, :, k * LANES:(k + 1) * LANES], (S5_NB, LANES)) for k in range(nk)]

    def step(c, carry):
        for k in range(nk):
            h_re = sre_ref[k, pl.ds(c, S5_NB, stride=pitch), :]
            h_im = sim_ref[k, pl.ds(c, S5_NB, stride=pitch), :]
            x_re = vre_ref[k, pl.ds(c, S5_NB, stride=n_chunks), :]
            x_im = vim_ref[k, pl.ds(c, S5_NB, stride=n_chunks), :]
            sre_ref[k, pl.ds(c + 1, S5_NB, stride=pitch), :] = a_re[k] * h_re - a_im[k] * h_im + x_re
            sim_ref[k, pl.ds(c + 1, S5_NB, stride=pitch), :] = a_re[k] * h_im + a_im[k] * h_re + x_im
        return carry

    lax.fori_loop(0, n_chunks, step, 0, unroll=2)
    starts = lambda ref: jnp.concatenate(
        [jnp.concatenate([ref[k, b * pitch:b * pitch + n_chunks, :] for b in range(S5_NB)], axis=0)
         for k in range(nk)], axis=-1)
    y = _dot(ub, tp_ref[0])
    y = y + _dot(starts(sre_ref).astype(BF16), cre_ref[0]) - _dot(starts(sim_ref).astype(BF16), cim_ref[0])
    y = jax.nn.gelu(y + dsk_ref[0] * u)
    for t in range(S5_T):
        y_ref[pl.ds(t, rows, stride=S5_T), :] = y[:, t * LANES:(t + 1) * LANES]
    final = lambda ref: jnp.concatenate(
        [jnp.concatenate([ref[k, b * pitch + n_chunks:b * pitch + n_chunks + 1, :] for b in range(S5_NB)], axis=0)
         for k in range(nk)], axis=-1)
    hre_ref[...] = final(sre_ref)
    him_ref[...] = final(sim_ref)


def s5_chunked(p, prm, n_seq, seq_len):
    n_chunks = seq_len // S5_T
    rows = S5_NB * n_chunks
    nblk = S5_GROUPS // S5_GB
    nhalf = n_seq // S5_NB
    pitch = n_chunks + S5_PITCH_PAD
    nk = S5_SL // LANES
    blk = lambda a: pl.BlockSpec((1,) + a.shape[1:], lambda j, i: (j,) + (0,) * (a.ndim - 1))
    small = [prm["toep"], prm["bst_re"], prm["bst_im"], prm["cst_re"], prm["cst_im"], prm["apow_re"],
             prm["apow_im"], prm["d_tiled"]]
    state = jax.ShapeDtypeStruct((nhalf, S5_NB, S5_CH), F32)
    sspec = pl.BlockSpec((None, S5_NB, S5_SL), lambda j, i: (i, 0, j))
    return pl.pallas_call(
        functools.partial(_s5_chunk_body, n_chunks=n_chunks),
        grid=(nblk, nhalf),
        in_specs=[pl.BlockSpec((S5_NB * seq_len, LANES), lambda j, i: (i, P_U // LANES + j)),
                  *[blk(a) for a in small]],
        out_specs=[pl.BlockSpec((S5_NB * seq_len, LANES), lambda j, i: (i, j)), sspec, sspec],
        out_shape=[jax.ShapeDtypeStruct((p.shape[0], S5_WIDTH), F32), state, state],
        scratch_shapes=[pltpu.VMEM((nk, rows, LANES), F32), pltpu.VMEM((nk, rows, LANES), F32),
                        pltpu.VMEM((nk, S5_NB * pitch, LANES), F32), pltpu.VMEM((nk, S5_NB * pitch, LANES), F32)],
        compiler_params=_cparams(("parallel", "arbitrary")),
        name="s5_chunked",
    )(p, *small)


def _hg_consts(r, t):
    i = np.arange(r)[:, None]
    j = np.arange(r)[None, :]
    sums, upper, pair = [], [], []
    s = 1
    while s < t:
        blk_i, blk_j = i // (2 * s), j // (2 * s)
        up_i = (i % (2 * s)) >= s
        mid_i = blk_i * 2 * s + s
        m_up = up_i & (j >= mid_i) & (j <= i)
        m_lo = (~up_i) & (j > i) & (j < mid_i)
        sums.append((m_up | m_lo).astype(np.float32))
        upper.append(np.broadcast_to(up_i, (r, 1)).astype(np.float32))
        pair.append(((blk_i == blk_j) & up_i & ((j % (2 * s)) < s)).astype(np.float32))
        s *= 2
    pair.append((i == j).astype(np.float32))
    return (jnp.asarray(np.stack(sums), BF16), jnp.asarray(np.stack(upper), F32),
            jnp.asarray(np.stack(pair), F32))


def _hg_gates(hf, lb):
    logf = -_softplus(-hf) + jnp.log1p(lb * jnp.exp(-hf))
    kk = (1.0 - lb) * jax.nn.sigmoid(-hf)
    return logf, kk


def _hg_intra(q, kk, v, logf, sums_ref, upper_ref, pair_ref):
    r = q.shape[0]
    nlev = sums_ref.shape[0]
    lf3 = _split3(logf)
    qb = q.astype(BF16)
    kb = kk.astype(BF16)
    vb = _pad_rows(v.astype(BF16), LANES) if r < LANES else v.astype(BF16)
    scores = [None] * HG_HEADS
    for lev in range(nlev + 1):
        if lev < nlev:
            m = sums_ref[lev]
            d = _dot(m, lf3[0]) + _dot(m, lf3[1]) + _dot(m, lf3[2])
            x = (jnp.where(upper_ref[lev] > 0.5, q, kk) * jnp.exp(d)).astype(BF16)
            xq, xk = x, x
        else:
            xq, xk = qb, kb
        mask = pair_ref[lev]
        for h in range(HG_HEADS):
            sl = slice(h * HG_KEY_DIM, (h + 1) * HG_KEY_DIM)
            sc = _dot_nt(xq[:, sl], xk[:, sl]) * mask
            scores[h] = sc if scores[h] is None else scores[h] + sc
    outs = [_dot(scores[h].astype(BF16), vb[:, h * HG_VAL_DIM:(h + 1) * HG_VAL_DIM]) for h in range(HG_HEADS)]
    return jnp.concatenate(outs, axis=-1)


def _hg_finish(o, hgate, ng):
    outs = []
    for h in range(HG_HEADS):
        oh = o[:, h * HG_VAL_DIM:(h + 1) * HG_VAL_DIM]
        outs.append(oh * _rms_scale(oh) * ng)
    return (jnp.concatenate(outs, axis=-1) * _silu(hgate)).astype(BF16)


def _hg_prompt_body(q_ref, f_ref, i_ref, gate_ref, lb_ref, ng_ref, lt_ref, last_ref, sums_ref, upper_ref,
                    pair_ref, o_ref, sout_ref, *, r):
    c = pl.program_id(1)

    @pl.when(c == 0)
    def _():
        sout_ref[...] = jnp.zeros_like(sout_ref)

    q = q_ref[...]
    v = i_ref[...]
    logf, kk = _hg_gates(f_ref[...], lb_ref[...])
    o = _hg_intra(q, kk, v, logf, sums_ref, upper_ref, pair_ref)
    b = _sel_dot(lt_ref[...], logf, 3)
    blast = b[r - 1:r, :]
    qe = (q * jnp.exp(b)).astype(BF16)
    kw = (kk * jnp.exp(blast - b)).astype(BF16)
    vb = v.astype(BF16)
    ones = jnp.ones((r, HG_VAL_DIM), BF16)
    rows = lax.broadcasted_iota(jnp.int32, (r, 1), 0)
    dh, dl = _split2(jnp.where(rows == r - 1, jnp.exp(b), 0.0))
    inter = []
    for h in range(HG_HEADS):
        sl = slice(h * HG_KEY_DIM, (h + 1) * HG_KEY_DIM)
        s = sout_ref[0, h]
        inter.append(_dot(qe[:, sl], s.astype(BF16)))
        dcol = _dot_tn(dh[:, sl], ones) + _dot_tn(dl[:, sl], ones)
        sout_ref[0, h] = s * dcol + _dot_tn(kw[:, sl], vb[:, sl])
    o = o + jnp.concatenate(inter, axis=-1)
    o_ref[...] = _hg_finish(o, gate_ref[...], ng_ref[...])


def hg_prompt(p, prm, n_seq, seq_len, r=128):
    nc = seq_len // r
    lt, last = _tile_consts(r, r)
    consts = [lt, last, *_hg_consts(r, r)]
    full = lambda a: pl.BlockSpec(a.shape, lambda b, c: (0,) * a.ndim)
    small = [prm["lb"], prm["norm"], *consts]
    col = lambda off: pl.BlockSpec((r, HG_WIDTH), lambda b, c: (b * nc + c, off // HG_WIDTH))
    return pl.pallas_call(
        functools.partial(_hg_prompt_body, r=r),
        grid=(n_seq, nc),
        in_specs=[col(P_HQ), col(P_HF), col(P_HI), col(P_HGATE), *[full(a) for a in small]],
        out_specs=[pl.BlockSpec((r, HG_WIDTH), lambda b, c: (b * nc + c, 0)),
                   pl.BlockSpec((1, HG_HEADS, HG_KEY_DIM, HG_VAL_DIM), lambda b, c: (b, 0, 0, 0))],
        out_shape=[jax.ShapeDtypeStruct((p.shape[0], HG_WIDTH), BF16),
                   jax.ShapeDtypeStruct((n_seq, HG_HEADS, HG_KEY_DIM, HG_VAL_DIM), F32)],
        compiler_params=_cparams(("parallel", "arbitrary")),
        name="hg_prompt",
    )(p, p, p, p, *small)


def _hg_sample_body(q_ref, f_ref, i_ref, gate_ref, s0_ref, lb_ref, ng_ref, lt_ref, last_ref, sums_ref,
                    upper_ref, pair_ref, *rest, r, t):
    o_ref, sout_ref = rest[-2:]
    q = q_ref[...]
    v = i_ref[...]
    logf, kk = _hg_gates(f_ref[...], lb_ref[...])
    o = _hg_intra(q, kk, v, logf, sums_ref, upper_ref, pair_ref)
    b = _sel_dot(lt_ref[...], logf, 3)
    blast = _sel_dot(last_ref[...], b, 3)
    qe = (q * jnp.exp(b)).astype(BF16)
    kw = kk * jnp.exp(blast - b)
    sdec = jnp.exp(blast)
    vb = v.astype(BF16)
    pr = 2 * t
    prow = lax.broadcasted_iota(jnp.int32, (pr, 1), 0)
    ones = jnp.ones((pr, HG_VAL_DIM), BF16)
    rows_out = []
    for p2 in range(r // pr):
        rs = slice(p2 * pr, (p2 + 1) * pr)
        heads = []
        for h in range(HG_HEADS):
            sl = slice(h * HG_KEY_DIM, (h + 1) * HG_KEY_DIM)
            acc = None
            for s in range(2):
                bi = 2 * p2 + s
                mine = (prow >= s * t) & (prow < (s + 1) * t)
                s0 = s0_ref[bi, h]
                oi = jnp.where(mine, _dot(qe[rs, sl], s0.astype(BF16)), 0.0)
                acc = oi if acc is None else acc + oi
                upd = _dot_tn(jnp.where(mine, kw[rs, sl], 0.0).astype(BF16), vb[rs, sl])
                dh, dl = _split2(jnp.where(prow == (s + 1) * t - 1, sdec[rs, sl], 0.0))
                dcol = _dot_tn(dh, ones) + _dot_tn(dl, ones)
                sout_ref[bi, h] = s0 * dcol + upd
            heads.append(acc)
        rows_out.append(jnp.concatenate(heads, axis=-1))
    o = o + jnp.concatenate(rows_out, axis=0)
    o_ref[...] = _hg_finish(o, gate_ref[...], ng_ref[...])


def hg_sample(p, s0, obuf, sprev, prm, layer, row0, n_seq, seq_len, r=128):
    nb = r // seq_len
    blk0 = row0 // r
    lt, last = _tile_consts(r, seq_len)
    consts = [lt, last, *_hg_consts(r, seq_len)]
    full = lambda a: pl.BlockSpec(a.shape, lambda i: (0,) * a.ndim)
    small = [prm["lb"], prm["norm"], *consts]
    col = lambda off: pl.BlockSpec((r, HG_WIDTH), lambda i: (blk0 + i, off // HG_WIDTH))
    sspec = pl.BlockSpec((None, nb, HG_HEADS, HG_KEY_DIM, HG_VAL_DIM), lambda i: (layer, i, 0, 0, 0))
    inplace = [obuf] if sprev is None else [obuf, sprev]
    n_in = 5 + len(small)
    return pl.pallas_call(
        functools.partial(_hg_sample_body, r=r, t=seq_len),
        grid=(n_seq // nb,),
        in_specs=[col(P_HQ), col(P_HF), col(P_HI), col(P_HGATE), sspec, *[full(a) for a in small],
                  *[pl.BlockSpec(memory_space=pl.ANY) for _ in inplace]],
        out_specs=[pl.BlockSpec((r, HG_WIDTH), lambda i: (blk0 + i, 0)), sspec],
        out_shape=[jax.ShapeDtypeStruct(obuf.shape, BF16), jax.ShapeDtypeStruct(s0.shape, F32)],
        input_output_aliases={n_in + k: k for k in range(len(inplace))},
        compiler_params=_cparams(("parallel",)),
        name="hg_sample",
    )(p, p, p, p, s0, *small, *inplace)


def _s5_params(a_re, a_im, log_dt, b_re, b_im, c_re, c_im, d_skip):
    dt = jnp.exp(log_dt)[:, None]
    mag = jnp.exp(a_re * dt)
    abar_re = mag * jnp.cos(a_im * dt)
    abar_im = mag * jnp.sin(a_im * dt)
    den = a_re * a_re + a_im * a_im
    nr = abar_re - 1.0
    coef_re = (nr * a_re + abar_im * a_im) / den
    coef_im = (abar_im * a_re - nr * a_im) / den
    bbar_re = coef_re[..., None] * b_re - coef_im[..., None] * b_im
    bbar_im = coef_re[..., None] * b_im + coef_im[..., None] * b_re
    gpb = S5_GROUPS // S5_BLK
    eye = jnp.eye(gpb, dtype=F32)

    def in_blocks(bbar):
        bb = bbar.reshape(S5_BLK, gpb, S5_STATE, S5_GROUP_SIZE)
        return jnp.einsum("qgnk,gh->qgkhn", bb, eye).reshape(
            S5_BLK, gpb * S5_GROUP_SIZE, gpb * S5_STATE).astype(BF16)

    def out_blocks(c):
        cc = c.reshape(S5_BLK, gpb, S5_GROUP_SIZE, S5_STATE)
        return jnp.einsum("qgkn,gh->qgnhk", cc, eye).reshape(
            S5_BLK, gpb * S5_STATE, gpb * S5_GROUP_SIZE).astype(BF16)

    prm = {
        "abar_re": abar_re.reshape(1, S5_CH), "abar_im": abar_im.reshape(1, S5_CH),
        "b_re": in_blocks(bbar_re), "b_im": in_blocks(bbar_im),
        "c_re": out_blocks(c_re), "c_im": out_blocks(c_im),
        "d_skip": d_skip.reshape(1, S5_WIDTH),
    }

    hi = lax.Precision.HIGHEST
    tau = jnp.arange(S5_T + 1, dtype=F32)[:, None, None]
    pmag = jnp.exp(tau * (a_re * dt))
    pw_re = pmag * jnp.cos(tau * (a_im * dt))
    pw_im = pmag * jnp.sin(tau * (a_im * dt))
    cp_re = c_re[None] * pw_re[:, :, None, :] - c_im[None] * pw_im[:, :, None, :]
    cp_im = c_re[None] * pw_im[:, :, None, :] + c_im[None] * pw_re[:, :, None, :]
    kern = (jnp.einsum("tgkn,gnj->tgkj", cp_re, bbar_re, precision=hi)
            - jnp.einsum("tgkn,gnj->tgkj", cp_im, bbar_im, precision=hi))
    s_idx = jnp.arange(S5_T)[:, None]
    t_idx = jnp.arange(S5_T)[None, :]
    lag = jnp.clip(t_idx - s_idx, 0, S5_T)
    nq = S5_GROUPS // S5_GB
    ks = S5_GROUP_SIZE
    gid = lambda n, w: np.arange(n) // w
    mask = lambda rows_w, cols_w: jnp.asarray(
        (gid(S5_GB * rows_w, rows_w)[:, None] == gid(S5_GB * cols_w, cols_w)[None, :]).astype(np.float32))

    def expand(x, rows_w, cols_w):
        lead = x.shape[:-2]
        xt = jnp.broadcast_to(x[..., None, :, :], lead + (S5_GB, rows_w, S5_GB * cols_w))
        return xt.reshape(lead + (S5_GB * rows_w, S5_GB * cols_w)) * mask(rows_w, cols_w)

    kj = kern.reshape(S5_T + 1, nq, S5_GB, ks, ks).transpose(0, 1, 4, 2, 3).reshape(S5_T + 1, nq, ks, LANES)
    kern_e = expand(kj, ks, ks).astype(BF16)
    zero = jnp.zeros_like(kern_e[0])
    toep = jnp.stack([jnp.concatenate([kern_e[t - s] if t >= s else zero for t in range(S5_T)], axis=-1)
                      for s in range(S5_T)], axis=1).reshape(nq, S5_CL, S5_CL)
    rev = pw_re[S5_T - 1 - jnp.arange(S5_T)], pw_im[S5_T - 1 - jnp.arange(S5_T)]
    bst_re = rev[0][..., None] * bbar_re[None] - rev[1][..., None] * bbar_im[None]
    bst_im = rev[0][..., None] * bbar_im[None] + rev[1][..., None] * bbar_re[None]

    def bst_blocks(b):
        bb = b.reshape(S5_T, nq, S5_GB, S5_STATE, ks).transpose(1, 0, 4, 2, 3).reshape(nq, S5_T, ks, S5_SL)
        return expand(bb, ks, S5_STATE).reshape(nq, S5_CL, S5_SL).astype(BF16)

    def cst_blocks(c):
        cc = c.reshape(S5_T, nq, S5_GB, ks, S5_STATE).transpose(0, 1, 4, 2, 3).reshape(S5_T, nq, S5_STATE, LANES)
        ce = expand(cc, S5_STATE, ks).astype(BF16)
        return jnp.concatenate([ce[t] for t in range(S5_T)], axis=-1)

    prm.update({
        "toep": toep,
        "bst_re": bst_blocks(bst_re), "bst_im": bst_blocks(bst_im),
        "cst_re": cst_blocks(cp_re[1:]), "cst_im": cst_blocks(cp_im[1:]),
        "apow_re": pw_re[S5_T].reshape(nq, 1, S5_SL), "apow_im": pw_im[S5_T].reshape(nq, 1, S5_SL),
        "d_tiled": jnp.broadcast_to(d_skip.reshape(nq, 1, 1, LANES), (nq, 1, S5_T, LANES)).reshape(nq, 1, S5_CL),
    })
    return prm


def _in_proj_weights(w_in):
    z0, xbc0, dt0, u0, gates0 = 0, 2048, 5120, 5152, 10272
    w = jnp.concatenate([w_in[..., gates0:], w_in[..., xbc0:dt0], w_in[..., u0:gates0], w_in[..., z0:xbc0]],
                        axis=-1)
    wdt = jnp.pad(w_in[..., dt0:u0], ((0, 0), (0, 0), (0, LANES - SSD_HEADS)))
    return w.astype(BF16), wdt.astype(BF16)


def _mem_rows(cache):
    dd, b, m, h, hd = cache.shape
    c = cache.reshape(dd, b, m, h, hd // LANES, LANES).transpose(0, 1, 2, 4, 3, 5)
    return c.reshape(dd, b, m * h * (hd // LANES), LANES)


def _pad_lanes(v):
    return jnp.pad(v.reshape(1, -1), ((0, 0), (0, LANES - v.shape[-1])))


def kernel(x_prompt, x_sample, cache_mem_k, cache_mem_v, state_ssd, state_ssd_conv, state_s5_re, state_s5_im,
           state_hgrn, mem_prompt, norm_ffn1, ffn1_w1, ffn1_w3, ffn1_w2, norm_mix, w_in, ssd_conv_w, ssd_conv_b,
           ssd_dt_bias, ssd_a_log, ssd_d, ssd_norm, ssd_w_out, s5_a_re, s5_a_im, s5_log_dt, s5_b_re, s5_b_im,
           s5_c_re, s5_c_im, s5_d, s5_w_glu_a, s5_w_glu_b, hg_lower_bounds, hg_norm, hg_w_out, w_mix_out,
           norm_xa, norm_mem, xa_wq, xa_wk, xa_wv, xa_wo, norm_ffn2, ffn2_w1, ffn2_w3, ffn2_w2, norm_final):
    bp, lp, d = x_prompt.shape
    bs, ls, _ = x_sample.shape
    mp, ms = bp * lp, bs * ls
    x = jnp.concatenate([x_prompt.reshape(mp, d), x_sample.reshape(ms, d)], axis=0)
    mem = mem_prompt.reshape(bp * MEM_LEN, d)
    row = lambda v: v.reshape(1, -1)
    bf = lambda w: w.astype(BF16)

    lb_p = jax.nn.softmax(hg_lower_bounds, axis=0)
    lb_all = jnp.cumsum(lb_p, axis=0) - lb_p[0]

    ssd_s0 = state_ssd.reshape(DEPTH, bs, SSD_WIDTH, SSD_STATE)
    mem_k = _mem_rows(cache_mem_k)
    mem_v = _mem_rows(cache_mem_v)
    ssd_states = None
    hg_states = None

    f1_w1, f1_w3, f1_w2 = bf(ffn1_w1), bf(ffn1_w3), bf(ffn1_w2)
    f2_w1, f2_w3, f2_w2 = bf(ffn2_w1), bf(ffn2_w3), bf(ffn2_w2)
    w_p, w_dt = _in_proj_weights(w_in)
    b_ssd, b_glu_a, b_glu_b, b_hg, b_mix = bf(ssd_w_out), bf(s5_w_glu_a), bf(s5_w_glu_b), bf(hg_w_out), bf(w_mix_out)
    b_wq, b_wo = bf(xa_wq), bf(xa_wo)
    b_wkv = bf(jnp.concatenate([xa_wk, xa_wv], axis=-1))

    outs = {k: [] for k in ("pk", "pv", "pss", "pcv", "psr", "psi", "phg", "scv", "ssr", "ssi")}
    for l in range(DEPTH):
        x = ffn(x, row(norm_ffn1[l]), f1_w1, f1_w3, f1_w2, row(norm_final), l, False)

        p, dt = in_proj(x, row(norm_mix[l]), w_p, w_dt, l)

        ssd_prm = {"conv_w": ssd_conv_w[l], "conv_b": row(ssd_conv_b[l]), "dt_bias": _pad_lanes(ssd_dt_bias[l]),
                   "a_log": _pad_lanes(ssd_a_log[l]), "d_skip": row(jnp.repeat(ssd_d[l], SSD_HEAD_DIM)),
                   "norm": row(ssd_norm[l])}
        ys, ss_p = ssd_prompt(p, dt, ssd_prm, bp, lp)
        ys, ssd_states = ssd_sample(p, dt, state_ssd_conv, ssd_s0, ys, ssd_states, ssd_prm, l, mp, bs, ls)

        s5_prm = _s5_params(s5_a_re[l], s5_a_im[l], s5_log_dt[l], s5_b_re[l], s5_b_im[l], s5_c_re[l],
                            s5_c_im[l], s5_d[l])
        gy, sr_p, si_p = s5_chunked(p, s5_prm, bp, lp)
        u_s = p[mp:, P_U:P_U + S5_WIDTH].reshape(bs // S5_SSEQ, S5_SSEQ, ls, S5_WIDTH).transpose(0, 2, 1, 3)
        u_s = u_s.reshape(ms, S5_WIDTH)
        gy_s, sr_s, si_s = s5_scan(u_s, state_s5_re[l].reshape(bs, S5_CH), state_s5_im[l].reshape(bs, S5_CH),
                                   s5_prm, bs // S5_SSEQ, S5_SSEQ, ls, ls)
        gy_s = gy_s.reshape(bs // S5_SSEQ, ls, S5_SSEQ, S5_WIDTH).transpose(0, 2, 1, 3).reshape(ms, S5_WIDTH)
        gy = lax.dynamic_update_slice(gy, gy_s.astype(F32), (mp, 0))

        hg_prm = {"lb": row(lb_all[l]), "norm": row(hg_norm[l])}
        o, hg_p = hg_prompt(p, hg_prm, bp, lp)
        o, hg_states = hg_sample(p, state_hgrn, o, hg_states, hg_prm, l, mp, bs, ls)

        x = branch_mix(x, ys, gy, o, p, b_ssd, b_glu_a, b_glu_b, b_hg, b_mix, l)

        q = norm_proj(x, row(norm_xa[l]), b_wq, l, BF16)
        kv = norm_proj(mem, row(norm_mem[l]), b_wkv, l, F32)
        at = xattn_prompt(q, kv, bp, lp)
        at = xattn_sample(q, mem_k, mem_v, at, l, mp, bs, ls)
        x = res_mm(x, at, b_wo, l)

        x = ffn(x, row(norm_ffn2[l]), f2_w1, f2_w3, f2_w2, row(norm_final), l, l == DEPTH - 1)

        tail = SSD_CONV - 1
        outs["pk"].append(kv[:, :d].reshape(bp, MEM_LEN, XA_HEADS, XA_HEAD_DIM))
        outs["pv"].append(kv[:, d:].reshape(bp, MEM_LEN, XA_HEADS, XA_HEAD_DIM))
        outs["pss"].append(ss_p.reshape(bp, SSD_HEADS, SSD_HEAD_DIM, SSD_STATE))
        outs["pcv"].append(jnp.stack([p[(b + 1) * lp - tail:(b + 1) * lp, P_XBC:P_XBC + SSD_CONV_DIM]
                                      for b in range(bp)]))
        outs["psr"].append(sr_p.reshape(bp, S5_GROUPS, S5_STATE))
        outs["psi"].append(si_p.reshape(bp, S5_GROUPS, S5_STATE))
        outs["phg"].append(hg_p)
        outs["scv"].append(p[mp:, P_XBC:P_XBC + SSD_CONV_DIM].reshape(bs, ls, SSD_CONV_DIM)[:, ls - tail:])
        outs["ssr"].append(sr_s.reshape(bs, S5_GROUPS, S5_STATE))
        outs["ssi"].append(si_s.reshape(bs, S5_GROUPS, S5_STATE))

    st = lambda k: jnp.stack(outs[k])
    return (x[:mp].reshape(bp, lp, d), x[mp:].reshape(bs, ls, d),
            st("pk"), st("pv"), st("pss"), st("pcv"), st("psr"), st("psi"), st("phg"),
            ssd_states.reshape(DEPTH, bs, SSD_HEADS, SSD_HEAD_DIM, SSD_STATE), st("scv"), st("ssr"), st("ssi"),
            hg_states)
```

```python
import functools
import math

import jax
import jax.numpy as jnp
import numpy as np
from jax import lax
from jax.experimental import pallas as pl
from jax.experimental.pallas import tpu as pltpu

F32 = jnp.float32
BF16 = jnp.bfloat16

D_MODEL = 2048
DEPTH = 2
NORM_EPS = 1e-5
SSD_HEAD_DIM = 64
SSD_HEADS = 32
SSD_GROUPS = 4
SSD_STATE = 128
SSD_CONV = 4
SSD_WIDTH = 2048
SSD_CONV_DIM = 3072
S5_WIDTH = 1024
S5_GROUP_SIZE = 16
S5_GROUPS = 64
S5_STATE = 64
HG_WIDTH = 1024
HG_HEADS = 8
HG_KEY_DIM = 128
HG_VAL_DIM = 128
MEM_LEN = 256
XA_HEADS = 4
XA_HEAD_DIM = 512
FFN_DIM = 5632

P_GATES, P_XBC, P_U, P_HQ, P_HF, P_HI, P_HGATE, P_Z = 0, 6144, 9216, 10240, 11264, 12288, 13312, 14336
P_WIDTH = 16384

V7X_VMEM_LIMIT = 56 * 1024 * 1024


def _cparams(sem, vmem=V7X_VMEM_LIMIT):
    return pltpu.CompilerParams(dimension_semantics=sem, vmem_limit_bytes=vmem)


def _rms_scale(x):
    return lax.rsqrt(jnp.mean(x * x, axis=-1, keepdims=True) + NORM_EPS)


def _silu(x):
    return x * jax.nn.sigmoid(x)


def _dot(a, b):
    return jnp.dot(a, b, preferred_element_type=F32)


def _dot_nt(a, b):
    return lax.dot_general(a, b, (((1,), (1,)), ((), ())), preferred_element_type=F32)


def _dot_tn(a, b):
    return lax.dot_general(a, b, (((0,), (0,)), ((), ())), preferred_element_type=F32)


def _split2(x):
    hi = x.astype(BF16)
    lo = (x - hi.astype(F32)).astype(BF16)
    return hi, lo


def _split3(x):
    hi = x.astype(BF16)
    r = x - hi.astype(F32)
    mid = r.astype(BF16)
    lo = (r - mid.astype(F32)).astype(BF16)
    return hi, mid, lo


def _sel_dot(sel, x, parts=3):
    ps = _split3(x) if parts == 3 else _split2(x)
    out = _dot(sel, ps[0])
    for p in ps[1:]:
        out = out + _dot(sel, p)
    return out


def _dot_sel(x, sel, parts=2):
    ps = _split3(x) if parts == 3 else _split2(x)
    out = _dot(ps[0], sel)
    for p in ps[1:]:
        out = out + _dot(p, sel)
    return out


def _norm_proj_body(x_ref, g_ref, w_ref, o_ref, h_ref):
    @pl.when(pl.program_id(1) == 0)
    def _():
        x = x_ref[...]
        h_ref[...] = (x * _rms_scale(x) * g_ref[...]).astype(BF16)

    o_ref[...] = _dot(h_ref[...], w_ref[...]).astype(o_ref.dtype)


def norm_proj(x, g, w, layer, out_dtype, bm=1024, bn=1024):
    m, k = x.shape
    n = w.shape[2]
    return pl.pallas_call(
        _norm_proj_body,
        grid=(m // bm, n // bn),
        in_specs=[pl.BlockSpec((bm, k), lambda i, j: (i, 0)),
                  pl.BlockSpec((1, k), lambda i, j: (0, 0)),
                  pl.BlockSpec((None, k, bn), lambda i, j: (layer, 0, j))],
        out_specs=pl.BlockSpec((bm, bn), lambda i, j: (i, j)),
        out_shape=jax.ShapeDtypeStruct((m, n), out_dtype),
        scratch_shapes=[pltpu.VMEM((bm, k), BF16)],
        compiler_params=_cparams(("parallel", "arbitrary")),
        name="norm_proj",
    )(x, g, w)


def _in_proj_body(x_ref, g_ref, w_ref, wdt_ref, o_ref, odt_ref, h_ref):
    @pl.when(pl.program_id(1) == 0)
    def _():
        x = x_ref[...]
        h = (x * _rms_scale(x) * g_ref[...]).astype(BF16)
        h_ref[...] = h
        odt_ref[...] = _dot(h, wdt_ref[...])

    o_ref[...] = _dot(h_ref[...], w_ref[...])


def in_proj(x, g, w, wdt, layer, bm=1024, bn=1024):
    m, k = x.shape
    n = w.shape[2]
    ndt = wdt.shape[2]
    return pl.pallas_call(
        _in_proj_body,
        grid=(m // bm, n // bn),
        in_specs=[pl.BlockSpec((bm, k), lambda i, j: (i, 0)),
                  pl.BlockSpec((1, k), lambda i, j: (0, 0)),
                  pl.BlockSpec((None, k, bn), lambda i, j: (layer, 0, j)),
                  pl.BlockSpec((None, k, ndt), lambda i, j: (layer, 0, 0))],
        out_specs=[pl.BlockSpec((bm, bn), lambda i, j: (i, j)),
                   pl.BlockSpec((bm, ndt), lambda i, j: (i, 0))],
        out_shape=[jax.ShapeDtypeStruct((m, n), F32), jax.ShapeDtypeStruct((m, ndt), F32)],
        scratch_shapes=[pltpu.VMEM((bm, k), BF16)],
        compiler_params=_cparams(("parallel", "arbitrary")),
        name="in_proj",
    )(x, g, w, wdt)


def _res_mm_body(x_ref, a_ref, w_ref, o_ref):
    o_ref[...] = x_ref[...] + _dot(a_ref[...], w_ref[...])


def res_mm(x, a, w, layer, bm=1024, bn=1024):
    m, n = x.shape
    k = a.shape[1]
    return pl.pallas_call(
        _res_mm_body,
        grid=(m // bm, n // bn),
        in_specs=[pl.BlockSpec((bm, bn), lambda i, j: (i, j)),
                  pl.BlockSpec((bm, k), lambda i, j: (i, 0)),
                  pl.BlockSpec((None, k, bn), lambda i, j: (layer, 0, j))],
        out_specs=pl.BlockSpec((bm, bn), lambda i, j: (i, j)),
        out_shape=jax.ShapeDtypeStruct((m, n), F32),
        compiler_params=_cparams(("parallel", "parallel")),
        name="res_mm",
    )(x, a, w)


def _ffn_body(x_ref, g_ref, w1_ref, w3_ref, w2_ref, gf_ref, o_ref, h_ref, acc_ref, *, final_norm):
    f = pl.program_id(1)

    @pl.when(f == 0)
    def _():
        x = x_ref[...]
        h_ref[...] = (x * _rms_scale(x) * g_ref[...]).astype(BF16)
        acc_ref[...] = jnp.zeros_like(acc_ref)

    h = h_ref[...]
    a = _silu(_dot(h, w1_ref[...])) * _dot(h, w3_ref[...])
    acc_ref[...] += _dot(a.astype(BF16), w2_ref[...])

    @pl.when(f == pl.num_programs(1) - 1)
    def _():
        y = x_ref[...] + 0.5 * acc_ref[...]
        if final_norm:
            y = y * _rms_scale(y) * gf_ref[...]
        o_ref[...] = y


def ffn(x, g, w1, w3, w2, gf, layer, final_norm, bm=512, bf=512):
    m, d = x.shape
    fdim = w1.shape[2]
    return pl.pallas_call(
        functools.partial(_ffn_body, final_norm=final_norm),
        grid=(m // bm, fdim // bf),
        in_specs=[pl.BlockSpec((bm, d), lambda i, f: (i, 0)),
                  pl.BlockSpec((1, d), lambda i, f: (0, 0)),
                  pl.BlockSpec((None, d, bf), lambda i, f: (layer, 0, f)),
                  pl.BlockSpec((None, d, bf), lambda i, f: (layer, 0, f)),
                  pl.BlockSpec((None, bf, d), lambda i, f: (layer, f, 0)),
                  pl.BlockSpec((1, d), lambda i, f: (0, 0))],
        out_specs=pl.BlockSpec((bm, d), lambda i, f: (i, 0)),
        out_shape=jax.ShapeDtypeStruct((m, d), F32),
        scratch_shapes=[pltpu.VMEM((bm, d), BF16), pltpu.VMEM((bm, d), F32)],
        compiler_params=_cparams(("parallel", "arbitrary")),
        name="ffn",
    )(x, g, w1, w3, w2, gf)


def _mix_body(x_ref, ys_ref, gy_ref, o_ref, ga_ref, gb_ref, gc_ref, wssd_ref, wa_ref, wb_ref, whg_ref,
              wmix_ref, out_ref, acc_ref):
    j = pl.program_id(1)

    @pl.when(j == 0)
    def _():
        acc_ref[...] = jnp.zeros_like(acc_ref)

    gy = gy_ref[...].astype(BF16)
    y_a = _dot(ys_ref[...], wssd_ref[...])
    y_b = _dot(gy, wa_ref[...]) * jax.nn.sigmoid(_dot(gy, wb_ref[...]))
    y_c = _dot(o_ref[...], whg_ref[...])
    mix = (jax.nn.sigmoid(ga_ref[...]) * y_a + jax.nn.sigmoid(gb_ref[...]) * y_b
           + jax.nn.sigmoid(gc_ref[...]) * y_c)
    acc_ref[...] += _dot(mix.astype(BF16), wmix_ref[...])

    @pl.when(j == pl.num_programs(1) - 1)
    def _():
        out_ref[...] = x_ref[...] + acc_ref[...]


def branch_mix(x, ys, gy, o, p, wssd, wa, wb, whg, wmix, layer, bm=512, bn=512):
    m, d = x.shape
    nj = d // bn
    return pl.pallas_call(
        _mix_body,
        grid=(m // bm, nj),
        in_specs=[pl.BlockSpec((bm, d), lambda i, j: (i, 0)),
                  pl.BlockSpec((bm, ys.shape[1]), lambda i, j: (i, 0)),
                  pl.BlockSpec((bm, gy.shape[1]), lambda i, j: (i, 0)),
                  pl.BlockSpec((bm, o.shape[1]), lambda i, j: (i, 0)),
                  pl.BlockSpec((bm, bn), lambda i, j: (i, j)),
                  pl.BlockSpec((bm, bn), lambda i, j: (i, nj + j)),
                  pl.BlockSpec((bm, bn), lambda i, j: (i, 2 * nj + j)),
                  pl.BlockSpec((None, wssd.shape[1], bn), lambda i, j: (layer, 0, j)),
                  pl.BlockSpec((None, wa.shape[1], bn), lambda i, j: (layer, 0, j)),
                  pl.BlockSpec((None, wb.shape[1], bn), lambda i, j: (layer, 0, j)),
                  pl.BlockSpec((None, whg.shape[1], bn), lambda i, j: (layer, 0, j)),
                  pl.BlockSpec((None, bn, d), lambda i, j: (layer, j, 0))],
        out_specs=pl.BlockSpec((bm, d), lambda i, j: (i, 0)),
        out_shape=jax.ShapeDtypeStruct((m, d), F32),
        scratch_shapes=[pltpu.VMEM((bm, d), F32)],
        compiler_params=_cparams(("parallel", "arbitrary")),
        name="branch_mix",
    )(x, ys, gy, o, p, p, p, wssd, wa, wb, whg, wmix)


def _attend(q, k_head, v_head):
    outs = []
    for h in range(XA_HEADS):
        sl = slice(h * XA_HEAD_DIM, (h + 1) * XA_HEAD_DIM)
        s = _dot_nt(q[:, sl], k_head(h).astype(BF16)) * (XA_HEAD_DIM ** -0.5)
        s = s - jnp.max(s, axis=-1, keepdims=True)
        e = jnp.exp(s)
        p = e / jnp.sum(e, axis=-1, keepdims=True)
        outs.append(_dot(p.astype(BF16), v_head(h).astype(BF16)))
    return jnp.concatenate(outs, axis=-1)


def _xattn_prompt_body(q_ref, kv_ref, o_ref):
    k_head = lambda h: kv_ref[:, h * XA_HEAD_DIM:(h + 1) * XA_HEAD_DIM]
    v_head = lambda h: kv_ref[:, D_MODEL + h * XA_HEAD_DIM:D_MODEL + (h + 1) * XA_HEAD_DIM]
    o_ref[...] = _attend(q_ref[...], k_head, v_head).astype(BF16)


def xattn_prompt(q, kv, n_seq, seq_len, bl=512):
    nl = seq_len // bl
    return pl.pallas_call(
        _xattn_prompt_body,
        grid=(n_seq, nl),
        in_specs=[pl.BlockSpec((bl, D_MODEL), lambda b, i: (b * nl + i, 0)),
                  pl.BlockSpec((MEM_LEN, 2 * D_MODEL), lambda b, i: (b, 0))],
        out_specs=pl.BlockSpec((bl, D_MODEL), lambda b, i: (b * nl + i, 0)),
        out_shape=jax.ShapeDtypeStruct(q.shape, BF16),
        compiler_params=_cparams(("parallel", "arbitrary")),
        name="xattn_prompt",
    )(q, kv)


def _xattn_sample_body(q_ref, k_ref, v_ref, buf_ref, o_ref, *, seq_len):
    del buf_ref
    q = q_ref[...]
    r = 2 * seq_len
    nc = XA_HEAD_DIM // LANES
    piece = lambda ref, b, h, c: ref[b, pl.ds(c * XA_HEADS + h, MEM_LEN, stride=nc * XA_HEADS), :].astype(BF16)
    scores = []
    for b in range(2):
        for h in range(XA_HEADS):
            s = None
            for c in range(nc):
                lo = h * XA_HEAD_DIM + c * LANES
                d = _dot_nt(q[:, lo:lo + LANES], piece(k_ref, b, h, c))
                s = d if s is None else s + d
            scores.append(s)
    s = jnp.concatenate(scores, axis=0) * (XA_HEAD_DIM ** -0.5)
    s = s - jnp.max(s, axis=-1, keepdims=True)
    e = jnp.exp(s)
    p = (e / jnp.sum(e, axis=-1, keepdims=True)).astype(BF16)
    rows = lax.broadcasted_iota(jnp.int32, (r, LANES), 0)
    for h in range(XA_HEADS):
        for c in range(nc):
            lo = h * XA_HEAD_DIM + c * LANES
            o0 = _dot(p[h * r:(h + 1) * r], piece(v_ref, 0, h, c))
            o1 = _dot(p[(XA_HEADS + h) * r:(XA_HEADS + h + 1) * r], piece(v_ref, 1, h, c))
            o_ref[:, lo:lo + LANES] = jnp.where(rows < seq_len, o0, o1).astype(BF16)


def xattn_sample(q, k, v, buf, layer, row0, n_seq, seq_len):
    r = 2 * seq_len
    blk0 = row0 // r
    mem_spec = pl.BlockSpec((None, 2, MEM_LEN * D_MODEL // LANES, LANES), lambda i: (layer, i, 0, 0))
    return pl.pallas_call(
        functools.partial(_xattn_sample_body, seq_len=seq_len),
        grid=(n_seq // 2,),
        in_specs=[pl.BlockSpec((r, D_MODEL), lambda i: (blk0 + i, 0)), mem_spec, mem_spec,
                  pl.BlockSpec(memory_space=pl.ANY)],
        out_specs=pl.BlockSpec((r, D_MODEL), lambda i: (blk0 + i, 0)),
        out_shape=jax.ShapeDtypeStruct(buf.shape, BF16),
        input_output_aliases={3: 0},
        compiler_params=_cparams(("parallel",)),
        name="xattn_sample",
    )(q, k, v, buf)


LANES = 128
HALO = 8


def _softplus(x):
    return jnp.maximum(x, 0.0) + jnp.log1p(jnp.exp(-jnp.abs(x)))


def _pad_rows(x, rows):
    if x.shape[0] == rows:
        return x
    return jnp.concatenate([x, jnp.zeros((rows - x.shape[0], x.shape[1]), x.dtype)], axis=0)


def _tile_consts(r, t):
    i = np.arange(r)[:, None]
    j = np.arange(r)[None, :]
    same = (i // t) == (j // t)
    lt = (same & (j <= i)).astype(np.float32)
    last = (same & (j % t == t - 1)).astype(np.float32)
    return jnp.asarray(lt, BF16), jnp.asarray(last, BF16)


def _head_expand(n_heads, width, rows=LANES):
    e = np.zeros((rows, n_heads * width), np.float32)
    for h in range(n_heads):
        e[h, h * width:(h + 1) * width] = 1.0
    return jnp.asarray(e, BF16)


def _ssd_tile(xc, dt_raw, dtb, a_log, lt, last, e, e128, r, t):
    xs = xc[:, :SSD_WIDTH]
    dt = _softplus(dt_raw + dtb)
    d_a = dt * (-jnp.exp(a_log))
    a = _sel_dot(lt, d_a, 3)
    a_e = _dot_sel(a, e, 3)
    dt_e = _dot_sel(dt, e, 2)
    alast_e = _sel_dot(last, a_e, 3)
    a_col = _dot_sel(a, e128, 3)
    a_t = _pad_rows(a, LANES).T

    row = lax.broadcasted_iota(jnp.int32, (r, LANES), 0)
    col = lax.broadcasted_iota(jnp.int32, (r, LANES), 1)
    valid = (col <= row) & (col >= (row // t) * t)
    lane_lo = col < SSD_HEAD_DIM

    xdt = _pad_rows((xs * dt_e).astype(BF16), LANES)
    hpg = SSD_HEADS // SSD_GROUPS
    ys = []
    for g in range(SSD_GROUPS):
        bg = xc[:, SSD_WIDTH + g * SSD_STATE:SSD_WIDTH + (g + 1) * SSD_STATE].astype(BF16)
        cg = xc[:, SSD_WIDTH + (SSD_GROUPS + g) * SSD_STATE:
                SSD_WIDTH + (SSD_GROUPS + g + 1) * SSD_STATE].astype(BF16)
        cb = _dot_nt(cg, _pad_rows(bg, LANES))
        for hp in range(hpg // 2):
            h0 = g * hpg + 2 * hp
            res = []
            for h in (h0, h0 + 1):
                rel = a_col[:, h * LANES:(h + 1) * LANES] - a_t[h:h + 1, :]
                dec = jnp.where(valid, jnp.exp(jnp.where(valid, rel, 0.0)), 0.0)
                res.append(_dot((cb * dec).astype(BF16), xdt[:, h0 * SSD_HEAD_DIM:(h0 + 2) * SSD_HEAD_DIM]))
            ys.append(jnp.where(lane_lo, res[0], res[1]))
    y_intra = jnp.concatenate(ys, axis=-1)
    return xs, y_intra, a_e, dt_e, alast_e


def _ssd_finish(y, xs, z, dsk, ng):
    y = y + dsk * xs
    y = y * _silu(z)
    return (y * _rms_scale(y) * ng).astype(BF16)


def _conv_silu(ext_ref, cw_ref, cb_ref, base, r):
    acc = cb_ref[...] + cw_ref[SSD_CONV - 1:SSD_CONV, :] * ext_ref[pl.ds(base, r), :]
    for k in range(1, SSD_CONV):
        acc = acc + cw_ref[SSD_CONV - 1 - k:SSD_CONV - k, :] * ext_ref[pl.ds(base - k, r), :]
    return _silu(acc)


def _ssd_prompt_body(xbc_ref, z_ref, dt_ref, cw_ref, cb_ref, dtb_ref, alog_ref, dsk_ref, ng_ref,
                     lt_ref, last_ref, e_ref, e128_ref, y_ref, sout_ref, ext_ref, st_ref, *, r):
    c = pl.program_id(1)

    @pl.when(c == 0)
    def _():
        ext_ref[0:HALO, :] = jnp.zeros((HALO, SSD_CONV_DIM), F32)
        st_ref[...] = jnp.zeros_like(st_ref)

    ext_ref[HALO:HALO + r, :] = xbc_ref[...]
    xc = _conv_silu(ext_ref, cw_ref, cb_ref, HALO, r)
    ext_ref[0:HALO, :] = xbc_ref[r - HALO:r, :]

    xs, y, a_e, dt_e, alast_e = _ssd_tile(xc, dt_ref[...], dtb_ref[...], alog_ref[...], lt_ref[...],
                                          last_ref[...], e_ref[...], e128_ref[...], r, r)
    ea_e = jnp.exp(a_e)
    xw = (xs * (dt_e * jnp.exp(alast_e - a_e))).astype(BF16)
    sdec = jnp.exp(alast_e[0:1, :])
    gw = SSD_WIDTH // SSD_GROUPS
    inter = []
    for g in range(SSD_GROUPS):
        bg = xc[:, SSD_WIDTH + g * SSD_STATE:SSD_WIDTH + (g + 1) * SSD_STATE].astype(BF16)
        cg = xc[:, SSD_WIDTH + (SSD_GROUPS + g) * SSD_STATE:
                SSD_WIDTH + (SSD_GROUPS + g + 1) * SSD_STATE].astype(BF16)
        st = st_ref[:, g * gw:(g + 1) * gw]
        inter.append(_dot(cg, st.astype(BF16)))
        st_ref[:, g * gw:(g + 1) * gw] = st * sdec[:, g * gw:(g + 1) * gw] + _dot_tn(bg, xw[:, g * gw:(g + 1) * gw])
    y = y + jnp.concatenate(inter, axis=-1) * ea_e
    y_ref[...] = _ssd_finish(y, xs, z_ref[...], dsk_ref[...], ng_ref[...])

    @pl.when(c == pl.num_programs(1) - 1)
    def _():
        sout_ref[0] = st_ref[...].T


def _ssd_consts(r, t):
    lt, last = _tile_consts(r, t)
    return lt, last, _head_expand(SSD_HEADS, SSD_HEAD_DIM), _head_expand(SSD_HEADS, LANES)


def ssd_prompt(p, dt, prm, n_seq, seq_len, r=128):
    nc = seq_len // r
    consts = _ssd_consts(r, r)
    full = lambda a: pl.BlockSpec(a.shape, lambda b, c: (0,) * a.ndim)
    small = [prm["conv_w"], prm["conv_b"], prm["dt_bias"], prm["a_log"], prm["d_skip"], prm["norm"], *consts]
    return pl.pallas_call(
        functools.partial(_ssd_prompt_body, r=r),
        grid=(n_seq, nc),
        in_specs=[pl.BlockSpec((r, SSD_CONV_DIM), lambda b, c: (b * nc + c, P_XBC // SSD_CONV_DIM)),
                  pl.BlockSpec((r, SSD_WIDTH), lambda b, c: (b * nc + c, P_Z // SSD_WIDTH)),
                  pl.BlockSpec((r, LANES), lambda b, c: (b * nc + c, 0)),
                  *[full(a) for a in small]],
        out_specs=[pl.BlockSpec((r, SSD_WIDTH), lambda b, c: (b * nc + c, 0)),
                   pl.BlockSpec((1, SSD_WIDTH, SSD_STATE), lambda b, c: (b, 0, 0))],
        out_shape=[jax.ShapeDtypeStruct((p.shape[0], SSD_WIDTH), BF16),
                   jax.ShapeDtypeStruct((n_seq, SSD_WIDTH, SSD_STATE), F32)],
        scratch_shapes=[pltpu.VMEM((HALO + r, SSD_CONV_DIM), F32), pltpu.VMEM((SSD_STATE, SSD_WIDTH), F32)],
        compiler_params=_cparams(("parallel", "arbitrary")),
        name="ssd_prompt",
    )(p, p, dt, *small)


def _ssd_sample_body(xbc_ref, z_ref, dt_ref, buf_ref, s0_ref, cw_ref, cb_ref, dtb_ref, alog_ref, dsk_ref,
                     ng_ref, lt_ref, last_ref, e_ref, e128_ref, *rest, r, t):
    y_ref, sout_ref, ext_ref = rest[-3:]
    nb = r // t
    pitch = HALO + t
    for b in range(nb):
        ext_ref[b * pitch + HALO - (SSD_CONV - 1):b * pitch + HALO, :] = buf_ref[b]
        ext_ref[b * pitch + HALO:(b + 1) * pitch, :] = xbc_ref[b * t:(b + 1) * t, :]
    xc = jnp.concatenate([_conv_silu(ext_ref, cw_ref, cb_ref, b * pitch + HALO, t) for b in range(nb)], axis=0)

    xs, y, a_e, dt_e, alast_e = _ssd_tile(xc, dt_ref[...], dtb_ref[...], alog_ref[...], lt_ref[...],
                                          last_ref[...], e_ref[...], e128_ref[...], r, t)
    ea_e = jnp.exp(a_e)
    xw = xs * (dt_e * jnp.exp(alast_e - a_e))
    sdec = jnp.exp(alast_e)
    gw = SSD_WIDTH // SSD_GROUPS
    pr = 2 * t
    prow = lax.broadcasted_iota(jnp.int32, (pr, 1), 0)
    ones = jnp.ones((pr, SSD_STATE), BF16)
    inter = []
    for g in range(SSD_GROUPS):
        bg = xc[:, SSD_WIDTH + g * SSD_STATE:SSD_WIDTH + (g + 1) * SSD_STATE].astype(BF16)
        cg = xc[:, SSD_WIDTH + (SSD_GROUPS + g) * SSD_STATE:
                SSD_WIDTH + (SSD_GROUPS + g + 1) * SSD_STATE].astype(BF16)
        cols = slice(g * gw, (g + 1) * gw)
        rows_out = []
        for q in range(nb // 2):
            rs = slice(q * pr, (q + 1) * pr)
            acc = None
            for s in range(2):
                b = 2 * q + s
                mine = (prow >= s * t) & (prow < (s + 1) * t)
                s0 = s0_ref[b, cols, :]
                yi = _dot_nt(cg[rs], s0.astype(BF16))
                acc = jnp.where(mine, yi, 0.0) if acc is None else acc + jnp.where(mine, yi, 0.0)
                upd = _dot_tn(jnp.where(mine, xw[rs, cols], 0.0).astype(BF16), bg[rs])
                lastrow = prow == (s + 1) * t - 1
                dh, dl = _split2(jnp.where(lastrow, sdec[rs, cols], 0.0))
                dcol = _dot_tn(dh, ones) + _dot_tn(dl, ones)
                sout_ref[b, cols, :] = s0 * dcol + upd
            rows_out.append(acc)
        inter.append(jnp.concatenate(rows_out, axis=0))
    y = y + jnp.concatenate(inter, axis=-1) * ea_e
    y_ref[...] = _ssd_finish(y, xs, z_ref[...], dsk_ref[...], ng_ref[...])


def ssd_sample(p, dt, conv_buf, s0, ybuf, sprev, prm, layer, row0, n_seq, seq_len, r=64):
    nb = r // seq_len
    blk0 = row0 // r
    consts = _ssd_consts(r, seq_len)
    full = lambda a: pl.BlockSpec(a.shape, lambda i: (0,) * a.ndim)
    small = [prm["conv_w"], prm["conv_b"], prm["dt_bias"], prm["a_log"], prm["d_skip"], prm["norm"], *consts]
    inplace = [ybuf] if sprev is None else [ybuf, sprev]
    n_in = 5 + len(small)
    return pl.pallas_call(
        functools.partial(_ssd_sample_body, r=r, t=seq_len),
        grid=(n_seq // nb,),
        in_specs=[pl.BlockSpec((r, SSD_CONV_DIM), lambda i: (blk0 + i, P_XBC // SSD_CONV_DIM)),
                  pl.BlockSpec((r, SSD_WIDTH), lambda i: (blk0 + i, P_Z // SSD_WIDTH)),
                  pl.BlockSpec((r, LANES), lambda i: (blk0 + i, 0)),
                  pl.BlockSpec((None, nb, SSD_CONV - 1, SSD_CONV_DIM), lambda i: (layer, i, 0, 0)),
                  pl.BlockSpec((None, nb, SSD_WIDTH, SSD_STATE), lambda i: (layer, i, 0, 0)),
                  *[full(a) for a in small],
                  *[pl.BlockSpec(memory_space=pl.ANY) for _ in inplace]],
        out_specs=[pl.BlockSpec((r, SSD_WIDTH), lambda i: (blk0 + i, 0)),
                   pl.BlockSpec((None, nb, SSD_WIDTH, SSD_STATE), lambda i: (layer, i, 0, 0))],
        out_shape=[jax.ShapeDtypeStruct(ybuf.shape, BF16),
                   jax.ShapeDtypeStruct(s0.shape, F32)],
        input_output_aliases={n_in + k: k for k in range(len(inplace))},
        scratch_shapes=[pltpu.VMEM((nb * (HALO + seq_len), SSD_CONV_DIM), F32)],
        compiler_params=_cparams(("parallel",)),
        name="ssd_sample",
    )(p, p, dt, conv_buf, s0, *small, *inplace)


S5_CH = S5_GROUPS * S5_STATE
S5_BLK = 4
S5_SSEQ = 32


def _s5_in(u, bre_ref, bim_ref):
    ub = u.astype(BF16)
    kin = S5_WIDTH // S5_BLK
    re = [_dot(ub[:, q * kin:(q + 1) * kin], bre_ref[q]) for q in range(S5_BLK)]
    im = [_dot(ub[:, q * kin:(q + 1) * kin], bim_ref[q]) for q in range(S5_BLK)]
    return jnp.concatenate(re, axis=-1), jnp.concatenate(im, axis=-1)


def _s5_out(h_re, h_im, u, cre_ref, cim_ref, dsk):
    kst = S5_CH // S5_BLK
    hr = h_re.astype(BF16)
    hi = h_im.astype(BF16)
    y = [_dot(hr[:, q * kst:(q + 1) * kst], cre_ref[q]) - _dot(hi[:, q * kst:(q + 1) * kst], cim_ref[q])
         for q in range(S5_BLK)]
    y = jnp.concatenate(y, axis=-1) + dsk * u
    return jax.nn.gelu(y).astype(BF16)


def _s5_scan_body(u_ref, h0re_ref, h0im_ref, are_ref, aim_ref, bre_ref, bim_ref, cre_ref, cim_ref, dsk_ref,
                  y_ref, hre_ref, him_ref, sre_ref, sim_ref, *, s, tc):
    c = pl.program_id(1)

    @pl.when(c == 0)
    def _():
        sre_ref[0:s, :] = h0re_ref[...]
        sim_ref[0:s, :] = h0im_ref[...]

    u = u_ref[...]
    bu_re, bu_im = _s5_in(u, bre_ref, bim_ref)
    sre_ref[s:, :] = bu_re
    sim_ref[s:, :] = bu_im
    a_re = jnp.broadcast_to(are_ref[...], (s, S5_CH))
    a_im = jnp.broadcast_to(aim_ref[...], (s, S5_CH))

    def step(t, carry):
        prev = pl.ds(pl.multiple_of(t * s, s), s)
        cur = pl.ds(pl.multiple_of((t + 1) * s, s), s)
        h_re = sre_ref[prev, :]
        h_im = sim_ref[prev, :]
        sre_ref[cur, :] = a_re * h_re - a_im * h_im + sre_ref[cur, :]
        sim_ref[cur, :] = a_re * h_im + a_im * h_re + sim_ref[cur, :]
        return carry

    lax.fori_loop(0, tc, step, 0, unroll=4)
    h_re = sre_ref[tc * s:, :]
    h_im = sim_ref[tc * s:, :]
    y_ref[...] = _s5_out(sre_ref[s:, :], sim_ref[s:, :], u, cre_ref, cim_ref, dsk_ref[...])
    sre_ref[0:s, :] = h_re
    sim_ref[0:s, :] = h_im
    hre_ref[...] = h_re
    him_ref[...] = h_im


def s5_scan(u, h0_re, h0_im, prm, n_blocks, s, n_steps, tc):
    nc = n_steps // tc
    r = tc * s
    full = lambda a: pl.BlockSpec(a.shape, lambda b, c: (0,) * a.ndim)
    small = [prm["abar_re"], prm["abar_im"], prm["b_re"], prm["b_im"], prm["c_re"], prm["c_im"], prm["d_skip"]]
    state = jax.ShapeDtypeStruct((n_blocks * s, S5_CH), F32)
    sspec = pl.BlockSpec((s, S5_CH), lambda b, c: (b, 0))
    return pl.pallas_call(
        functools.partial(_s5_scan_body, s=s, tc=tc),
        grid=(n_blocks, nc),
        in_specs=[pl.BlockSpec((r, S5_WIDTH), lambda b, c: (b * nc + c, 0)), sspec, sspec,
                  *[full(a) for a in small]],
        out_specs=[pl.BlockSpec((r, S5_WIDTH), lambda b, c: (b * nc + c, 0)), sspec, sspec],
        out_shape=[jax.ShapeDtypeStruct((u.shape[0], S5_WIDTH), BF16), state, state],
        scratch_shapes=[pltpu.VMEM((s + r, S5_CH), F32), pltpu.VMEM((s + r, S5_CH), F32)],
        compiler_params=_cparams(("parallel", "arbitrary")),
        name="s5_scan",
    )(u, h0_re, h0_im, *small)


S5_T = 16
S5_GB = LANES // S5_GROUP_SIZE
S5_CL = S5_T * LANES
S5_SL = S5_GB * S5_STATE
S5_NB = 2
S5_PITCH_PAD = 8


def _block_diag_rows(x, rows_w, cols_w):
    xt = jnp.concatenate([x] * S5_GB, axis=0)
    r = lax.broadcasted_iota(jnp.int32, xt.shape, 0) // rows_w
    c = lax.broadcasted_iota(jnp.int32, xt.shape, 1) // cols_w
    return jnp.where(r == c, xt, jnp.zeros_like(xt))


def _s5_chunk_body(u_ref, kj_ref, bcre_ref, bcim_ref, ccre_ref, ccim_ref, are_ref, aim_ref, dsk_ref,
                   y_ref, hre_ref, him_ref, vre_ref, vim_ref, sre_ref, sim_ref,
                   tp_ref, bre_ref, bim_ref, cre_ref, cim_ref, *, n_chunks):
    ks = S5_GROUP_SIZE

    @pl.when(pl.program_id(1) == 0)
    def _():
        zero = jnp.zeros((LANES, LANES), BF16)
        for tau in range(S5_T):
            piece = _block_diag_rows(kj_ref[0, tau], ks, ks)
            for s in range(S5_T - tau):
                tp_ref[s * LANES:(s + 1) * LANES, (s + tau) * LANES:(s + tau + 1) * LANES] = piece
        for s in range(1, S5_T):
            for t in range(s):
                tp_ref[s * LANES:(s + 1) * LANES, t * LANES:(t + 1) * LANES] = zero
        for s in range(S5_T):
            bre_ref[s * LANES:(s + 1) * LANES, :] = _block_diag_rows(bcre_ref[0, s], ks, S5_STATE)
            bim_ref[s * LANES:(s + 1) * LANES, :] = _block_diag_rows(bcim_ref[0, s], ks, S5_STATE)
            cre_ref[:, s * LANES:(s + 1) * LANES] = _block_diag_rows(ccre_ref[0, s], S5_STATE, ks)
            cim_ref[:, s * LANES:(s + 1) * LANES] = _block_diag_rows(ccim_ref[0, s], S5_STATE, ks)

    rows = S5_NB * n_chunks
    pitch = n_chunks + S5_PITCH_PAD
    nk = S5_SL // LANES
    u = jnp.concatenate([u_ref[pl.ds(t, rows, stride=S5_T), :] for t in range(S5_T)], axis=-1)
    ub = u.astype(BF16)
    v_re = _dot(ub, bre_ref[...])
    v_im = _dot(ub, bim_ref[...])
    for k in range(nk):
        vre_ref[k] = v_re[:, k * LANES:(k + 1) * LANES]
        vim_ref[k] = v_im[:, k * LANES:(k + 1) * LANES]
        sre_ref[k] = jnp.zeros((S5_NB * pitch, LANES), F32)
        sim_ref[k] = jnp.zeros((S5_NB * pitch, LANES), F32)
    a_re = [jnp.broadcast_to(are_ref[0, :, k * LANES:(k + 1) * LANES], (S5_NB, LANES)) for k in range(nk)]
    a_im = [jnp.broadcast_to(aim_ref[0, :, k * LANES:(k + 1) * LANES], (S5_NB, LANES)) for k in range(nk)]

    def step(c, carry):
        for k in range(nk):
            h_re = sre_ref[k, pl.ds(c, S5_NB, stride=pitch), :]
            h_im = sim_ref[k, pl.ds(c, S5_NB, stride=pitch), :]
            x_re = vre_ref[k, pl.ds(c, S5_NB, stride=n_chunks), :]
            x_im = vim_ref[k, pl.ds(c, S5_NB, stride=n_chunks), :]
            sre_ref[k, pl.ds(c + 1, S5_NB, stride=pitch), :] = a_re[k] * h_re - a_im[k] * h_im + x_re
            sim_ref[k, pl.ds(c + 1, S5_NB, stride=pitch), :] = a_re[k] * h_im + a_im[k] * h_re + x_im
        return carry

    lax.fori_loop(0, n_chunks, step, 0, unroll=2)
    starts = lambda ref: jnp.concatenate(
        [jnp.concatenate([ref[k, b * pitch:b * pitch + n_chunks, :] for b in range(S5_NB)], axis=0)
         for k in range(nk)], axis=-1)
    y = _dot(ub, tp_ref[...])
    y = y + _dot(starts(sre_ref).astype(BF16), cre_ref[...]) - _dot(starts(sim_ref).astype(BF16), cim_ref[...])
    y = jax.nn.gelu(y + dsk_ref[0] * u)
    for t in range(S5_T):
        y_ref[pl.ds(t, rows, stride=S5_T), :] = y[:, t * LANES:(t + 1) * LANES]
    final = lambda ref: jnp.concatenate(
        [jnp.concatenate([ref[k, b * pitch + n_chunks:b * pitch + n_chunks + 1, :] for b in range(S5_NB)], axis=0)
         for k in range(nk)], axis=-1)
    hre_ref[...] = final(sre_ref)
    him_ref[...] = final(sim_ref)


def s5_chunked(p, prm, n_seq, seq_len):
    n_chunks = seq_len // S5_T
    rows = S5_NB * n_chunks
    nblk = S5_GROUPS // S5_GB
    nhalf = n_seq // S5_NB
    pitch = n_chunks + S5_PITCH_PAD
    nk = S5_SL // LANES
    blk = lambda a: pl.BlockSpec((1,) + a.shape[1:], lambda j, i: (j,) + (0,) * (a.ndim - 1))
    small = [prm["kern_j"], prm["bst_re"], prm["bst_im"], prm["cst_re"], prm["cst_im"], prm["apow_re"],
             prm["apow_im"], prm["d_tiled"]]
    state = jax.ShapeDtypeStruct((nhalf, S5_NB, S5_CH), F32)
    sspec = pl.BlockSpec((None, S5_NB, S5_SL), lambda j, i: (i, 0, j))
    return pl.pallas_call(
        functools.partial(_s5_chunk_body, n_chunks=n_chunks),
        grid=(nblk, nhalf),
        in_specs=[pl.BlockSpec((S5_NB * seq_len, LANES), lambda j, i: (i, P_U // LANES + j)),
                  *[blk(a) for a in small]],
        out_specs=[pl.BlockSpec((S5_NB * seq_len, LANES), lambda j, i: (i, j)), sspec, sspec],
        out_shape=[jax.ShapeDtypeStruct((p.shape[0], S5_WIDTH), F32), state, state],
        scratch_shapes=[pltpu.VMEM((nk, rows, LANES), F32), pltpu.VMEM((nk, rows, LANES), F32),
                        pltpu.VMEM((nk, S5_NB * pitch, LANES), F32), pltpu.VMEM((nk, S5_NB * pitch, LANES), F32),
                        pltpu.VMEM((S5_CL, S5_CL), BF16), pltpu.VMEM((S5_CL, S5_SL), BF16),
                        pltpu.VMEM((S5_CL, S5_SL), BF16), pltpu.VMEM((S5_SL, S5_CL), BF16),
                        pltpu.VMEM((S5_SL, S5_CL), BF16)],
        compiler_params=_cparams(("parallel", "arbitrary")),
        name="s5_chunked",
    )(p, *small)


def _hg_consts(r, t):
    i = np.arange(r)[:, None]
    j = np.arange(r)[None, :]
    sums, upper, pair = [], [], []
    s = 1
    while s < t:
        blk_i, blk_j = i // (2 * s), j // (2 * s)
        up_i = (i % (2 * s)) >= s
        mid_i = blk_i * 2 * s + s
        m_up = up_i & (j >= mid_i) & (j <= i)
        m_lo = (~up_i) & (j > i) & (j < mid_i)
        sums.append((m_up | m_lo).astype(np.float32))
        upper.append(np.broadcast_to(up_i, (r, 1)).astype(np.float32))
        pair.append(((blk_i == blk_j) & up_i & ((j % (2 * s)) < s)).astype(np.float32))
        s *= 2
    pair.append((i == j).astype(np.float32))
    return (jnp.asarray(np.stack(sums), BF16), jnp.asarray(np.stack(upper), F32),
            jnp.asarray(np.stack(pair), F32))


def _hg_gates(hf, lb):
    logf = -_softplus(-hf) + jnp.log1p(lb * jnp.exp(-hf))
    kk = (1.0 - lb) * jax.nn.sigmoid(-hf)
    return logf, kk


def _hg_intra(q, kk, v, logf, sums_ref, upper_ref, pair_ref):
    r = q.shape[0]
    nlev = sums_ref.shape[0]
    lf3 = _split3(logf)
    qb = q.astype(BF16)
    kb = kk.astype(BF16)
    vb = _pad_rows(v.astype(BF16), LANES) if r < LANES else v.astype(BF16)
    scores = [None] * HG_HEADS
    for lev in range(nlev + 1):
        if lev < nlev:
            m = sums_ref[lev]
            d = _dot(m, lf3[0]) + _dot(m, lf3[1]) + _dot(m, lf3[2])
            x = (jnp.where(upper_ref[lev] > 0.5, q, kk) * jnp.exp(d)).astype(BF16)
            xq, xk = x, x
        else:
            xq, xk = qb, kb
        mask = pair_ref[lev]
        for h in range(HG_HEADS):
            sl = slice(h * HG_KEY_DIM, (h + 1) * HG_KEY_DIM)
            sc = _dot_nt(xq[:, sl], xk[:, sl]) * mask
            scores[h] = sc if scores[h] is None else scores[h] + sc
    outs = [_dot(scores[h].astype(BF16), vb[:, h * HG_VAL_DIM:(h + 1) * HG_VAL_DIM]) for h in range(HG_HEADS)]
    return jnp.concatenate(outs, axis=-1)


def _hg_finish(o, hgate, ng):
    outs = []
    for h in range(HG_HEADS):
        oh = o[:, h * HG_VAL_DIM:(h + 1) * HG_VAL_DIM]
        outs.append(oh * _rms_scale(oh) * ng)
    return (jnp.concatenate(outs, axis=-1) * _silu(hgate)).astype(BF16)


def _hg_prompt_body(q_ref, f_ref, i_ref, gate_ref, lb_ref, ng_ref, lt_ref, last_ref, sums_ref, upper_ref,
                    pair_ref, o_ref, sout_ref, *, r):
    c = pl.program_id(1)

    @pl.when(c == 0)
    def _():
        sout_ref[...] = jnp.zeros_like(sout_ref)

    q = q_ref[...]
    v = i_ref[...]
    logf, kk = _hg_gates(f_ref[...], lb_ref[...])
    o = _hg_intra(q, kk, v, logf, sums_ref, upper_ref, pair_ref)
    b = _sel_dot(lt_ref[...], logf, 3)
    blast = b[r - 1:r, :]
    qe = (q * jnp.exp(b)).astype(BF16)
    kw = (kk * jnp.exp(blast - b)).astype(BF16)
    vb = v.astype(BF16)
    ones = jnp.ones((r, HG_VAL_DIM), BF16)
    rows = lax.broadcasted_iota(jnp.int32, (r, 1), 0)
    dh, dl = _split2(jnp.where(rows == r - 1, jnp.exp(b), 0.0))
    inter = []
    for h in range(HG_HEADS):
        sl = slice(h * HG_KEY_DIM, (h + 1) * HG_KEY_DIM)
        s = sout_ref[0, h]
        inter.append(_dot(qe[:, sl], s.astype(BF16)))
        dcol = _dot_tn(dh[:, sl], ones) + _dot_tn(dl[:, sl], ones)
        sout_ref[0, h] = s * dcol + _dot_tn(kw[:, sl], vb[:, sl])
    o = o + jnp.concatenate(inter, axis=-1)
    o_ref[...] = _hg_finish(o, gate_ref[...], ng_ref[...])


def hg_prompt(p, prm, n_seq, seq_len, r=128):
    nc = seq_len // r
    lt, last = _tile_consts(r, r)
    consts = [lt, last, *_hg_consts(r, r)]
    full = lambda a: pl.BlockSpec(a.shape, lambda b, c: (0,) * a.ndim)
    small = [prm["lb"], prm["norm"], *consts]
    col = lambda off: pl.BlockSpec((r, HG_WIDTH), lambda b, c: (b * nc + c, off // HG_WIDTH))
    return pl.pallas_call(
        functools.partial(_hg_prompt_body, r=r),
        grid=(n_seq, nc),
        in_specs=[col(P_HQ), col(P_HF), col(P_HI), col(P_HGATE), *[full(a) for a in small]],
        out_specs=[pl.BlockSpec((r, HG_WIDTH), lambda b, c: (b * nc + c, 0)),
                   pl.BlockSpec((1, HG_HEADS, HG_KEY_DIM, HG_VAL_DIM), lambda b, c: (b, 0, 0, 0))],
        out_shape=[jax.ShapeDtypeStruct((p.shape[0], HG_WIDTH), BF16),
                   jax.ShapeDtypeStruct((n_seq, HG_HEADS, HG_KEY_DIM, HG_VAL_DIM), F32)],
        compiler_params=_cparams(("parallel", "arbitrary")),
        name="hg_prompt",
    )(p, p, p, p, *small)


def _hg_sample_body(q_ref, f_ref, i_ref, gate_ref, s0_ref, lb_ref, ng_ref, lt_ref, last_ref, sums_ref,
                    upper_ref, pair_ref, *rest, r, t):
    o_ref, sout_ref = rest[-2:]
    q = q_ref[...]
    v = i_ref[...]
    logf, kk = _hg_gates(f_ref[...], lb_ref[...])
    o = _hg_intra(q, kk, v, logf, sums_ref, upper_ref, pair_ref)
    b = _sel_dot(lt_ref[...], logf, 3)
    blast = _sel_dot(last_ref[...], b, 3)
    qe = (q * jnp.exp(b)).astype(BF16)
    kw = kk * jnp.exp(blast - b)
    sdec = jnp.exp(blast)
    vb = v.astype(BF16)
    pr = 2 * t
    prow = lax.broadcasted_iota(jnp.int32, (pr, 1), 0)
    ones = jnp.ones((pr, HG_VAL_DIM), BF16)
    rows_out = []
    for p2 in range(r // pr):
        rs = slice(p2 * pr, (p2 + 1) * pr)
        heads = []
        for h in range(HG_HEADS):
            sl = slice(h * HG_KEY_DIM, (h + 1) * HG_KEY_DIM)
            acc = None
            for s in range(2):
                bi = 2 * p2 + s
                mine = (prow >= s * t) & (prow < (s + 1) * t)
                s0 = s0_ref[bi, h]
                oi = jnp.where(mine, _dot(qe[rs, sl], s0.astype(BF16)), 0.0)
                acc = oi if acc is None else acc + oi
                upd = _dot_tn(jnp.where(mine, kw[rs, sl], 0.0).astype(BF16), vb[rs, sl])
                dh, dl = _split2(jnp.where(prow == (s + 1) * t - 1, sdec[rs, sl], 0.0))
                dcol = _dot_tn(dh, ones) + _dot_tn(dl, ones)
                sout_ref[bi, h] = s0 * dcol + upd
            heads.append(acc)
        rows_out.append(jnp.concatenate(heads, axis=-1))
    o = o + jnp.concatenate(rows_out, axis=0)
    o_ref[...] = _hg_finish(o, gate_ref[...], ng_ref[...])


def hg_sample(p, s0, obuf, sprev, prm, layer, row0, n_seq, seq_len, r=128):
    nb = r // seq_len
    blk0 = row0 // r
    lt, last = _tile_consts(r, seq_len)
    consts = [lt, last, *_hg_consts(r, seq_len)]
    full = lambda a: pl.BlockSpec(a.shape, lambda i: (0,) * a.ndim)
    small = [prm["lb"], prm["norm"], *consts]
    col = lambda off: pl.BlockSpec((r, HG_WIDTH), lambda i: (blk0 + i, off // HG_WIDTH))
    sspec = pl.BlockSpec((None, nb, HG_HEADS, HG_KEY_DIM, HG_VAL_DIM), lambda i: (layer, i, 0, 0, 0))
    inplace = [obuf] if sprev is None else [obuf, sprev]
    n_in = 5 + len(small)
    return pl.pallas_call(
        functools.partial(_hg_sample_body, r=r, t=seq_len),
        grid=(n_seq // nb,),
        in_specs=[col(P_HQ), col(P_HF), col(P_HI), col(P_HGATE), sspec, *[full(a) for a in small],
                  *[pl.BlockSpec(memory_space=pl.ANY) for _ in inplace]],
        out_specs=[pl.BlockSpec((r, HG_WIDTH), lambda i: (blk0 + i, 0)), sspec],
        out_shape=[jax.ShapeDtypeStruct(obuf.shape, BF16), jax.ShapeDtypeStruct(s0.shape, F32)],
        input_output_aliases={n_in + k: k for k in range(len(inplace))},
        compiler_params=_cparams(("parallel",)),
        name="hg_sample",
    )(p, p, p, p, s0, *small, *inplace)


def _s5_params(a_re, a_im, log_dt, b_re, b_im, c_re, c_im, d_skip):
    dt = jnp.exp(log_dt)[:, None]
    mag = jnp.exp(a_re * dt)
    abar_re = mag * jnp.cos(a_im * dt)
    abar_im = mag * jnp.sin(a_im * dt)
    den = a_re * a_re + a_im * a_im
    nr = abar_re - 1.0
    coef_re = (nr * a_re + abar_im * a_im) / den
    coef_im = (abar_im * a_re - nr * a_im) / den
    bbar_re = coef_re[..., None] * b_re - coef_im[..., None] * b_im
    bbar_im = coef_re[..., None] * b_im + coef_im[..., None] * b_re
    gpb = S5_GROUPS // S5_BLK
    eye = jnp.eye(gpb, dtype=F32)

    def in_blocks(bbar):
        bb = bbar.reshape(S5_BLK, gpb, S5_STATE, S5_GROUP_SIZE)
        return jnp.einsum("qgnk,gh->qgkhn", bb, eye).reshape(
            S5_BLK, gpb * S5_GROUP_SIZE, gpb * S5_STATE).astype(BF16)

    def out_blocks(c):
        cc = c.reshape(S5_BLK, gpb, S5_GROUP_SIZE, S5_STATE)
        return jnp.einsum("qgkn,gh->qgnhk", cc, eye).reshape(
            S5_BLK, gpb * S5_STATE, gpb * S5_GROUP_SIZE).astype(BF16)

    prm = {
        "abar_re": abar_re.reshape(1, S5_CH), "abar_im": abar_im.reshape(1, S5_CH),
        "b_re": in_blocks(bbar_re), "b_im": in_blocks(bbar_im),
        "c_re": out_blocks(c_re), "c_im": out_blocks(c_im),
        "d_skip": d_skip.reshape(1, S5_WIDTH),
    }

    tau = jnp.arange(S5_T + 1, dtype=F32)[:, None, None]
    pmag = jnp.exp(tau * (a_re * dt))
    pw_re = pmag * jnp.cos(tau * (a_im * dt))
    pw_im = pmag * jnp.sin(tau * (a_im * dt))
    cp_re = c_re[None] * pw_re[:, :, None, :] - c_im[None] * pw_im[:, :, None, :]
    cp_im = c_re[None] * pw_im[:, :, None, :] + c_im[None] * pw_re[:, :, None, :]
    bt_re = bbar_re.transpose(0, 2, 1)[None, :, None]
    bt_im = bbar_im.transpose(0, 2, 1)[None, :, None]
    kern = jnp.sum(cp_re[:, :, :, None, :] * bt_re - cp_im[:, :, :, None, :] * bt_im, axis=-1)
    nq = S5_GROUPS // S5_GB
    ks = S5_GROUP_SIZE
    kern_j = (kern[:S5_T].reshape(S5_T, nq, S5_GB, ks, ks).transpose(1, 0, 4, 2, 3)
              .reshape(nq, S5_T, ks, LANES).astype(BF16))
    rev = pw_re[S5_T - 1 - jnp.arange(S5_T)], pw_im[S5_T - 1 - jnp.arange(S5_T)]
    bst_re = rev[0][..., None] * bbar_re[None] - rev[1][..., None] * bbar_im[None]
    bst_im = rev[0][..., None] * bbar_im[None] + rev[1][..., None] * bbar_re[None]

    def bst_blocks(b):
        return (b.reshape(S5_T, nq, S5_GB, S5_STATE, ks).transpose(1, 0, 4, 2, 3)
                .reshape(nq, S5_T, ks, S5_SL).astype(BF16))

    def cst_blocks(c):
        return (c.reshape(S5_T, nq, S5_GB, ks, S5_STATE).transpose(1, 0, 4, 2, 3)
                .reshape(nq, S5_T, S5_STATE, LANES).astype(BF16))

    prm.update({
        "kern_j": kern_j,
        "bst_re": bst_blocks(bst_re), "bst_im": bst_blocks(bst_im),
        "cst_re": cst_blocks(cp_re[1:]), "cst_im": cst_blocks(cp_im[1:]),
        "apow_re": pw_re[S5_T].reshape(nq, 1, S5_SL), "apow_im": pw_im[S5_T].reshape(nq, 1, S5_SL),
        "d_tiled": jnp.broadcast_to(d_skip.reshape(nq, 1, 1, LANES), (nq, 1, S5_T, LANES)).reshape(nq, 1, S5_CL),
    })
    return prm


def _in_proj_weights(w_in):
    z0, xbc0, dt0, u0, gates0 = 0, 2048, 5120, 5152, 10272
    w_in = w_in.astype(BF16)
    w = jnp.concatenate([w_in[..., gates0:], w_in[..., xbc0:dt0], w_in[..., u0:gates0], w_in[..., z0:xbc0]],
                        axis=-1)
    wdt = jnp.pad(w_in[..., dt0:u0], ((0, 0), (0, 0), (0, LANES - SSD_HEADS)))
    return w, wdt


def _mem_rows(cache):
    dd, b, m, h, hd = cache.shape
    c = cache.reshape(dd, b, m, h, hd // LANES, LANES).transpose(0, 1, 2, 4, 3, 5)
    return c.reshape(dd, b, m * h * (hd // LANES), LANES)


def _pad_lanes(v):
    return jnp.pad(v.reshape(1, -1), ((0, 0), (0, LANES - v.shape[-1])))


def kernel(x_prompt, x_sample, cache_mem_k, cache_mem_v, state_ssd, state_ssd_conv, state_s5_re, state_s5_im,
           state_hgrn, mem_prompt, norm_ffn1, ffn1_w1, ffn1_w3, ffn1_w2, norm_mix, w_in, ssd_conv_w, ssd_conv_b,
           ssd_dt_bias, ssd_a_log, ssd_d, ssd_norm, ssd_w_out, s5_a_re, s5_a_im, s5_log_dt, s5_b_re, s5_b_im,
           s5_c_re, s5_c_im, s5_d, s5_w_glu_a, s5_w_glu_b, hg_lower_bounds, hg_norm, hg_w_out, w_mix_out,
           norm_xa, norm_mem, xa_wq, xa_wk, xa_wv, xa_wo, norm_ffn2, ffn2_w1, ffn2_w3, ffn2_w2, norm_final):
    bp, lp, d = x_prompt.shape
    bs, ls, _ = x_sample.shape
    mp, ms = bp * lp, bs * ls
    x = jnp.concatenate([x_prompt.reshape(mp, d), x_sample.reshape(ms, d)], axis=0)
    mem = mem_prompt.reshape(bp * MEM_LEN, d)
    row = lambda v: v.reshape(1, -1)
    bf = lambda w: w.astype(BF16)

    lb_p = jax.nn.softmax(hg_lower_bounds, axis=0)
    lb_all = jnp.cumsum(lb_p, axis=0) - lb_p[0]

    ssd_s0 = state_ssd.reshape(DEPTH, bs, SSD_WIDTH, SSD_STATE)
    mem_k = _mem_rows(cache_mem_k)
    mem_v = _mem_rows(cache_mem_v)
    ssd_states = None
    hg_states = None

    f1_w1, f1_w3, f1_w2 = bf(ffn1_w1), bf(ffn1_w3), bf(ffn1_w2)
    f2_w1, f2_w3, f2_w2 = bf(ffn2_w1), bf(ffn2_w3), bf(ffn2_w2)
    w_p, w_dt = _in_proj_weights(w_in)
    b_ssd, b_glu_a, b_glu_b, b_hg, b_mix = bf(ssd_w_out), bf(s5_w_glu_a), bf(s5_w_glu_b), bf(hg_w_out), bf(w_mix_out)
    b_wq, b_wo = bf(xa_wq), bf(xa_wo)
    b_wkv = bf(jnp.concatenate([xa_wk, xa_wv], axis=-1))

    outs = {k: [] for k in ("pk", "pv", "pss", "pcv", "psr", "psi", "phg", "scv", "ssr", "ssi")}
    for l in range(DEPTH):
        x = ffn(x, row(norm_ffn1[l]), f1_w1, f1_w3, f1_w2, row(norm_final), l, False)

        p, dt = in_proj(x, row(norm_mix[l]), w_p, w_dt, l)

        ssd_prm = {"conv_w": ssd_conv_w[l], "conv_b": row(ssd_conv_b[l]), "dt_bias": _pad_lanes(ssd_dt_bias[l]),
                   "a_log": _pad_lanes(ssd_a_log[l]), "d_skip": row(jnp.repeat(ssd_d[l], SSD_HEAD_DIM)),
                   "norm": row(ssd_norm[l])}
        ys, ss_p = ssd_prompt(p, dt, ssd_prm, bp, lp)
        ys, ssd_states = ssd_sample(p, dt, state_ssd_conv, ssd_s0, ys, ssd_states, ssd_prm, l, mp, bs, ls)

        s5_prm = _s5_params(s5_a_re[l], s5_a_im[l], s5_log_dt[l], s5_b_re[l], s5_b_im[l], s5_c_re[l],
                            s5_c_im[l], s5_d[l])
        gy, sr_p, si_p = s5_chunked(p, s5_prm, bp, lp)
        u_s = p[mp:, P_U:P_U + S5_WIDTH].reshape(bs // S5_SSEQ, S5_SSEQ, ls, S5_WIDTH).transpose(0, 2, 1, 3)
        u_s = u_s.reshape(ms, S5_WIDTH)
        gy_s, sr_s, si_s = s5_scan(u_s, state_s5_re[l].reshape(bs, S5_CH), state_s5_im[l].reshape(bs, S5_CH),
                                   s5_prm, bs // S5_SSEQ, S5_SSEQ, ls, ls)
        gy_s = gy_s.reshape(bs // S5_SSEQ, ls, S5_SSEQ, S5_WIDTH).transpose(0, 2, 1, 3).reshape(ms, S5_WIDTH)
        gy = lax.dynamic_update_slice(gy, gy_s.astype(F32), (mp, 0))

        hg_prm = {"lb": row(lb_all[l]), "norm": row(hg_norm[l])}
        o, hg_p = hg_prompt(p, hg_prm, bp, lp)
        o, hg_states = hg_sample(p, state_hgrn, o, hg_states, hg_prm, l, mp, bs, ls)

        x = branch_mix(x, ys, gy, o, p, b_ssd, b_glu_a, b_glu_b, b_hg, b_mix, l)

        q = norm_proj(x, row(norm_xa[l]), b_wq, l, BF16)
        kv = norm_proj(mem, row(norm_mem[l]), b_wkv, l, F32)
        at = xattn_prompt(q, kv, bp, lp)
        at = xattn_sample(q, mem_k, mem_v, at, l, mp, bs, ls)
        x = res_mm(x, at, b_wo, l)

        x = ffn(x, row(norm_ffn2[l]), f2_w1, f2_w3, f2_w2, row(norm_final), l, l == DEPTH - 1)

        tail = SSD_CONV - 1
        outs["pk"].append(kv[:, :d].reshape(bp, MEM_LEN, XA_HEADS, XA_HEAD_DIM))
        outs["pv"].append(kv[:, d:].reshape(bp, MEM_LEN, XA_HEADS, XA_HEAD_DIM))
        outs["pss"].append(ss_p.reshape(bp, SSD_HEADS, SSD_HEAD_DIM, SSD_STATE))
        outs["pcv"].append(jnp.stack([p[(b + 1) * lp - tail:(b + 1) * lp, P_XBC:P_XBC + SSD_CONV_DIM]
                                      for b in range(bp)]))
        outs["psr"].append(sr_p.reshape(bp, S5_GROUPS, S5_STATE))
        outs["psi"].append(si_p.reshape(bp, S5_GROUPS, S5_STATE))
        outs["phg"].append(hg_p)
        outs["scv"].append(p[mp:, P_XBC:P_XBC + SSD_CONV_DIM].reshape(bs, ls, SSD_CONV_DIM)[:, ls - tail:])
        outs["ssr"].append(sr_s.reshape(bs, S5_GROUPS, S5_STATE))
        outs["ssi"].append(si_s.reshape(bs, S5_GROUPS, S5_STATE))

    st = lambda k: jnp.stack(outs[k])
    return (x[:mp].reshape(bp, lp, d), x[mp:].reshape(bs, ls, d),
            st("pk"), st("pv"), st("pss"), st("pcv"), st("psr"), st("psi"), st("phg"),
            ssd_states.reshape(DEPTH, bs, SSD_HEADS, SSD_HEAD_DIM, SSD_STATE), st("scv"), st("ssr"), st("ssi"),
            hg_states)
```

```python
import functools
import math

import jax
import jax.numpy as jnp
import numpy as np
from jax import lax
from jax.experimental import pallas as pl
from jax.experimental.pallas import tpu as pltpu

F32 = jnp.float32
BF16 = jnp.bfloat16

D_MODEL = 2048
DEPTH = 2
NORM_EPS = 1e-5
SSD_HEAD_DIM = 64
SSD_HEADS = 32
SSD_GROUPS = 4
SSD_STATE = 128
SSD_CONV = 4
SSD_WIDTH = 2048
SSD_CONV_DIM = 3072
S5_WIDTH = 1024
S5_GROUP_SIZE = 16
S5_GROUPS = 64
S5_STATE = 64
HG_WIDTH = 1024
HG_HEADS = 8
HG_KEY_DIM = 128
HG_VAL_DIM = 128
MEM_LEN = 256
XA_HEADS = 4
XA_HEAD_DIM = 512
FFN_DIM = 5632

P_GATES, P_XBC, P_U, P_HQ, P_HF, P_HI, P_HGATE, P_Z = 0, 6144, 9216, 10240, 11264, 12288, 13312, 14336
P_WIDTH = 16384

V7X_VMEM_LIMIT = 56 * 1024 * 1024


def _cparams(sem, vmem=V7X_VMEM_LIMIT):
    return pltpu.CompilerParams(dimension_semantics=sem, vmem_limit_bytes=vmem)


def _rms_scale(x):
    return lax.rsqrt(jnp.mean(x * x, axis=-1, keepdims=True) + NORM_EPS)


def _silu(x):
    return x * jax.nn.sigmoid(x)


def _dot(a, b):
    return jnp.dot(a, b, preferred_element_type=F32)


def _dot_nt(a, b):
    return lax.dot_general(a, b, (((1,), (1,)), ((), ())), preferred_element_type=F32)


def _dot_tn(a, b):
    return lax.dot_general(a, b, (((0,), (0,)), ((), ())), preferred_element_type=F32)


def _split2(x):
    hi = x.astype(BF16)
    lo = (x - hi.astype(F32)).astype(BF16)
    return hi, lo


def _split3(x):
    hi = x.astype(BF16)
    r = x - hi.astype(F32)
    mid = r.astype(BF16)
    lo = (r - mid.astype(F32)).astype(BF16)
    return hi, mid, lo


def _sel_dot(sel, x, parts=3):
    ps = _split3(x) if parts == 3 else _split2(x)
    out = _dot(sel, ps[0])
    for p in ps[1:]:
        out = out + _dot(sel, p)
    return out


def _dot_sel(x, sel, parts=2):
    ps = _split3(x) if parts == 3 else _split2(x)
    out = _dot(ps[0], sel)
    for p in ps[1:]:
        out = out + _dot(p, sel)
    return out


def _norm_proj_body(x_ref, g_ref, w_ref, o_ref, h_ref):
    @pl.when(pl.program_id(1) == 0)
    def _():
        x = x_ref[...]
        h_ref[...] = (x * _rms_scale(x) * g_ref[...]).astype(BF16)

    o_ref[...] = _dot(h_ref[...], w_ref[...]).astype(o_ref.dtype)


def norm_proj(x, g, w, layer, out_dtype, bm=1024, bn=1024):
    m, k = x.shape
    n = w.shape[2]
    return pl.pallas_call(
        _norm_proj_body,
        grid=(m // bm, n // bn),
        in_specs=[pl.BlockSpec((bm, k), lambda i, j: (i, 0)),
                  pl.BlockSpec((1, k), lambda i, j: (0, 0)),
                  pl.BlockSpec((None, k, bn), lambda i, j: (layer, 0, j))],
        out_specs=pl.BlockSpec((bm, bn), lambda i, j: (i, j)),
        out_shape=jax.ShapeDtypeStruct((m, n), out_dtype),
        scratch_shapes=[pltpu.VMEM((bm, k), BF16)],
        compiler_params=_cparams(("parallel", "arbitrary")),
        name="norm_proj",
    )(x, g, w)


IN_BLOCK = 1024
IN_SEGMENTS = ((2048, P_Z), (3072, P_XBC), (1024, P_U), (1024, P_HQ), (1024, P_HF), (1024, P_HI), (1024, P_HGATE),
               (6144, P_GATES))
IN_DT_START = 2048 + 3072
IN_LO_BLOCKS = IN_DT_START // IN_BLOCK


def _in_proj_dest(j):
    dest = j
    start = 0
    for width, off in IN_SEGMENTS:
        nb = width // IN_BLOCK
        dest = jnp.where((j >= start) & (j < start + nb), off // IN_BLOCK + (j - start), dest)
        start += nb
    return dest


def _in_proj_body(x_ref, g_ref, wlo_ref, whi_ref, wdt_ref, o_ref, odt_ref, h_ref):
    j = pl.program_id(1)

    @pl.when(j == 0)
    def _():
        x = x_ref[...]
        h = (x * _rms_scale(x) * g_ref[...]).astype(BF16)
        h_ref[...] = h
        odt_ref[...] = _dot(h, wdt_ref[...])

    @pl.when(j < IN_LO_BLOCKS)
    def _():
        o_ref[...] = _dot(h_ref[...], wlo_ref[...])

    @pl.when(j >= IN_LO_BLOCKS)
    def _():
        o_ref[...] = _dot(h_ref[...], whi_ref[...])


def in_proj(x, g, w, w_hi, wdt, layer, bm=1024):
    m, k = x.shape
    bn = IN_BLOCK
    ndt = wdt.shape[2]
    return pl.pallas_call(
        _in_proj_body,
        grid=(m // bm, P_WIDTH // bn),
        in_specs=[pl.BlockSpec((bm, k), lambda i, j: (i, 0)),
                  pl.BlockSpec((1, k), lambda i, j: (0, 0)),
                  pl.BlockSpec((None, k, bn), lambda i, j: (layer, 0, jnp.minimum(j, IN_LO_BLOCKS - 1))),
                  pl.BlockSpec((None, k, bn), lambda i, j: (layer, 0, jnp.maximum(j - IN_LO_BLOCKS, 0))),
                  pl.BlockSpec((None, k, ndt), lambda i, j: (layer, 0, 0))],
        out_specs=[pl.BlockSpec((bm, bn), lambda i, j: (i, _in_proj_dest(j))),
                   pl.BlockSpec((bm, ndt), lambda i, j: (i, 0))],
        out_shape=[jax.ShapeDtypeStruct((m, P_WIDTH), F32), jax.ShapeDtypeStruct((m, ndt), F32)],
        scratch_shapes=[pltpu.VMEM((bm, k), BF16)],
        compiler_params=_cparams(("parallel", "arbitrary")),
        name="in_proj",
    )(x, g, w, w_hi, wdt)


def _res_mm_body(x_ref, a_ref, w_ref, o_ref):
    o_ref[...] = x_ref[...] + _dot(a_ref[...], w_ref[...])


def res_mm(x, a, w, layer, bm=1024, bn=1024):
    m, n = x.shape
    k = a.shape[1]
    return pl.pallas_call(
        _res_mm_body,
        grid=(m // bm, n // bn),
        in_specs=[pl.BlockSpec((bm, bn), lambda i, j: (i, j)),
                  pl.BlockSpec((bm, k), lambda i, j: (i, 0)),
                  pl.BlockSpec((None, k, bn), lambda i, j: (layer, 0, j))],
        out_specs=pl.BlockSpec((bm, bn), lambda i, j: (i, j)),
        out_shape=jax.ShapeDtypeStruct((m, n), F32),
        compiler_params=_cparams(("parallel", "parallel")),
        name="res_mm",
    )(x, a, w)


def _ffn_body(x_ref, g_ref, w1_ref, w3_ref, w2_ref, gf_ref, o_ref, h_ref, acc_ref, *, final_norm):
    f = pl.program_id(1)

    @pl.when(f == 0)
    def _():
        x = x_ref[...]
        h_ref[...] = (x * _rms_scale(x) * g_ref[...]).astype(BF16)
        acc_ref[...] = jnp.zeros_like(acc_ref)

    h = h_ref[...]
    a = _silu(_dot(h, w1_ref[...])) * _dot(h, w3_ref[...])
    acc_ref[...] += _dot(a.astype(BF16), w2_ref[...])

    @pl.when(f == pl.num_programs(1) - 1)
    def _():
        y = x_ref[...] + 0.5 * acc_ref[...]
        if final_norm:
            y = y * _rms_scale(y) * gf_ref[...]
        o_ref[...] = y


def ffn(x, g, w1, w3, w2, gf, layer, final_norm, bm=512, bf=512):
    m, d = x.shape
    fdim = w1.shape[2]
    return pl.pallas_call(
        functools.partial(_ffn_body, final_norm=final_norm),
        grid=(m // bm, fdim // bf),
        in_specs=[pl.BlockSpec((bm, d), lambda i, f: (i, 0)),
                  pl.BlockSpec((1, d), lambda i, f: (0, 0)),
                  pl.BlockSpec((None, d, bf), lambda i, f: (layer, 0, f)),
                  pl.BlockSpec((None, d, bf), lambda i, f: (layer, 0, f)),
                  pl.BlockSpec((None, bf, d), lambda i, f: (layer, f, 0)),
                  pl.BlockSpec((1, d), lambda i, f: (0, 0))],
        out_specs=pl.BlockSpec((bm, d), lambda i, f: (i, 0)),
        out_shape=jax.ShapeDtypeStruct((m, d), F32),
        scratch_shapes=[pltpu.VMEM((bm, d), BF16), pltpu.VMEM((bm, d), F32)],
        compiler_params=_cparams(("parallel", "arbitrary")),
        name="ffn",
    )(x, g, w1, w3, w2, gf)


def _mix_body(x_ref, ys_ref, gy_ref, o_ref, ga_ref, gb_ref, gc_ref, wssd_ref, wa_ref, wb_ref, whg_ref,
              wmix_ref, out_ref, acc_ref):
    j = pl.program_id(1)

    @pl.when(j == 0)
    def _():
        acc_ref[...] = jnp.zeros_like(acc_ref)

    gy = gy_ref[...].astype(BF16)
    y_a = _dot(ys_ref[...], wssd_ref[...])
    y_b = _dot(gy, wa_ref[...]) * jax.nn.sigmoid(_dot(gy, wb_ref[...]))
    y_c = _dot(o_ref[...], whg_ref[...])
    mix = (jax.nn.sigmoid(ga_ref[...]) * y_a + jax.nn.sigmoid(gb_ref[...]) * y_b
           + jax.nn.sigmoid(gc_ref[...]) * y_c)
    acc_ref[...] += _dot(mix.astype(BF16), wmix_ref[...])

    @pl.when(j == pl.num_programs(1) - 1)
    def _():
        out_ref[...] = x_ref[...] + acc_ref[...]


def branch_mix(x, ys, gy, o, p, wssd, wa, wb, whg, wmix, layer, bm=512, bn=512):
    m, d = x.shape
    nj = d // bn
    return pl.pallas_call(
        _mix_body,
        grid=(m // bm, nj),
        in_specs=[pl.BlockSpec((bm, d), lambda i, j: (i, 0)),
                  pl.BlockSpec((bm, ys.shape[1]), lambda i, j: (i, 0)),
                  pl.BlockSpec((bm, gy.shape[1]), lambda i, j: (i, 0)),
                  pl.BlockSpec((bm, o.shape[1]), lambda i, j: (i, 0)),
                  pl.BlockSpec((bm, bn), lambda i, j: (i, j)),
                  pl.BlockSpec((bm, bn), lambda i, j: (i, nj + j)),
                  pl.BlockSpec((bm, bn), lambda i, j: (i, 2 * nj + j)),
                  pl.BlockSpec((None, wssd.shape[1], bn), lambda i, j: (layer, 0, j)),
                  pl.BlockSpec((None, wa.shape[1], bn), lambda i, j: (layer, 0, j)),
                  pl.BlockSpec((None, wb.shape[1], bn), lambda i, j: (layer, 0, j)),
                  pl.BlockSpec((None, whg.shape[1], bn), lambda i, j: (layer, 0, j)),
                  pl.BlockSpec((None, bn, d), lambda i, j: (layer, j, 0))],
        out_specs=pl.BlockSpec((bm, d), lambda i, j: (i, 0)),
        out_shape=jax.ShapeDtypeStruct((m, d), F32),
        scratch_shapes=[pltpu.VMEM((bm, d), F32)],
        compiler_params=_cparams(("parallel", "arbitrary")),
        name="branch_mix",
    )(x, ys, gy, o, p, p, p, wssd, wa, wb, whg, wmix)


def _attend(q, k_head, v_head):
    outs = []
    for h in range(XA_HEADS):
        sl = slice(h * XA_HEAD_DIM, (h + 1) * XA_HEAD_DIM)
        s = _dot_nt(q[:, sl], k_head(h).astype(BF16)) * (XA_HEAD_DIM ** -0.5)
        s = s - jnp.max(s, axis=-1, keepdims=True)
        e = jnp.exp(s)
        p = e / jnp.sum(e, axis=-1, keepdims=True)
        outs.append(_dot(p.astype(BF16), v_head(h).astype(BF16)))
    return jnp.concatenate(outs, axis=-1)


def _xattn_prompt_body(q_ref, kv_ref, o_ref):
    k_head = lambda h: kv_ref[:, h * XA_HEAD_DIM:(h + 1) * XA_HEAD_DIM]
    v_head = lambda h: kv_ref[:, D_MODEL + h * XA_HEAD_DIM:D_MODEL + (h + 1) * XA_HEAD_DIM]
    o_ref[...] = _attend(q_ref[...], k_head, v_head).astype(BF16)


def xattn_prompt(q, kv, n_seq, seq_len, bl=512):
    nl = seq_len // bl
    return pl.pallas_call(
        _xattn_prompt_body,
        grid=(n_seq, nl),
        in_specs=[pl.BlockSpec((bl, D_MODEL), lambda b, i: (b * nl + i, 0)),
                  pl.BlockSpec((MEM_LEN, 2 * D_MODEL), lambda b, i: (b, 0))],
        out_specs=pl.BlockSpec((bl, D_MODEL), lambda b, i: (b * nl + i, 0)),
        out_shape=jax.ShapeDtypeStruct(q.shape, BF16),
        compiler_params=_cparams(("parallel", "arbitrary")),
        name="xattn_prompt",
    )(q, kv)


def _xattn_sample_body(q_ref, k_ref, v_ref, buf_ref, o_ref, *, seq_len):
    del buf_ref
    q = q_ref[...]
    r = 2 * seq_len
    nc = XA_HEAD_DIM // LANES
    piece = lambda ref, b, h, c: ref[b, pl.ds(c * XA_HEADS + h, MEM_LEN, stride=nc * XA_HEADS), :].astype(BF16)
    scores = []
    for b in range(2):
        for h in range(XA_HEADS):
            s = None
            for c in range(nc):
                lo = h * XA_HEAD_DIM + c * LANES
                d = _dot_nt(q[:, lo:lo + LANES], piece(k_ref, b, h, c))
                s = d if s is None else s + d
            scores.append(s)
    s = jnp.concatenate(scores, axis=0) * (XA_HEAD_DIM ** -0.5)
    s = s - jnp.max(s, axis=-1, keepdims=True)
    e = jnp.exp(s)
    p = (e / jnp.sum(e, axis=-1, keepdims=True)).astype(BF16)
    rows = lax.broadcasted_iota(jnp.int32, (r, LANES), 0)
    for h in range(XA_HEADS):
        for c in range(nc):
            lo = h * XA_HEAD_DIM + c * LANES
            o0 = _dot(p[h * r:(h + 1) * r], piece(v_ref, 0, h, c))
            o1 = _dot(p[(XA_HEADS + h) * r:(XA_HEADS + h + 1) * r], piece(v_ref, 1, h, c))
            o_ref[:, lo:lo + LANES] = jnp.where(rows < seq_len, o0, o1).astype(BF16)


def xattn_sample(q, k, v, buf, layer, row0, n_seq, seq_len):
    r = 2 * seq_len
    blk0 = row0 // r
    mem_spec = pl.BlockSpec((None, 2, MEM_LEN * D_MODEL // LANES, LANES), lambda i: (layer, i, 0, 0))
    return pl.pallas_call(
        functools.partial(_xattn_sample_body, seq_len=seq_len),
        grid=(n_seq // 2,),
        in_specs=[pl.BlockSpec((r, D_MODEL), lambda i: (blk0 + i, 0)), mem_spec, mem_spec,
                  pl.BlockSpec(memory_space=pl.ANY)],
        out_specs=pl.BlockSpec((r, D_MODEL), lambda i: (blk0 + i, 0)),
        out_shape=jax.ShapeDtypeStruct(buf.shape, BF16),
        input_output_aliases={3: 0},
        compiler_params=_cparams(("parallel",)),
        name="xattn_sample",
    )(q, k, v, buf)


LANES = 128
HALO = 8


def _softplus(x):
    return jnp.maximum(x, 0.0) + jnp.log1p(jnp.exp(-jnp.abs(x)))


def _pad_rows(x, rows):
    if x.shape[0] == rows:
        return x
    return jnp.concatenate([x, jnp.zeros((rows - x.shape[0], x.shape[1]), x.dtype)], axis=0)


def _tile_consts(r, t):
    i = np.arange(r)[:, None]
    j = np.arange(r)[None, :]
    same = (i // t) == (j // t)
    lt = (same & (j <= i)).astype(np.float32)
    last = (same & (j % t == t - 1)).astype(np.float32)
    return jnp.asarray(lt, BF16), jnp.asarray(last, BF16)


def _head_expand(n_heads, width, rows=LANES):
    e = np.zeros((rows, n_heads * width), np.float32)
    for h in range(n_heads):
        e[h, h * width:(h + 1) * width] = 1.0
    return jnp.asarray(e, BF16)


def _ssd_tile(xc, dt_raw, dtb, a_log, lt, last, e, e128, r, t):
    xs = xc[:, :SSD_WIDTH]
    dt = _softplus(dt_raw + dtb)
    d_a = dt * (-jnp.exp(a_log))
    a = _sel_dot(lt, d_a, 3)
    a_e = _dot_sel(a, e, 3)
    dt_e = _dot_sel(dt, e, 2)
    if t == r:
        alast_e = jnp.broadcast_to(a_e[r - 1:r, :], a_e.shape)
    else:
        alast_e = _sel_dot(last, a_e, 3)
    a_col = _dot_sel(a, e128, 3)
    a_t = _pad_rows(a, LANES).T

    row = lax.broadcasted_iota(jnp.int32, (r, LANES), 0)
    col = lax.broadcasted_iota(jnp.int32, (r, LANES), 1)
    valid = (col <= row) & (col >= (row // t) * t)
    lane_lo = col < SSD_HEAD_DIM

    xdt = _pad_rows((xs * dt_e).astype(BF16), LANES)
    hpg = SSD_HEADS // SSD_GROUPS
    ys = []
    for g in range(SSD_GROUPS):
        bg = xc[:, SSD_WIDTH + g * SSD_STATE:SSD_WIDTH + (g + 1) * SSD_STATE].astype(BF16)
        cg = xc[:, SSD_WIDTH + (SSD_GROUPS + g) * SSD_STATE:
                SSD_WIDTH + (SSD_GROUPS + g + 1) * SSD_STATE].astype(BF16)
        cb = _dot_nt(cg, _pad_rows(bg, LANES))
        for hp in range(hpg // 2):
            h0 = g * hpg + 2 * hp
            res = []
            for h in (h0, h0 + 1):
                rel = a_col[:, h * LANES:(h + 1) * LANES] - a_t[h:h + 1, :]
                dec = jnp.where(valid, jnp.exp(jnp.where(valid, rel, 0.0)), 0.0)
                res.append(_dot((cb * dec).astype(BF16), xdt[:, h0 * SSD_HEAD_DIM:(h0 + 2) * SSD_HEAD_DIM]))
            ys.append(jnp.where(lane_lo, res[0], res[1]))
    y_intra = jnp.concatenate(ys, axis=-1)
    return xs, y_intra, a_e, dt_e, alast_e


def _ssd_finish(y, xs, z, dsk, ng):
    y = y + dsk * xs
    y = y * _silu(z)
    return (y * _rms_scale(y) * ng).astype(BF16)


def _conv_silu(ext_ref, cw_ref, cb_ref, base, r):
    acc = cb_ref[...] + cw_ref[SSD_CONV - 1:SSD_CONV, :] * ext_ref[pl.ds(base, r), :]
    for k in range(1, SSD_CONV):
        acc = acc + cw_ref[SSD_CONV - 1 - k:SSD_CONV - k, :] * ext_ref[pl.ds(base - k, r), :]
    return _silu(acc)


def _ssd_prompt_body(xbc_ref, z_ref, dt_ref, cw_ref, cb_ref, dtb_ref, alog_ref, dsk_ref, ng_ref,
                     lt_ref, last_ref, e_ref, e128_ref, y_ref, sout_ref, ext_ref, st_ref, *, r):
    c = pl.program_id(1)

    @pl.when(c == 0)
    def _():
        ext_ref[0:HALO, :] = jnp.zeros((HALO, SSD_CONV_DIM), F32)
        st_ref[...] = jnp.zeros_like(st_ref)

    ext_ref[HALO:HALO + r, :] = xbc_ref[...]
    xc = _conv_silu(ext_ref, cw_ref, cb_ref, HALO, r)
    ext_ref[0:HALO, :] = xbc_ref[r - HALO:r, :]

    xs, y, a_e, dt_e, alast_e = _ssd_tile(xc, dt_ref[...], dtb_ref[...], alog_ref[...], lt_ref[...],
                                          last_ref[...], e_ref[...], e128_ref[...], r, r)
    ea_e = jnp.exp(a_e)
    xw = (xs * (dt_e * jnp.exp(alast_e - a_e))).astype(BF16)
    sdec = jnp.exp(alast_e[0:1, :])
    gw = SSD_WIDTH // SSD_GROUPS
    inter = []
    for g in range(SSD_GROUPS):
        bg = xc[:, SSD_WIDTH + g * SSD_STATE:SSD_WIDTH + (g + 1) * SSD_STATE].astype(BF16)
        cg = xc[:, SSD_WIDTH + (SSD_GROUPS + g) * SSD_STATE:
                SSD_WIDTH + (SSD_GROUPS + g + 1) * SSD_STATE].astype(BF16)
        st = st_ref[:, g * gw:(g + 1) * gw]
        inter.append(_dot(cg, st.astype(BF16)))
        st_ref[:, g * gw:(g + 1) * gw] = st * sdec[:, g * gw:(g + 1) * gw] + _dot_tn(bg, xw[:, g * gw:(g + 1) * gw])
    y = y + jnp.concatenate(inter, axis=-1) * ea_e
    y_ref[...] = _ssd_finish(y, xs, z_ref[...], dsk_ref[...], ng_ref[...])

    @pl.when(c == pl.num_programs(1) - 1)
    def _():
        sout_ref[0] = st_ref[...].T


def _ssd_consts(r, t):
    lt, last = _tile_consts(r, t)
    return lt, last, _head_expand(SSD_HEADS, SSD_HEAD_DIM), _head_expand(SSD_HEADS, LANES)


def ssd_prompt(p, dt, prm, n_seq, seq_len, r=128):
    nc = seq_len // r
    consts = _ssd_consts(r, r)
    full = lambda a: pl.BlockSpec(a.shape, lambda b, c: (0,) * a.ndim)
    small = [prm["conv_w"], prm["conv_b"], prm["dt_bias"], prm["a_log"], prm["d_skip"], prm["norm"], *consts]
    return pl.pallas_call(
        functools.partial(_ssd_prompt_body, r=r),
        grid=(n_seq, nc),
        in_specs=[pl.BlockSpec((r, SSD_CONV_DIM), lambda b, c: (b * nc + c, P_XBC // SSD_CONV_DIM)),
                  pl.BlockSpec((r, SSD_WIDTH), lambda b, c: (b * nc + c, P_Z // SSD_WIDTH)),
                  pl.BlockSpec((r, LANES), lambda b, c: (b * nc + c, 0)),
                  *[full(a) for a in small]],
        out_specs=[pl.BlockSpec((r, SSD_WIDTH), lambda b, c: (b * nc + c, 0)),
                   pl.BlockSpec((1, SSD_WIDTH, SSD_STATE), lambda b, c: (b, 0, 0))],
        out_shape=[jax.ShapeDtypeStruct((p.shape[0], SSD_WIDTH), BF16),
                   jax.ShapeDtypeStruct((n_seq, SSD_WIDTH, SSD_STATE), F32)],
        scratch_shapes=[pltpu.VMEM((HALO + r, SSD_CONV_DIM), F32), pltpu.VMEM((SSD_STATE, SSD_WIDTH), F32)],
        compiler_params=_cparams(("parallel", "arbitrary")),
        name="ssd_prompt",
    )(p, p, dt, *small)


def _ssd_sample_body(xbc_ref, z_ref, dt_ref, buf_ref, s0_ref, cw_ref, cb_ref, dtb_ref, alog_ref, dsk_ref,
                     ng_ref, lt_ref, last_ref, e_ref, e128_ref, *rest, r, t):
    y_ref, sout_ref, ext_ref = rest[-3:]
    nb = r // t
    pitch = HALO + t
    for b in range(nb):
        ext_ref[b * pitch + HALO - (SSD_CONV - 1):b * pitch + HALO, :] = buf_ref[b]
        ext_ref[b * pitch + HALO:(b + 1) * pitch, :] = xbc_ref[b * t:(b + 1) * t, :]
    xc = jnp.concatenate([_conv_silu(ext_ref, cw_ref, cb_ref, b * pitch + HALO, t) for b in range(nb)], axis=0)

    xs, y, a_e, dt_e, alast_e = _ssd_tile(xc, dt_ref[...], dtb_ref[...], alog_ref[...], lt_ref[...],
                                          last_ref[...], e_ref[...], e128_ref[...], r, t)
    ea_e = jnp.exp(a_e)
    xw = xs * (dt_e * jnp.exp(alast_e - a_e))
    sdec = jnp.exp(alast_e)
    gw = SSD_WIDTH // SSD_GROUPS
    pr = 2 * t
    prow = lax.broadcasted_iota(jnp.int32, (pr, 1), 0)
    ones = jnp.ones((pr, SSD_STATE), BF16)
    inter = []
    for g in range(SSD_GROUPS):
        bg = xc[:, SSD_WIDTH + g * SSD_STATE:SSD_WIDTH + (g + 1) * SSD_STATE].astype(BF16)
        cg = xc[:, SSD_WIDTH + (SSD_GROUPS + g) * SSD_STATE:
                SSD_WIDTH + (SSD_GROUPS + g + 1) * SSD_STATE].astype(BF16)
        cols = slice(g * gw, (g + 1) * gw)
        rows_out = []
        for q in range(nb // 2):
            rs = slice(q * pr, (q + 1) * pr)
            acc = None
            for s in range(2):
                b = 2 * q + s
                mine = (prow >= s * t) & (prow < (s + 1) * t)
                s0 = s0_ref[b, cols, :]
                yi = _dot_nt(cg[rs], s0.astype(BF16))
                acc = jnp.where(mine, yi, 0.0) if acc is None else acc + jnp.where(mine, yi, 0.0)
                upd = _dot_tn(jnp.where(mine, xw[rs, cols], 0.0).astype(BF16), bg[rs])
                lastrow = prow == (s + 1) * t - 1
                dh, dl = _split2(jnp.where(lastrow, sdec[rs, cols], 0.0))
                dcol = _dot_tn(dh, ones) + _dot_tn(dl, ones)
                sout_ref[b, cols, :] = s0 * dcol + upd
            rows_out.append(acc)
        inter.append(jnp.concatenate(rows_out, axis=0))
    y = y + jnp.concatenate(inter, axis=-1) * ea_e
    y_ref[...] = _ssd_finish(y, xs, z_ref[...], dsk_ref[...], ng_ref[...])


def ssd_sample(p, dt, conv_buf, s0, ybuf, sprev, prm, layer, row0, n_seq, seq_len, r=64):
    nb = r // seq_len
    blk0 = row0 // r
    consts = _ssd_consts(r, seq_len)
    full = lambda a: pl.BlockSpec(a.shape, lambda i: (0,) * a.ndim)
    small = [prm["conv_w"], prm["conv_b"], prm["dt_bias"], prm["a_log"], prm["d_skip"], prm["norm"], *consts]
    inplace = [ybuf] if sprev is None else [ybuf, sprev]
    n_in = 5 + len(small)
    return pl.pallas_call(
        functools.partial(_ssd_sample_body, r=r, t=seq_len),
        grid=(n_seq // nb,),
        in_specs=[pl.BlockSpec((r, SSD_CONV_DIM), lambda i: (blk0 + i, P_XBC // SSD_CONV_DIM)),
                  pl.BlockSpec((r, SSD_WIDTH), lambda i: (blk0 + i, P_Z // SSD_WIDTH)),
                  pl.BlockSpec((r, LANES), lambda i: (blk0 + i, 0)),
                  pl.BlockSpec((None, nb, SSD_CONV - 1, SSD_CONV_DIM), lambda i: (layer, i, 0, 0)),
                  pl.BlockSpec((None, nb, SSD_WIDTH, SSD_STATE), lambda i: (layer, i, 0, 0)),
                  *[full(a) for a in small],
                  *[pl.BlockSpec(memory_space=pl.ANY) for _ in inplace]],
        out_specs=[pl.BlockSpec((r, SSD_WIDTH), lambda i: (blk0 + i, 0)),
                   pl.BlockSpec((None, nb, SSD_WIDTH, SSD_STATE), lambda i: (layer, i, 0, 0))],
        out_shape=[jax.ShapeDtypeStruct(ybuf.shape, BF16),
                   jax.ShapeDtypeStruct(s0.shape, F32)],
        input_output_aliases={n_in + k: k for k in range(len(inplace))},
        scratch_shapes=[pltpu.VMEM((nb * (HALO + seq_len), SSD_CONV_DIM), F32)],
        compiler_params=_cparams(("parallel",)),
        name="ssd_sample",
    )(p, p, dt, conv_buf, s0, *small, *inplace)


S5_CH = S5_GROUPS * S5_STATE
S5_BLK = 4
S5_SSEQ = 32


def _s5_in(u, bre_ref, bim_ref):
    ub = u.astype(BF16)
    kin = S5_WIDTH // S5_BLK
    re = [_dot(ub[:, q * kin:(q + 1) * kin], bre_ref[q]) for q in range(S5_BLK)]
    im = [_dot(ub[:, q * kin:(q + 1) * kin], bim_ref[q]) for q in range(S5_BLK)]
    return jnp.concatenate(re, axis=-1), jnp.concatenate(im, axis=-1)


def _s5_out(h_re, h_im, u, cre_ref, cim_ref, dsk):
    kst = S5_CH // S5_BLK
    hr = h_re.astype(BF16)
    hi = h_im.astype(BF16)
    y = [_dot(hr[:, q * kst:(q + 1) * kst], cre_ref[q]) - _dot(hi[:, q * kst:(q + 1) * kst], cim_ref[q])
         for q in range(S5_BLK)]
    y = jnp.concatenate(y, axis=-1) + dsk * u
    return jax.nn.gelu(y).astype(BF16)


def _s5_scan_body(u_ref, h0re_ref, h0im_ref, are_ref, aim_ref, bre_ref, bim_ref, cre_ref, cim_ref, dsk_ref,
                  y_ref, hre_ref, him_ref, sre_ref, sim_ref, *, s, tc):
    c = pl.program_id(1)

    @pl.when(c == 0)
    def _():
        sre_ref[0:s, :] = h0re_ref[...]
        sim_ref[0:s, :] = h0im_ref[...]

    u = u_ref[...]
    bu_re, bu_im = _s5_in(u, bre_ref, bim_ref)
    sre_ref[s:, :] = bu_re
    sim_ref[s:, :] = bu_im
    a_re = jnp.broadcast_to(are_ref[...], (s, S5_CH))
    a_im = jnp.broadcast_to(aim_ref[...], (s, S5_CH))

    def step(t, carry):
        prev = pl.ds(pl.multiple_of(t * s, s), s)
        cur = pl.ds(pl.multiple_of((t + 1) * s, s), s)
        h_re = sre_ref[prev, :]
        h_im = sim_ref[prev, :]
        sre_ref[cur, :] = a_re * h_re - a_im * h_im + sre_ref[cur, :]
        sim_ref[cur, :] = a_re * h_im + a_im * h_re + sim_ref[cur, :]
        return carry

    lax.fori_loop(0, tc, step, 0, unroll=4)
    h_re = sre_ref[tc * s:, :]
    h_im = sim_ref[tc * s:, :]
    y_ref[...] = _s5_out(sre_ref[s:, :], sim_ref[s:, :], u, cre_ref, cim_ref, dsk_ref[...])
    sre_ref[0:s, :] = h_re
    sim_ref[0:s, :] = h_im
    hre_ref[...] = h_re
    him_ref[...] = h_im


def s5_scan(u, h0_re, h0_im, prm, n_blocks, s, n_steps, tc):
    nc = n_steps // tc
    r = tc * s
    full = lambda a: pl.BlockSpec(a.shape, lambda b, c: (0,) * a.ndim)
    small = [prm["abar_re"], prm["abar_im"], prm["b_re"], prm["b_im"], prm["c_re"], prm["c_im"], prm["d_skip"]]
    state = jax.ShapeDtypeStruct((n_blocks * s, S5_CH), F32)
    sspec = pl.BlockSpec((s, S5_CH), lambda b, c: (b, 0))
    return pl.pallas_call(
        functools.partial(_s5_scan_body, s=s, tc=tc),
        grid=(n_blocks, nc),
        in_specs=[pl.BlockSpec((r, S5_WIDTH), lambda b, c: (b * nc + c, 0)), sspec, sspec,
                  *[full(a) for a in small]],
        out_specs=[pl.BlockSpec((r, S5_WIDTH), lambda b, c: (b * nc + c, 0)), sspec, sspec],
        out_shape=[jax.ShapeDtypeStruct((u.shape[0], S5_WIDTH), BF16), state, state],
        scratch_shapes=[pltpu.VMEM((s + r, S5_CH), F32), pltpu.VMEM((s + r, S5_CH), F32)],
        compiler_params=_cparams(("parallel", "arbitrary")),
        name="s5_scan",
    )(u, h0_re, h0_im, *small)


S5_T = 16
S5_GB = LANES // S5_GROUP_SIZE
S5_CL = S5_T * LANES
S5_SL = S5_GB * S5_STATE
S5_NB = 2
S5_PITCH_PAD = 8


def _block_diag_rows(x, rows_w, cols_w):
    xt = jnp.concatenate([x] * S5_GB, axis=0)
    r = lax.broadcasted_iota(jnp.int32, xt.shape, 0) // rows_w
    c = lax.broadcasted_iota(jnp.int32, xt.shape, 1) // cols_w
    return jnp.where(r == c, xt, jnp.zeros_like(xt))


def _s5_chunk_body(u_ref, kj_ref, bcre_ref, bcim_ref, ccre_ref, ccim_ref, are_ref, aim_ref, dsk_ref,
                   y_ref, hre_ref, him_ref, vre_ref, vim_ref, sre_ref, sim_ref,
                   tp_ref, bre_ref, bim_ref, cre_ref, cim_ref, *, n_chunks):
    ks = S5_GROUP_SIZE

    @pl.when(pl.program_id(1) == 0)
    def _():
        zero = jnp.zeros((LANES, LANES), BF16)
        for tau in range(S5_T):
            piece = _block_diag_rows(kj_ref[0, tau], ks, ks)
            for s in range(S5_T - tau):
                tp_ref[s * LANES:(s + 1) * LANES, (s + tau) * LANES:(s + tau + 1) * LANES] = piece
        for s in range(1, S5_T):
            for t in range(s):
                tp_ref[s * LANES:(s + 1) * LANES, t * LANES:(t + 1) * LANES] = zero
        for s in range(S5_T):
            bre_ref[s * LANES:(s + 1) * LANES, :] = _block_diag_rows(bcre_ref[0, s], ks, S5_STATE)
            bim_ref[s * LANES:(s + 1) * LANES, :] = _block_diag_rows(bcim_ref[0, s], ks, S5_STATE)
            cre_ref[:, s * LANES:(s + 1) * LANES] = _block_diag_rows(ccre_ref[0, s], S5_STATE, ks)
            cim_ref[:, s * LANES:(s + 1) * LANES] = _block_diag_rows(ccim_ref[0, s], S5_STATE, ks)

    rows = S5_NB * n_chunks
    pitch = n_chunks + S5_PITCH_PAD
    nk = S5_SL // LANES
    u = jnp.concatenate([u_ref[pl.ds(t, rows, stride=S5_T), :] for t in range(S5_T)], axis=-1)
    ub = u.astype(BF16)
    v_re = _dot(ub, bre_ref[...])
    v_im = _dot(ub, bim_ref[...])
    for k in range(nk):
        vre_ref[k] = v_re[:, k * LANES:(k + 1) * LANES]
        vim_ref[k] = v_im[:, k * LANES:(k + 1) * LANES]
        sre_ref[k] = jnp.zeros((S5_NB * pitch, LANES), F32)
        sim_ref[k] = jnp.zeros((S5_NB * pitch, LANES), F32)
    a_re = [jnp.broadcast_to(are_ref[0, :, k * LANES:(k + 1) * LANES], (S5_NB, LANES)) for k in range(nk)]
    a_im = [jnp.broadcast_to(aim_ref[0, :, k * LANES:(k + 1) * LANES], (S5_NB, LANES)) for k in range(nk)]

    def step(c, carry):
        for k in range(nk):
            h_re = sre_ref[k, pl.ds(c, S5_NB, stride=pitch), :]
            h_im = sim_ref[k, pl.ds(c, S5_NB, stride=pitch), :]
            x_re = vre_ref[k, pl.ds(c, S5_NB, stride=n_chunks), :]
            x_im = vim_ref[k, pl.ds(c, S5_NB, stride=n_chunks), :]
            sre_ref[k, pl.ds(c + 1, S5_NB, stride=pitch), :] = a_re[k] * h_re - a_im[k] * h_im + x_re
            sim_ref[k, pl.ds(c + 1, S5_NB, stride=pitch), :] = a_re[k] * h_im + a_im[k] * h_re + x_im
        return carry

    lax.fori_loop(0, n_chunks, step, 0, unroll=2)
    starts = lambda ref: jnp.concatenate(
        [jnp.concatenate([ref[k, b * pitch:b * pitch + n_chunks, :] for b in range(S5_NB)], axis=0)
         for k in range(nk)], axis=-1)
    y = _dot(ub, tp_ref[...])
    y = y + _dot(starts(sre_ref).astype(BF16), cre_ref[...]) - _dot(starts(sim_ref).astype(BF16), cim_ref[...])
    y = jax.nn.gelu(y + dsk_ref[0] * u)
    for t in range(S5_T):
        y_ref[pl.ds(t, rows, stride=S5_T), :] = y[:, t * LANES:(t + 1) * LANES]
    final = lambda ref: jnp.concatenate(
        [jnp.concatenate([ref[k, b * pitch + n_chunks:b * pitch + n_chunks + 1, :] for b in range(S5_NB)], axis=0)
         for k in range(nk)], axis=-1)
    hre_ref[...] = final(sre_ref)
    him_ref[...] = final(sim_ref)


def s5_chunked(p, prm, n_seq, seq_len):
    n_chunks = seq_len // S5_T
    rows = S5_NB * n_chunks
    nblk = S5_GROUPS // S5_GB
    nhalf = n_seq // S5_NB
    pitch = n_chunks + S5_PITCH_PAD
    nk = S5_SL // LANES
    blk = lambda a: pl.BlockSpec((1,) + a.shape[1:], lambda j, i: (j,) + (0,) * (a.ndim - 1))
    small = [prm["kern_j"], prm["bst_re"], prm["bst_im"], prm["cst_re"], prm["cst_im"], prm["apow_re"],
             prm["apow_im"], prm["d_tiled"]]
    state = jax.ShapeDtypeStruct((nhalf, S5_NB, S5_CH), F32)
    sspec = pl.BlockSpec((None, S5_NB, S5_SL), lambda j, i: (i, 0, j))
    return pl.pallas_call(
        functools.partial(_s5_chunk_body, n_chunks=n_chunks),
        grid=(nblk, nhalf),
        in_specs=[pl.BlockSpec((S5_NB * seq_len, LANES), lambda j, i: (i, P_U // LANES + j)),
                  *[blk(a) for a in small]],
        out_specs=[pl.BlockSpec((S5_NB * seq_len, LANES), lambda j, i: (i, j)), sspec, sspec],
        out_shape=[jax.ShapeDtypeStruct((p.shape[0], S5_WIDTH), F32), state, state],
        scratch_shapes=[pltpu.VMEM((nk, rows, LANES), F32), pltpu.VMEM((nk, rows, LANES), F32),
                        pltpu.VMEM((nk, S5_NB * pitch, LANES), F32), pltpu.VMEM((nk, S5_NB * pitch, LANES), F32),
                        pltpu.VMEM((S5_CL, S5_CL), BF16), pltpu.VMEM((S5_CL, S5_SL), BF16),
                        pltpu.VMEM((S5_CL, S5_SL), BF16), pltpu.VMEM((S5_SL, S5_CL), BF16),
                        pltpu.VMEM((S5_SL, S5_CL), BF16)],
        compiler_params=_cparams(("parallel", "arbitrary")),
        name="s5_chunked",
    )(p, *small)


def _hg_consts(r, t):
    i = np.arange(r)[:, None]
    j = np.arange(r)[None, :]
    sums, upper, pair = [], [], []
    s = 1
    while s < t:
        blk_i, blk_j = i // (2 * s), j // (2 * s)
        up_i = (i % (2 * s)) >= s
        mid_i = blk_i * 2 * s + s
        m_up = up_i & (j >= mid_i) & (j <= i)
        m_lo = (~up_i) & (j > i) & (j < mid_i)
        sums.append((m_up | m_lo).astype(np.float32))
        upper.append(np.broadcast_to(up_i, (r, 1)).astype(np.float32))
        pair.append(((blk_i == blk_j) & up_i & ((j % (2 * s)) < s)).astype(np.float32))
        s *= 2
    pair.append((i == j).astype(np.float32))
    return (jnp.asarray(np.stack(sums), BF16), jnp.asarray(np.stack(upper), F32),
            jnp.asarray(np.stack(pair), F32))


def _hg_gates(hf, lb):
    logf = -_softplus(-hf) + jnp.log1p(lb * jnp.exp(-hf))
    kk = (1.0 - lb) * jax.nn.sigmoid(-hf)
    return logf, kk


def _hg_intra(q, kk, v, logf, sums_ref, upper_ref, pair_ref):
    r = q.shape[0]
    nlev = sums_ref.shape[0]
    lf3 = _split3(logf)
    qb = q.astype(BF16)
    kb = kk.astype(BF16)
    vb = _pad_rows(v.astype(BF16), LANES) if r < LANES else v.astype(BF16)
    scores = [None] * HG_HEADS
    for lev in range(nlev + 1):
        if lev < nlev:
            m = sums_ref[lev]
            d = _dot(m, lf3[0]) + _dot(m, lf3[1]) + _dot(m, lf3[2])
            x = (jnp.where(upper_ref[lev] > 0.5, q, kk) * jnp.exp(d)).astype(BF16)
            xq, xk = x, x
        else:
            xq, xk = qb, kb
        mask = pair_ref[lev]
        for h in range(HG_HEADS):
            sl = slice(h * HG_KEY_DIM, (h + 1) * HG_KEY_DIM)
            sc = _dot_nt(xq[:, sl], xk[:, sl]) * mask
            scores[h] = sc if scores[h] is None else scores[h] + sc
    outs = [_dot(scores[h].astype(BF16), vb[:, h * HG_VAL_DIM:(h + 1) * HG_VAL_DIM]) for h in range(HG_HEADS)]
    return jnp.concatenate(outs, axis=-1)


def _hg_finish(o, hgate, ng):
    outs = []
    for h in range(HG_HEADS):
        oh = o[:, h * HG_VAL_DIM:(h + 1) * HG_VAL_DIM]
        outs.append(oh * _rms_scale(oh) * ng)
    return (jnp.concatenate(outs, axis=-1) * _silu(hgate)).astype(BF16)


def _hg_prompt_body(q_ref, f_ref, i_ref, gate_ref, lb_ref, ng_ref, lt_ref, last_ref, sums_ref, upper_ref,
                    pair_ref, o_ref, sout_ref, *, r):
    c = pl.program_id(1)

    @pl.when(c == 0)
    def _():
        sout_ref[...] = jnp.zeros_like(sout_ref)

    q = q_ref[...]
    v = i_ref[...]
    logf, kk = _hg_gates(f_ref[...], lb_ref[...])
    o = _hg_intra(q, kk, v, logf, sums_ref, upper_ref, pair_ref)
    b = _sel_dot(lt_ref[...], logf, 3)
    blast = b[r - 1:r, :]
    qe = (q * jnp.exp(b)).astype(BF16)
    kw = (kk * jnp.exp(blast - b)).astype(BF16)
    vb = v.astype(BF16)
    ones = jnp.ones((r, HG_VAL_DIM), BF16)
    rows = lax.broadcasted_iota(jnp.int32, (r, 1), 0)
    dh, dl = _split2(jnp.where(rows == r - 1, jnp.exp(b), 0.0))
    inter = []
    for h in range(HG_HEADS):
        sl = slice(h * HG_KEY_DIM, (h + 1) * HG_KEY_DIM)
        s = sout_ref[0, h]
        inter.append(_dot(qe[:, sl], s.astype(BF16)))
        dcol = _dot_tn(dh[:, sl], ones) + _dot_tn(dl[:, sl], ones)
        sout_ref[0, h] = s * dcol + _dot_tn(kw[:, sl], vb[:, sl])
    o = o + jnp.concatenate(inter, axis=-1)
    o_ref[...] = _hg_finish(o, gate_ref[...], ng_ref[...])


def hg_prompt(p, prm, n_seq, seq_len, r=128):
    nc = seq_len // r
    lt, last = _tile_consts(r, r)
    consts = [lt, last, *_hg_consts(r, r)]
    full = lambda a: pl.BlockSpec(a.shape, lambda b, c: (0,) * a.ndim)
    small = [prm["lb"], prm["norm"], *consts]
    col = lambda off: pl.BlockSpec((r, HG_WIDTH), lambda b, c: (b * nc + c, off // HG_WIDTH))
    return pl.pallas_call(
        functools.partial(_hg_prompt_body, r=r),
        grid=(n_seq, nc),
        in_specs=[col(P_HQ), col(P_HF), col(P_HI), col(P_HGATE), *[full(a) for a in small]],
        out_specs=[pl.BlockSpec((r, HG_WIDTH), lambda b, c: (b * nc + c, 0)),
                   pl.BlockSpec((1, HG_HEADS, HG_KEY_DIM, HG_VAL_DIM), lambda b, c: (b, 0, 0, 0))],
        out_shape=[jax.ShapeDtypeStruct((p.shape[0], HG_WIDTH), BF16),
                   jax.ShapeDtypeStruct((n_seq, HG_HEADS, HG_KEY_DIM, HG_VAL_DIM), F32)],
        compiler_params=_cparams(("parallel", "arbitrary")),
        name="hg_prompt",
    )(p, p, p, p, *small)


def _hg_sample_body(q_ref, f_ref, i_ref, gate_ref, s0_ref, lb_ref, ng_ref, lt_ref, last_ref, sums_ref,
                    upper_ref, pair_ref, *rest, r, t):
    o_ref, sout_ref = rest[-2:]
    q = q_ref[...]
    v = i_ref[...]
    logf, kk = _hg_gates(f_ref[...], lb_ref[...])
    o = _hg_intra(q, kk, v, logf, sums_ref, upper_ref, pair_ref)
    b = _sel_dot(lt_ref[...], logf, 3)
    blast = _sel_dot(last_ref[...], b, 3)
    qe = (q * jnp.exp(b)).astype(BF16)
    kw = kk * jnp.exp(blast - b)
    sdec = jnp.exp(blast)
    vb = v.astype(BF16)
    pr = 2 * t
    prow = lax.broadcasted_iota(jnp.int32, (pr, 1), 0)
    ones = jnp.ones((pr, HG_VAL_DIM), BF16)
    rows_out = []
    for p2 in range(r // pr):
        rs = slice(p2 * pr, (p2 + 1) * pr)
        heads = []
        for h in range(HG_HEADS):
            sl = slice(h * HG_KEY_DIM, (h + 1) * HG_KEY_DIM)
            acc = None
            for s in range(2):
                bi = 2 * p2 + s
                mine = (prow >= s * t) & (prow < (s + 1) * t)
                s0 = s0_ref[bi, h]
                oi = jnp.where(mine, _dot(qe[rs, sl], s0.astype(BF16)), 0.0)
                acc = oi if acc is None else acc + oi
                upd = _dot_tn(jnp.where(mine, kw[rs, sl], 0.0).astype(BF16), vb[rs, sl])
                dh, dl = _split2(jnp.where(prow == (s + 1) * t - 1, sdec[rs, sl], 0.0))
                dcol = _dot_tn(dh, ones) + _dot_tn(dl, ones)
                sout_ref[bi, h] = s0 * dcol + upd
            heads.append(acc)
        rows_out.append(jnp.concatenate(heads, axis=-1))
    o = o + jnp.concatenate(rows_out, axis=0)
    o_ref[...] = _hg_finish(o, gate_ref[...], ng_ref[...])


def hg_sample(p, s0, obuf, sprev, prm, layer, row0, n_seq, seq_len, r=128):
    nb = r // seq_len
    blk0 = row0 // r
    lt, last = _tile_consts(r, seq_len)
    consts = [lt, last, *_hg_consts(r, seq_len)]
    full = lambda a: pl.BlockSpec(a.shape, lambda i: (0,) * a.ndim)
    small = [prm["lb"], prm["norm"], *consts]
    col = lambda off: pl.BlockSpec((r, HG_WIDTH), lambda i: (blk0 + i, off // HG_WIDTH))
    sspec = pl.BlockSpec((None, nb, HG_HEADS, HG_KEY_DIM, HG_VAL_DIM), lambda i: (layer, i, 0, 0, 0))
    inplace = [obuf] if sprev is None else [obuf, sprev]
    n_in = 5 + len(small)
    return pl.pallas_call(
        functools.partial(_hg_sample_body, r=r, t=seq_len),
        grid=(n_seq // nb,),
        in_specs=[col(P_HQ), col(P_HF), col(P_HI), col(P_HGATE), sspec, *[full(a) for a in small],
                  *[pl.BlockSpec(memory_space=pl.ANY) for _ in inplace]],
        out_specs=[pl.BlockSpec((r, HG_WIDTH), lambda i: (blk0 + i, 0)), sspec],
        out_shape=[jax.ShapeDtypeStruct(obuf.shape, BF16), jax.ShapeDtypeStruct(s0.shape, F32)],
        input_output_aliases={n_in + k: k for k in range(len(inplace))},
        compiler_params=_cparams(("parallel",)),
        name="hg_sample",
    )(p, p, p, p, s0, *small, *inplace)


def _s5_params(a_re, a_im, log_dt, b_re, b_im, c_re, c_im, d_skip):
    dt = jnp.exp(log_dt)[:, None]
    mag = jnp.exp(a_re * dt)
    abar_re = mag * jnp.cos(a_im * dt)
    abar_im = mag * jnp.sin(a_im * dt)
    den = a_re * a_re + a_im * a_im
    nr = abar_re - 1.0
    coef_re = (nr * a_re + abar_im * a_im) / den
    coef_im = (abar_im * a_re - nr * a_im) / den
    bbar_re = coef_re[..., None] * b_re - coef_im[..., None] * b_im
    bbar_im = coef_re[..., None] * b_im + coef_im[..., None] * b_re
    gpb = S5_GROUPS // S5_BLK
    eye = jnp.eye(gpb, dtype=F32)

    def in_blocks(bbar):
        bb = bbar.reshape(S5_BLK, gpb, S5_STATE, S5_GROUP_SIZE)
        return jnp.einsum("qgnk,gh->qgkhn", bb, eye).reshape(
            S5_BLK, gpb * S5_GROUP_SIZE, gpb * S5_STATE).astype(BF16)

    def out_blocks(c):
        cc = c.reshape(S5_BLK, gpb, S5_GROUP_SIZE, S5_STATE)
        return jnp.einsum("qgkn,gh->qgnhk", cc, eye).reshape(
            S5_BLK, gpb * S5_STATE, gpb * S5_GROUP_SIZE).astype(BF16)

    prm = {
        "abar_re": abar_re.reshape(1, S5_CH), "abar_im": abar_im.reshape(1, S5_CH),
        "b_re": in_blocks(bbar_re), "b_im": in_blocks(bbar_im),
        "c_re": out_blocks(c_re), "c_im": out_blocks(c_im),
        "d_skip": d_skip.reshape(1, S5_WIDTH),
    }

    tau = jnp.arange(S5_T + 1, dtype=F32)[:, None, None]
    pmag = jnp.exp(tau * (a_re * dt))
    pw_re = pmag * jnp.cos(tau * (a_im * dt))
    pw_im = pmag * jnp.sin(tau * (a_im * dt))
    cp_re = c_re[None] * pw_re[:, :, None, :] - c_im[None] * pw_im[:, :, None, :]
    cp_im = c_re[None] * pw_im[:, :, None, :] + c_im[None] * pw_re[:, :, None, :]
    bt_re = bbar_re.transpose(0, 2, 1)[None, :, None]
    bt_im = bbar_im.transpose(0, 2, 1)[None, :, None]
    kern = jnp.sum(cp_re[:, :, :, None, :] * bt_re - cp_im[:, :, :, None, :] * bt_im, axis=-1)
    nq = S5_GROUPS // S5_GB
    ks = S5_GROUP_SIZE
    kern_j = (kern[:S5_T].reshape(S5_T, nq, S5_GB, ks, ks).transpose(1, 0, 4, 2, 3)
              .reshape(nq, S5_T, ks, LANES).astype(BF16))
    rev = pw_re[S5_T - 1 - jnp.arange(S5_T)], pw_im[S5_T - 1 - jnp.arange(S5_T)]
    bst_re = rev[0][..., None] * bbar_re[None] - rev[1][..., None] * bbar_im[None]
    bst_im = rev[0][..., None] * bbar_im[None] + rev[1][..., None] * bbar_re[None]

    def bst_blocks(b):
        return (b.reshape(S5_T, nq, S5_GB, S5_STATE, ks).transpose(1, 0, 4, 2, 3)
                .reshape(nq, S5_T, ks, S5_SL).astype(BF16))

    def cst_blocks(c):
        return (c.reshape(S5_T, nq, S5_GB, ks, S5_STATE).transpose(1, 0, 4, 2, 3)
                .reshape(nq, S5_T, S5_STATE, LANES).astype(BF16))

    prm.update({
        "kern_j": kern_j,
        "bst_re": bst_blocks(bst_re), "bst_im": bst_blocks(bst_im),
        "cst_re": cst_blocks(cp_re[1:]), "cst_im": cst_blocks(cp_im[1:]),
        "apow_re": pw_re[S5_T].reshape(nq, 1, S5_SL), "apow_im": pw_im[S5_T].reshape(nq, 1, S5_SL),
        "d_tiled": jnp.broadcast_to(d_skip.reshape(nq, 1, 1, LANES), (nq, 1, S5_T, LANES)).reshape(nq, 1, S5_CL),
    })
    return prm


def _in_proj_weights(w_in):
    w = w_in.astype(BF16)
    w_hi = w[..., IN_DT_START + SSD_HEADS:]
    wdt = jnp.pad(w[..., IN_DT_START:IN_DT_START + SSD_HEADS], ((0, 0), (0, 0), (0, LANES - SSD_HEADS)))
    return w, w_hi, wdt


def _mem_rows(cache):
    dd, b, m, h, hd = cache.shape
    c = cache.reshape(dd, b, m, h, hd // LANES, LANES).transpose(0, 1, 2, 4, 3, 5)
    return c.reshape(dd, b, m * h * (hd // LANES), LANES)


def _pad_lanes(v):
    return jnp.pad(v.reshape(1, -1), ((0, 0), (0, LANES - v.shape[-1])))


def kernel(x_prompt, x_sample, cache_mem_k, cache_mem_v, state_ssd, state_ssd_conv, state_s5_re, state_s5_im,
           state_hgrn, mem_prompt, norm_ffn1, ffn1_w1, ffn1_w3, ffn1_w2, norm_mix, w_in, ssd_conv_w, ssd_conv_b,
           ssd_dt_bias, ssd_a_log, ssd_d, ssd_norm, ssd_w_out, s5_a_re, s5_a_im, s5_log_dt, s5_b_re, s5_b_im,
           s5_c_re, s5_c_im, s5_d, s5_w_glu_a, s5_w_glu_b, hg_lower_bounds, hg_norm, hg_w_out, w_mix_out,
           norm_xa, norm_mem, xa_wq, xa_wk, xa_wv, xa_wo, norm_ffn2, ffn2_w1, ffn2_w3, ffn2_w2, norm_final):
    bp, lp, d = x_prompt.shape
    bs, ls, _ = x_sample.shape
    mp, ms = bp * lp, bs * ls
    x = jnp.concatenate([x_prompt.reshape(mp, d), x_sample.reshape(ms, d)], axis=0)
    mem = mem_prompt.reshape(bp * MEM_LEN, d)
    row = lambda v: v.reshape(1, -1)
    bf = lambda w: w.astype(BF16)

    lb_p = jax.nn.softmax(hg_lower_bounds, axis=0)
    lb_all = jnp.cumsum(lb_p, axis=0) - lb_p[0]

    ssd_s0 = state_ssd.reshape(DEPTH, bs, SSD_WIDTH, SSD_STATE)
    mem_k = _mem_rows(cache_mem_k)
    mem_v = _mem_rows(cache_mem_v)
    ssd_states = None
    hg_states = None

    f1_w1, f1_w3, f1_w2 = bf(ffn1_w1), bf(ffn1_w3), bf(ffn1_w2)
    f2_w1, f2_w3, f2_w2 = bf(ffn2_w1), bf(ffn2_w3), bf(ffn2_w2)
    w_p, w_p_hi, w_dt = _in_proj_weights(w_in)
    b_ssd, b_glu_a, b_glu_b, b_hg, b_mix = bf(ssd_w_out), bf(s5_w_glu_a), bf(s5_w_glu_b), bf(hg_w_out), bf(w_mix_out)
    b_wq, b_wo = bf(xa_wq), bf(xa_wo)
    b_wkv = bf(jnp.concatenate([xa_wk, xa_wv], axis=-1))

    outs = {k: [] for k in ("pk", "pv", "pss", "pcv", "psr", "psi", "phg", "scv", "ssr", "ssi")}
    for l in range(DEPTH):
        x = ffn(x, row(norm_ffn1[l]), f1_w1, f1_w3, f1_w2, row(norm_final), l, False)

        p, dt = in_proj(x, row(norm_mix[l]), w_p, w_p_hi, w_dt, l)

        ssd_prm = {"conv_w": ssd_conv_w[l], "conv_b": row(ssd_conv_b[l]), "dt_bias": _pad_lanes(ssd_dt_bias[l]),
                   "a_log": _pad_lanes(ssd_a_log[l]), "d_skip": row(jnp.repeat(ssd_d[l], SSD_HEAD_DIM)),
                   "norm": row(ssd_norm[l])}
        ys, ss_p = ssd_prompt(p, dt, ssd_prm, bp, lp)
        ys, ssd_states = ssd_sample(p, dt, state_ssd_conv, ssd_s0, ys, ssd_states, ssd_prm, l, mp, bs, ls)

        s5_prm = _s5_params(s5_a_re[l], s5_a_im[l], s5_log_dt[l], s5_b_re[l], s5_b_im[l], s5_c_re[l],
                            s5_c_im[l], s5_d[l])
        gy, sr_p, si_p = s5_chunked(p, s5_prm, bp, lp)
        u_s = p[mp:, P_U:P_U + S5_WIDTH].reshape(bs // S5_SSEQ, S5_SSEQ, ls, S5_WIDTH).transpose(0, 2, 1, 3)
        u_s = u_s.reshape(ms, S5_WIDTH)
        gy_s, sr_s, si_s = s5_scan(u_s, state_s5_re[l].reshape(bs, S5_CH), state_s5_im[l].reshape(bs, S5_CH),
                                   s5_prm, bs // S5_SSEQ, S5_SSEQ, ls, ls)
        gy_s = gy_s.reshape(bs // S5_SSEQ, ls, S5_SSEQ, S5_WIDTH).transpose(0, 2, 1, 3).reshape(ms, S5_WIDTH)
        gy = lax.dynamic_update_slice(gy, gy_s.astype(F32), (mp, 0))

        hg_prm = {"lb": row(lb_all[l]), "norm": row(hg_norm[l])}
        o, hg_p = hg_prompt(p, hg_prm, bp, lp)
        o, hg_states = hg_sample(p, state_hgrn, o, hg_states, hg_prm, l, mp, bs, ls)

        x = branch_mix(x, ys, gy, o, p, b_ssd, b_glu_a, b_glu_b, b_hg, b_mix, l)

        q = norm_proj(x, row(norm_xa[l]), b_wq, l, BF16)
        kv = norm_proj(mem, row(norm_mem[l]), b_wkv, l, F32)
        at = xattn_prompt(q, kv, bp, lp)
        at = xattn_sample(q, mem_k, mem_v, at, l, mp, bs, ls)
        x = res_mm(x, at, b_wo, l)

        x = ffn(x, row(norm_ffn2[l]), f2_w1, f2_w3, f2_w2, row(norm_final), l, l == DEPTH - 1)

        tail = SSD_CONV - 1
        outs["pk"].append(kv[:, :d].reshape(bp, MEM_LEN, XA_HEADS, XA_HEAD_DIM))
        outs["pv"].append(kv[:, d:].reshape(bp, MEM_LEN, XA_HEADS, XA_HEAD_DIM))
        outs["pss"].append(ss_p.reshape(bp, SSD_HEADS, SSD_HEAD_DIM, SSD_STATE))
        outs["pcv"].append(jnp.stack([p[(b + 1) * lp - tail:(b + 1) * lp, P_XBC:P_XBC + SSD_CONV_DIM]
                                      for b in range(bp)]))
        outs["psr"].append(sr_p.reshape(bp, S5_GROUPS, S5_STATE))
        outs["psi"].append(si_p.reshape(bp, S5_GROUPS, S5_STATE))
        outs["phg"].append(hg_p)
        outs["scv"].append(p[mp:, P_XBC:P_XBC + SSD_CONV_DIM].reshape(bs, ls, SSD_CONV_DIM)[:, ls - tail:])
        outs["ssr"].append(sr_s.reshape(bs, S5_GROUPS, S5_STATE))
        outs["ssi"].append(si_s.reshape(bs, S5_GROUPS, S5_STATE))

    st = lambda k: jnp.stack(outs[k])
    return (x[:mp].reshape(bp, lp, d), x[mp:].reshape(bs, ls, d),
            st("pk"), st("pv"), st("pss"), st("pcv"), st("psr"), st("psi"), st("phg"),
            ssd_states.reshape(DEPTH, bs, SSD_HEADS, SSD_HEAD_DIM, SSD_STATE), st("scv"), st("ssr"), st("ssi"),
            hg_states)
```

```python
import functools
import math

import jax
import jax.numpy as jnp
import numpy as np
from jax import lax
from jax.experimental import pallas as pl
from jax.experimental.pallas import tpu as pltpu

F32 = jnp.float32
BF16 = jnp.bfloat16

D_MODEL = 2048
DEPTH = 2
NORM_EPS = 1e-5
SSD_HEAD_DIM = 64
SSD_HEADS = 32
SSD_GROUPS = 4
SSD_STATE = 128
SSD_CONV = 4
SSD_WIDTH = 2048
SSD_CONV_DIM = 3072
S5_WIDTH = 1024
S5_GROUP_SIZE = 16
S5_GROUPS = 64
S5_STATE = 64
HG_WIDTH = 1024
HG_HEADS = 8
HG_KEY_DIM = 128
HG_VAL_DIM = 128
MEM_LEN = 256
XA_HEADS = 4
XA_HEAD_DIM = 512
FFN_DIM = 5632

P_GATES, P_XBC, P_U, P_HQ, P_HF, P_HI, P_HGATE, P_Z = 0, 6144, 9216, 10240, 11264, 12288, 13312, 14336
P_WIDTH = 16384

V7X_VMEM_LIMIT = 56 * 1024 * 1024


def _cparams(sem, vmem=V7X_VMEM_LIMIT):
    return pltpu.CompilerParams(dimension_semantics=sem, vmem_limit_bytes=vmem)


def _rms_scale(x):
    return lax.rsqrt(jnp.mean(x * x, axis=-1, keepdims=True) + NORM_EPS)


def _silu(x):
    return x * jax.nn.sigmoid(x)


def _dot(a, b):
    return jnp.dot(a, b, preferred_element_type=F32)


def _dot_nt(a, b):
    return lax.dot_general(a, b, (((1,), (1,)), ((), ())), preferred_element_type=F32)


def _dot_tn(a, b):
    return lax.dot_general(a, b, (((0,), (0,)), ((), ())), preferred_element_type=F32)


def _split2(x):
    hi = x.astype(BF16)
    lo = (x - hi.astype(F32)).astype(BF16)
    return hi, lo


def _split3(x):
    hi = x.astype(BF16)
    r = x - hi.astype(F32)
    mid = r.astype(BF16)
    lo = (r - mid.astype(F32)).astype(BF16)
    return hi, mid, lo


def _sel_dot(sel, x, parts=3):
    ps = _split3(x) if parts == 3 else _split2(x)
    out = _dot(sel, ps[0])
    for p in ps[1:]:
        out = out + _dot(sel, p)
    return out


def _dot_sel(x, sel, parts=2):
    ps = _split3(x) if parts == 3 else _split2(x)
    out = _dot(ps[0], sel)
    for p in ps[1:]:
        out = out + _dot(p, sel)
    return out


def _norm_proj_body(x_ref, g_ref, w_ref, o_ref, h_ref):
    @pl.when(pl.program_id(1) == 0)
    def _():
        x = x_ref[...]
        h_ref[...] = (x * _rms_scale(x) * g_ref[...]).astype(BF16)

    o_ref[...] = _dot(h_ref[...], w_ref[...]).astype(o_ref.dtype)


def norm_proj(x, g, w, layer, out_dtype, bm=1024, bn=1024):
    m, k = x.shape
    n = w.shape[2]
    return pl.pallas_call(
        _norm_proj_body,
        grid=(m // bm, n // bn),
        in_specs=[pl.BlockSpec((bm, k), lambda i, j: (i, 0)),
                  pl.BlockSpec((1, k), lambda i, j: (0, 0)),
                  pl.BlockSpec((None, k, bn), lambda i, j: (layer, 0, j))],
        out_specs=pl.BlockSpec((bm, bn), lambda i, j: (i, j)),
        out_shape=jax.ShapeDtypeStruct((m, n), out_dtype),
        scratch_shapes=[pltpu.VMEM((bm, k), BF16)],
        compiler_params=_cparams(("parallel", "arbitrary")),
        name="norm_proj",
    )(x, g, w)


IN_BLOCK = 1024
IN_SEGMENTS = ((2048, P_Z), (3072, P_XBC), (1024, P_U), (1024, P_HQ), (1024, P_HF), (1024, P_HI), (1024, P_HGATE),
               (6144, P_GATES))
IN_DT_START = 2048 + 3072
IN_LO_BLOCKS = IN_DT_START // IN_BLOCK


def _in_proj_dest(j):
    dest = j
    start = 0
    for width, off in IN_SEGMENTS:
        nb = width // IN_BLOCK
        dest = jnp.where((j >= start) & (j < start + nb), off // IN_BLOCK + (j - start), dest)
        start += nb
    return dest


def _in_proj_body(x_ref, g_ref, wlo_ref, whi_ref, wdt_ref, o_ref, odt_ref, h_ref):
    j = pl.program_id(1)

    @pl.when(j == 0)
    def _():
        x = x_ref[...]
        h = (x * _rms_scale(x) * g_ref[...]).astype(BF16)
        h_ref[...] = h
        odt_ref[...] = _dot(h, wdt_ref[...])

    @pl.when(j < IN_LO_BLOCKS)
    def _():
        o_ref[...] = _dot(h_ref[...], wlo_ref[...])

    @pl.when(j >= IN_LO_BLOCKS)
    def _():
        o_ref[...] = _dot(h_ref[...], whi_ref[...])


def in_proj(x, g, w, w_hi, wdt, layer, bm=1024):
    m, k = x.shape
    bn = IN_BLOCK
    ndt = wdt.shape[2]
    return pl.pallas_call(
        _in_proj_body,
        grid=(m // bm, P_WIDTH // bn),
        in_specs=[pl.BlockSpec((bm, k), lambda i, j: (i, 0)),
                  pl.BlockSpec((1, k), lambda i, j: (0, 0)),
                  pl.BlockSpec((None, k, bn), lambda i, j: (layer, 0, jnp.minimum(j, IN_LO_BLOCKS - 1))),
                  pl.BlockSpec((None, k, bn), lambda i, j: (layer, 0, jnp.maximum(j - IN_LO_BLOCKS, 0))),
                  pl.BlockSpec((None, k, ndt), lambda i, j: (layer, 0, 0))],
        out_specs=[pl.BlockSpec((bm, bn), lambda i, j: (i, _in_proj_dest(j))),
                   pl.BlockSpec((bm, ndt), lambda i, j: (i, 0))],
        out_shape=[jax.ShapeDtypeStruct((m, P_WIDTH), F32), jax.ShapeDtypeStruct((m, ndt), F32)],
        scratch_shapes=[pltpu.VMEM((bm, k), BF16)],
        compiler_params=_cparams(("parallel", "arbitrary")),
        name="in_proj",
    )(x, g, w, w_hi, wdt)


def _res_mm_body(x_ref, a_ref, w_ref, o_ref):
    o_ref[...] = x_ref[...] + _dot(a_ref[...], w_ref[...])


def res_mm(x, a, w, layer, bm=1024, bn=1024):
    m, n = x.shape
    k = a.shape[1]
    return pl.pallas_call(
        _res_mm_body,
        grid=(m // bm, n // bn),
        in_specs=[pl.BlockSpec((bm, bn), lambda i, j: (i, j)),
                  pl.BlockSpec((bm, k), lambda i, j: (i, 0)),
                  pl.BlockSpec((None, k, bn), lambda i, j: (layer, 0, j))],
        out_specs=pl.BlockSpec((bm, bn), lambda i, j: (i, j)),
        out_shape=jax.ShapeDtypeStruct((m, n), F32),
        compiler_params=_cparams(("parallel", "parallel")),
        name="res_mm",
    )(x, a, w)


def _ffn_body(x_ref, g_ref, w1_ref, w3_ref, w2_ref, gf_ref, o_ref, h_ref, acc_ref, *, final_norm):
    f = pl.program_id(1)

    @pl.when(f == 0)
    def _():
        x = x_ref[...]
        h_ref[...] = (x * _rms_scale(x) * g_ref[...]).astype(BF16)
        acc_ref[...] = jnp.zeros_like(acc_ref)

    h = h_ref[...]
    a = _silu(_dot(h, w1_ref[...])) * _dot(h, w3_ref[...])
    acc_ref[...] += _dot(a.astype(BF16), w2_ref[...])

    @pl.when(f == pl.num_programs(1) - 1)
    def _():
        y = x_ref[...] + 0.5 * acc_ref[...]
        if final_norm:
            y = y * _rms_scale(y) * gf_ref[...]
        o_ref[...] = y


def ffn(x, g, w1, w3, w2, gf, layer, final_norm, bm=512, bf=512):
    m, d = x.shape
    fdim = w1.shape[2]
    return pl.pallas_call(
        functools.partial(_ffn_body, final_norm=final_norm),
        grid=(m // bm, fdim // bf),
        in_specs=[pl.BlockSpec((bm, d), lambda i, f: (i, 0)),
                  pl.BlockSpec((1, d), lambda i, f: (0, 0)),
                  pl.BlockSpec((None, d, bf), lambda i, f: (layer, 0, f)),
                  pl.BlockSpec((None, d, bf), lambda i, f: (layer, 0, f)),
                  pl.BlockSpec((None, bf, d), lambda i, f: (layer, f, 0)),
                  pl.BlockSpec((1, d), lambda i, f: (0, 0))],
        out_specs=pl.BlockSpec((bm, d), lambda i, f: (i, 0)),
        out_shape=jax.ShapeDtypeStruct((m, d), F32),
        scratch_shapes=[pltpu.VMEM((bm, d), BF16), pltpu.VMEM((bm, d), F32)],
        compiler_params=_cparams(("parallel", "arbitrary")),
        name="ffn",
    )(x, g, w1, w3, w2, gf)


def _mix_body(x_ref, ys_ref, gy_ref, o_ref, ga_ref, gb_ref, gc_ref, wssd_ref, wa_ref, wb_ref, whg_ref,
              wmix_ref, out_ref, acc_ref):
    j = pl.program_id(1)

    @pl.when(j == 0)
    def _():
        acc_ref[...] = jnp.zeros_like(acc_ref)

    gy = gy_ref[...].astype(BF16)
    y_a = _dot(ys_ref[...], wssd_ref[...])
    y_b = _dot(gy, wa_ref[...]) * jax.nn.sigmoid(_dot(gy, wb_ref[...]))
    y_c = _dot(o_ref[...], whg_ref[...])
    mix = (jax.nn.sigmoid(ga_ref[...]) * y_a + jax.nn.sigmoid(gb_ref[...]) * y_b
           + jax.nn.sigmoid(gc_ref[...]) * y_c)
    acc_ref[...] += _dot(mix.astype(BF16), wmix_ref[...])

    @pl.when(j == pl.num_programs(1) - 1)
    def _():
        out_ref[...] = x_ref[...] + acc_ref[...]


def branch_mix(x, ys, gy, o, p, wssd, wa, wb, whg, wmix, layer, bm=512, bn=512):
    m, d = x.shape
    nj = d // bn
    return pl.pallas_call(
        _mix_body,
        grid=(m // bm, nj),
        in_specs=[pl.BlockSpec((bm, d), lambda i, j: (i, 0)),
                  pl.BlockSpec((bm, ys.shape[1]), lambda i, j: (i, 0)),
                  pl.BlockSpec((bm, gy.shape[1]), lambda i, j: (i, 0)),
                  pl.BlockSpec((bm, o.shape[1]), lambda i, j: (i, 0)),
                  pl.BlockSpec((bm, bn), lambda i, j: (i, j)),
                  pl.BlockSpec((bm, bn), lambda i, j: (i, nj + j)),
                  pl.BlockSpec((bm, bn), lambda i, j: (i, 2 * nj + j)),
                  pl.BlockSpec((None, wssd.shape[1], bn), lambda i, j: (layer, 0, j)),
                  pl.BlockSpec((None, wa.shape[1], bn), lambda i, j: (layer, 0, j)),
                  pl.BlockSpec((None, wb.shape[1], bn), lambda i, j: (layer, 0, j)),
                  pl.BlockSpec((None, whg.shape[1], bn), lambda i, j: (layer, 0, j)),
                  pl.BlockSpec((None, bn, d), lambda i, j: (layer, j, 0))],
        out_specs=pl.BlockSpec((bm, d), lambda i, j: (i, 0)),
        out_shape=jax.ShapeDtypeStruct((m, d), F32),
        scratch_shapes=[pltpu.VMEM((bm, d), F32)],
        compiler_params=_cparams(("parallel", "arbitrary")),
        name="branch_mix",
    )(x, ys, gy, o, p, p, p, wssd, wa, wb, whg, wmix)


def _attend(q, k_head, v_head):
    outs = []
    for h in range(XA_HEADS):
        sl = slice(h * XA_HEAD_DIM, (h + 1) * XA_HEAD_DIM)
        s = _dot_nt(q[:, sl], k_head(h).astype(BF16)) * (XA_HEAD_DIM ** -0.5)
        s = s - jnp.max(s, axis=-1, keepdims=True)
        e = jnp.exp(s)
        p = e / jnp.sum(e, axis=-1, keepdims=True)
        outs.append(_dot(p.astype(BF16), v_head(h).astype(BF16)))
    return jnp.concatenate(outs, axis=-1)


def _xattn_prompt_body(q_ref, kv_ref, o_ref):
    k_head = lambda h: kv_ref[:, h * XA_HEAD_DIM:(h + 1) * XA_HEAD_DIM]
    v_head = lambda h: kv_ref[:, D_MODEL + h * XA_HEAD_DIM:D_MODEL + (h + 1) * XA_HEAD_DIM]
    o_ref[...] = _attend(q_ref[...], k_head, v_head).astype(BF16)


def xattn_prompt(q, kv, n_seq, seq_len, bl=512):
    nl = seq_len // bl
    return pl.pallas_call(
        _xattn_prompt_body,
        grid=(n_seq, nl),
        in_specs=[pl.BlockSpec((bl, D_MODEL), lambda b, i: (b * nl + i, 0)),
                  pl.BlockSpec((MEM_LEN, 2 * D_MODEL), lambda b, i: (b, 0))],
        out_specs=pl.BlockSpec((bl, D_MODEL), lambda b, i: (b * nl + i, 0)),
        out_shape=jax.ShapeDtypeStruct(q.shape, BF16),
        compiler_params=_cparams(("parallel", "arbitrary")),
        name="xattn_prompt",
    )(q, kv)


def _xattn_sample_body(q_ref, k_ref, v_ref, buf_ref, o_ref, *, seq_len):
    del buf_ref
    q = q_ref[...]
    r = 2 * seq_len
    nc = XA_HEAD_DIM // LANES
    piece = lambda ref, b, h, c: ref[b, pl.ds(c * XA_HEADS + h, MEM_LEN, stride=nc * XA_HEADS), :].astype(BF16)
    scores = []
    for b in range(2):
        for h in range(XA_HEADS):
            s = None
            for c in range(nc):
                lo = h * XA_HEAD_DIM + c * LANES
                d = _dot_nt(q[:, lo:lo + LANES], piece(k_ref, b, h, c))
                s = d if s is None else s + d
            scores.append(s)
    s = jnp.concatenate(scores, axis=0) * (XA_HEAD_DIM ** -0.5)
    s = s - jnp.max(s, axis=-1, keepdims=True)
    e = jnp.exp(s)
    p = (e / jnp.sum(e, axis=-1, keepdims=True)).astype(BF16)
    rows = lax.broadcasted_iota(jnp.int32, (r, LANES), 0)
    for h in range(XA_HEADS):
        for c in range(nc):
            lo = h * XA_HEAD_DIM + c * LANES
            o0 = _dot(p[h * r:(h + 1) * r], piece(v_ref, 0, h, c))
            o1 = _dot(p[(XA_HEADS + h) * r:(XA_HEADS + h + 1) * r], piece(v_ref, 1, h, c))
            o_ref[:, lo:lo + LANES] = jnp.where(rows < seq_len, o0, o1).astype(BF16)


def xattn_sample(q, k, v, buf, layer, row0, n_seq, seq_len):
    r = 2 * seq_len
    blk0 = row0 // r
    mem_spec = pl.BlockSpec((None, 2, MEM_LEN * D_MODEL // LANES, LANES), lambda i: (layer, i, 0, 0))
    return pl.pallas_call(
        functools.partial(_xattn_sample_body, seq_len=seq_len),
        grid=(n_seq // 2,),
        in_specs=[pl.BlockSpec((r, D_MODEL), lambda i: (blk0 + i, 0)), mem_spec, mem_spec,
                  pl.BlockSpec(memory_space=pl.ANY)],
        out_specs=pl.BlockSpec((r, D_MODEL), lambda i: (blk0 + i, 0)),
        out_shape=jax.ShapeDtypeStruct(buf.shape, BF16),
        input_output_aliases={3: 0},
        compiler_params=_cparams(("parallel",)),
        name="xattn_sample",
    )(q, k, v, buf)


LANES = 128
HALO = 8


def _softplus(x):
    return jnp.maximum(x, 0.0) + jnp.log1p(jnp.exp(-jnp.abs(x)))


def _pad_rows(x, rows):
    if x.shape[0] == rows:
        return x
    return jnp.concatenate([x, jnp.zeros((rows - x.shape[0], x.shape[1]), x.dtype)], axis=0)


def _tile_consts(r, t):
    i = np.arange(r)[:, None]
    j = np.arange(r)[None, :]
    same = (i // t) == (j // t)
    lt = (same & (j <= i)).astype(np.float32)
    last = (same & (j % t == t - 1)).astype(np.float32)
    return jnp.asarray(lt, BF16), jnp.asarray(last, BF16)


def _head_expand(n_heads, width, rows=LANES):
    e = np.zeros((rows, n_heads * width), np.float32)
    for h in range(n_heads):
        e[h, h * width:(h + 1) * width] = 1.0
    return jnp.asarray(e, BF16)


def _ssd_tile(xc, dt_raw, dtb, a_log, lt, last, e, e128, r, t):
    xs = xc[:, :SSD_WIDTH]
    dt = _softplus(dt_raw + dtb)
    d_a = dt * (-jnp.exp(a_log))
    a = _sel_dot(lt, d_a, 3)
    a_e = _dot_sel(a, e, 3)
    dt_e = _dot_sel(dt, e, 2)
    if t == r:
        alast_e = jnp.broadcast_to(a_e[r - 1:r, :], a_e.shape)
    else:
        alast_e = _sel_dot(last, a_e, 3)
    a_col = _dot_sel(a, e128, 3)
    a_t = _pad_rows(a, LANES).T

    row = lax.broadcasted_iota(jnp.int32, (r, LANES), 0)
    col = lax.broadcasted_iota(jnp.int32, (r, LANES), 1)
    valid = (col <= row) & (col >= (row // t) * t)
    lane_lo = col < SSD_HEAD_DIM

    xdt = _pad_rows((xs * dt_e).astype(BF16), LANES)
    hpg = SSD_HEADS // SSD_GROUPS
    ys = []
    for g in range(SSD_GROUPS):
        bg = xc[:, SSD_WIDTH + g * SSD_STATE:SSD_WIDTH + (g + 1) * SSD_STATE].astype(BF16)
        cg = xc[:, SSD_WIDTH + (SSD_GROUPS + g) * SSD_STATE:
                SSD_WIDTH + (SSD_GROUPS + g + 1) * SSD_STATE].astype(BF16)
        cb = _dot_nt(cg, _pad_rows(bg, LANES))
        for hp in range(hpg // 2):
            h0 = g * hpg + 2 * hp
            res = []
            for h in (h0, h0 + 1):
                rel = a_col[:, h * LANES:(h + 1) * LANES] - a_t[h:h + 1, :]
                dec = jnp.where(valid, jnp.exp(jnp.where(valid, rel, 0.0)), 0.0)
                res.append(_dot((cb * dec).astype(BF16), xdt[:, h0 * SSD_HEAD_DIM:(h0 + 2) * SSD_HEAD_DIM]))
            ys.append(jnp.where(lane_lo, res[0], res[1]))
    y_intra = jnp.concatenate(ys, axis=-1)
    return xs, y_intra, a_e, dt_e, alast_e


def _ssd_finish(y, xs, z, dsk, ng):
    y = y + dsk * xs
    y = y * _silu(z)
    return (y * _rms_scale(y) * ng).astype(BF16)


def _conv_silu(ext_ref, cw_ref, cb_ref, base, r):
    acc = cb_ref[...] + cw_ref[SSD_CONV - 1:SSD_CONV, :] * ext_ref[pl.ds(base, r), :]
    for k in range(1, SSD_CONV):
        acc = acc + cw_ref[SSD_CONV - 1 - k:SSD_CONV - k, :] * ext_ref[pl.ds(base - k, r), :]
    return _silu(acc)


def _ssd_prompt_body(xbc_ref, z_ref, dt_ref, cw_ref, cb_ref, dtb_ref, alog_ref, dsk_ref, ng_ref,
                     lt_ref, last_ref, e_ref, e128_ref, y_ref, sout_ref, ext_ref, st_ref, *, r):
    c = pl.program_id(1)

    @pl.when(c == 0)
    def _():
        ext_ref[0:HALO, :] = jnp.zeros((HALO, SSD_CONV_DIM), F32)
        st_ref[...] = jnp.zeros_like(st_ref)

    ext_ref[HALO:HALO + r, :] = xbc_ref[...]
    xc = _conv_silu(ext_ref, cw_ref, cb_ref, HALO, r)
    ext_ref[0:HALO, :] = xbc_ref[r - HALO:r, :]

    xs, y, a_e, dt_e, alast_e = _ssd_tile(xc, dt_ref[...], dtb_ref[...], alog_ref[...], lt_ref[...],
                                          last_ref[...], e_ref[...], e128_ref[...], r, r)
    ea_e = jnp.exp(a_e)
    xw = (xs * (dt_e * jnp.exp(alast_e - a_e))).astype(BF16)
    sdec = jnp.exp(alast_e[0:1, :])
    gw = SSD_WIDTH // SSD_GROUPS
    inter = []
    for g in range(SSD_GROUPS):
        bg = xc[:, SSD_WIDTH + g * SSD_STATE:SSD_WIDTH + (g + 1) * SSD_STATE].astype(BF16)
        cg = xc[:, SSD_WIDTH + (SSD_GROUPS + g) * SSD_STATE:
                SSD_WIDTH + (SSD_GROUPS + g + 1) * SSD_STATE].astype(BF16)
        st = st_ref[:, g * gw:(g + 1) * gw]
        inter.append(_dot(cg, st.astype(BF16)))
        st_ref[:, g * gw:(g + 1) * gw] = st * sdec[:, g * gw:(g + 1) * gw] + _dot_tn(bg, xw[:, g * gw:(g + 1) * gw])
    y = y + jnp.concatenate(inter, axis=-1) * ea_e
    y_ref[...] = _ssd_finish(y, xs, z_ref[...], dsk_ref[...], ng_ref[...])

    @pl.when(c == pl.num_programs(1) - 1)
    def _():
        sout_ref[0] = st_ref[...].T


def _ssd_consts(r, t):
    lt, last = _tile_consts(r, t)
    return lt, last, _head_expand(SSD_HEADS, SSD_HEAD_DIM), _head_expand(SSD_HEADS, LANES)


def ssd_prompt(p, dt, prm, n_seq, seq_len, r=128):
    nc = seq_len // r
    consts = _ssd_consts(r, r)
    full = lambda a: pl.BlockSpec(a.shape, lambda b, c: (0,) * a.ndim)
    small = [prm["conv_w"], prm["conv_b"], prm["dt_bias"], prm["a_log"], prm["d_skip"], prm["norm"], *consts]
    return pl.pallas_call(
        functools.partial(_ssd_prompt_body, r=r),
        grid=(n_seq, nc),
        in_specs=[pl.BlockSpec((r, SSD_CONV_DIM), lambda b, c: (b * nc + c, P_XBC // SSD_CONV_DIM)),
                  pl.BlockSpec((r, SSD_WIDTH), lambda b, c: (b * nc + c, P_Z // SSD_WIDTH)),
                  pl.BlockSpec((r, LANES), lambda b, c: (b * nc + c, 0)),
                  *[full(a) for a in small]],
        out_specs=[pl.BlockSpec((r, SSD_WIDTH), lambda b, c: (b * nc + c, 0)),
                   pl.BlockSpec((1, SSD_WIDTH, SSD_STATE), lambda b, c: (b, 0, 0))],
        out_shape=[jax.ShapeDtypeStruct((p.shape[0], SSD_WIDTH), BF16),
                   jax.ShapeDtypeStruct((n_seq, SSD_WIDTH, SSD_STATE), F32)],
        scratch_shapes=[pltpu.VMEM((HALO + r, SSD_CONV_DIM), F32), pltpu.VMEM((SSD_STATE, SSD_WIDTH), F32)],
        compiler_params=_cparams(("parallel", "arbitrary")),
        name="ssd_prompt",
    )(p, p, dt, *small)


def _ssd_sample_body(xbc_ref, z_ref, dt_ref, buf_ref, s0_ref, cw_ref, cb_ref, dtb_ref, alog_ref, dsk_ref,
                     ng_ref, lt_ref, last_ref, e_ref, e128_ref, *rest, r, t):
    y_ref, sout_ref, ext_ref = rest[-3:]
    nb = r // t
    pitch = HALO + t
    for b in range(nb):
        ext_ref[b * pitch + HALO - (SSD_CONV - 1):b * pitch + HALO, :] = buf_ref[b]
        ext_ref[b * pitch + HALO:(b + 1) * pitch, :] = xbc_ref[b * t:(b + 1) * t, :]
    xc = jnp.concatenate([_conv_silu(ext_ref, cw_ref, cb_ref, b * pitch + HALO, t) for b in range(nb)], axis=0)

    xs, y, a_e, dt_e, alast_e = _ssd_tile(xc, dt_ref[...], dtb_ref[...], alog_ref[...], lt_ref[...],
                                          last_ref[...], e_ref[...], e128_ref[...], r, t)
    ea_e = jnp.exp(a_e)
    xw = xs * (dt_e * jnp.exp(alast_e - a_e))
    sdec = jnp.exp(alast_e)
    gw = SSD_WIDTH // SSD_GROUPS
    pr = 2 * t
    prow = lax.broadcasted_iota(jnp.int32, (pr, 1), 0)
    ones = jnp.ones((pr, SSD_STATE), BF16)
    inter = []
    for g in range(SSD_GROUPS):
        bg = xc[:, SSD_WIDTH + g * SSD_STATE:SSD_WIDTH + (g + 1) * SSD_STATE].astype(BF16)
        cg = xc[:, SSD_WIDTH + (SSD_GROUPS + g) * SSD_STATE:
                SSD_WIDTH + (SSD_GROUPS + g + 1) * SSD_STATE].astype(BF16)
        cols = slice(g * gw, (g + 1) * gw)
        rows_out = []
        for q in range(nb // 2):
            rs = slice(q * pr, (q + 1) * pr)
            acc = None
            for s in range(2):
                b = 2 * q + s
                mine = (prow >= s * t) & (prow < (s + 1) * t)
                s0 = s0_ref[b, cols, :]
                yi = _dot_nt(cg[rs], s0.astype(BF16))
                acc = jnp.where(mine, yi, 0.0) if acc is None else acc + jnp.where(mine, yi, 0.0)
                upd = _dot_tn(jnp.where(mine, xw[rs, cols], 0.0).astype(BF16), bg[rs])
                lastrow = prow == (s + 1) * t - 1
                dh, dl = _split2(jnp.where(lastrow, sdec[rs, cols], 0.0))
                dcol = _dot_tn(dh, ones) + _dot_tn(dl, ones)
                sout_ref[b, cols, :] = s0 * dcol + upd
            rows_out.append(acc)
        inter.append(jnp.concatenate(rows_out, axis=0))
    y = y + jnp.concatenate(inter, axis=-1) * ea_e
    y_ref[...] = _ssd_finish(y, xs, z_ref[...], dsk_ref[...], ng_ref[...])


def ssd_sample(p, dt, conv_buf, s0, ybuf, sprev, prm, layer, row0, n_seq, seq_len, r=64):
    nb = r // seq_len
    blk0 = row0 // r
    consts = _ssd_consts(r, seq_len)
    full = lambda a: pl.BlockSpec(a.shape, lambda i: (0,) * a.ndim)
    small = [prm["conv_w"], prm["conv_b"], prm["dt_bias"], prm["a_log"], prm["d_skip"], prm["norm"], *consts]
    inplace = [ybuf] if sprev is None else [ybuf, sprev]
    n_in = 5 + len(small)
    return pl.pallas_call(
        functools.partial(_ssd_sample_body, r=r, t=seq_len),
        grid=(n_seq // nb,),
        in_specs=[pl.BlockSpec((r, SSD_CONV_DIM), lambda i: (blk0 + i, P_XBC // SSD_CONV_DIM)),
                  pl.BlockSpec((r, SSD_WIDTH), lambda i: (blk0 + i, P_Z // SSD_WIDTH)),
                  pl.BlockSpec((r, LANES), lambda i: (blk0 + i, 0)),
                  pl.BlockSpec((None, nb, SSD_CONV - 1, SSD_CONV_DIM), lambda i: (layer, i, 0, 0)),
                  pl.BlockSpec((None, nb, SSD_WIDTH, SSD_STATE), lambda i: (layer, i, 0, 0)),
                  *[full(a) for a in small],
                  *[pl.BlockSpec(memory_space=pl.ANY) for _ in inplace]],
        out_specs=[pl.BlockSpec((r, SSD_WIDTH), lambda i: (blk0 + i, 0)),
                   pl.BlockSpec((None, nb, SSD_WIDTH, SSD_STATE), lambda i: (layer, i, 0, 0))],
        out_shape=[jax.ShapeDtypeStruct(ybuf.shape, BF16),
                   jax.ShapeDtypeStruct(s0.shape, F32)],
        input_output_aliases={n_in + k: k for k in range(len(inplace))},
        scratch_shapes=[pltpu.VMEM((nb * (HALO + seq_len), SSD_CONV_DIM), F32)],
        compiler_params=_cparams(("parallel",)),
        name="ssd_sample",
    )(p, p, dt, conv_buf, s0, *small, *inplace)


S5_CH = S5_GROUPS * S5_STATE
S5_BLK = 4
S5_SSEQ = 32


def _s5_in(u, bre_ref, bim_ref):
    ub = u.astype(BF16)
    kin = S5_WIDTH // S5_BLK
    re = [_dot(ub[:, q * kin:(q + 1) * kin], bre_ref[q]) for q in range(S5_BLK)]
    im = [_dot(ub[:, q * kin:(q + 1) * kin], bim_ref[q]) for q in range(S5_BLK)]
    return jnp.concatenate(re, axis=-1), jnp.concatenate(im, axis=-1)


def _s5_out(h_re, h_im, u, cre_ref, cim_ref, dsk):
    kst = S5_CH // S5_BLK
    hr = h_re.astype(BF16)
    hi = h_im.astype(BF16)
    y = [_dot(hr[:, q * kst:(q + 1) * kst], cre_ref[q]) - _dot(hi[:, q * kst:(q + 1) * kst], cim_ref[q])
         for q in range(S5_BLK)]
    y = jnp.concatenate(y, axis=-1) + dsk * u
    return jax.nn.gelu(y).astype(BF16)


def _s5_scan_body(u_ref, h0re_ref, h0im_ref, are_ref, aim_ref, bre_ref, bim_ref, cre_ref, cim_ref, dsk_ref,
                  y_ref, hre_ref, him_ref, sre_ref, sim_ref, *, s, tc):
    c = pl.program_id(1)

    @pl.when(c == 0)
    def _():
        sre_ref[0:s, :] = h0re_ref[...]
        sim_ref[0:s, :] = h0im_ref[...]

    u = u_ref[...]
    bu_re, bu_im = _s5_in(u, bre_ref, bim_ref)
    sre_ref[s:, :] = bu_re
    sim_ref[s:, :] = bu_im
    a_re = jnp.broadcast_to(are_ref[...], (s, S5_CH))
    a_im = jnp.broadcast_to(aim_ref[...], (s, S5_CH))

    def step(t, carry):
        prev = pl.ds(pl.multiple_of(t * s, s), s)
        cur = pl.ds(pl.multiple_of((t + 1) * s, s), s)
        h_re = sre_ref[prev, :]
        h_im = sim_ref[prev, :]
        sre_ref[cur, :] = a_re * h_re - a_im * h_im + sre_ref[cur, :]
        sim_ref[cur, :] = a_re * h_im + a_im * h_re + sim_ref[cur, :]
        return carry

    lax.fori_loop(0, tc, step, 0, unroll=4)
    h_re = sre_ref[tc * s:, :]
    h_im = sim_ref[tc * s:, :]
    y_ref[...] = _s5_out(sre_ref[s:, :], sim_ref[s:, :], u, cre_ref, cim_ref, dsk_ref[...])
    sre_ref[0:s, :] = h_re
    sim_ref[0:s, :] = h_im
    hre_ref[...] = h_re
    him_ref[...] = h_im


def s5_scan(u, h0_re, h0_im, prm, n_blocks, s, n_steps, tc):
    nc = n_steps // tc
    r = tc * s
    full = lambda a: pl.BlockSpec(a.shape, lambda b, c: (0,) * a.ndim)
    small = [prm["abar_re"], prm["abar_im"], prm["b_re"], prm["b_im"], prm["c_re"], prm["c_im"], prm["d_skip"]]
    state = jax.ShapeDtypeStruct((n_blocks * s, S5_CH), F32)
    sspec = pl.BlockSpec((s, S5_CH), lambda b, c: (b, 0))
    return pl.pallas_call(
        functools.partial(_s5_scan_body, s=s, tc=tc),
        grid=(n_blocks, nc),
        in_specs=[pl.BlockSpec((r, S5_WIDTH), lambda b, c: (b * nc + c, 0)), sspec, sspec,
                  *[full(a) for a in small]],
        out_specs=[pl.BlockSpec((r, S5_WIDTH), lambda b, c: (b * nc + c, 0)), sspec, sspec],
        out_shape=[jax.ShapeDtypeStruct((u.shape[0], S5_WIDTH), BF16), state, state],
        scratch_shapes=[pltpu.VMEM((s + r, S5_CH), F32), pltpu.VMEM((s + r, S5_CH), F32)],
        compiler_params=_cparams(("parallel", "arbitrary")),
        name="s5_scan",
    )(u, h0_re, h0_im, *small)


S5_T = 16
S5_GB = LANES // S5_GROUP_SIZE
S5_CL = S5_T * LANES
S5_SL = S5_GB * S5_STATE
S5_NB = 2
S5_PITCH_PAD = 8


def _block_diag_rows(x, rows_w, cols_w):
    xt = jnp.concatenate([x] * S5_GB, axis=0)
    r = lax.broadcasted_iota(jnp.int32, xt.shape, 0) // rows_w
    c = lax.broadcasted_iota(jnp.int32, xt.shape, 1) // cols_w
    return jnp.where(r == c, xt, jnp.zeros_like(xt))


def _s5_chunk_body(u_ref, kj_ref, bcre_ref, bcim_ref, ccre_ref, ccim_ref, are_ref, aim_ref, dsk_ref,
                   y_ref, hre_ref, him_ref, vre_ref, vim_ref, sre_ref, sim_ref,
                   tp_ref, bre_ref, bim_ref, cre_ref, cim_ref, *, n_chunks):
    ks = S5_GROUP_SIZE

    @pl.when(pl.program_id(1) == 0)
    def _():
        zero = jnp.zeros((LANES, LANES), BF16)
        for tau in range(S5_T):
            piece = _block_diag_rows(kj_ref[0, tau], ks, ks)
            for s in range(S5_T - tau):
                tp_ref[s * LANES:(s + 1) * LANES, (s + tau) * LANES:(s + tau + 1) * LANES] = piece
        for s in range(1, S5_T):
            for t in range(s):
                tp_ref[s * LANES:(s + 1) * LANES, t * LANES:(t + 1) * LANES] = zero
        for s in range(S5_T):
            bre_ref[s * LANES:(s + 1) * LANES, :] = _block_diag_rows(bcre_ref[0, s], ks, S5_STATE)
            bim_ref[s * LANES:(s + 1) * LANES, :] = _block_diag_rows(bcim_ref[0, s], ks, S5_STATE)
            cre_ref[:, s * LANES:(s + 1) * LANES] = _block_diag_rows(ccre_ref[0, s], S5_STATE, ks)
            cim_ref[:, s * LANES:(s + 1) * LANES] = _block_diag_rows(ccim_ref[0, s], S5_STATE, ks)

    rows = S5_NB * n_chunks
    pitch = n_chunks + S5_PITCH_PAD
    nk = S5_SL // LANES
    u = jnp.concatenate([u_ref[pl.ds(t, rows, stride=S5_T), :] for t in range(S5_T)], axis=-1)
    ub = u.astype(BF16)
    v_re = _dot(ub, bre_ref[...])
    v_im = _dot(ub, bim_ref[...])
    for k in range(nk):
        vre_ref[k] = v_re[:, k * LANES:(k + 1) * LANES]
        vim_ref[k] = v_im[:, k * LANES:(k + 1) * LANES]
        sre_ref[k] = jnp.zeros((S5_NB * pitch, LANES), F32)
        sim_ref[k] = jnp.zeros((S5_NB * pitch, LANES), F32)
    a_re = [jnp.broadcast_to(are_ref[0, :, k * LANES:(k + 1) * LANES], (S5_NB, LANES)) for k in range(nk)]
    a_im = [jnp.broadcast_to(aim_ref[0, :, k * LANES:(k + 1) * LANES], (S5_NB, LANES)) for k in range(nk)]

    def step(c, carry):
        for k in range(nk):
            h_re = sre_ref[k, pl.ds(c, S5_NB, stride=pitch), :]
            h_im = sim_ref[k, pl.ds(c, S5_NB, stride=pitch), :]
            x_re = vre_ref[k, pl.ds(c, S5_NB, stride=n_chunks), :]
            x_im = vim_ref[k, pl.ds(c, S5_NB, stride=n_chunks), :]
            sre_ref[k, pl.ds(c + 1, S5_NB, stride=pitch), :] = a_re[k] * h_re - a_im[k] * h_im + x_re
            sim_ref[k, pl.ds(c + 1, S5_NB, stride=pitch), :] = a_re[k] * h_im + a_im[k] * h_re + x_im
        return carry

    lax.fori_loop(0, n_chunks, step, 0, unroll=2)
    starts = lambda ref: jnp.concatenate(
        [jnp.concatenate([ref[k, b * pitch:b * pitch + n_chunks, :] for b in range(S5_NB)], axis=0)
         for k in range(nk)], axis=-1)
    y = _dot(ub, tp_ref[...])
    y = y + _dot(starts(sre_ref).astype(BF16), cre_ref[...]) - _dot(starts(sim_ref).astype(BF16), cim_ref[...])
    y = jax.nn.gelu(y + dsk_ref[0] * u)
    for t in range(S5_T):
        y_ref[pl.ds(t, rows, stride=S5_T), :] = y[:, t * LANES:(t + 1) * LANES]
    final = lambda ref: jnp.concatenate(
        [jnp.concatenate([ref[k, b * pitch + n_chunks:b * pitch + n_chunks + 1, :] for b in range(S5_NB)], axis=0)
         for k in range(nk)], axis=-1)
    hre_ref[...] = final(sre_ref)
    him_ref[...] = final(sim_ref)


def s5_chunked(p, prm, n_seq, seq_len):
    n_chunks = seq_len // S5_T
    rows = S5_NB * n_chunks
    nblk = S5_GROUPS // S5_GB
    nhalf = n_seq // S5_NB
    pitch = n_chunks + S5_PITCH_PAD
    nk = S5_SL // LANES
    blk = lambda a: pl.BlockSpec((1,) + a.shape[1:], lambda j, i: (j,) + (0,) * (a.ndim - 1))
    small = [prm["kern_j"], prm["bst_re"], prm["bst_im"], prm["cst_re"], prm["cst_im"], prm["apow_re"],
             prm["apow_im"], prm["d_tiled"]]
    state = jax.ShapeDtypeStruct((nhalf, S5_NB, S5_CH), F32)
    sspec = pl.BlockSpec((None, S5_NB, S5_SL), lambda j, i: (i, 0, j))
    return pl.pallas_call(
        functools.partial(_s5_chunk_body, n_chunks=n_chunks),
        grid=(nblk, nhalf),
        in_specs=[pl.BlockSpec((S5_NB * seq_len, LANES), lambda j, i: (i, P_U // LANES + j)),
                  *[blk(a) for a in small]],
        out_specs=[pl.BlockSpec((S5_NB * seq_len, LANES), lambda j, i: (i, j)), sspec, sspec],
        out_shape=[jax.ShapeDtypeStruct((p.shape[0], S5_WIDTH), F32), state, state],
        scratch_shapes=[pltpu.VMEM((nk, rows, LANES), F32), pltpu.VMEM((nk, rows, LANES), F32),
                        pltpu.VMEM((nk, S5_NB * pitch, LANES), F32), pltpu.VMEM((nk, S5_NB * pitch, LANES), F32),
                        pltpu.VMEM((S5_CL, S5_CL), BF16), pltpu.VMEM((S5_CL, S5_SL), BF16),
                        pltpu.VMEM((S5_CL, S5_SL), BF16), pltpu.VMEM((S5_SL, S5_CL), BF16),
                        pltpu.VMEM((S5_SL, S5_CL), BF16)],
        compiler_params=_cparams(("parallel", "arbitrary")),
        name="s5_chunked",
    )(p, *small)


def _hg_consts(r, t):
    i = np.arange(r)[:, None]
    j = np.arange(r)[None, :]
    sums, upper, pair = [], [], []
    s = 1
    while s < t:
        blk_i, blk_j = i // (2 * s), j // (2 * s)
        up_i = (i % (2 * s)) >= s
        mid_i = blk_i * 2 * s + s
        m_up = up_i & (j >= mid_i) & (j <= i)
        m_lo = (~up_i) & (j > i) & (j < mid_i)
        sums.append((m_up | m_lo).astype(np.float32))
        upper.append(np.broadcast_to(up_i, (r, 1)).astype(np.float32))
        pair.append(((blk_i == blk_j) & up_i & ((j % (2 * s)) < s)).astype(np.float32))
        s *= 2
    pair.append((i == j).astype(np.float32))
    return (jnp.asarray(np.stack(sums), BF16), jnp.asarray(np.stack(upper), F32),
            jnp.asarray(np.stack(pair), F32))


def _hg_gates(hf, lb):
    logf = -_softplus(-hf) + jnp.log1p(lb * jnp.exp(-hf))
    kk = (1.0 - lb) * jax.nn.sigmoid(-hf)
    return logf, kk


def _hg_intra(q, kk, v, logf, b, sums_ref, upper_ref, pair_ref):
    r = q.shape[0]
    nlev = sums_ref.shape[0]
    lf3 = _split3(logf)
    qb = q.astype(BF16)
    kb = kk.astype(BF16)
    vb = _pad_rows(v.astype(BF16), LANES) if r < LANES else v.astype(BF16)
    scores = [None] * HG_HEADS
    for lev in range(nlev + 1):
        if lev < nlev:
            upper = upper_ref[lev] > 0.5
            half = 2 ** lev
            if half >= HALO:
                blocks = b.reshape(r // (2 * half), 2 * half, HG_WIDTH)
                ref = jnp.broadcast_to(blocks[:, half - 1:half, :], blocks.shape).reshape(r, HG_WIDTH)
                d = jnp.where(upper, b - ref, ref - b)
            else:
                m = sums_ref[lev]
                d = _dot(m, lf3[0]) + _dot(m, lf3[1]) + _dot(m, lf3[2])
            x = (jnp.where(upper, q, kk) * jnp.exp(d)).astype(BF16)
            xq, xk = x, x
        else:
            xq, xk = qb, kb
        mask = pair_ref[lev]
        for h in range(HG_HEADS):
            sl = slice(h * HG_KEY_DIM, (h + 1) * HG_KEY_DIM)
            sc = _dot_nt(xq[:, sl], xk[:, sl]) * mask
            scores[h] = sc if scores[h] is None else scores[h] + sc
    outs = [_dot(scores[h].astype(BF16), vb[:, h * HG_VAL_DIM:(h + 1) * HG_VAL_DIM]) for h in range(HG_HEADS)]
    return jnp.concatenate(outs, axis=-1)


def _hg_finish(o, hgate, ng):
    outs = []
    for h in range(HG_HEADS):
        oh = o[:, h * HG_VAL_DIM:(h + 1) * HG_VAL_DIM]
        outs.append(oh * _rms_scale(oh) * ng)
    return (jnp.concatenate(outs, axis=-1) * _silu(hgate)).astype(BF16)


def _hg_prompt_body(q_ref, f_ref, i_ref, gate_ref, lb_ref, ng_ref, lt_ref, last_ref, sums_ref, upper_ref,
                    pair_ref, o_ref, sout_ref, *, r):
    c = pl.program_id(1)

    @pl.when(c == 0)
    def _():
        sout_ref[...] = jnp.zeros_like(sout_ref)

    q = q_ref[...]
    v = i_ref[...]
    logf, kk = _hg_gates(f_ref[...], lb_ref[...])
    b = _sel_dot(lt_ref[...], logf, 3)
    o = _hg_intra(q, kk, v, logf, b, sums_ref, upper_ref, pair_ref)
    blast = b[r - 1:r, :]
    qe = (q * jnp.exp(b)).astype(BF16)
    kw = (kk * jnp.exp(blast - b)).astype(BF16)
    vb = v.astype(BF16)
    ones = jnp.ones((r, HG_VAL_DIM), BF16)
    rows = lax.broadcasted_iota(jnp.int32, (r, 1), 0)
    dh, dl = _split2(jnp.where(rows == r - 1, jnp.exp(b), 0.0))
    inter = []
    for h in range(HG_HEADS):
        sl = slice(h * HG_KEY_DIM, (h + 1) * HG_KEY_DIM)
        s = sout_ref[0, h]
        inter.append(_dot(qe[:, sl], s.astype(BF16)))
        dcol = _dot_tn(dh[:, sl], ones) + _dot_tn(dl[:, sl], ones)
        sout_ref[0, h] = s * dcol + _dot_tn(kw[:, sl], vb[:, sl])
    o = o + jnp.concatenate(inter, axis=-1)
    o_ref[...] = _hg_finish(o, gate_ref[...], ng_ref[...])


def hg_prompt(p, prm, n_seq, seq_len, r=128):
    nc = seq_len // r
    lt, last = _tile_consts(r, r)
    consts = [lt, last, *_hg_consts(r, r)]
    full = lambda a: pl.BlockSpec(a.shape, lambda b, c: (0,) * a.ndim)
    small = [prm["lb"], prm["norm"], *consts]
    col = lambda off: pl.BlockSpec((r, HG_WIDTH), lambda b, c: (b * nc + c, off // HG_WIDTH))
    return pl.pallas_call(
        functools.partial(_hg_prompt_body, r=r),
        grid=(n_seq, nc),
        in_specs=[col(P_HQ), col(P_HF), col(P_HI), col(P_HGATE), *[full(a) for a in small]],
        out_specs=[pl.BlockSpec((r, HG_WIDTH), lambda b, c: (b * nc + c, 0)),
                   pl.BlockSpec((1, HG_HEADS, HG_KEY_DIM, HG_VAL_DIM), lambda b, c: (b, 0, 0, 0))],
        out_shape=[jax.ShapeDtypeStruct((p.shape[0], HG_WIDTH), BF16),
                   jax.ShapeDtypeStruct((n_seq, HG_HEADS, HG_KEY_DIM, HG_VAL_DIM), F32)],
        compiler_params=_cparams(("parallel", "arbitrary")),
        name="hg_prompt",
    )(p, p, p, p, *small)


def _hg_sample_body(q_ref, f_ref, i_ref, gate_ref, s0_ref, lb_ref, ng_ref, lt_ref, last_ref, sums_ref,
                    upper_ref, pair_ref, *rest, r, t):
    o_ref, sout_ref = rest[-2:]
    q = q_ref[...]
    v = i_ref[...]
    logf, kk = _hg_gates(f_ref[...], lb_ref[...])
    b = _sel_dot(lt_ref[...], logf, 3)
    o = _hg_intra(q, kk, v, logf, b, sums_ref, upper_ref, pair_ref)
    blast = _sel_dot(last_ref[...], b, 3)
    qe = (q * jnp.exp(b)).astype(BF16)
    kw = kk * jnp.exp(blast - b)
    sdec = jnp.exp(blast)
    vb = v.astype(BF16)
    pr = 2 * t
    prow = lax.broadcasted_iota(jnp.int32, (pr, 1), 0)
    ones = jnp.ones((pr, HG_VAL_DIM), BF16)
    rows_out = []
    for p2 in range(r // pr):
        rs = slice(p2 * pr, (p2 + 1) * pr)
        heads = []
        for h in range(HG_HEADS):
            sl = slice(h * HG_KEY_DIM, (h + 1) * HG_KEY_DIM)
            acc = None
            for s in range(2):
                bi = 2 * p2 + s
                mine = (prow >= s * t) & (prow < (s + 1) * t)
                s0 = s0_ref[bi, h]
                oi = jnp.where(mine, _dot(qe[rs, sl], s0.astype(BF16)), 0.0)
                acc = oi if acc is None else acc + oi
                upd = _dot_tn(jnp.where(mine, kw[rs, sl], 0.0).astype(BF16), vb[rs, sl])
                dh, dl = _split2(jnp.where(prow == (s + 1) * t - 1, sdec[rs, sl], 0.0))
                dcol = _dot_tn(dh, ones) + _dot_tn(dl, ones)
                sout_ref[bi, h] = s0 * dcol + upd
            heads.append(acc)
        rows_out.append(jnp.concatenate(heads, axis=-1))
    o = o + jnp.concatenate(rows_out, axis=0)
    o_ref[...] = _hg_finish(o, gate_ref[...], ng_ref[...])


def hg_sample(p, s0, obuf, sprev, prm, layer, row0, n_seq, seq_len, r=128):
    nb = r // seq_len
    blk0 = row0 // r
    lt, last = _tile_consts(r, seq_len)
    consts = [lt, last, *_hg_consts(r, seq_len)]
    full = lambda a: pl.BlockSpec(a.shape, lambda i: (0,) * a.ndim)
    small = [prm["lb"], prm["norm"], *consts]
    col = lambda off: pl.BlockSpec((r, HG_WIDTH), lambda i: (blk0 + i, off // HG_WIDTH))
    sspec = pl.BlockSpec((None, nb, HG_HEADS, HG_KEY_DIM, HG_VAL_DIM), lambda i: (layer, i, 0, 0, 0))
    inplace = [obuf] if sprev is None else [obuf, sprev]
    n_in = 5 + len(small)
    return pl.pallas_call(
        functools.partial(_hg_sample_body, r=r, t=seq_len),
        grid=(n_seq // nb,),
        in_specs=[col(P_HQ), col(P_HF), col(P_HI), col(P_HGATE), sspec, *[full(a) for a in small],
                  *[pl.BlockSpec(memory_space=pl.ANY) for _ in inplace]],
        out_specs=[pl.BlockSpec((r, HG_WIDTH), lambda i: (blk0 + i, 0)), sspec],
        out_shape=[jax.ShapeDtypeStruct(obuf.shape, BF16), jax.ShapeDtypeStruct(s0.shape, F32)],
        input_output_aliases={n_in + k: k for k in range(len(inplace))},
        compiler_params=_cparams(("parallel",)),
        name="hg_sample",
    )(p, p, p, p, s0, *small, *inplace)


def _s5_params(a_re, a_im, log_dt, b_re, b_im, c_re, c_im, d_skip):
    dt = jnp.exp(log_dt)[:, None]
    mag = jnp.exp(a_re * dt)
    abar_re = mag * jnp.cos(a_im * dt)
    abar_im = mag * jnp.sin(a_im * dt)
    den = a_re * a_re + a_im * a_im
    nr = abar_re - 1.0
    coef_re = (nr * a_re + abar_im * a_im) / den
    coef_im = (abar_im * a_re - nr * a_im) / den
    bbar_re = coef_re[..., None] * b_re - coef_im[..., None] * b_im
    bbar_im = coef_re[..., None] * b_im + coef_im[..., None] * b_re
    gpb = S5_GROUPS // S5_BLK
    eye = jnp.eye(gpb, dtype=F32)

    def in_blocks(bbar):
        bb = bbar.reshape(S5_BLK, gpb, S5_STATE, S5_GROUP_SIZE)
        return jnp.einsum("qgnk,gh->qgkhn", bb, eye).reshape(
            S5_BLK, gpb * S5_GROUP_SIZE, gpb * S5_STATE).astype(BF16)

    def out_blocks(c):
        cc = c.reshape(S5_BLK, gpb, S5_GROUP_SIZE, S5_STATE)
        return jnp.einsum("qgkn,gh->qgnhk", cc, eye).reshape(
            S5_BLK, gpb * S5_STATE, gpb * S5_GROUP_SIZE).astype(BF16)

    prm = {
        "abar_re": abar_re.reshape(1, S5_CH), "abar_im": abar_im.reshape(1, S5_CH),
        "b_re": in_blocks(bbar_re), "b_im": in_blocks(bbar_im),
        "c_re": out_blocks(c_re), "c_im": out_blocks(c_im),
        "d_skip": d_skip.reshape(1, S5_WIDTH),
    }

    tau = jnp.arange(S5_T + 1, dtype=F32)[:, None, None]
    pmag = jnp.exp(tau * (a_re * dt))
    pw_re = pmag * jnp.cos(tau * (a_im * dt))
    pw_im = pmag * jnp.sin(tau * (a_im * dt))
    cp_re = c_re[None] * pw_re[:, :, None, :] - c_im[None] * pw_im[:, :, None, :]
    cp_im = c_re[None] * pw_im[:, :, None, :] + c_im[None] * pw_re[:, :, None, :]
    bt_re = bbar_re.transpose(0, 2, 1)[None, :, None]
    bt_im = bbar_im.transpose(0, 2, 1)[None, :, None]
    kern = jnp.sum(cp_re[:, :, :, None, :] * bt_re - cp_im[:, :, :, None, :] * bt_im, axis=-1)
    nq = S5_GROUPS // S5_GB
    ks = S5_GROUP_SIZE
    kern_j = (kern[:S5_T].reshape(S5_T, nq, S5_GB, ks, ks).transpose(1, 0, 4, 2, 3)
              .reshape(nq, S5_T, ks, LANES).astype(BF16))
    rev = pw_re[S5_T - 1 - jnp.arange(S5_T)], pw_im[S5_T - 1 - jnp.arange(S5_T)]
    bst_re = rev[0][..., None] * bbar_re[None] - rev[1][..., None] * bbar_im[None]
    bst_im = rev[0][..., None] * bbar_im[None] + rev[1][..., None] * bbar_re[None]

    def bst_blocks(b):
        return (b.reshape(S5_T, nq, S5_GB, S5_STATE, ks).transpose(1, 0, 4, 2, 3)
                .reshape(nq, S5_T, ks, S5_SL).astype(BF16))

    def cst_blocks(c):
        return (c.reshape(S5_T, nq, S5_GB, ks, S5_STATE).transpose(1, 0, 4, 2, 3)
                .reshape(nq, S5_T, S5_STATE, LANES).astype(BF16))

    prm.update({
        "kern_j": kern_j,
        "bst_re": bst_blocks(bst_re), "bst_im": bst_blocks(bst_im),
        "cst_re": cst_blocks(cp_re[1:]), "cst_im": cst_blocks(cp_im[1:]),
        "apow_re": pw_re[S5_T].reshape(nq, 1, S5_SL), "apow_im": pw_im[S5_T].reshape(nq, 1, S5_SL),
        "d_tiled": jnp.broadcast_to(d_skip.reshape(nq, 1, 1, LANES), (nq, 1, S5_T, LANES)).reshape(nq, 1, S5_CL),
    })
    return prm


def _in_proj_weights(w_in):
    w_lo = w_in[..., :IN_DT_START].astype(BF16)
    w_hi = w_in[..., IN_DT_START + SSD_HEADS:].astype(BF16)
    wdt = jnp.pad(w_in[..., IN_DT_START:IN_DT_START + SSD_HEADS].astype(BF16),
                  ((0, 0), (0, 0), (0, LANES - SSD_HEADS)))
    return w_lo, w_hi, wdt


def _mem_rows(cache):
    dd, b, m, h, hd = cache.shape
    c = cache.reshape(dd, b, m, h, hd // LANES, LANES).transpose(0, 1, 2, 4, 3, 5)
    return c.reshape(dd, b, m * h * (hd // LANES), LANES)


def _pad_lanes(v):
    return jnp.pad(v.reshape(1, -1), ((0, 0), (0, LANES - v.shape[-1])))


def kernel(x_prompt, x_sample, cache_mem_k, cache_mem_v, state_ssd, state_ssd_conv, state_s5_re, state_s5_im,
           state_hgrn, mem_prompt, norm_ffn1, ffn1_w1, ffn1_w3, ffn1_w2, norm_mix, w_in, ssd_conv_w, ssd_conv_b,
           ssd_dt_bias, ssd_a_log, ssd_d, ssd_norm, ssd_w_out, s5_a_re, s5_a_im, s5_log_dt, s5_b_re, s5_b_im,
           s5_c_re, s5_c_im, s5_d, s5_w_glu_a, s5_w_glu_b, hg_lower_bounds, hg_norm, hg_w_out, w_mix_out,
           norm_xa, norm_mem, xa_wq, xa_wk, xa_wv, xa_wo, norm_ffn2, ffn2_w1, ffn2_w3, ffn2_w2, norm_final):
    bp, lp, d = x_prompt.shape
    bs, ls, _ = x_sample.shape
    mp, ms = bp * lp, bs * ls
    x = jnp.concatenate([x_prompt.reshape(mp, d), x_sample.reshape(ms, d)], axis=0)
    mem = mem_prompt.reshape(bp * MEM_LEN, d)
    row = lambda v: v.reshape(1, -1)
    bf = lambda w: w.astype(BF16)

    lb_p = jax.nn.softmax(hg_lower_bounds, axis=0)
    lb_all = jnp.cumsum(lb_p, axis=0) - lb_p[0]

    ssd_s0 = state_ssd.reshape(DEPTH, bs, SSD_WIDTH, SSD_STATE)
    mem_k = _mem_rows(cache_mem_k)
    mem_v = _mem_rows(cache_mem_v)
    ssd_states = None
    hg_states = None

    f1_w1, f1_w3, f1_w2 = bf(ffn1_w1), bf(ffn1_w3), bf(ffn1_w2)
    f2_w1, f2_w3, f2_w2 = bf(ffn2_w1), bf(ffn2_w3), bf(ffn2_w2)
    w_p, w_p_hi, w_dt = _in_proj_weights(w_in)
    b_ssd, b_glu_a, b_glu_b, b_hg, b_mix = bf(ssd_w_out), bf(s5_w_glu_a), bf(s5_w_glu_b), bf(hg_w_out), bf(w_mix_out)
    b_wq, b_wo = bf(xa_wq), bf(xa_wo)
    b_wkv = bf(jnp.concatenate([xa_wk, xa_wv], axis=-1))

    outs = {k: [] for k in ("pk", "pv", "pss", "pcv", "psr", "psi", "phg", "scv", "ssr", "ssi")}
    for l in range(DEPTH):
        x = ffn(x, row(norm_ffn1[l]), f1_w1, f1_w3, f1_w2, row(norm_final), l, False)

        p, dt = in_proj(x, row(norm_mix[l]), w_p, w_p_hi, w_dt, l)

        ssd_prm = {"conv_w": ssd_conv_w[l], "conv_b": row(ssd_conv_b[l]), "dt_bias": _pad_lanes(ssd_dt_bias[l]),
                   "a_log": _pad_lanes(ssd_a_log[l]), "d_skip": row(jnp.repeat(ssd_d[l], SSD_HEAD_DIM)),
                   "norm": row(ssd_norm[l])}
        ys, ss_p = ssd_prompt(p, dt, ssd_prm, bp, lp)
        ys, ssd_states = ssd_sample(p, dt, state_ssd_conv, ssd_s0, ys, ssd_states, ssd_prm, l, mp, bs, ls)

        s5_prm = _s5_params(s5_a_re[l], s5_a_im[l], s5_log_dt[l], s5_b_re[l], s5_b_im[l], s5_c_re[l],
                            s5_c_im[l], s5_d[l])
        gy, sr_p, si_p = s5_chunked(p, s5_prm, bp, lp)
        u_s = p[mp:, P_U:P_U + S5_WIDTH].reshape(bs // S5_SSEQ, S5_SSEQ, ls, S5_WIDTH).transpose(0, 2, 1, 3)
        u_s = u_s.reshape(ms, S5_WIDTH)
        gy_s, sr_s, si_s = s5_scan(u_s, state_s5_re[l].reshape(bs, S5_CH), state_s5_im[l].reshape(bs, S5_CH),
                                   s5_prm, bs // S5_SSEQ, S5_SSEQ, ls, ls)
        gy_s = gy_s.reshape(bs // S5_SSEQ, ls, S5_SSEQ, S5_WIDTH).transpose(0, 2, 1, 3).reshape(ms, S5_WIDTH)
        gy = lax.dynamic_update_slice(gy, gy_s.astype(F32), (mp, 0))

        hg_prm = {"lb": row(lb_all[l]), "norm": row(hg_norm[l])}
        o, hg_p = hg_prompt(p, hg_prm, bp, lp)
        o, hg_states = hg_sample(p, state_hgrn, o, hg_states, hg_prm, l, mp, bs, ls)

        x = branch_mix(x, ys, gy, o, p, b_ssd, b_glu_a, b_glu_b, b_hg, b_mix, l)

        q = norm_proj(x, row(norm_xa[l]), b_wq, l, BF16)
        kv = norm_proj(mem, row(norm_mem[l]), b_wkv, l, F32)
        at = xattn_prompt(q, kv, bp, lp)
        at = xattn_sample(q, mem_k, mem_v, at, l, mp, bs, ls)
        x = res_mm(x, at, b_wo, l)

        x = ffn(x, row(norm_ffn2[l]), f2_w1, f2_w3, f2_w2, row(norm_final), l, l == DEPTH - 1)

        tail = SSD_CONV - 1
        outs["pk"].append(kv[:, :d].reshape(bp, MEM_LEN, XA_HEADS, XA_HEAD_DIM))
        outs["pv"].append(kv[:, d:].reshape(bp, MEM_LEN, XA_HEADS, XA_HEAD_DIM))
        outs["pss"].append(ss_p.reshape(bp, SSD_HEADS, SSD_HEAD_DIM, SSD_STATE))
        outs["pcv"].append(jnp.stack([p[(b + 1) * lp - tail:(b + 1) * lp, P_XBC:P_XBC + SSD_CONV_DIM]
                                      for b in range(bp)]))
        outs["psr"].append(sr_p.reshape(bp, S5_GROUPS, S5_STATE))
        outs["psi"].append(si_p.reshape(bp, S5_GROUPS, S5_STATE))
        outs["phg"].append(hg_p)
        outs["scv"].append(p[mp:, P_XBC:P_XBC + SSD_CONV_DIM].reshape(bs, ls, SSD_CONV_DIM)[:, ls - tail:])
        outs["ssr"].append(sr_s.reshape(bs, S5_GROUPS, S5_STATE))
        outs["ssi"].append(si_s.reshape(bs, S5_GROUPS, S5_STATE))

    st = lambda k: jnp.stack(outs[k])
    return (x[:mp].reshape(bp, lp, d), x[mp:].reshape(bs, ls, d),
            st("pk"), st("pv"), st("pss"), st("pcv"), st("psr"), st("psi"), st("phg"),
            ssd_states.reshape(DEPTH, bs, SSD_HEADS, SSD_HEAD_DIM, SSD_STATE), st("scv"), st("ssr"), st("ssi"),
            hg_states)
```

```python
import functools
import math

import jax
import jax.numpy as jnp
import numpy as np
from jax import lax
from jax.experimental import pallas as pl
from jax.experimental.pallas import tpu as pltpu

F32 = jnp.float32
BF16 = jnp.bfloat16

D_MODEL = 2048
DEPTH = 2
NORM_EPS = 1e-5
SSD_HEAD_DIM = 64
SSD_HEADS = 32
SSD_GROUPS = 4
SSD_STATE = 128
SSD_CONV = 4
SSD_WIDTH = 2048
SSD_CONV_DIM = 3072
S5_WIDTH = 1024
S5_GROUP_SIZE = 16
S5_GROUPS = 64
S5_STATE = 64
HG_WIDTH = 1024
HG_HEADS = 8
HG_KEY_DIM = 128
HG_VAL_DIM = 128
MEM_LEN = 256
XA_HEADS = 4
XA_HEAD_DIM = 512
FFN_DIM = 5632

P_GATES, P_XBC, P_U, P_HQ, P_HF, P_HI, P_HGATE, P_Z = 0, 6144, 9216, 10240, 11264, 12288, 13312, 14336
P_WIDTH = 16384

V7X_VMEM_LIMIT = 56 * 1024 * 1024


def _cparams(sem, vmem=V7X_VMEM_LIMIT):
    return pltpu.CompilerParams(dimension_semantics=sem, vmem_limit_bytes=vmem)


def _rms_scale(x):
    return lax.rsqrt(jnp.mean(x * x, axis=-1, keepdims=True) + NORM_EPS)


def _silu(x):
    return x * jax.nn.sigmoid(x)


def _dot(a, b):
    return jnp.dot(a, b, preferred_element_type=F32)


def _dot_nt(a, b):
    return lax.dot_general(a, b, (((1,), (1,)), ((), ())), preferred_element_type=F32)


def _dot_tn(a, b):
    return lax.dot_general(a, b, (((0,), (0,)), ((), ())), preferred_element_type=F32)


def _split2(x):
    hi = x.astype(BF16)
    lo = (x - hi.astype(F32)).astype(BF16)
    return hi, lo


def _split3(x):
    hi = x.astype(BF16)
    r = x - hi.astype(F32)
    mid = r.astype(BF16)
    lo = (r - mid.astype(F32)).astype(BF16)
    return hi, mid, lo


def _sel_dot(sel, x, parts=3):
    ps = _split3(x) if parts == 3 else _split2(x)
    out = _dot(sel, ps[0])
    for p in ps[1:]:
        out = out + _dot(sel, p)
    return out


def _dot_sel(x, sel, parts=2):
    ps = _split3(x) if parts == 3 else _split2(x)
    out = _dot(ps[0], sel)
    for p in ps[1:]:
        out = out + _dot(p, sel)
    return out


def _norm_proj_body(x_ref, g_ref, w_ref, o_ref, h_ref):
    @pl.when(pl.program_id(1) == 0)
    def _():
        x = x_ref[...]
        h_ref[...] = (x * _rms_scale(x) * g_ref[...]).astype(BF16)

    o_ref[...] = _dot(h_ref[...], w_ref[...]).astype(o_ref.dtype)


def norm_proj(x, g, w, layer, out_dtype, bm=1024, bn=1024):
    m, k = x.shape
    n = w.shape[2]
    return pl.pallas_call(
        _norm_proj_body,
        grid=(m // bm, n // bn),
        in_specs=[pl.BlockSpec((bm, k), lambda i, j: (i, 0)),
                  pl.BlockSpec((1, k), lambda i, j: (0, 0)),
                  pl.BlockSpec((None, k, bn), lambda i, j: (layer, 0, j))],
        out_specs=pl.BlockSpec((bm, bn), lambda i, j: (i, j)),
        out_shape=jax.ShapeDtypeStruct((m, n), out_dtype),
        scratch_shapes=[pltpu.VMEM((bm, k), BF16)],
        compiler_params=_cparams(("parallel", "arbitrary")),
        name="norm_proj",
    )(x, g, w)


IN_SEGMENTS = (("z", 2048, P_Z), ("xbc", 3072, P_XBC), ("dt", SSD_HEADS, None), ("u", 1024, P_U),
               ("hq", 1024, P_HQ), ("hf", 1024, P_HF), ("hi", 1024, P_HI), ("hgate", 1024, P_HGATE),
               ("gates", 6144, P_GATES))
IN_DIM = sum(width for _, width, _ in IN_SEGMENTS)


def _in_weights_body(w_ref, o_ref, odt_ref):
    col = 0
    for _, width, off in IN_SEGMENTS:
        piece = w_ref[:, col:col + width].astype(BF16)
        if off is None:
            odt_ref[...] = jnp.zeros_like(odt_ref)
            odt_ref[:, 0:width] = piece
        else:
            o_ref[:, off:off + width] = piece
        col += width


def in_proj_weights(w_in, rows=256):
    dd, k, n = w_in.shape
    return pl.pallas_call(
        _in_weights_body,
        grid=(dd, k // rows),
        in_specs=[pl.BlockSpec((None, rows, n), lambda l, i: (l, i, 0))],
        out_specs=[pl.BlockSpec((None, rows, P_WIDTH), lambda l, i: (l, i, 0)),
                   pl.BlockSpec((None, rows, LANES), lambda l, i: (l, i, 0))],
        out_shape=[jax.ShapeDtypeStruct((dd, k, P_WIDTH), BF16), jax.ShapeDtypeStruct((dd, k, LANES), BF16)],
        compiler_params=_cparams(("parallel", "parallel")),
        name="in_proj_weights",
    )(w_in)


def _in_proj_body(x_ref, g_ref, w_ref, wdt_ref, o_ref, odt_ref, h_ref):
    @pl.when(pl.program_id(1) == 0)
    def _():
        x = x_ref[...]
        h = (x * _rms_scale(x) * g_ref[...]).astype(BF16)
        h_ref[...] = h
        odt_ref[...] = _dot(h, wdt_ref[...])

    o_ref[...] = _dot(h_ref[...], w_ref[...])


def in_proj(x, g, w, wdt, layer, bm=1024, bn=1024):
    m, k = x.shape
    n = w.shape[2]
    ndt = wdt.shape[2]
    return pl.pallas_call(
        _in_proj_body,
        grid=(m // bm, n // bn),
        in_specs=[pl.BlockSpec((bm, k), lambda i, j: (i, 0)),
                  pl.BlockSpec((1, k), lambda i, j: (0, 0)),
                  pl.BlockSpec((None, k, bn), lambda i, j: (layer, 0, j)),
                  pl.BlockSpec((None, k, ndt), lambda i, j: (layer, 0, 0))],
        out_specs=[pl.BlockSpec((bm, bn), lambda i, j: (i, j)),
                   pl.BlockSpec((bm, ndt), lambda i, j: (i, 0))],
        out_shape=[jax.ShapeDtypeStruct((m, n), F32), jax.ShapeDtypeStruct((m, ndt), F32)],
        scratch_shapes=[pltpu.VMEM((bm, k), BF16)],
        compiler_params=_cparams(("parallel", "arbitrary")),
        name="in_proj",
    )(x, g, w, wdt)


def _res_mm_body(x_ref, a_ref, w_ref, o_ref):
    o_ref[...] = x_ref[...] + _dot(a_ref[...], w_ref[...])


def res_mm(x, a, w, layer, bm=1024, bn=1024):
    m, n = x.shape
    k = a.shape[1]
    return pl.pallas_call(
        _res_mm_body,
        grid=(m // bm, n // bn),
        in_specs=[pl.BlockSpec((bm, bn), lambda i, j: (i, j)),
                  pl.BlockSpec((bm, k), lambda i, j: (i, 0)),
                  pl.BlockSpec((None, k, bn), lambda i, j: (layer, 0, j))],
        out_specs=pl.BlockSpec((bm, bn), lambda i, j: (i, j)),
        out_shape=jax.ShapeDtypeStruct((m, n), F32),
        compiler_params=_cparams(("parallel", "parallel")),
        name="res_mm",
    )(x, a, w)


def _ffn_body(x_ref, g_ref, w1_ref, w3_ref, w2_ref, gf_ref, o_ref, h_ref, acc_ref, *, final_norm):
    f = pl.program_id(1)

    @pl.when(f == 0)
    def _():
        x = x_ref[...]
        h_ref[...] = (x * _rms_scale(x) * g_ref[...]).astype(BF16)
        acc_ref[...] = jnp.zeros_like(acc_ref)

    h = h_ref[...]
    a = _silu(_dot(h, w1_ref[...])) * _dot(h, w3_ref[...])
    acc_ref[...] += _dot(a.astype(BF16), w2_ref[...])

    @pl.when(f == pl.num_programs(1) - 1)
    def _():
        y = x_ref[...] + 0.5 * acc_ref[...]
        if final_norm:
            y = y * _rms_scale(y) * gf_ref[...]
        o_ref[...] = y


def ffn(x, g, w1, w3, w2, gf, layer, final_norm, bm=512, bf=512):
    m, d = x.shape
    fdim = w1.shape[2]
    return pl.pallas_call(
        functools.partial(_ffn_body, final_norm=final_norm),
        grid=(m // bm, fdim // bf),
        in_specs=[pl.BlockSpec((bm, d), lambda i, f: (i, 0)),
                  pl.BlockSpec((1, d), lambda i, f: (0, 0)),
                  pl.BlockSpec((None, d, bf), lambda i, f: (layer, 0, f)),
                  pl.BlockSpec((None, d, bf), lambda i, f: (layer, 0, f)),
                  pl.BlockSpec((None, bf, d), lambda i, f: (layer, f, 0)),
                  pl.BlockSpec((1, d), lambda i, f: (0, 0))],
        out_specs=pl.BlockSpec((bm, d), lambda i, f: (i, 0)),
        out_shape=jax.ShapeDtypeStruct((m, d), F32),
        scratch_shapes=[pltpu.VMEM((bm, d), BF16), pltpu.VMEM((bm, d), F32)],
        compiler_params=_cparams(("parallel", "arbitrary")),
        name="ffn",
    )(x, g, w1, w3, w2, gf)


def _mix_body(x_ref, ys_ref, gy_ref, o_ref, ga_ref, gb_ref, gc_ref, wssd_ref, wa_ref, wb_ref, whg_ref,
              wmix_ref, out_ref, acc_ref):
    j = pl.program_id(1)

    @pl.when(j == 0)
    def _():
        acc_ref[...] = jnp.zeros_like(acc_ref)

    gy = gy_ref[...].astype(BF16)
    y_a = _dot(ys_ref[...], wssd_ref[...])
    y_b = _dot(gy, wa_ref[...]) * jax.nn.sigmoid(_dot(gy, wb_ref[...]))
    y_c = _dot(o_ref[...], whg_ref[...])
    mix = (jax.nn.sigmoid(ga_ref[...]) * y_a + jax.nn.sigmoid(gb_ref[...]) * y_b
           + jax.nn.sigmoid(gc_ref[...]) * y_c)
    acc_ref[...] += _dot(mix.astype(BF16), wmix_ref[...])

    @pl.when(j == pl.num_programs(1) - 1)
    def _():
        out_ref[...] = x_ref[...] + acc_ref[...]


def branch_mix(x, ys, gy, o, p, wssd, wa, wb, whg, wmix, layer, bm=512, bn=512):
    m, d = x.shape
    nj = d // bn
    return pl.pallas_call(
        _mix_body,
        grid=(m // bm, nj),
        in_specs=[pl.BlockSpec((bm, d), lambda i, j: (i, 0)),
                  pl.BlockSpec((bm, ys.shape[1]), lambda i, j: (i, 0)),
                  pl.BlockSpec((bm, gy.shape[1]), lambda i, j: (i, 0)),
                  pl.BlockSpec((bm, o.shape[1]), lambda i, j: (i, 0)),
                  pl.BlockSpec((bm, bn), lambda i, j: (i, j)),
                  pl.BlockSpec((bm, bn), lambda i, j: (i, nj + j)),
                  pl.BlockSpec((bm, bn), lambda i, j: (i, 2 * nj + j)),
                  pl.BlockSpec((None, wssd.shape[1], bn), lambda i, j: (layer, 0, j)),
                  pl.BlockSpec((None, wa.shape[1], bn), lambda i, j: (layer, 0, j)),
                  pl.BlockSpec((None, wb.shape[1], bn), lambda i, j: (layer, 0, j)),
                  pl.BlockSpec((None, whg.shape[1], bn), lambda i, j: (layer, 0, j)),
                  pl.BlockSpec((None, bn, d), lambda i, j: (layer, j, 0))],
        out_specs=pl.BlockSpec((bm, d), lambda i, j: (i, 0)),
        out_shape=jax.ShapeDtypeStruct((m, d), F32),
        scratch_shapes=[pltpu.VMEM((bm, d), F32)],
        compiler_params=_cparams(("parallel", "arbitrary")),
        name="branch_mix",
    )(x, ys, gy, o, p, p, p, wssd, wa, wb, whg, wmix)


def _attend(q, k_head, v_head):
    outs = []
    for h in range(XA_HEADS):
        sl = slice(h * XA_HEAD_DIM, (h + 1) * XA_HEAD_DIM)
        s = _dot_nt(q[:, sl], k_head(h).astype(BF16)) * (XA_HEAD_DIM ** -0.5)
        s = s - jnp.max(s, axis=-1, keepdims=True)
        e = jnp.exp(s)
        p = e / jnp.sum(e, axis=-1, keepdims=True)
        outs.append(_dot(p.astype(BF16), v_head(h).astype(BF16)))
    return jnp.concatenate(outs, axis=-1)


def _xattn_prompt_body(q_ref, kv_ref, o_ref):
    k_head = lambda h: kv_ref[:, h * XA_HEAD_DIM:(h + 1) * XA_HEAD_DIM]
    v_head = lambda h: kv_ref[:, D_MODEL + h * XA_HEAD_DIM:D_MODEL + (h + 1) * XA_HEAD_DIM]
    o_ref[...] = _attend(q_ref[...], k_head, v_head).astype(BF16)


def xattn_prompt(q, kv, n_seq, seq_len, bl=512):
    nl = seq_len // bl
    return pl.pallas_call(
        _xattn_prompt_body,
        grid=(n_seq, nl),
        in_specs=[pl.BlockSpec((bl, D_MODEL), lambda b, i: (b * nl + i, 0)),
                  pl.BlockSpec((MEM_LEN, 2 * D_MODEL), lambda b, i: (b, 0))],
        out_specs=pl.BlockSpec((bl, D_MODEL), lambda b, i: (b * nl + i, 0)),
        out_shape=jax.ShapeDtypeStruct(q.shape, BF16),
        compiler_params=_cparams(("parallel", "arbitrary")),
        name="xattn_prompt",
    )(q, kv)


def _xattn_sample_body(q_ref, k_ref, v_ref, buf_ref, o_ref, *, seq_len):
    del buf_ref
    q = q_ref[...]
    r = 2 * seq_len
    nc = XA_HEAD_DIM // LANES
    piece = lambda ref, b, h, c: ref[b, pl.ds(c * XA_HEADS + h, MEM_LEN, stride=nc * XA_HEADS), :].astype(BF16)
    scores = []
    for b in range(2):
        for h in range(XA_HEADS):
            s = None
            for c in range(nc):
                lo = h * XA_HEAD_DIM + c * LANES
                d = _dot_nt(q[:, lo:lo + LANES], piece(k_ref, b, h, c))
                s = d if s is None else s + d
            scores.append(s)
    s = jnp.concatenate(scores, axis=0) * (XA_HEAD_DIM ** -0.5)
    s = s - jnp.max(s, axis=-1, keepdims=True)
    e = jnp.exp(s)
    p = (e / jnp.sum(e, axis=-1, keepdims=True)).astype(BF16)
    rows = lax.broadcasted_iota(jnp.int32, (r, LANES), 0)
    for h in range(XA_HEADS):
        for c in range(nc):
            lo = h * XA_HEAD_DIM + c * LANES
            o0 = _dot(p[h * r:(h + 1) * r], piece(v_ref, 0, h, c))
            o1 = _dot(p[(XA_HEADS + h) * r:(XA_HEADS + h + 1) * r], piece(v_ref, 1, h, c))
            o_ref[:, lo:lo + LANES] = jnp.where(rows < seq_len, o0, o1).astype(BF16)


def xattn_sample(q, k, v, buf, layer, row0, n_seq, seq_len):
    r = 2 * seq_len
    blk0 = row0 // r
    mem_spec = pl.BlockSpec((None, 2, MEM_LEN * D_MODEL // LANES, LANES), lambda i: (layer, i, 0, 0))
    return pl.pallas_call(
        functools.partial(_xattn_sample_body, seq_len=seq_len),
        grid=(n_seq // 2,),
        in_specs=[pl.BlockSpec((r, D_MODEL), lambda i: (blk0 + i, 0)), mem_spec, mem_spec,
                  pl.BlockSpec(memory_space=pl.ANY)],
        out_specs=pl.BlockSpec((r, D_MODEL), lambda i: (blk0 + i, 0)),
        out_shape=jax.ShapeDtypeStruct(buf.shape, BF16),
        input_output_aliases={3: 0},
        compiler_params=_cparams(("parallel",)),
        name="xattn_sample",
    )(q, k, v, buf)


LANES = 128
HALO = 8


def _softplus(x):
    return jnp.maximum(x, 0.0) + jnp.log1p(jnp.exp(-jnp.abs(x)))


def _pad_rows(x, rows):
    if x.shape[0] == rows:
        return x
    return jnp.concatenate([x, jnp.zeros((rows - x.shape[0], x.shape[1]), x.dtype)], axis=0)


def _tile_consts(r, t):
    i = np.arange(r)[:, None]
    j = np.arange(r)[None, :]
    same = (i // t) == (j // t)
    lt = (same & (j <= i)).astype(np.float32)
    last = (same & (j % t == t - 1)).astype(np.float32)
    return jnp.asarray(lt, BF16), jnp.asarray(last, BF16)


def _head_expand(n_heads, width, rows=LANES):
    e = np.zeros((rows, n_heads * width), np.float32)
    for h in range(n_heads):
        e[h, h * width:(h + 1) * width] = 1.0
    return jnp.asarray(e, BF16)


def _ssd_tile(xc, dt_raw, dtb, a_log, lt, last, e, e128, r, t):
    xs = xc[:, :SSD_WIDTH]
    dt = _softplus(dt_raw + dtb)
    d_a = dt * (-jnp.exp(a_log))
    a = _sel_dot(lt, d_a, 3)
    a_e = _dot_sel(a, e, 3)
    dt_e = _dot_sel(dt, e, 2)
    if t == r:
        alast_e = jnp.broadcast_to(a_e[r - 1:r, :], a_e.shape)
    else:
        alast_e = _sel_dot(last, a_e, 3)
    a_col = _dot_sel(a, e128, 3)
    a_t = _pad_rows(a, LANES).T

    row = lax.broadcasted_iota(jnp.int32, (r, LANES), 0)
    col = lax.broadcasted_iota(jnp.int32, (r, LANES), 1)
    valid = (col <= row) & (col >= (row // t) * t)
    lane_lo = col < SSD_HEAD_DIM

    xdt = _pad_rows((xs * dt_e).astype(BF16), LANES)
    hpg = SSD_HEADS // SSD_GROUPS
    ys = []
    for g in range(SSD_GROUPS):
        bg = xc[:, SSD_WIDTH + g * SSD_STATE:SSD_WIDTH + (g + 1) * SSD_STATE].astype(BF16)
        cg = xc[:, SSD_WIDTH + (SSD_GROUPS + g) * SSD_STATE:
                SSD_WIDTH + (SSD_GROUPS + g + 1) * SSD_STATE].astype(BF16)
        cb = _dot_nt(cg, _pad_rows(bg, LANES))
        for hp in range(hpg // 2):
            h0 = g * hpg + 2 * hp
            res = []
            for h in (h0, h0 + 1):
                rel = a_col[:, h * LANES:(h + 1) * LANES] - a_t[h:h + 1, :]
                dec = jnp.where(valid, jnp.exp(jnp.where(valid, rel, 0.0)), 0.0)
                res.append(_dot((cb * dec).astype(BF16), xdt[:, h0 * SSD_HEAD_DIM:(h0 + 2) * SSD_HEAD_DIM]))
            ys.append(jnp.where(lane_lo, res[0], res[1]))
    y_intra = jnp.concatenate(ys, axis=-1)
    return xs, y_intra, a_e, dt_e, alast_e


def _ssd_finish(y, xs, z, dsk, ng):
    y = y + dsk * xs
    y = y * _silu(z)
    return (y * _rms_scale(y) * ng).astype(BF16)


def _conv_silu(ext_ref, cw_ref, cb_ref, base, r):
    acc = cb_ref[...] + cw_ref[SSD_CONV - 1:SSD_CONV, :] * ext_ref[pl.ds(base, r), :]
    for k in range(1, SSD_CONV):
        acc = acc + cw_ref[SSD_CONV - 1 - k:SSD_CONV - k, :] * ext_ref[pl.ds(base - k, r), :]
    return _silu(acc)


def _ssd_prompt_body(xbc_ref, z_ref, dt_ref, cw_ref, cb_ref, dtb_ref, alog_ref, dsk_ref, ng_ref,
                     lt_ref, last_ref, e_ref, e128_ref, y_ref, sout_ref, ext_ref, st_ref, *, r):
    c = pl.program_id(1)

    @pl.when(c == 0)
    def _():
        ext_ref[0:HALO, :] = jnp.zeros((HALO, SSD_CONV_DIM), F32)
        st_ref[...] = jnp.zeros_like(st_ref)

    ext_ref[HALO:HALO + r, :] = xbc_ref[...]
    xc = _conv_silu(ext_ref, cw_ref, cb_ref, HALO, r)
    ext_ref[0:HALO, :] = xbc_ref[r - HALO:r, :]

    xs, y, a_e, dt_e, alast_e = _ssd_tile(xc, dt_ref[...], dtb_ref[...], alog_ref[...], lt_ref[...],
                                          last_ref[...], e_ref[...], e128_ref[...], r, r)
    ea_e = jnp.exp(a_e)
    xw = (xs * (dt_e * jnp.exp(alast_e - a_e))).astype(BF16)
    sdec = jnp.exp(alast_e[0:1, :])
    gw = SSD_WIDTH // SSD_GROUPS
    inter = []
    for g in range(SSD_GROUPS):
        bg = xc[:, SSD_WIDTH + g * SSD_STATE:SSD_WIDTH + (g + 1) * SSD_STATE].astype(BF16)
        cg = xc[:, SSD_WIDTH + (SSD_GROUPS + g) * SSD_STATE:
                SSD_WIDTH + (SSD_GROUPS + g + 1) * SSD_STATE].astype(BF16)
        st = st_ref[:, g * gw:(g + 1) * gw]
        inter.append(_dot(cg, st.astype(BF16)))
        st_ref[:, g * gw:(g + 1) * gw] = st * sdec[:, g * gw:(g + 1) * gw] + _dot_tn(bg, xw[:, g * gw:(g + 1) * gw])
    y = y + jnp.concatenate(inter, axis=-1) * ea_e
    y_ref[...] = _ssd_finish(y, xs, z_ref[...], dsk_ref[...], ng_ref[...])

    @pl.when(c == pl.num_programs(1) - 1)
    def _():
        sout_ref[0] = st_ref[...].T


def _ssd_consts(r, t):
    lt, last = _tile_consts(r, t)
    return lt, last, _head_expand(SSD_HEADS, SSD_HEAD_DIM), _head_expand(SSD_HEADS, LANES)


def ssd_prompt(p, dt, prm, n_seq, seq_len, r=128):
    nc = seq_len // r
    consts = _ssd_consts(r, r)
    full = lambda a: pl.BlockSpec(a.shape, lambda b, c: (0,) * a.ndim)
    small = [prm["conv_w"], prm["conv_b"], prm["dt_bias"], prm["a_log"], prm["d_skip"], prm["norm"], *consts]
    return pl.pallas_call(
        functools.partial(_ssd_prompt_body, r=r),
        grid=(n_seq, nc),
        in_specs=[pl.BlockSpec((r, SSD_CONV_DIM), lambda b, c: (b * nc + c, P_XBC // SSD_CONV_DIM)),
                  pl.BlockSpec((r, SSD_WIDTH), lambda b, c: (b * nc + c, P_Z // SSD_WIDTH)),
                  pl.BlockSpec((r, LANES), lambda b, c: (b * nc + c, 0)),
                  *[full(a) for a in small]],
        out_specs=[pl.BlockSpec((r, SSD_WIDTH), lambda b, c: (b * nc + c, 0)),
                   pl.BlockSpec((1, SSD_WIDTH, SSD_STATE), lambda b, c: (b, 0, 0))],
        out_shape=[jax.ShapeDtypeStruct((p.shape[0], SSD_WIDTH), BF16),
                   jax.ShapeDtypeStruct((n_seq, SSD_WIDTH, SSD_STATE), F32)],
        scratch_shapes=[pltpu.VMEM((HALO + r, SSD_CONV_DIM), F32), pltpu.VMEM((SSD_STATE, SSD_WIDTH), F32)],
        compiler_params=_cparams(("parallel", "arbitrary")),
        name="ssd_prompt",
    )(p, p, dt, *small)


def _ssd_sample_body(xbc_ref, z_ref, dt_ref, buf_ref, s0_ref, cw_ref, cb_ref, dtb_ref, alog_ref, dsk_ref,
                     ng_ref, lt_ref, last_ref, e_ref, e128_ref, *rest, r, t):
    y_ref, sout_ref, ext_ref = rest[-3:]
    nb = r // t
    pitch = HALO + t
    for b in range(nb):
        ext_ref[b * pitch + HALO - (SSD_CONV - 1):b * pitch + HALO, :] = buf_ref[b]
        ext_ref[b * pitch + HALO:(b + 1) * pitch, :] = xbc_ref[b * t:(b + 1) * t, :]
    xc = jnp.concatenate([_conv_silu(ext_ref, cw_ref, cb_ref, b * pitch + HALO, t) for b in range(nb)], axis=0)

    xs, y, a_e, dt_e, alast_e = _ssd_tile(xc, dt_ref[...], dtb_ref[...], alog_ref[...], lt_ref[...],
                                          last_ref[...], e_ref[...], e128_ref[...], r, t)
    ea_e = jnp.exp(a_e)
    xw = xs * (dt_e * jnp.exp(alast_e - a_e))
    sdec = jnp.exp(alast_e)
    gw = SSD_WIDTH // SSD_GROUPS
    pr = 2 * t
    prow = lax.broadcasted_iota(jnp.int32, (pr, 1), 0)
    ones = jnp.ones((pr, SSD_STATE), BF16)
    inter = []
    for g in range(SSD_GROUPS):
        bg = xc[:, SSD_WIDTH + g * SSD_STATE:SSD_WIDTH + (g + 1) * SSD_STATE].astype(BF16)
        cg = xc[:, SSD_WIDTH + (SSD_GROUPS + g) * SSD_STATE:
                SSD_WIDTH + (SSD_GROUPS + g + 1) * SSD_STATE].astype(BF16)
        cols = slice(g * gw, (g + 1) * gw)
        rows_out = []
        for q in range(nb // 2):
            rs = slice(q * pr, (q + 1) * pr)
            acc = None
            for s in range(2):
                b = 2 * q + s
                mine = (prow >= s * t) & (prow < (s + 1) * t)
                s0 = s0_ref[b, cols, :]
                yi = _dot_nt(cg[rs], s0.astype(BF16))
                acc = jnp.where(mine, yi, 0.0) if acc is None else acc + jnp.where(mine, yi, 0.0)
                upd = _dot_tn(jnp.where(mine, xw[rs, cols], 0.0).astype(BF16), bg[rs])
                lastrow = prow == (s + 1) * t - 1
                dh, dl = _split2(jnp.where(lastrow, sdec[rs, cols], 0.0))
                dcol = _dot_tn(dh, ones) + _dot_tn(dl, ones)
                sout_ref[b, cols, :] = s0 * dcol + upd
            rows_out.append(acc)
        inter.append(jnp.concatenate(rows_out, axis=0))
    y = y + jnp.concatenate(inter, axis=-1) * ea_e
    y_ref[...] = _ssd_finish(y, xs, z_ref[...], dsk_ref[...], ng_ref[...])


def ssd_sample(p, dt, conv_buf, s0, ybuf, sprev, prm, layer, row0, n_seq, seq_len, r=64):
    nb = r // seq_len
    blk0 = row0 // r
    consts = _ssd_consts(r, seq_len)
    full = lambda a: pl.BlockSpec(a.shape, lambda i: (0,) * a.ndim)
    small = [prm["conv_w"], prm["conv_b"], prm["dt_bias"], prm["a_log"], prm["d_skip"], prm["norm"], *consts]
    inplace = [ybuf] if sprev is None else [ybuf, sprev]
    n_in = 5 + len(small)
    return pl.pallas_call(
        functools.partial(_ssd_sample_body, r=r, t=seq_len),
        grid=(n_seq // nb,),
        in_specs=[pl.BlockSpec((r, SSD_CONV_DIM), lambda i: (blk0 + i, P_XBC // SSD_CONV_DIM)),
                  pl.BlockSpec((r, SSD_WIDTH), lambda i: (blk0 + i, P_Z // SSD_WIDTH)),
                  pl.BlockSpec((r, LANES), lambda i: (blk0 + i, 0)),
                  pl.BlockSpec((None, nb, SSD_CONV - 1, SSD_CONV_DIM), lambda i: (layer, i, 0, 0)),
                  pl.BlockSpec((None, nb, SSD_WIDTH, SSD_STATE), lambda i: (layer, i, 0, 0)),
                  *[full(a) for a in small],
                  *[pl.BlockSpec(memory_space=pl.ANY) for _ in inplace]],
        out_specs=[pl.BlockSpec((r, SSD_WIDTH), lambda i: (blk0 + i, 0)),
                   pl.BlockSpec((None, nb, SSD_WIDTH, SSD_STATE), lambda i: (layer, i, 0, 0))],
        out_shape=[jax.ShapeDtypeStruct(ybuf.shape, BF16),
                   jax.ShapeDtypeStruct(s0.shape, F32)],
        input_output_aliases={n_in + k: k for k in range(len(inplace))},
        scratch_shapes=[pltpu.VMEM((nb * (HALO + seq_len), SSD_CONV_DIM), F32)],
        compiler_params=_cparams(("parallel",)),
        name="ssd_sample",
    )(p, p, dt, conv_buf, s0, *small, *inplace)


S5_CH = S5_GROUPS * S5_STATE
S5_BLK = 4
S5_SSEQ = 32


def _s5_in(u, bre_ref, bim_ref):
    ub = u.astype(BF16)
    kin = S5_WIDTH // S5_BLK
    re = [_dot(ub[:, q * kin:(q + 1) * kin], bre_ref[q]) for q in range(S5_BLK)]
    im = [_dot(ub[:, q * kin:(q + 1) * kin], bim_ref[q]) for q in range(S5_BLK)]
    return jnp.concatenate(re, axis=-1), jnp.concatenate(im, axis=-1)


def _s5_out(h_re, h_im, u, cre_ref, cim_ref, dsk):
    kst = S5_CH // S5_BLK
    hr = h_re.astype(BF16)
    hi = h_im.astype(BF16)
    y = [_dot(hr[:, q * kst:(q + 1) * kst], cre_ref[q]) - _dot(hi[:, q * kst:(q + 1) * kst], cim_ref[q])
         for q in range(S5_BLK)]
    y = jnp.concatenate(y, axis=-1) + dsk * u
    return jax.nn.gelu(y).astype(BF16)


def _s5_scan_body(u_ref, h0re_ref, h0im_ref, are_ref, aim_ref, bre_ref, bim_ref, cre_ref, cim_ref, dsk_ref,
                  y_ref, hre_ref, him_ref, sre_ref, sim_ref, *, s, tc):
    c = pl.program_id(1)

    @pl.when(c == 0)
    def _():
        sre_ref[0:s, :] = h0re_ref[...]
        sim_ref[0:s, :] = h0im_ref[...]

    u = u_ref[...]
    bu_re, bu_im = _s5_in(u, bre_ref, bim_ref)
    sre_ref[s:, :] = bu_re
    sim_ref[s:, :] = bu_im
    a_re = jnp.broadcast_to(are_ref[...], (s, S5_CH))
    a_im = jnp.broadcast_to(aim_ref[...], (s, S5_CH))

    def step(t, carry):
        prev = pl.ds(pl.multiple_of(t * s, s), s)
        cur = pl.ds(pl.multiple_of((t + 1) * s, s), s)
        h_re = sre_ref[prev, :]
        h_im = sim_ref[prev, :]
        sre_ref[cur, :] = a_re * h_re - a_im * h_im + sre_ref[cur, :]
        sim_ref[cur, :] = a_re * h_im + a_im * h_re + sim_ref[cur, :]
        return carry

    lax.fori_loop(0, tc, step, 0, unroll=4)
    h_re = sre_ref[tc * s:, :]
    h_im = sim_ref[tc * s:, :]
    y_ref[...] = _s5_out(sre_ref[s:, :], sim_ref[s:, :], u, cre_ref, cim_ref, dsk_ref[...])
    sre_ref[0:s, :] = h_re
    sim_ref[0:s, :] = h_im
    hre_ref[...] = h_re
    him_ref[...] = h_im


def s5_scan(u, h0_re, h0_im, prm, n_blocks, s, n_steps, tc):
    nc = n_steps // tc
    r = tc * s
    full = lambda a: pl.BlockSpec(a.shape, lambda b, c: (0,) * a.ndim)
    small = [prm["abar_re"], prm["abar_im"], prm["b_re"], prm["b_im"], prm["c_re"], prm["c_im"], prm["d_skip"]]
    state = jax.ShapeDtypeStruct((n_blocks * s, S5_CH), F32)
    sspec = pl.BlockSpec((s, S5_CH), lambda b, c: (b, 0))
    return pl.pallas_call(
        functools.partial(_s5_scan_body, s=s, tc=tc),
        grid=(n_blocks, nc),
        in_specs=[pl.BlockSpec((r, S5_WIDTH), lambda b, c: (b * nc + c, 0)), sspec, sspec,
                  *[full(a) for a in small]],
        out_specs=[pl.BlockSpec((r, S5_WIDTH), lambda b, c: (b * nc + c, 0)), sspec, sspec],
        out_shape=[jax.ShapeDtypeStruct((u.shape[0], S5_WIDTH), BF16), state, state],
        scratch_shapes=[pltpu.VMEM((s + r, S5_CH), F32), pltpu.VMEM((s + r, S5_CH), F32)],
        compiler_params=_cparams(("parallel", "arbitrary")),
        name="s5_scan",
    )(u, h0_re, h0_im, *small)


S5_T = 16
S5_GB = LANES // S5_GROUP_SIZE
S5_CL = S5_T * LANES
S5_SL = S5_GB * S5_STATE
S5_NB = 2
S5_PITCH_PAD = 8


def _block_diag_rows(x, rows_w, cols_w):
    xt = jnp.concatenate([x] * S5_GB, axis=0)
    r = lax.broadcasted_iota(jnp.int32, xt.shape, 0) // rows_w
    c = lax.broadcasted_iota(jnp.int32, xt.shape, 1) // cols_w
    return jnp.where(r == c, xt, jnp.zeros_like(xt))


def _s5_chunk_body(u_ref, kj_ref, bcre_ref, bcim_ref, ccre_ref, ccim_ref, are_ref, aim_ref, dsk_ref,
                   y_ref, hre_ref, him_ref, vre_ref, vim_ref, sre_ref, sim_ref,
                   tp_ref, bre_ref, bim_ref, cre_ref, cim_ref, *, n_chunks):
    ks = S5_GROUP_SIZE

    @pl.when(pl.program_id(1) == 0)
    def _():
        zero = jnp.zeros((LANES, LANES), BF16)
        for tau in range(S5_T):
            piece = _block_diag_rows(kj_ref[0, tau], ks, ks)
            for s in range(S5_T - tau):
                tp_ref[s * LANES:(s + 1) * LANES, (s + tau) * LANES:(s + tau + 1) * LANES] = piece
        for s in range(1, S5_T):
            for t in range(s):
                tp_ref[s * LANES:(s + 1) * LANES, t * LANES:(t + 1) * LANES] = zero
        for s in range(S5_T):
            bre_ref[s * LANES:(s + 1) * LANES, :] = _block_diag_rows(bcre_ref[0, s], ks, S5_STATE)
            bim_ref[s * LANES:(s + 1) * LANES, :] = _block_diag_rows(bcim_ref[0, s], ks, S5_STATE)
            cre_ref[:, s * LANES:(s + 1) * LANES] = _block_diag_rows(ccre_ref[0, s], S5_STATE, ks)
            cim_ref[:, s * LANES:(s + 1) * LANES] = _block_diag_rows(ccim_ref[0, s], S5_STATE, ks)

    rows = S5_NB * n_chunks
    pitch = n_chunks + S5_PITCH_PAD
    nk = S5_SL // LANES
    u = jnp.concatenate([u_ref[pl.ds(t, rows, stride=S5_T), :] for t in range(S5_T)], axis=-1)
    ub = u.astype(BF16)
    v_re = _dot(ub, bre_ref[...])
    v_im = _dot(ub, bim_ref[...])
    for k in range(nk):
        vre_ref[k] = v_re[:, k * LANES:(k + 1) * LANES]
        vim_ref[k] = v_im[:, k * LANES:(k + 1) * LANES]
        sre_ref[k] = jnp.zeros((S5_NB * pitch, LANES), F32)
        sim_ref[k] = jnp.zeros((S5_NB * pitch, LANES), F32)
    a_re = [jnp.broadcast_to(are_ref[0, :, k * LANES:(k + 1) * LANES], (S5_NB, LANES)) for k in range(nk)]
    a_im = [jnp.broadcast_to(aim_ref[0, :, k * LANES:(k + 1) * LANES], (S5_NB, LANES)) for k in range(nk)]

    def step(c, carry):
        for k in range(nk):
            h_re = sre_ref[k, pl.ds(c, S5_NB, stride=pitch), :]
            h_im = sim_ref[k, pl.ds(c, S5_NB, stride=pitch), :]
            x_re = vre_ref[k, pl.ds(c, S5_NB, stride=n_chunks), :]
            x_im = vim_ref[k, pl.ds(c, S5_NB, stride=n_chunks), :]
            sre_ref[k, pl.ds(c + 1, S5_NB, stride=pitch), :] = a_re[k] * h_re - a_im[k] * h_im + x_re
            sim_ref[k, pl.ds(c + 1, S5_NB, stride=pitch), :] = a_re[k] * h_im + a_im[k] * h_re + x_im
        return carry

    lax.fori_loop(0, n_chunks, step, 0, unroll=2)
    starts = lambda ref: jnp.concatenate(
        [jnp.concatenate([ref[k, b * pitch:b * pitch + n_chunks, :] for b in range(S5_NB)], axis=0)
         for k in range(nk)], axis=-1)
    y = _dot(ub, tp_ref[...])
    y = y + _dot(starts(sre_ref).astype(BF16), cre_ref[...]) - _dot(starts(sim_ref).astype(BF16), cim_ref[...])
    y = jax.nn.gelu(y + dsk_ref[0] * u)
    for t in range(S5_T):
        y_ref[pl.ds(t, rows, stride=S5_T), :] = y[:, t * LANES:(t + 1) * LANES]
    final = lambda ref: jnp.concatenate(
        [jnp.concatenate([ref[k, b * pitch + n_chunks:b * pitch + n_chunks + 1, :] for b in range(S5_NB)], axis=0)
         for k in range(nk)], axis=-1)
    hre_ref[...] = final(sre_ref)
    him_ref[...] = final(sim_ref)


def s5_chunked(p, prm, n_seq, seq_len):
    n_chunks = seq_len // S5_T
    rows = S5_NB * n_chunks
    nblk = S5_GROUPS // S5_GB
    nhalf = n_seq // S5_NB
    pitch = n_chunks + S5_PITCH_PAD
    nk = S5_SL // LANES
    blk = lambda a: pl.BlockSpec((1,) + a.shape[1:], lambda j, i: (j,) + (0,) * (a.ndim - 1))
    small = [prm["kern_j"], prm["bst_re"], prm["bst_im"], prm["cst_re"], prm["cst_im"], prm["apow_re"],
             prm["apow_im"], prm["d_tiled"]]
    state = jax.ShapeDtypeStruct((nhalf, S5_NB, S5_CH), F32)
    sspec = pl.BlockSpec((None, S5_NB, S5_SL), lambda j, i: (i, 0, j))
    return pl.pallas_call(
        functools.partial(_s5_chunk_body, n_chunks=n_chunks),
        grid=(nblk, nhalf),
        in_specs=[pl.BlockSpec((S5_NB * seq_len, LANES), lambda j, i: (i, P_U // LANES + j)),
                  *[blk(a) for a in small]],
        out_specs=[pl.BlockSpec((S5_NB * seq_len, LANES), lambda j, i: (i, j)), sspec, sspec],
        out_shape=[jax.ShapeDtypeStruct((p.shape[0], S5_WIDTH), F32), state, state],
        scratch_shapes=[pltpu.VMEM((nk, rows, LANES), F32), pltpu.VMEM((nk, rows, LANES), F32),
                        pltpu.VMEM((nk, S5_NB * pitch, LANES), F32), pltpu.VMEM((nk, S5_NB * pitch, LANES), F32),
                        pltpu.VMEM((S5_CL, S5_CL), BF16), pltpu.VMEM((S5_CL, S5_SL), BF16),
                        pltpu.VMEM((S5_CL, S5_SL), BF16), pltpu.VMEM((S5_SL, S5_CL), BF16),
                        pltpu.VMEM((S5_SL, S5_CL), BF16)],
        compiler_params=_cparams(("parallel", "arbitrary")),
        name="s5_chunked",
    )(p, *small)


def _hg_consts(r, t):
    i = np.arange(r)[:, None]
    j = np.arange(r)[None, :]
    sums, upper, pair = [], [], []
    s = 1
    while s < t:
        blk_i, blk_j = i // (2 * s), j // (2 * s)
        up_i = (i % (2 * s)) >= s
        mid_i = blk_i * 2 * s + s
        m_up = up_i & (j >= mid_i) & (j <= i)
        m_lo = (~up_i) & (j > i) & (j < mid_i)
        sums.append((m_up | m_lo).astype(np.float32))
        upper.append(np.broadcast_to(up_i, (r, 1)).astype(np.float32))
        pair.append(((blk_i == blk_j) & up_i & ((j % (2 * s)) < s)).astype(np.float32))
        s *= 2
    pair.append((i == j).astype(np.float32))
    return (jnp.asarray(np.stack(sums), BF16), jnp.asarray(np.stack(upper), F32),
            jnp.asarray(np.stack(pair), F32))


def _hg_gates(hf, lb):
    logf = -_softplus(-hf) + jnp.log1p(lb * jnp.exp(-hf))
    kk = (1.0 - lb) * jax.nn.sigmoid(-hf)
    return logf, kk


def _hg_intra(q, kk, v, logf, b, sums_ref, upper_ref, pair_ref):
    r = q.shape[0]
    nlev = sums_ref.shape[0]
    lf3 = _split3(logf)
    qb = q.astype(BF16)
    kb = kk.astype(BF16)
    vb = _pad_rows(v.astype(BF16), LANES) if r < LANES else v.astype(BF16)
    scores = [None] * HG_HEADS
    for lev in range(nlev + 1):
        if lev < nlev:
            upper = upper_ref[lev] > 0.5
            half = 2 ** lev
            if half >= HALO:
                blocks = b.reshape(r // (2 * half), 2 * half, HG_WIDTH)
                ref = jnp.broadcast_to(blocks[:, half - 1:half, :], blocks.shape).reshape(r, HG_WIDTH)
                d = jnp.where(upper, b - ref, ref - b)
            else:
                m = sums_ref[lev]
                d = _dot(m, lf3[0]) + _dot(m, lf3[1]) + _dot(m, lf3[2])
            x = (jnp.where(upper, q, kk) * jnp.exp(d)).astype(BF16)
            xq, xk = x, x
        else:
            xq, xk = qb, kb
        mask = pair_ref[lev]
        for h in range(HG_HEADS):
            sl = slice(h * HG_KEY_DIM, (h + 1) * HG_KEY_DIM)
            sc = _dot_nt(xq[:, sl], xk[:, sl]) * mask
            scores[h] = sc if scores[h] is None else scores[h] + sc
    outs = [_dot(scores[h].astype(BF16), vb[:, h * HG_VAL_DIM:(h + 1) * HG_VAL_DIM]) for h in range(HG_HEADS)]
    return jnp.concatenate(outs, axis=-1)


def _hg_finish(o, hgate, ng):
    outs = []
    for h in range(HG_HEADS):
        oh = o[:, h * HG_VAL_DIM:(h + 1) * HG_VAL_DIM]
        outs.append(oh * _rms_scale(oh) * ng)
    return (jnp.concatenate(outs, axis=-1) * _silu(hgate)).astype(BF16)


def _hg_prompt_body(q_ref, f_ref, i_ref, gate_ref, lb_ref, ng_ref, lt_ref, last_ref, sums_ref, upper_ref,
                    pair_ref, o_ref, sout_ref, *, r):
    c = pl.program_id(1)

    @pl.when(c == 0)
    def _():
        sout_ref[...] = jnp.zeros_like(sout_ref)

    q = q_ref[...]
    v = i_ref[...]
    logf, kk = _hg_gates(f_ref[...], lb_ref[...])
    b = _sel_dot(lt_ref[...], logf, 3)
    o = _hg_intra(q, kk, v, logf, b, sums_ref, upper_ref, pair_ref)
    blast = b[r - 1:r, :]
    qe = (q * jnp.exp(b)).astype(BF16)
    kw = (kk * jnp.exp(blast - b)).astype(BF16)
    vb = v.astype(BF16)
    ones = jnp.ones((r, HG_VAL_DIM), BF16)
    rows = lax.broadcasted_iota(jnp.int32, (r, 1), 0)
    dh, dl = _split2(jnp.where(rows == r - 1, jnp.exp(b), 0.0))
    inter = []
    for h in range(HG_HEADS):
        sl = slice(h * HG_KEY_DIM, (h + 1) * HG_KEY_DIM)
        s = sout_ref[0, h]
        inter.append(_dot(qe[:, sl], s.astype(BF16)))
        dcol = _dot_tn(dh[:, sl], ones) + _dot_tn(dl[:, sl], ones)
        sout_ref[0, h] = s * dcol + _dot_tn(kw[:, sl], vb[:, sl])
    o = o + jnp.concatenate(inter, axis=-1)
    o_ref[...] = _hg_finish(o, gate_ref[...], ng_ref[...])


def hg_prompt(p, prm, n_seq, seq_len, r=128):
    nc = seq_len // r
    lt, last = _tile_consts(r, r)
    consts = [lt, last, *_hg_consts(r, r)]
    full = lambda a: pl.BlockSpec(a.shape, lambda b, c: (0,) * a.ndim)
    small = [prm["lb"], prm["norm"], *consts]
    col = lambda off: pl.BlockSpec((r, HG_WIDTH), lambda b, c: (b * nc + c, off // HG_WIDTH))
    return pl.pallas_call(
        functools.partial(_hg_prompt_body, r=r),
        grid=(n_seq, nc),
        in_specs=[col(P_HQ), col(P_HF), col(P_HI), col(P_HGATE), *[full(a) for a in small]],
        out_specs=[pl.BlockSpec((r, HG_WIDTH), lambda b, c: (b * nc + c, 0)),
                   pl.BlockSpec((1, HG_HEADS, HG_KEY_DIM, HG_VAL_DIM), lambda b, c: (b, 0, 0, 0))],
        out_shape=[jax.ShapeDtypeStruct((p.shape[0], HG_WIDTH), BF16),
                   jax.ShapeDtypeStruct((n_seq, HG_HEADS, HG_KEY_DIM, HG_VAL_DIM), F32)],
        compiler_params=_cparams(("parallel", "arbitrary")),
        name="hg_prompt",
    )(p, p, p, p, *small)


def _hg_sample_body(q_ref, f_ref, i_ref, gate_ref, s0_ref, lb_ref, ng_ref, lt_ref, last_ref, sums_ref,
                    upper_ref, pair_ref, *rest, r, t):
    o_ref, sout_ref = rest[-2:]
    q = q_ref[...]
    v = i_ref[...]
    logf, kk = _hg_gates(f_ref[...], lb_ref[...])
    b = _sel_dot(lt_ref[...], logf, 3)
    o = _hg_intra(q, kk, v, logf, b, sums_ref, upper_ref, pair_ref)
    blast = _sel_dot(last_ref[...], b, 3)
    qe = (q * jnp.exp(b)).astype(BF16)
    kw = kk * jnp.exp(blast - b)
    sdec = jnp.exp(blast)
    vb = v.astype(BF16)
    pr = 2 * t
    prow = lax.broadcasted_iota(jnp.int32, (pr, 1), 0)
    ones = jnp.ones((pr, HG_VAL_DIM), BF16)
    rows_out = []
    for p2 in range(r // pr):
        rs = slice(p2 * pr, (p2 + 1) * pr)
        heads = []
        for h in range(HG_HEADS):
            sl = slice(h * HG_KEY_DIM, (h + 1) * HG_KEY_DIM)
            acc = None
            for s in range(2):
                bi = 2 * p2 + s
                mine = (prow >= s * t) & (prow < (s + 1) * t)
                s0 = s0_ref[bi, h]
                oi = jnp.where(mine, _dot(qe[rs, sl], s0.astype(BF16)), 0.0)
                acc = oi if acc is None else acc + oi
                upd = _dot_tn(jnp.where(mine, kw[rs, sl], 0.0).astype(BF16), vb[rs, sl])
                dh, dl = _split2(jnp.where(prow == (s + 1) * t - 1, sdec[rs, sl], 0.0))
                dcol = _dot_tn(dh, ones) + _dot_tn(dl, ones)
                sout_ref[bi, h] = s0 * dcol + upd
            heads.append(acc)
        rows_out.append(jnp.concatenate(heads, axis=-1))
    o = o + jnp.concatenate(rows_out, axis=0)
    o_ref[...] = _hg_finish(o, gate_ref[...], ng_ref[...])


def hg_sample(p, s0, obuf, sprev, prm, layer, row0, n_seq, seq_len, r=128):
    nb = r // seq_len
    blk0 = row0 // r
    lt, last = _tile_consts(r, seq_len)
    consts = [lt, last, *_hg_consts(r, seq_len)]
    full = lambda a: pl.BlockSpec(a.shape, lambda i: (0,) * a.ndim)
    small = [prm["lb"], prm["norm"], *consts]
    col = lambda off: pl.BlockSpec((r, HG_WIDTH), lambda i: (blk0 + i, off // HG_WIDTH))
    sspec = pl.BlockSpec((None, nb, HG_HEADS, HG_KEY_DIM, HG_VAL_DIM), lambda i: (layer, i, 0, 0, 0))
    inplace = [obuf] if sprev is None else [obuf, sprev]
    n_in = 5 + len(small)
    return pl.pallas_call(
        functools.partial(_hg_sample_body, r=r, t=seq_len),
        grid=(n_seq // nb,),
        in_specs=[col(P_HQ), col(P_HF), col(P_HI), col(P_HGATE), sspec, *[full(a) for a in small],
                  *[pl.BlockSpec(memory_space=pl.ANY) for _ in inplace]],
        out_specs=[pl.BlockSpec((r, HG_WIDTH), lambda i: (blk0 + i, 0)), sspec],
        out_shape=[jax.ShapeDtypeStruct(obuf.shape, BF16), jax.ShapeDtypeStruct(s0.shape, F32)],
        input_output_aliases={n_in + k: k for k in range(len(inplace))},
        compiler_params=_cparams(("parallel",)),
        name="hg_sample",
    )(p, p, p, p, s0, *small, *inplace)


def _s5_params(a_re, a_im, log_dt, b_re, b_im, c_re, c_im, d_skip):
    dt = jnp.exp(log_dt)[:, None]
    mag = jnp.exp(a_re * dt)
    abar_re = mag * jnp.cos(a_im * dt)
    abar_im = mag * jnp.sin(a_im * dt)
    den = a_re * a_re + a_im * a_im
    nr = abar_re - 1.0
    coef_re = (nr * a_re + abar_im * a_im) / den
    coef_im = (abar_im * a_re - nr * a_im) / den
    bbar_re = coef_re[..., None] * b_re - coef_im[..., None] * b_im
    bbar_im = coef_re[..., None] * b_im + coef_im[..., None] * b_re
    gpb = S5_GROUPS // S5_BLK
    eye = jnp.eye(gpb, dtype=F32)

    def in_blocks(bbar):
        bb = bbar.reshape(S5_BLK, gpb, S5_STATE, S5_GROUP_SIZE)
        return jnp.einsum("qgnk,gh->qgkhn", bb, eye).reshape(
            S5_BLK, gpb * S5_GROUP_SIZE, gpb * S5_STATE).astype(BF16)

    def out_blocks(c):
        cc = c.reshape(S5_BLK, gpb, S5_GROUP_SIZE, S5_STATE)
        return jnp.einsum("qgkn,gh->qgnhk", cc, eye).reshape(
            S5_BLK, gpb * S5_STATE, gpb * S5_GROUP_SIZE).astype(BF16)

    prm = {
        "abar_re": abar_re.reshape(1, S5_CH), "abar_im": abar_im.reshape(1, S5_CH),
        "b_re": in_blocks(bbar_re), "b_im": in_blocks(bbar_im),
        "c_re": out_blocks(c_re), "c_im": out_blocks(c_im),
        "d_skip": d_skip.reshape(1, S5_WIDTH),
    }

    tau = jnp.arange(S5_T + 1, dtype=F32)[:, None, None]
    pmag = jnp.exp(tau * (a_re * dt))
    pw_re = pmag * jnp.cos(tau * (a_im * dt))
    pw_im = pmag * jnp.sin(tau * (a_im * dt))
    cp_re = c_re[None] * pw_re[:, :, None, :] - c_im[None] * pw_im[:, :, None, :]
    cp_im = c_re[None] * pw_im[:, :, None, :] + c_im[None] * pw_re[:, :, None, :]
    bt_re = bbar_re.transpose(0, 2, 1)[None, :, None]
    bt_im = bbar_im.transpose(0, 2, 1)[None, :, None]
    kern = jnp.sum(cp_re[:, :, :, None, :] * bt_re - cp_im[:, :, :, None, :] * bt_im, axis=-1)
    nq = S5_GROUPS // S5_GB
    ks = S5_GROUP_SIZE
    kern_j = (kern[:S5_T].reshape(S5_T, nq, S5_GB, ks, ks).transpose(1, 0, 4, 2, 3)
              .reshape(nq, S5_T, ks, LANES).astype(BF16))
    rev = pw_re[S5_T - 1 - jnp.arange(S5_T)], pw_im[S5_T - 1 - jnp.arange(S5_T)]
    bst_re = rev[0][..., None] * bbar_re[None] - rev[1][..., None] * bbar_im[None]
    bst_im = rev[0][..., None] * bbar_im[None] + rev[1][..., None] * bbar_re[None]

    def bst_blocks(b):
        return (b.reshape(S5_T, nq, S5_GB, S5_STATE, ks).transpose(1, 0, 4, 2, 3)
                .reshape(nq, S5_T, ks, S5_SL).astype(BF16))

    def cst_blocks(c):
        return (c.reshape(S5_T, nq, S5_GB, ks, S5_STATE).transpose(1, 0, 4, 2, 3)
                .reshape(nq, S5_T, S5_STATE, LANES).astype(BF16))

    prm.update({
        "kern_j": kern_j,
        "bst_re": bst_blocks(bst_re), "bst_im": bst_blocks(bst_im),
        "cst_re": cst_blocks(cp_re[1:]), "cst_im": cst_blocks(cp_im[1:]),
        "apow_re": pw_re[S5_T].reshape(nq, 1, S5_SL), "apow_im": pw_im[S5_T].reshape(nq, 1, S5_SL),
        "d_tiled": jnp.broadcast_to(d_skip.reshape(nq, 1, 1, LANES), (nq, 1, S5_T, LANES)).reshape(nq, 1, S5_CL),
    })
    return prm


def _mem_rows(cache):
    dd, b, m, h, hd = cache.shape
    c = cache.reshape(dd, b, m, h, hd // LANES, LANES).transpose(0, 1, 2, 4, 3, 5)
    return c.reshape(dd, b, m * h * (hd // LANES), LANES)


def _pad_lanes(v):
    return jnp.pad(v.reshape(1, -1), ((0, 0), (0, LANES - v.shape[-1])))


def kernel(x_prompt, x_sample, cache_mem_k, cache_mem_v, state_ssd, state_ssd_conv, state_s5_re, state_s5_im,
           state_hgrn, mem_prompt, norm_ffn1, ffn1_w1, ffn1_w3, ffn1_w2, norm_mix, w_in, ssd_conv_w, ssd_conv_b,
           ssd_dt_bias, ssd_a_log, ssd_d, ssd_norm, ssd_w_out, s5_a_re, s5_a_im, s5_log_dt, s5_b_re, s5_b_im,
           s5_c_re, s5_c_im, s5_d, s5_w_glu_a, s5_w_glu_b, hg_lower_bounds, hg_norm, hg_w_out, w_mix_out,
           norm_xa, norm_mem, xa_wq, xa_wk, xa_wv, xa_wo, norm_ffn2, ffn2_w1, ffn2_w3, ffn2_w2, norm_final):
    bp, lp, d = x_prompt.shape
    bs, ls, _ = x_sample.shape
    mp, ms = bp * lp, bs * ls
    x = jnp.concatenate([x_prompt.reshape(mp, d), x_sample.reshape(ms, d)], axis=0)
    mem = mem_prompt.reshape(bp * MEM_LEN, d)
    row = lambda v: v.reshape(1, -1)
    bf = lambda w: w.astype(BF16)

    lb_p = jax.nn.softmax(hg_lower_bounds, axis=0)
    lb_all = jnp.cumsum(lb_p, axis=0) - lb_p[0]

    ssd_s0 = state_ssd.reshape(DEPTH, bs, SSD_WIDTH, SSD_STATE)
    mem_k = _mem_rows(cache_mem_k)
    mem_v = _mem_rows(cache_mem_v)
    ssd_states = None
    hg_states = None

    f1_w1, f1_w3, f1_w2 = bf(ffn1_w1), bf(ffn1_w3), bf(ffn1_w2)
    f2_w1, f2_w3, f2_w2 = bf(ffn2_w1), bf(ffn2_w3), bf(ffn2_w2)
    w_p, w_dt = in_proj_weights(w_in)
    b_ssd, b_glu_a, b_glu_b, b_hg, b_mix = bf(ssd_w_out), bf(s5_w_glu_a), bf(s5_w_glu_b), bf(hg_w_out), bf(w_mix_out)
    b_wq, b_wo = bf(xa_wq), bf(xa_wo)
    b_wkv = bf(jnp.concatenate([xa_wk, xa_wv], axis=-1))

    outs = {k: [] for k in ("pk", "pv", "pss", "pcv", "psr", "psi", "phg", "scv", "ssr", "ssi")}
    for l in range(DEPTH):
        x = ffn(x, row(norm_ffn1[l]), f1_w1, f1_w3, f1_w2, row(norm_final), l, False)

        p, dt = in_proj(x, row(norm_mix[l]), w_p, w_dt, l)

        ssd_prm = {"conv_w": ssd_conv_w[l], "conv_b": row(ssd_conv_b[l]), "dt_bias": _pad_lanes(ssd_dt_bias[l]),
                   "a_log": _pad_lanes(ssd_a_log[l]), "d_skip": row(jnp.repeat(ssd_d[l], SSD_HEAD_DIM)),
                   "norm": row(ssd_norm[l])}
        ys, ss_p = ssd_prompt(p, dt, ssd_prm, bp, lp)
        ys, ssd_states = ssd_sample(p, dt, state_ssd_conv, ssd_s0, ys, ssd_states, ssd_prm, l, mp, bs, ls)

        s5_prm = _s5_params(s5_a_re[l], s5_a_im[l], s5_log_dt[l], s5_b_re[l], s5_b_im[l], s5_c_re[l],
                            s5_c_im[l], s5_d[l])
        gy, sr_p, si_p = s5_chunked(p, s5_prm, bp, lp)
        u_s = p[mp:, P_U:P_U + S5_WIDTH].reshape(bs // S5_SSEQ, S5_SSEQ, ls, S5_WIDTH).transpose(0, 2, 1, 3)
        u_s = u_s.reshape(ms, S5_WIDTH)
        gy_s, sr_s, si_s = s5_scan(u_s, state_s5_re[l].reshape(bs, S5_CH), state_s5_im[l].reshape(bs, S5_CH),
                                   s5_prm, bs // S5_SSEQ, S5_SSEQ, ls, ls)
        gy_s = gy_s.reshape(bs // S5_SSEQ, ls, S5_SSEQ, S5_WIDTH).transpose(0, 2, 1, 3).reshape(ms, S5_WIDTH)
        gy = lax.dynamic_update_slice(gy, gy_s.astype(F32), (mp, 0))

        hg_prm = {"lb": row(lb_all[l]), "norm": row(hg_norm[l])}
        o, hg_p = hg_prompt(p, hg_prm, bp, lp)
        o, hg_states = hg_sample(p, state_hgrn, o, hg_states, hg_prm, l, mp, bs, ls)

        x = branch_mix(x, ys, gy, o, p, b_ssd, b_glu_a, b_glu_b, b_hg, b_mix, l)

        q = norm_proj(x, row(norm_xa[l]), b_wq, l, BF16)
        kv = norm_proj(mem, row(norm_mem[l]), b_wkv, l, F32)
        at = xattn_prompt(q, kv, bp, lp)
        at = xattn_sample(q, mem_k, mem_v, at, l, mp, bs, ls)
        x = res_mm(x, at, b_wo, l)

        x = ffn(x, row(norm_ffn2[l]), f2_w1, f2_w3, f2_w2, row(norm_final), l, l == DEPTH - 1)

        tail = SSD_CONV - 1
        outs["pk"].append(kv[:, :d].reshape(bp, MEM_LEN, XA_HEADS, XA_HEAD_DIM))
        outs["pv"].append(kv[:, d:].reshape(bp, MEM_LEN, XA_HEADS, XA_HEAD_DIM))
        outs["pss"].append(ss_p.reshape(bp, SSD_HEADS, SSD_HEAD_DIM, SSD_STATE))
        outs["pcv"].append(jnp.stack([p[(b + 1) * lp - tail:(b + 1) * lp, P_XBC:P_XBC + SSD_CONV_DIM]
                                      for b in range(bp)]))
        outs["psr"].append(sr_p.reshape(bp, S5_GROUPS, S5_STATE))
        outs["psi"].append(si_p.reshape(bp, S5_GROUPS, S5_STATE))
        outs["phg"].append(hg_p)
        outs["scv"].append(p[mp:, P_XBC:P_XBC + SSD_CONV_DIM].reshape(bs, ls, SSD_CONV_DIM)[:, ls - tail:])
        outs["ssr"].append(sr_s.reshape(bs, S5_GROUPS, S5_STATE))
        outs["ssi"].append(si_s.reshape(bs, S5_GROUPS, S5_STATE))

    st = lambda k: jnp.stack(outs[k])
    return (x[:mp].reshape(bp, lp, d), x[mp:].reshape(bs, ls, d),
            st("pk"), st("pv"), st("pss"), st("pcv"), st("psr"), st("psi"), st("phg"),
            ssd_states.reshape(DEPTH, bs, SSD_HEADS, SSD_HEAD_DIM, SSD_STATE), st("scv"), st("ssr"), st("ssi"),
            hg_states)
```

```python
import functools
import math

import jax
import jax.numpy as jnp
import numpy as np
from jax import lax
from jax.experimental import pallas as pl
from jax.experimental.pallas import tpu as pltpu

F32 = jnp.float32
BF16 = jnp.bfloat16

D_MODEL = 2048
DEPTH = 2
NORM_EPS = 1e-5
SSD_HEAD_DIM = 64
SSD_HEADS = 32
SSD_GROUPS = 4
SSD_STATE = 128
SSD_CONV = 4
SSD_WIDTH = 2048
SSD_CONV_DIM = 3072
S5_WIDTH = 1024
S5_GROUP_SIZE = 16
S5_GROUPS = 64
S5_STATE = 64
HG_WIDTH = 1024
HG_HEADS = 8
HG_KEY_DIM = 128
HG_VAL_DIM = 128
MEM_LEN = 256
XA_HEADS = 4
XA_HEAD_DIM = 512
FFN_DIM = 5632

P_GATES, P_XBC, P_U, P_HQ, P_HF, P_HI, P_HGATE, P_Z = 0, 6144, 9216, 10240, 11264, 12288, 13312, 14336
P_WIDTH = 16384

V7X_VMEM_LIMIT = 56 * 1024 * 1024


def _cparams(sem, vmem=V7X_VMEM_LIMIT):
    return pltpu.CompilerParams(dimension_semantics=sem, vmem_limit_bytes=vmem)


def _rms_scale(x):
    return lax.rsqrt(jnp.mean(x * x, axis=-1, keepdims=True) + NORM_EPS)


def _silu(x):
    return x * jax.nn.sigmoid(x)


def _dot(a, b):
    return jnp.dot(a, b, preferred_element_type=F32)


def _dot_nt(a, b):
    return lax.dot_general(a, b, (((1,), (1,)), ((), ())), preferred_element_type=F32)


def _dot_tn(a, b):
    return lax.dot_general(a, b, (((0,), (0,)), ((), ())), preferred_element_type=F32)


def _split2(x):
    hi = x.astype(BF16)
    lo = (x - hi.astype(F32)).astype(BF16)
    return hi, lo


def _split3(x):
    hi = x.astype(BF16)
    r = x - hi.astype(F32)
    mid = r.astype(BF16)
    lo = (r - mid.astype(F32)).astype(BF16)
    return hi, mid, lo


def _sel_dot(sel, x, parts=3):
    ps = _split3(x) if parts == 3 else _split2(x)
    out = _dot(sel, ps[0])
    for p in ps[1:]:
        out = out + _dot(sel, p)
    return out


def _dot_sel(x, sel, parts=2):
    ps = _split3(x) if parts == 3 else _split2(x)
    out = _dot(ps[0], sel)
    for p in ps[1:]:
        out = out + _dot(p, sel)
    return out


def _norm_proj_body(x_ref, g_ref, w_ref, o_ref, h_ref):
    @pl.when(pl.program_id(1) == 0)
    def _():
        x = x_ref[...]
        h_ref[...] = (x * _rms_scale(x) * g_ref[...]).astype(BF16)

    o_ref[...] = _dot(h_ref[...], w_ref[...]).astype(o_ref.dtype)


def norm_proj(x, g, w, layer, out_dtype, bm=1024, bn=1024):
    m, k = x.shape
    n = w.shape[2]
    return pl.pallas_call(
        _norm_proj_body,
        grid=(m // bm, n // bn),
        in_specs=[pl.BlockSpec((bm, k), lambda i, j: (i, 0)),
                  pl.BlockSpec((1, k), lambda i, j: (0, 0)),
                  pl.BlockSpec((None, k, bn), lambda i, j: (layer, 0, j))],
        out_specs=pl.BlockSpec((bm, bn), lambda i, j: (i, j)),
        out_shape=jax.ShapeDtypeStruct((m, n), out_dtype),
        scratch_shapes=[pltpu.VMEM((bm, k), BF16)],
        compiler_params=_cparams(("parallel", "arbitrary")),
        name="norm_proj",
    )(x, g, w)


IN_SEGMENTS = (("z", 2048, P_Z), ("xbc", 3072, P_XBC), ("dt", SSD_HEADS, None), ("u", 1024, P_U),
               ("hq", 1024, P_HQ), ("hf", 1024, P_HF), ("hi", 1024, P_HI), ("hgate", 1024, P_HGATE),
               ("gates", 6144, P_GATES))
IN_DIM = sum(width for _, width, _ in IN_SEGMENTS)


def _in_weights_body(w_ref, o_ref, odt_ref):
    col = 0
    for _, width, off in IN_SEGMENTS:
        piece = w_ref[col:col + width, :].astype(BF16)
        if off is None:
            odt_ref[...] = jnp.zeros_like(odt_ref)
            odt_ref[0:width, :] = piece
        else:
            o_ref[off:off + width, :] = piece
        col += width


def in_proj_weights(w_in, lanes=256):
    wt = w_in.transpose(0, 2, 1)
    dd, n, k = wt.shape
    return pl.pallas_call(
        _in_weights_body,
        grid=(dd, k // lanes),
        in_specs=[pl.BlockSpec((None, n, lanes), lambda l, i: (l, 0, i))],
        out_specs=[pl.BlockSpec((None, P_WIDTH, lanes), lambda l, i: (l, 0, i)),
                   pl.BlockSpec((None, LANES, lanes), lambda l, i: (l, 0, i))],
        out_shape=[jax.ShapeDtypeStruct((dd, P_WIDTH, k), BF16), jax.ShapeDtypeStruct((dd, LANES, k), BF16)],
        compiler_params=_cparams(("parallel", "parallel")),
        name="in_proj_weights",
    )(wt)


def _in_proj_body(x_ref, g_ref, w_ref, wdt_ref, o_ref, odt_ref, h_ref):
    @pl.when(pl.program_id(1) == 0)
    def _():
        x = x_ref[...]
        h = (x * _rms_scale(x) * g_ref[...]).astype(BF16)
        h_ref[...] = h
        odt_ref[...] = _dot_nt(h, wdt_ref[...])

    o_ref[...] = _dot_nt(h_ref[...], w_ref[...])


def in_proj(x, g, w, wdt, layer, bm=1024, bn=1024):
    m, k = x.shape
    n = w.shape[1]
    ndt = wdt.shape[1]
    return pl.pallas_call(
        _in_proj_body,
        grid=(m // bm, n // bn),
        in_specs=[pl.BlockSpec((bm, k), lambda i, j: (i, 0)),
                  pl.BlockSpec((1, k), lambda i, j: (0, 0)),
                  pl.BlockSpec((None, bn, k), lambda i, j: (layer, j, 0)),
                  pl.BlockSpec((None, ndt, k), lambda i, j: (layer, 0, 0))],
        out_specs=[pl.BlockSpec((bm, bn), lambda i, j: (i, j)),
                   pl.BlockSpec((bm, ndt), lambda i, j: (i, 0))],
        out_shape=[jax.ShapeDtypeStruct((m, n), F32), jax.ShapeDtypeStruct((m, ndt), F32)],
        scratch_shapes=[pltpu.VMEM((bm, k), BF16)],
        compiler_params=_cparams(("parallel", "arbitrary")),
        name="in_proj",
    )(x, g, w, wdt)


def _res_mm_body(x_ref, a_ref, w_ref, o_ref):
    o_ref[...] = x_ref[...] + _dot(a_ref[...], w_ref[...])


def res_mm(x, a, w, layer, bm=1024, bn=1024):
    m, n = x.shape
    k = a.shape[1]
    return pl.pallas_call(
        _res_mm_body,
        grid=(m // bm, n // bn),
        in_specs=[pl.BlockSpec((bm, bn), lambda i, j: (i, j)),
                  pl.BlockSpec((bm, k), lambda i, j: (i, 0)),
                  pl.BlockSpec((None, k, bn), lambda i, j: (layer, 0, j))],
        out_specs=pl.BlockSpec((bm, bn), lambda i, j: (i, j)),
        out_shape=jax.ShapeDtypeStruct((m, n), F32),
        compiler_params=_cparams(("parallel", "parallel")),
        name="res_mm",
    )(x, a, w)


def _ffn_body(x_ref, g_ref, w1_ref, w3_ref, w2_ref, gf_ref, o_ref, h_ref, acc_ref, *, final_norm):
    f = pl.program_id(1)

    @pl.when(f == 0)
    def _():
        x = x_ref[...]
        h_ref[...] = (x * _rms_scale(x) * g_ref[...]).astype(BF16)
        acc_ref[...] = jnp.zeros_like(acc_ref)

    h = h_ref[...]
    a = _silu(_dot(h, w1_ref[...])) * _dot(h, w3_ref[...])
    acc_ref[...] += _dot(a.astype(BF16), w2_ref[...])

    @pl.when(f == pl.num_programs(1) - 1)
    def _():
        y = x_ref[...] + 0.5 * acc_ref[...]
        if final_norm:
            y = y * _rms_scale(y) * gf_ref[...]
        o_ref[...] = y


def ffn(x, g, w1, w3, w2, gf, layer, final_norm, bm=512, bf=512):
    m, d = x.shape
    fdim = w1.shape[2]
    return pl.pallas_call(
        functools.partial(_ffn_body, final_norm=final_norm),
        grid=(m // bm, fdim // bf),
        in_specs=[pl.BlockSpec((bm, d), lambda i, f: (i, 0)),
                  pl.BlockSpec((1, d), lambda i, f: (0, 0)),
                  pl.BlockSpec((None, d, bf), lambda i, f: (layer, 0, f)),
                  pl.BlockSpec((None, d, bf), lambda i, f: (layer, 0, f)),
                  pl.BlockSpec((None, bf, d), lambda i, f: (layer, f, 0)),
                  pl.BlockSpec((1, d), lambda i, f: (0, 0))],
        out_specs=pl.BlockSpec((bm, d), lambda i, f: (i, 0)),
        out_shape=jax.ShapeDtypeStruct((m, d), F32),
        scratch_shapes=[pltpu.VMEM((bm, d), BF16), pltpu.VMEM((bm, d), F32)],
        compiler_params=_cparams(("parallel", "arbitrary")),
        name="ffn",
    )(x, g, w1, w3, w2, gf)


def _mix_body(x_ref, ys_ref, gy_ref, o_ref, ga_ref, gb_ref, gc_ref, wssd_ref, wa_ref, wb_ref, whg_ref,
              wmix_ref, out_ref, acc_ref):
    j = pl.program_id(1)

    @pl.when(j == 0)
    def _():
        acc_ref[...] = jnp.zeros_like(acc_ref)

    gy = gy_ref[...].astype(BF16)
    y_a = _dot(ys_ref[...], wssd_ref[...])
    y_b = _dot(gy, wa_ref[...]) * jax.nn.sigmoid(_dot(gy, wb_ref[...]))
    y_c = _dot(o_ref[...], whg_ref[...])
    mix = (jax.nn.sigmoid(ga_ref[...]) * y_a + jax.nn.sigmoid(gb_ref[...]) * y_b
           + jax.nn.sigmoid(gc_ref[...]) * y_c)
    acc_ref[...] += _dot(mix.astype(BF16), wmix_ref[...])

    @pl.when(j == pl.num_programs(1) - 1)
    def _():
        out_ref[...] = x_ref[...] + acc_ref[...]


def branch_mix(x, ys, gy, o, p, wssd, wa, wb, whg, wmix, layer, bm=512, bn=512):
    m, d = x.shape
    nj = d // bn
    return pl.pallas_call(
        _mix_body,
        grid=(m // bm, nj),
        in_specs=[pl.BlockSpec((bm, d), lambda i, j: (i, 0)),
                  pl.BlockSpec((bm, ys.shape[1]), lambda i, j: (i, 0)),
                  pl.BlockSpec((bm, gy.shape[1]), lambda i, j: (i, 0)),
                  pl.BlockSpec((bm, o.shape[1]), lambda i, j: (i, 0)),
                  pl.BlockSpec((bm, bn), lambda i, j: (i, j)),
                  pl.BlockSpec((bm, bn), lambda i, j: (i, nj + j)),
                  pl.BlockSpec((bm, bn), lambda i, j: (i, 2 * nj + j)),
                  pl.BlockSpec((None, wssd.shape[1], bn), lambda i, j: (layer, 0, j)),
                  pl.BlockSpec((None, wa.shape[1], bn), lambda i, j: (layer, 0, j)),
                  pl.BlockSpec((None, wb.shape[1], bn), lambda i, j: (layer, 0, j)),
                  pl.BlockSpec((None, whg.shape[1], bn), lambda i, j: (layer, 0, j)),
                  pl.BlockSpec((None, bn, d), lambda i, j: (layer, j, 0))],
        out_specs=pl.BlockSpec((bm, d), lambda i, j: (i, 0)),
        out_shape=jax.ShapeDtypeStruct((m, d), F32),
        scratch_shapes=[pltpu.VMEM((bm, d), F32)],
        compiler_params=_cparams(("parallel", "arbitrary")),
        name="branch_mix",
    )(x, ys, gy, o, p, p, p, wssd, wa, wb, whg, wmix)


def _attend(q, k_head, v_head):
    outs = []
    for h in range(XA_HEADS):
        sl = slice(h * XA_HEAD_DIM, (h + 1) * XA_HEAD_DIM)
        s = _dot_nt(q[:, sl], k_head(h).astype(BF16)) * (XA_HEAD_DIM ** -0.5)
        s = s - jnp.max(s, axis=-1, keepdims=True)
        e = jnp.exp(s)
        p = e / jnp.sum(e, axis=-1, keepdims=True)
        outs.append(_dot(p.astype(BF16), v_head(h).astype(BF16)))
    return jnp.concatenate(outs, axis=-1)


def _xattn_prompt_body(q_ref, kv_ref, o_ref):
    k_head = lambda h: kv_ref[:, h * XA_HEAD_DIM:(h + 1) * XA_HEAD_DIM]
    v_head = lambda h: kv_ref[:, D_MODEL + h * XA_HEAD_DIM:D_MODEL + (h + 1) * XA_HEAD_DIM]
    o_ref[...] = _attend(q_ref[...], k_head, v_head).astype(BF16)


def xattn_prompt(q, kv, n_seq, seq_len, bl=512):
    nl = seq_len // bl
    return pl.pallas_call(
        _xattn_prompt_body,
        grid=(n_seq, nl),
        in_specs=[pl.BlockSpec((bl, D_MODEL), lambda b, i: (b * nl + i, 0)),
                  pl.BlockSpec((MEM_LEN, 2 * D_MODEL), lambda b, i: (b, 0))],
        out_specs=pl.BlockSpec((bl, D_MODEL), lambda b, i: (b * nl + i, 0)),
        out_shape=jax.ShapeDtypeStruct(q.shape, BF16),
        compiler_params=_cparams(("parallel", "arbitrary")),
        name="xattn_prompt",
    )(q, kv)


def _xattn_sample_body(q_ref, k_ref, v_ref, buf_ref, o_ref, *, seq_len):
    del buf_ref
    q = q_ref[...]
    r = 2 * seq_len
    nc = XA_HEAD_DIM // LANES
    piece = lambda ref, b, h, c: ref[b, pl.ds(c * XA_HEADS + h, MEM_LEN, stride=nc * XA_HEADS), :].astype(BF16)
    scores = []
    for b in range(2):
        for h in range(XA_HEADS):
            s = None
            for c in range(nc):
                lo = h * XA_HEAD_DIM + c * LANES
                d = _dot_nt(q[:, lo:lo + LANES], piece(k_ref, b, h, c))
                s = d if s is None else s + d
            scores.append(s)
    s = jnp.concatenate(scores, axis=0) * (XA_HEAD_DIM ** -0.5)
    s = s - jnp.max(s, axis=-1, keepdims=True)
    e = jnp.exp(s)
    p = (e / jnp.sum(e, axis=-1, keepdims=True)).astype(BF16)
    rows = lax.broadcasted_iota(jnp.int32, (r, LANES), 0)
    for h in range(XA_HEADS):
        for c in range(nc):
            lo = h * XA_HEAD_DIM + c * LANES
            o0 = _dot(p[h * r:(h + 1) * r], piece(v_ref, 0, h, c))
            o1 = _dot(p[(XA_HEADS + h) * r:(XA_HEADS + h + 1) * r], piece(v_ref, 1, h, c))
            o_ref[:, lo:lo + LANES] = jnp.where(rows < seq_len, o0, o1).astype(BF16)


def xattn_sample(q, k, v, buf, layer, row0, n_seq, seq_len):
    r = 2 * seq_len
    blk0 = row0 // r
    mem_spec = pl.BlockSpec((None, 2, MEM_LEN * D_MODEL // LANES, LANES), lambda i: (layer, i, 0, 0))
    return pl.pallas_call(
        functools.partial(_xattn_sample_body, seq_len=seq_len),
        grid=(n_seq // 2,),
        in_specs=[pl.BlockSpec((r, D_MODEL), lambda i: (blk0 + i, 0)), mem_spec, mem_spec,
                  pl.BlockSpec(memory_space=pl.ANY)],
        out_specs=pl.BlockSpec((r, D_MODEL), lambda i: (blk0 + i, 0)),
        out_shape=jax.ShapeDtypeStruct(buf.shape, BF16),
        input_output_aliases={3: 0},
        compiler_params=_cparams(("parallel",)),
        name="xattn_sample",
    )(q, k, v, buf)


LANES = 128
HALO = 8


def _softplus(x):
    return jnp.maximum(x, 0.0) + jnp.log1p(jnp.exp(-jnp.abs(x)))


def _pad_rows(x, rows):
    if x.shape[0] == rows:
        return x
    return jnp.concatenate([x, jnp.zeros((rows - x.shape[0], x.shape[1]), x.dtype)], axis=0)


def _tile_consts(r, t):
    i = np.arange(r)[:, None]
    j = np.arange(r)[None, :]
    same = (i // t) == (j // t)
    lt = (same & (j <= i)).astype(np.float32)
    last = (same & (j % t == t - 1)).astype(np.float32)
    return jnp.asarray(lt, BF16), jnp.asarray(last, BF16)


def _head_expand(n_heads, width, rows=LANES):
    e = np.zeros((rows, n_heads * width), np.float32)
    for h in range(n_heads):
        e[h, h * width:(h + 1) * width] = 1.0
    return jnp.asarray(e, BF16)


def _ssd_tile(xc, dt_raw, dtb, a_log, lt, last, e, e128, r, t):
    xs = xc[:, :SSD_WIDTH]
    dt = _softplus(dt_raw + dtb)
    d_a = dt * (-jnp.exp(a_log))
    a = _sel_dot(lt, d_a, 3)
    a_e = _dot_sel(a, e, 3)
    dt_e = _dot_sel(dt, e, 2)
    if t == r:
        alast_e = jnp.broadcast_to(a_e[r - 1:r, :], a_e.shape)
    else:
        alast_e = _sel_dot(last, a_e, 3)
    a_col = _dot_sel(a, e128, 3)
    a_t = _pad_rows(a, LANES).T

    row = lax.broadcasted_iota(jnp.int32, (r, LANES), 0)
    col = lax.broadcasted_iota(jnp.int32, (r, LANES), 1)
    valid = (col <= row) & (col >= (row // t) * t)
    lane_lo = col < SSD_HEAD_DIM

    xdt = _pad_rows((xs * dt_e).astype(BF16), LANES)
    hpg = SSD_HEADS // SSD_GROUPS
    ys = []
    for g in range(SSD_GROUPS):
        bg = xc[:, SSD_WIDTH + g * SSD_STATE:SSD_WIDTH + (g + 1) * SSD_STATE].astype(BF16)
        cg = xc[:, SSD_WIDTH + (SSD_GROUPS + g) * SSD_STATE:
                SSD_WIDTH + (SSD_GROUPS + g + 1) * SSD_STATE].astype(BF16)
        cb = _dot_nt(cg, _pad_rows(bg, LANES))
        for hp in range(hpg // 2):
            h0 = g * hpg + 2 * hp
            res = []
            for h in (h0, h0 + 1):
                rel = a_col[:, h * LANES:(h + 1) * LANES] - a_t[h:h + 1, :]
                dec = jnp.where(valid, jnp.exp(jnp.where(valid, rel, 0.0)), 0.0)
                res.append(_dot((cb * dec).astype(BF16), xdt[:, h0 * SSD_HEAD_DIM:(h0 + 2) * SSD_HEAD_DIM]))
            ys.append(jnp.where(lane_lo, res[0], res[1]))
    y_intra = jnp.concatenate(ys, axis=-1)
    return xs, y_intra, a_e, dt_e, alast_e


def _ssd_finish(y, xs, z, dsk, ng):
    y = y + dsk * xs
    y = y * _silu(z)
    return (y * _rms_scale(y) * ng).astype(BF16)


def _conv_silu(ext_ref, cw_ref, cb_ref, base, r):
    acc = cb_ref[...] + cw_ref[SSD_CONV - 1:SSD_CONV, :] * ext_ref[pl.ds(base, r), :]
    for k in range(1, SSD_CONV):
        acc = acc + cw_ref[SSD_CONV - 1 - k:SSD_CONV - k, :] * ext_ref[pl.ds(base - k, r), :]
    return _silu(acc)


def _ssd_prompt_body(xbc_ref, z_ref, dt_ref, cw_ref, cb_ref, dtb_ref, alog_ref, dsk_ref, ng_ref,
                     lt_ref, last_ref, e_ref, e128_ref, y_ref, sout_ref, ext_ref, st_ref, *, r):
    c = pl.program_id(1)

    @pl.when(c == 0)
    def _():
        ext_ref[0:HALO, :] = jnp.zeros((HALO, SSD_CONV_DIM), F32)
        st_ref[...] = jnp.zeros_like(st_ref)

    ext_ref[HALO:HALO + r, :] = xbc_ref[...]
    xc = _conv_silu(ext_ref, cw_ref, cb_ref, HALO, r)
    ext_ref[0:HALO, :] = xbc_ref[r - HALO:r, :]

    xs, y, a_e, dt_e, alast_e = _ssd_tile(xc, dt_ref[...], dtb_ref[...], alog_ref[...], lt_ref[...],
                                          last_ref[...], e_ref[...], e128_ref[...], r, r)
    ea_e = jnp.exp(a_e)
    xw = (xs * (dt_e * jnp.exp(alast_e - a_e))).astype(BF16)
    sdec = jnp.exp(alast_e[0:1, :])
    gw = SSD_WIDTH // SSD_GROUPS
    inter = []
    for g in range(SSD_GROUPS):
        bg = xc[:, SSD_WIDTH + g * SSD_STATE:SSD_WIDTH + (g + 1) * SSD_STATE].astype(BF16)
        cg = xc[:, SSD_WIDTH + (SSD_GROUPS + g) * SSD_STATE:
                SSD_WIDTH + (SSD_GROUPS + g + 1) * SSD_STATE].astype(BF16)
        st = st_ref[:, g * gw:(g + 1) * gw]
        inter.append(_dot(cg, st.astype(BF16)))
        st_ref[:, g * gw:(g + 1) * gw] = st * sdec[:, g * gw:(g + 1) * gw] + _dot_tn(bg, xw[:, g * gw:(g + 1) * gw])
    y = y + jnp.concatenate(inter, axis=-1) * ea_e
    y_ref[...] = _ssd_finish(y, xs, z_ref[...], dsk_ref[...], ng_ref[...])

    @pl.when(c == pl.num_programs(1) - 1)
    def _():
        sout_ref[0] = st_ref[...].T


def _ssd_consts(r, t):
    lt, last = _tile_consts(r, t)
    return lt, last, _head_expand(SSD_HEADS, SSD_HEAD_DIM), _head_expand(SSD_HEADS, LANES)


def ssd_prompt(p, dt, prm, n_seq, seq_len, r=128):
    nc = seq_len // r
    consts = _ssd_consts(r, r)
    full = lambda a: pl.BlockSpec(a.shape, lambda b, c: (0,) * a.ndim)
    small = [prm["conv_w"], prm["conv_b"], prm["dt_bias"], prm["a_log"], prm["d_skip"], prm["norm"], *consts]
    return pl.pallas_call(
        functools.partial(_ssd_prompt_body, r=r),
        grid=(n_seq, nc),
        in_specs=[pl.BlockSpec((r, SSD_CONV_DIM), lambda b, c: (b * nc + c, P_XBC // SSD_CONV_DIM)),
                  pl.BlockSpec((r, SSD_WIDTH), lambda b, c: (b * nc + c, P_Z // SSD_WIDTH)),
                  pl.BlockSpec((r, LANES), lambda b, c: (b * nc + c, 0)),
                  *[full(a) for a in small]],
        out_specs=[pl.BlockSpec((r, SSD_WIDTH), lambda b, c: (b * nc + c, 0)),
                   pl.BlockSpec((1, SSD_WIDTH, SSD_STATE), lambda b, c: (b, 0, 0))],
        out_shape=[jax.ShapeDtypeStruct((p.shape[0], SSD_WIDTH), BF16),
                   jax.ShapeDtypeStruct((n_seq, SSD_WIDTH, SSD_STATE), F32)],
        scratch_shapes=[pltpu.VMEM((HALO + r, SSD_CONV_DIM), F32), pltpu.VMEM((SSD_STATE, SSD_WIDTH), F32)],
        compiler_params=_cparams(("parallel", "arbitrary")),
        name="ssd_prompt",
    )(p, p, dt, *small)


def _ssd_sample_body(xbc_ref, z_ref, dt_ref, buf_ref, s0_ref, cw_ref, cb_ref, dtb_ref, alog_ref, dsk_ref,
                     ng_ref, lt_ref, last_ref, e_ref, e128_ref, *rest, r, t):
    y_ref, sout_ref, ext_ref = rest[-3:]
    nb = r // t
    pitch = HALO + t
    for b in range(nb):
        ext_ref[b * pitch + HALO - (SSD_CONV - 1):b * pitch + HALO, :] = buf_ref[b]
        ext_ref[b * pitch + HALO:(b + 1) * pitch, :] = xbc_ref[b * t:(b + 1) * t, :]
    xc = jnp.concatenate([_conv_silu(ext_ref, cw_ref, cb_ref, b * pitch + HALO, t) for b in range(nb)], axis=0)

    xs, y, a_e, dt_e, alast_e = _ssd_tile(xc, dt_ref[...], dtb_ref[...], alog_ref[...], lt_ref[...],
                                          last_ref[...], e_ref[...], e128_ref[...], r, t)
    ea_e = jnp.exp(a_e)
    xw = xs * (dt_e * jnp.exp(alast_e - a_e))
    sdec = jnp.exp(alast_e)
    gw = SSD_WIDTH // SSD_GROUPS
    pr = 2 * t
    prow = lax.broadcasted_iota(jnp.int32, (pr, 1), 0)
    ones = jnp.ones((pr, SSD_STATE), BF16)
    inter = []
    for g in range(SSD_GROUPS):
        bg = xc[:, SSD_WIDTH + g * SSD_STATE:SSD_WIDTH + (g + 1) * SSD_STATE].astype(BF16)
        cg = xc[:, SSD_WIDTH + (SSD_GROUPS + g) * SSD_STATE:
                SSD_WIDTH + (SSD_GROUPS + g + 1) * SSD_STATE].astype(BF16)
        cols = slice(g * gw, (g + 1) * gw)
        rows_out = []
        for q in range(nb // 2):
            rs = slice(q * pr, (q + 1) * pr)
            acc = None
            for s in range(2):
                b = 2 * q + s
                mine = (prow >= s * t) & (prow < (s + 1) * t)
                s0 = s0_ref[b, cols, :]
                yi = _dot_nt(cg[rs], s0.astype(BF16))
                acc = jnp.where(mine, yi, 0.0) if acc is None else acc + jnp.where(mine, yi, 0.0)
                upd = _dot_tn(jnp.where(mine, xw[rs, cols], 0.0).astype(BF16), bg[rs])
                lastrow = prow == (s + 1) * t - 1
                dh, dl = _split2(jnp.where(lastrow, sdec[rs, cols], 0.0))
                dcol = _dot_tn(dh, ones) + _dot_tn(dl, ones)
                sout_ref[b, cols, :] = s0 * dcol + upd
            rows_out.append(acc)
        inter.append(jnp.concatenate(rows_out, axis=0))
    y = y + jnp.concatenate(inter, axis=-1) * ea_e
    y_ref[...] = _ssd_finish(y, xs, z_ref[...], dsk_ref[...], ng_ref[...])


def ssd_sample(p, dt, conv_buf, s0, ybuf, sprev, prm, layer, row0, n_seq, seq_len, r=64):
    nb = r // seq_len
    blk0 = row0 // r
    consts = _ssd_consts(r, seq_len)
    full = lambda a: pl.BlockSpec(a.shape, lambda i: (0,) * a.ndim)
    small = [prm["conv_w"], prm["conv_b"], prm["dt_bias"], prm["a_log"], prm["d_skip"], prm["norm"], *consts]
    inplace = [ybuf] if sprev is None else [ybuf, sprev]
    n_in = 5 + len(small)
    return pl.pallas_call(
        functools.partial(_ssd_sample_body, r=r, t=seq_len),
        grid=(n_seq // nb,),
        in_specs=[pl.BlockSpec((r, SSD_CONV_DIM), lambda i: (blk0 + i, P_XBC // SSD_CONV_DIM)),
                  pl.BlockSpec((r, SSD_WIDTH), lambda i: (blk0 + i, P_Z // SSD_WIDTH)),
                  pl.BlockSpec((r, LANES), lambda i: (blk0 + i, 0)),
                  pl.BlockSpec((None, nb, SSD_CONV - 1, SSD_CONV_DIM), lambda i: (layer, i, 0, 0)),
                  pl.BlockSpec((None, nb, SSD_WIDTH, SSD_STATE), lambda i: (layer, i, 0, 0)),
                  *[full(a) for a in small],
                  *[pl.BlockSpec(memory_space=pl.ANY) for _ in inplace]],
        out_specs=[pl.BlockSpec((r, SSD_WIDTH), lambda i: (blk0 + i, 0)),
                   pl.BlockSpec((None, nb, SSD_WIDTH, SSD_STATE), lambda i: (layer, i, 0, 0))],
        out_shape=[jax.ShapeDtypeStruct(ybuf.shape, BF16),
                   jax.ShapeDtypeStruct(s0.shape, F32)],
        input_output_aliases={n_in + k: k for k in range(len(inplace))},
        scratch_shapes=[pltpu.VMEM((nb * (HALO + seq_len), SSD_CONV_DIM), F32)],
        compiler_params=_cparams(("parallel",)),
        name="ssd_sample",
    )(p, p, dt, conv_buf, s0, *small, *inplace)


S5_CH = S5_GROUPS * S5_STATE
S5_BLK = 4
S5_SSEQ = 32


def _s5_in(u, bre_ref, bim_ref):
    ub = u.astype(BF16)
    kin = S5_WIDTH // S5_BLK
    re = [_dot(ub[:, q * kin:(q + 1) * kin], bre_ref[q]) for q in range(S5_BLK)]
    im = [_dot(ub[:, q * kin:(q + 1) * kin], bim_ref[q]) for q in range(S5_BLK)]
    return jnp.concatenate(re, axis=-1), jnp.concatenate(im, axis=-1)


def _s5_out(h_re, h_im, u, cre_ref, cim_ref, dsk):
    kst = S5_CH // S5_BLK
    hr = h_re.astype(BF16)
    hi = h_im.astype(BF16)
    y = [_dot(hr[:, q * kst:(q + 1) * kst], cre_ref[q]) - _dot(hi[:, q * kst:(q + 1) * kst], cim_ref[q])
         for q in range(S5_BLK)]
    y = jnp.concatenate(y, axis=-1) + dsk * u
    return jax.nn.gelu(y).astype(BF16)


def _s5_scan_body(u_ref, h0re_ref, h0im_ref, are_ref, aim_ref, bre_ref, bim_ref, cre_ref, cim_ref, dsk_ref,
                  y_ref, hre_ref, him_ref, sre_ref, sim_ref, *, s, tc):
    c = pl.program_id(1)

    @pl.when(c == 0)
    def _():
        sre_ref[0:s, :] = h0re_ref[...]
        sim_ref[0:s, :] = h0im_ref[...]

    u = u_ref[...]
    bu_re, bu_im = _s5_in(u, bre_ref, bim_ref)
    sre_ref[s:, :] = bu_re
    sim_ref[s:, :] = bu_im
    a_re = jnp.broadcast_to(are_ref[...], (s, S5_CH))
    a_im = jnp.broadcast_to(aim_ref[...], (s, S5_CH))

    def step(t, carry):
        prev = pl.ds(pl.multiple_of(t * s, s), s)
        cur = pl.ds(pl.multiple_of((t + 1) * s, s), s)
        h_re = sre_ref[prev, :]
        h_im = sim_ref[prev, :]
        sre_ref[cur, :] = a_re * h_re - a_im * h_im + sre_ref[cur, :]
        sim_ref[cur, :] = a_re * h_im + a_im * h_re + sim_ref[cur, :]
        return carry

    lax.fori_loop(0, tc, step, 0, unroll=4)
    h_re = sre_ref[tc * s:, :]
    h_im = sim_ref[tc * s:, :]
    y_ref[...] = _s5_out(sre_ref[s:, :], sim_ref[s:, :], u, cre_ref, cim_ref, dsk_ref[...])
    sre_ref[0:s, :] = h_re
    sim_ref[0:s, :] = h_im
    hre_ref[...] = h_re
    him_ref[...] = h_im


def s5_scan(u, h0_re, h0_im, prm, n_blocks, s, n_steps, tc):
    nc = n_steps // tc
    r = tc * s
    full = lambda a: pl.BlockSpec(a.shape, lambda b, c: (0,) * a.ndim)
    small = [prm["abar_re"], prm["abar_im"], prm["b_re"], prm["b_im"], prm["c_re"], prm["c_im"], prm["d_skip"]]
    state = jax.ShapeDtypeStruct((n_blocks * s, S5_CH), F32)
    sspec = pl.BlockSpec((s, S5_CH), lambda b, c: (b, 0))
    return pl.pallas_call(
        functools.partial(_s5_scan_body, s=s, tc=tc),
        grid=(n_blocks, nc),
        in_specs=[pl.BlockSpec((r, S5_WIDTH), lambda b, c: (b * nc + c, 0)), sspec, sspec,
                  *[full(a) for a in small]],
        out_specs=[pl.BlockSpec((r, S5_WIDTH), lambda b, c: (b * nc + c, 0)), sspec, sspec],
        out_shape=[jax.ShapeDtypeStruct((u.shape[0], S5_WIDTH), BF16), state, state],
        scratch_shapes=[pltpu.VMEM((s + r, S5_CH), F32), pltpu.VMEM((s + r, S5_CH), F32)],
        compiler_params=_cparams(("parallel", "arbitrary")),
        name="s5_scan",
    )(u, h0_re, h0_im, *small)


S5_T = 16
S5_GB = LANES // S5_GROUP_SIZE
S5_CL = S5_T * LANES
S5_SL = S5_GB * S5_STATE
S5_NB = 2
S5_PITCH_PAD = 8


def _block_diag_rows(x, rows_w, cols_w):
    xt = jnp.concatenate([x] * S5_GB, axis=0)
    r = lax.broadcasted_iota(jnp.int32, xt.shape, 0) // rows_w
    c = lax.broadcasted_iota(jnp.int32, xt.shape, 1) // cols_w
    return jnp.where(r == c, xt, jnp.zeros_like(xt))


def _s5_chunk_body(u_ref, kj_ref, bcre_ref, bcim_ref, ccre_ref, ccim_ref, are_ref, aim_ref, dsk_ref,
                   y_ref, hre_ref, him_ref, vre_ref, vim_ref, sre_ref, sim_ref,
                   tp_ref, bre_ref, bim_ref, cre_ref, cim_ref, *, n_chunks):
    ks = S5_GROUP_SIZE

    @pl.when(pl.program_id(1) == 0)
    def _():
        zero = jnp.zeros((LANES, LANES), BF16)
        for tau in range(S5_T):
            piece = _block_diag_rows(kj_ref[0, tau], ks, ks)
            for s in range(S5_T - tau):
                tp_ref[s * LANES:(s + 1) * LANES, (s + tau) * LANES:(s + tau + 1) * LANES] = piece
        for s in range(1, S5_T):
            for t in range(s):
                tp_ref[s * LANES:(s + 1) * LANES, t * LANES:(t + 1) * LANES] = zero
        for s in range(S5_T):
            bre_ref[s * LANES:(s + 1) * LANES, :] = _block_diag_rows(bcre_ref[0, s], ks, S5_STATE)
            bim_ref[s * LANES:(s + 1) * LANES, :] = _block_diag_rows(bcim_ref[0, s], ks, S5_STATE)
            cre_ref[:, s * LANES:(s + 1) * LANES] = _block_diag_rows(ccre_ref[0, s], S5_STATE, ks)
            cim_ref[:, s * LANES:(s + 1) * LANES] = _block_diag_rows(ccim_ref[0, s], S5_STATE, ks)

    rows = S5_NB * n_chunks
    pitch = n_chunks + S5_PITCH_PAD
    nk = S5_SL // LANES
    u = jnp.concatenate([u_ref[pl.ds(t, rows, stride=S5_T), :] for t in range(S5_T)], axis=-1)
    ub = u.astype(BF16)
    v_re = _dot(ub, bre_ref[...])
    v_im = _dot(ub, bim_ref[...])
    for k in range(nk):
        vre_ref[k] = v_re[:, k * LANES:(k + 1) * LANES]
        vim_ref[k] = v_im[:, k * LANES:(k + 1) * LANES]
        sre_ref[k] = jnp.zeros((S5_NB * pitch, LANES), F32)
        sim_ref[k] = jnp.zeros((S5_NB * pitch, LANES), F32)
    a_re = [jnp.broadcast_to(are_ref[0, :, k * LANES:(k + 1) * LANES], (S5_NB, LANES)) for k in range(nk)]
    a_im = [jnp.broadcast_to(aim_ref[0, :, k * LANES:(k + 1) * LANES], (S5_NB, LANES)) for k in range(nk)]

    def step(c, carry):
        for k in range(nk):
            h_re = sre_ref[k, pl.ds(c, S5_NB, stride=pitch), :]
            h_im = sim_ref[k, pl.ds(c, S5_NB, stride=pitch), :]
            x_re = vre_ref[k, pl.ds(c, S5_NB, stride=n_chunks), :]
            x_im = vim_ref[k, pl.ds(c, S5_NB, stride=n_chunks), :]
            sre_ref[k, pl.ds(c + 1, S5_NB, stride=pitch), :] = a_re[k] * h_re - a_im[k] * h_im + x_re
            sim_ref[k, pl.ds(c + 1, S5_NB, stride=pitch), :] = a_re[k] * h_im + a_im[k] * h_re + x_im
        return carry

    lax.fori_loop(0, n_chunks, step, 0, unroll=2)
    starts = lambda ref: jnp.concatenate(
        [jnp.concatenate([ref[k, b * pitch:b * pitch + n_chunks, :] for b in range(S5_NB)], axis=0)
         for k in range(nk)], axis=-1)
    y = _dot(ub, tp_ref[...])
    y = y + _dot(starts(sre_ref).astype(BF16), cre_ref[...]) - _dot(starts(sim_ref).astype(BF16), cim_ref[...])
    y = jax.nn.gelu(y + dsk_ref[0] * u)
    for t in range(S5_T):
        y_ref[pl.ds(t, rows, stride=S5_T), :] = y[:, t * LANES:(t + 1) * LANES]
    final = lambda ref: jnp.concatenate(
        [jnp.concatenate([ref[k, b * pitch + n_chunks:b * pitch + n_chunks + 1, :] for b in range(S5_NB)], axis=0)
         for k in range(nk)], axis=-1)
    hre_ref[...] = final(sre_ref)
    him_ref[...] = final(sim_ref)


def s5_chunked(p, prm, n_seq, seq_len):
    n_chunks = seq_len // S5_T
    rows = S5_NB * n_chunks
    nblk = S5_GROUPS // S5_GB
    nhalf = n_seq // S5_NB
    pitch = n_chunks + S5_PITCH_PAD
    nk = S5_SL // LANES
    blk = lambda a: pl.BlockSpec((1,) + a.shape[1:], lambda j, i: (j,) + (0,) * (a.ndim - 1))
    small = [prm["kern_j"], prm["bst_re"], prm["bst_im"], prm["cst_re"], prm["cst_im"], prm["apow_re"],
             prm["apow_im"], prm["d_tiled"]]
    state = jax.ShapeDtypeStruct((nhalf, S5_NB, S5_CH), F32)
    sspec = pl.BlockSpec((None, S5_NB, S5_SL), lambda j, i: (i, 0, j))
    return pl.pallas_call(
        functools.partial(_s5_chunk_body, n_chunks=n_chunks),
        grid=(nblk, nhalf),
        in_specs=[pl.BlockSpec((S5_NB * seq_len, LANES), lambda j, i: (i, P_U // LANES + j)),
                  *[blk(a) for a in small]],
        out_specs=[pl.BlockSpec((S5_NB * seq_len, LANES), lambda j, i: (i, j)), sspec, sspec],
        out_shape=[jax.ShapeDtypeStruct((p.shape[0], S5_WIDTH), F32), state, state],
        scratch_shapes=[pltpu.VMEM((nk, rows, LANES), F32), pltpu.VMEM((nk, rows, LANES), F32),
                        pltpu.VMEM((nk, S5_NB * pitch, LANES), F32), pltpu.VMEM((nk, S5_NB * pitch, LANES), F32),
                        pltpu.VMEM((S5_CL, S5_CL), BF16), pltpu.VMEM((S5_CL, S5_SL), BF16),
                        pltpu.VMEM((S5_CL, S5_SL), BF16), pltpu.VMEM((S5_SL, S5_CL), BF16),
                        pltpu.VMEM((S5_SL, S5_CL), BF16)],
        compiler_params=_cparams(("parallel", "arbitrary")),
        name="s5_chunked",
    )(p, *small)


def _hg_consts(r, t):
    i = np.arange(r)[:, None]
    j = np.arange(r)[None, :]
    sums, upper, pair = [], [], []
    s = 1
    while s < t:
        blk_i, blk_j = i // (2 * s), j // (2 * s)
        up_i = (i % (2 * s)) >= s
        mid_i = blk_i * 2 * s + s
        m_up = up_i & (j >= mid_i) & (j <= i)
        m_lo = (~up_i) & (j > i) & (j < mid_i)
        sums.append((m_up | m_lo).astype(np.float32))
        upper.append(np.broadcast_to(up_i, (r, 1)).astype(np.float32))
        pair.append(((blk_i == blk_j) & up_i & ((j % (2 * s)) < s)).astype(np.float32))
        s *= 2
    pair.append((i == j).astype(np.float32))
    return (jnp.asarray(np.stack(sums), BF16), jnp.asarray(np.stack(upper), F32),
            jnp.asarray(np.stack(pair), F32))


def _hg_gates(hf, lb):
    logf = -_softplus(-hf) + jnp.log1p(lb * jnp.exp(-hf))
    kk = (1.0 - lb) * jax.nn.sigmoid(-hf)
    return logf, kk


def _hg_intra(q, kk, v, logf, b, sums_ref, upper_ref, pair_ref):
    r = q.shape[0]
    nlev = sums_ref.shape[0]
    lf3 = _split3(logf)
    qb = q.astype(BF16)
    kb = kk.astype(BF16)
    vb = _pad_rows(v.astype(BF16), LANES) if r < LANES else v.astype(BF16)
    scores = [None] * HG_HEADS
    for lev in range(nlev + 1):
        if lev < nlev:
            upper = upper_ref[lev] > 0.5
            half = 2 ** lev
            if half >= HALO:
                blocks = b.reshape(r // (2 * half), 2 * half, HG_WIDTH)
                ref = jnp.broadcast_to(blocks[:, half - 1:half, :], blocks.shape).reshape(r, HG_WIDTH)
                d = jnp.where(upper, b - ref, ref - b)
            else:
                m = sums_ref[lev]
                d = _dot(m, lf3[0]) + _dot(m, lf3[1]) + _dot(m, lf3[2])
            x = (jnp.where(upper, q, kk) * jnp.exp(d)).astype(BF16)
            xq, xk = x, x
        else:
            xq, xk = qb, kb
        mask = pair_ref[lev]
        for h in range(HG_HEADS):
            sl = slice(h * HG_KEY_DIM, (h + 1) * HG_KEY_DIM)
            sc = _dot_nt(xq[:, sl], xk[:, sl]) * mask
            scores[h] = sc if scores[h] is None else scores[h] + sc
    outs = [_dot(scores[h].astype(BF16), vb[:, h * HG_VAL_DIM:(h + 1) * HG_VAL_DIM]) for h in range(HG_HEADS)]
    return jnp.concatenate(outs, axis=-1)


def _hg_finish(o, hgate, ng):
    outs = []
    for h in range(HG_HEADS):
        oh = o[:, h * HG_VAL_DIM:(h + 1) * HG_VAL_DIM]
        outs.append(oh * _rms_scale(oh) * ng)
    return (jnp.concatenate(outs, axis=-1) * _silu(hgate)).astype(BF16)


def _hg_prompt_body(q_ref, f_ref, i_ref, gate_ref, lb_ref, ng_ref, lt_ref, last_ref, sums_ref, upper_ref,
                    pair_ref, o_ref, sout_ref, *, r):
    c = pl.program_id(1)

    @pl.when(c == 0)
    def _():
        sout_ref[...] = jnp.zeros_like(sout_ref)

    q = q_ref[...]
    v = i_ref[...]
    logf, kk = _hg_gates(f_ref[...], lb_ref[...])
    b = _sel_dot(lt_ref[...], logf, 3)
    o = _hg_intra(q, kk, v, logf, b, sums_ref, upper_ref, pair_ref)
    blast = b[r - 1:r, :]
    qe = (q * jnp.exp(b)).astype(BF16)
    kw = (kk * jnp.exp(blast - b)).astype(BF16)
    vb = v.astype(BF16)
    ones = jnp.ones((r, HG_VAL_DIM), BF16)
    rows = lax.broadcasted_iota(jnp.int32, (r, 1), 0)
    dh, dl = _split2(jnp.where(rows == r - 1, jnp.exp(b), 0.0))
    inter = []
    for h in range(HG_HEADS):
        sl = slice(h * HG_KEY_DIM, (h + 1) * HG_KEY_DIM)
        s = sout_ref[0, h]
        inter.append(_dot(qe[:, sl], s.astype(BF16)))
        dcol = _dot_tn(dh[:, sl], ones) + _dot_tn(dl[:, sl], ones)
        sout_ref[0, h] = s * dcol + _dot_tn(kw[:, sl], vb[:, sl])
    o = o + jnp.concatenate(inter, axis=-1)
    o_ref[...] = _hg_finish(o, gate_ref[...], ng_ref[...])


def hg_prompt(p, prm, n_seq, seq_len, r=128):
    nc = seq_len // r
    lt, last = _tile_consts(r, r)
    consts = [lt, last, *_hg_consts(r, r)]
    full = lambda a: pl.BlockSpec(a.shape, lambda b, c: (0,) * a.ndim)
    small = [prm["lb"], prm["norm"], *consts]
    col = lambda off: pl.BlockSpec((r, HG_WIDTH), lambda b, c: (b * nc + c, off // HG_WIDTH))
    return pl.pallas_call(
        functools.partial(_hg_prompt_body, r=r),
        grid=(n_seq, nc),
        in_specs=[col(P_HQ), col(P_HF), col(P_HI), col(P_HGATE), *[full(a) for a in small]],
        out_specs=[pl.BlockSpec((r, HG_WIDTH), lambda b, c: (b * nc + c, 0)),
                   pl.BlockSpec((1, HG_HEADS, HG_KEY_DIM, HG_VAL_DIM), lambda b, c: (b, 0, 0, 0))],
        out_shape=[jax.ShapeDtypeStruct((p.shape[0], HG_WIDTH), BF16),
                   jax.ShapeDtypeStruct((n_seq, HG_HEADS, HG_KEY_DIM, HG_VAL_DIM), F32)],
        compiler_params=_cparams(("parallel", "arbitrary")),
        name="hg_prompt",
    )(p, p, p, p, *small)


def _hg_sample_body(q_ref, f_ref, i_ref, gate_ref, s0_ref, lb_ref, ng_ref, lt_ref, last_ref, sums_ref,
                    upper_ref, pair_ref, *rest, r, t):
    o_ref, sout_ref = rest[-2:]
    q = q_ref[...]
    v = i_ref[...]
    logf, kk = _hg_gates(f_ref[...], lb_ref[...])
    b = _sel_dot(lt_ref[...], logf, 3)
    o = _hg_intra(q, kk, v, logf, b, sums_ref, upper_ref, pair_ref)
    blast = _sel_dot(last_ref[...], b, 3)
    qe = (q * jnp.exp(b)).astype(BF16)
    kw = kk * jnp.exp(blast - b)
    sdec = jnp.exp(blast)
    vb = v.astype(BF16)
    pr = 2 * t
    prow = lax.broadcasted_iota(jnp.int32, (pr, 1), 0)
    ones = jnp.ones((pr, HG_VAL_DIM), BF16)
    rows_out = []
    for p2 in range(r // pr):
        rs = slice(p2 * pr, (p2 + 1) * pr)
        heads = []
        for h in range(HG_HEADS):
            sl = slice(h * HG_KEY_DIM, (h + 1) * HG_KEY_DIM)
            acc = None
            for s in range(2):
                bi = 2 * p2 + s
                mine = (prow >= s * t) & (prow < (s + 1) * t)
                s0 = s0_ref[bi, h]
                oi = jnp.where(mine, _dot(qe[rs, sl], s0.astype(BF16)), 0.0)
                acc = oi if acc is None else acc + oi
                upd = _dot_tn(jnp.where(mine, kw[rs, sl], 0.0).astype(BF16), vb[rs, sl])
                dh, dl = _split2(jnp.where(prow == (s + 1) * t - 1, sdec[rs, sl], 0.0))
                dcol = _dot_tn(dh, ones) + _dot_tn(dl, ones)
                sout_ref[bi, h] = s0 * dcol + upd
            heads.append(acc)
        rows_out.append(jnp.concatenate(heads, axis=-1))
    o = o + jnp.concatenate(rows_out, axis=0)
    o_ref[...] = _hg_finish(o, gate_ref[...], ng_ref[...])


def hg_sample(p, s0, obuf, sprev, prm, layer, row0, n_seq, seq_len, r=128):
    nb = r // seq_len
    blk0 = row0 // r
    lt, last = _tile_consts(r, seq_len)
    consts = [lt, last, *_hg_consts(r, seq_len)]
    full = lambda a: pl.BlockSpec(a.shape, lambda i: (0,) * a.ndim)
    small = [prm["lb"], prm["norm"], *consts]
    col = lambda off: pl.BlockSpec((r, HG_WIDTH), lambda i: (blk0 + i, off // HG_WIDTH))
    sspec = pl.BlockSpec((None, nb, HG_HEADS, HG_KEY_DIM, HG_VAL_DIM), lambda i: (layer, i, 0, 0, 0))
    inplace = [obuf] if sprev is None else [obuf, sprev]
    n_in = 5 + len(small)
    return pl.pallas_call(
        functools.partial(_hg_sample_body, r=r, t=seq_len),
        grid=(n_seq // nb,),
        in_specs=[col(P_HQ), col(P_HF), col(P_HI), col(P_HGATE), sspec, *[full(a) for a in small],
                  *[pl.BlockSpec(memory_space=pl.ANY) for _ in inplace]],
        out_specs=[pl.BlockSpec((r, HG_WIDTH), lambda i: (blk0 + i, 0)), sspec],
        out_shape=[jax.ShapeDtypeStruct(obuf.shape, BF16), jax.ShapeDtypeStruct(s0.shape, F32)],
        input_output_aliases={n_in + k: k for k in range(len(inplace))},
        compiler_params=_cparams(("parallel",)),
        name="hg_sample",
    )(p, p, p, p, s0, *small, *inplace)


def _s5_params(a_re, a_im, log_dt, b_re, b_im, c_re, c_im, d_skip):
    dt = jnp.exp(log_dt)[:, None]
    mag = jnp.exp(a_re * dt)
    abar_re = mag * jnp.cos(a_im * dt)
    abar_im = mag * jnp.sin(a_im * dt)
    den = a_re * a_re + a_im * a_im
    nr = abar_re - 1.0
    coef_re = (nr * a_re + abar_im * a_im) / den
    coef_im = (abar_im * a_re - nr * a_im) / den
    bbar_re = coef_re[..., None] * b_re - coef_im[..., None] * b_im
    bbar_im = coef_re[..., None] * b_im + coef_im[..., None] * b_re
    gpb = S5_GROUPS // S5_BLK
    eye = jnp.eye(gpb, dtype=F32)

    def in_blocks(bbar):
        bb = bbar.reshape(S5_BLK, gpb, S5_STATE, S5_GROUP_SIZE)
        return jnp.einsum("qgnk,gh->qgkhn", bb, eye).reshape(
            S5_BLK, gpb * S5_GROUP_SIZE, gpb * S5_STATE).astype(BF16)

    def out_blocks(c):
        cc = c.reshape(S5_BLK, gpb, S5_GROUP_SIZE, S5_STATE)
        return jnp.einsum("qgkn,gh->qgnhk", cc, eye).reshape(
            S5_BLK, gpb * S5_STATE, gpb * S5_GROUP_SIZE).astype(BF16)

    prm = {
        "abar_re": abar_re.reshape(1, S5_CH), "abar_im": abar_im.reshape(1, S5_CH),
        "b_re": in_blocks(bbar_re), "b_im": in_blocks(bbar_im),
        "c_re": out_blocks(c_re), "c_im": out_blocks(c_im),
        "d_skip": d_skip.reshape(1, S5_WIDTH),
    }

    tau = jnp.arange(S5_T + 1, dtype=F32)[:, None, None]
    pmag = jnp.exp(tau * (a_re * dt))
    pw_re = pmag * jnp.cos(tau * (a_im * dt))
    pw_im = pmag * jnp.sin(tau * (a_im * dt))
    cp_re = c_re[None] * pw_re[:, :, None, :] - c_im[None] * pw_im[:, :, None, :]
    cp_im = c_re[None] * pw_im[:, :, None, :] + c_im[None] * pw_re[:, :, None, :]
    bt_re = bbar_re.transpose(0, 2, 1)[None, :, None]
    bt_im = bbar_im.transpose(0, 2, 1)[None, :, None]
    kern = jnp.sum(cp_re[:, :, :, None, :] * bt_re - cp_im[:, :, :, None, :] * bt_im, axis=-1)
    nq = S5_GROUPS // S5_GB
    ks = S5_GROUP_SIZE
    kern_j = (kern[:S5_T].reshape(S5_T, nq, S5_GB, ks, ks).transpose(1, 0, 4, 2, 3)
              .reshape(nq, S5_T, ks, LANES).astype(BF16))
    rev = pw_re[S5_T - 1 - jnp.arange(S5_T)], pw_im[S5_T - 1 - jnp.arange(S5_T)]
    bst_re = rev[0][..., None] * bbar_re[None] - rev[1][..., None] * bbar_im[None]
    bst_im = rev[0][..., None] * bbar_im[None] + rev[1][..., None] * bbar_re[None]

    def bst_blocks(b):
        return (b.reshape(S5_T, nq, S5_GB, S5_STATE, ks).transpose(1, 0, 4, 2, 3)
                .reshape(nq, S5_T, ks, S5_SL).astype(BF16))

    def cst_blocks(c):
        return (c.reshape(S5_T, nq, S5_GB, ks, S5_STATE).transpose(1, 0, 4, 2, 3)
                .reshape(nq, S5_T, S5_STATE, LANES).astype(BF16))

    prm.update({
        "kern_j": kern_j,
        "bst_re": bst_blocks(bst_re), "bst_im": bst_blocks(bst_im),
        "cst_re": cst_blocks(cp_re[1:]), "cst_im": cst_blocks(cp_im[1:]),
        "apow_re": pw_re[S5_T].reshape(nq, 1, S5_SL), "apow_im": pw_im[S5_T].reshape(nq, 1, S5_SL),
        "d_tiled": jnp.broadcast_to(d_skip.reshape(nq, 1, 1, LANES), (nq, 1, S5_T, LANES)).reshape(nq, 1, S5_CL),
    })
    return prm


def _mem_rows(cache):
    dd, b, m, h, hd = cache.shape
    c = cache.reshape(dd, b, m, h, hd // LANES, LANES).transpose(0, 1, 2, 4, 3, 5)
    return c.reshape(dd, b, m * h * (hd // LANES), LANES)


def _pad_lanes(v):
    return jnp.pad(v.reshape(1, -1), ((0, 0), (0, LANES - v.shape[-1])))


def kernel(x_prompt, x_sample, cache_mem_k, cache_mem_v, state_ssd, state_ssd_conv, state_s5_re, state_s5_im,
           state_hgrn, mem_prompt, norm_ffn1, ffn1_w1, ffn1_w3, ffn1_w2, norm_mix, w_in, ssd_conv_w, ssd_conv_b,
           ssd_dt_bias, ssd_a_log, ssd_d, ssd_norm, ssd_w_out, s5_a_re, s5_a_im, s5_log_dt, s5_b_re, s5_b_im,
           s5_c_re, s5_c_im, s5_d, s5_w_glu_a, s5_w_glu_b, hg_lower_bounds, hg_norm, hg_w_out, w_mix_out,
           norm_xa, norm_mem, xa_wq, xa_wk, xa_wv, xa_wo, norm_ffn2, ffn2_w1, ffn2_w3, ffn2_w2, norm_final):
    bp, lp, d = x_prompt.shape
    bs, ls, _ = x_sample.shape
    mp, ms = bp * lp, bs * ls
    x = jnp.concatenate([x_prompt.reshape(mp, d), x_sample.reshape(ms, d)], axis=0)
    mem = mem_prompt.reshape(bp * MEM_LEN, d)
    row = lambda v: v.reshape(1, -1)
    bf = lambda w: w.astype(BF16)

    lb_p = jax.nn.softmax(hg_lower_bounds, axis=0)
    lb_all = jnp.cumsum(lb_p, axis=0) - lb_p[0]

    ssd_s0 = state_ssd.reshape(DEPTH, bs, SSD_WIDTH, SSD_STATE)
    mem_k = _mem_rows(cache_mem_k)
    mem_v = _mem_rows(cache_mem_v)
    ssd_states = None
    hg_states = None

    f1_w1, f1_w3, f1_w2 = bf(ffn1_w1), bf(ffn1_w3), bf(ffn1_w2)
    f2_w1, f2_w3, f2_w2 = bf(ffn2_w1), bf(ffn2_w3), bf(ffn2_w2)
    w_p, w_dt = in_proj_weights(w_in)
    b_ssd, b_glu_a, b_glu_b, b_hg, b_mix = bf(ssd_w_out), bf(s5_w_glu_a), bf(s5_w_glu_b), bf(hg_w_out), bf(w_mix_out)
    b_wq, b_wo = bf(xa_wq), bf(xa_wo)
    b_wkv = bf(jnp.concatenate([xa_wk, xa_wv], axis=-1))

    outs = {k: [] for k in ("pk", "pv", "pss", "pcv", "psr", "psi", "phg", "scv", "ssr", "ssi")}
    for l in range(DEPTH):
        x = ffn(x, row(norm_ffn1[l]), f1_w1, f1_w3, f1_w2, row(norm_final), l, False)

        p, dt = in_proj(x, row(norm_mix[l]), w_p, w_dt, l)

        ssd_prm = {"conv_w": ssd_conv_w[l], "conv_b": row(ssd_conv_b[l]), "dt_bias": _pad_lanes(ssd_dt_bias[l]),
                   "a_log": _pad_lanes(ssd_a_log[l]), "d_skip": row(jnp.repeat(ssd_d[l], SSD_HEAD_DIM)),
                   "norm": row(ssd_norm[l])}
        ys, ss_p = ssd_prompt(p, dt, ssd_prm, bp, lp)
        ys, ssd_states = ssd_sample(p, dt, state_ssd_conv, ssd_s0, ys, ssd_states, ssd_prm, l, mp, bs, ls)

        s5_prm = _s5_params(s5_a_re[l], s5_a_im[l], s5_log_dt[l], s5_b_re[l], s5_b_im[l], s5_c_re[l],
                            s5_c_im[l], s5_d[l])
        gy, sr_p, si_p = s5_chunked(p, s5_prm, bp, lp)
        u_s = p[mp:, P_U:P_U + S5_WIDTH].reshape(bs // S5_SSEQ, S5_SSEQ, ls, S5_WIDTH).transpose(0, 2, 1, 3)
        u_s = u_s.reshape(ms, S5_WIDTH)
        gy_s, sr_s, si_s = s5_scan(u_s, state_s5_re[l].reshape(bs, S5_CH), state_s5_im[l].reshape(bs, S5_CH),
                                   s5_prm, bs // S5_SSEQ, S5_SSEQ, ls, ls)
        gy_s = gy_s.reshape(bs // S5_SSEQ, ls, S5_SSEQ, S5_WIDTH).transpose(0, 2, 1, 3).reshape(ms, S5_WIDTH)
        gy = lax.dynamic_update_slice(gy, gy_s.astype(F32), (mp, 0))

        hg_prm = {"lb": row(lb_all[l]), "norm": row(hg_norm[l])}
        o, hg_p = hg_prompt(p, hg_prm, bp, lp)
        o, hg_states = hg_sample(p, state_hgrn, o, hg_states, hg_prm, l, mp, bs, ls)

        x = branch_mix(x, ys, gy, o, p, b_ssd, b_glu_a, b_glu_b, b_hg, b_mix, l)

        q = norm_proj(x, row(norm_xa[l]), b_wq, l, BF16)
        kv = norm_proj(mem, row(norm_mem[l]), b_wkv, l, F32)
        at = xattn_prompt(q, kv, bp, lp)
        at = xattn_sample(q, mem_k, mem_v, at, l, mp, bs, ls)
        x = res_mm(x, at, b_wo, l)

        x = ffn(x, row(norm_ffn2[l]), f2_w1, f2_w3, f2_w2, row(norm_final), l, l == DEPTH - 1)

        tail = SSD_CONV - 1
        outs["pk"].append(kv[:, :d].reshape(bp, MEM_LEN, XA_HEADS, XA_HEAD_DIM))
        outs["pv"].append(kv[:, d:].reshape(bp, MEM_LEN, XA_HEADS, XA_HEAD_DIM))
        outs["pss"].append(ss_p.reshape(bp, SSD_HEADS, SSD_HEAD_DIM, SSD_STATE))
        outs["pcv"].append(jnp.stack([p[(b + 1) * lp - tail:(b + 1) * lp, P_XBC:P_XBC + SSD_CONV_DIM]
                                      for b in range(bp)]))
        outs["psr"].append(sr_p.reshape(bp, S5_GROUPS, S5_STATE))
        outs["psi"].append(si_p.reshape(bp, S5_GROUPS, S5_STATE))
        outs["phg"].append(hg_p)
        outs["scv"].append(p[mp:, P_XBC:P_XBC + SSD_CONV_DIM].reshape(bs, ls, SSD_CONV_DIM)[:, ls - tail:])
        outs["ssr"].append(sr_s.reshape(bs, S5_GROUPS, S5_STATE))
        outs["ssi"].append(si_s.reshape(bs, S5_GROUPS, S5_STATE))

    st = lambda k: jnp.stack(outs[k])
    return (x[:mp].reshape(bp, lp, d), x[mp:].reshape(bs, ls, d),
            st("pk"), st("pv"), st("pss"), st("pcv"), st("psr"), st("psi"), st("phg"),
            ssd_states.reshape(DEPTH, bs, SSD_HEADS, SSD_HEAD_DIM, SSD_STATE), st("scv"), st("ssr"), st("ssi"),
            hg_states)
```

```python
import functools
import math

import jax
import jax.numpy as jnp
import numpy as np
from jax import lax
from jax.experimental import pallas as pl
from jax.experimental.pallas import tpu as pltpu

F32 = jnp.float32
BF16 = jnp.bfloat16

D_MODEL = 2048
DEPTH = 2
NORM_EPS = 1e-5
SSD_HEAD_DIM = 64
SSD_HEADS = 32
SSD_GROUPS = 4
SSD_STATE = 128
SSD_CONV = 4
SSD_WIDTH = 2048
SSD_CONV_DIM = 3072
S5_WIDTH = 1024
S5_GROUP_SIZE = 16
S5_GROUPS = 64
S5_STATE = 64
HG_WIDTH = 1024
HG_HEADS = 8
HG_KEY_DIM = 128
HG_VAL_DIM = 128
MEM_LEN = 256
XA_HEADS = 4
XA_HEAD_DIM = 512
FFN_DIM = 5632

P_GATES, P_XBC, P_U, P_HQ, P_HF, P_HI, P_HGATE, P_Z = 0, 6144, 9216, 10240, 11264, 12288, 13312, 14336
P_WIDTH = 16384

V7X_VMEM_LIMIT = 56 * 1024 * 1024


def _cparams(sem, vmem=V7X_VMEM_LIMIT):
    return pltpu.CompilerParams(dimension_semantics=sem, vmem_limit_bytes=vmem)


def _rms_scale(x):
    return lax.rsqrt(jnp.mean(x * x, axis=-1, keepdims=True) + NORM_EPS)


def _silu(x):
    return x * jax.nn.sigmoid(x)


def _dot(a, b):
    return jnp.dot(a, b, preferred_element_type=F32)


def _dot_nt(a, b):
    return lax.dot_general(a, b, (((1,), (1,)), ((), ())), preferred_element_type=F32)


def _dot_tn(a, b):
    return lax.dot_general(a, b, (((0,), (0,)), ((), ())), preferred_element_type=F32)


def _split2(x):
    hi = x.astype(BF16)
    lo = (x - hi.astype(F32)).astype(BF16)
    return hi, lo


def _split3(x):
    hi = x.astype(BF16)
    r = x - hi.astype(F32)
    mid = r.astype(BF16)
    lo = (r - mid.astype(F32)).astype(BF16)
    return hi, mid, lo


def _sel_dot(sel, x, parts=3):
    ps = _split3(x) if parts == 3 else _split2(x)
    out = _dot(sel, ps[0])
    for p in ps[1:]:
        out = out + _dot(sel, p)
    return out


def _dot_sel(x, sel, parts=2):
    ps = _split3(x) if parts == 3 else _split2(x)
    out = _dot(ps[0], sel)
    for p in ps[1:]:
        out = out + _dot(p, sel)
    return out


def _norm_proj_body(x_ref, g_ref, w_ref, o_ref, h_ref):
    @pl.when(pl.program_id(1) == 0)
    def _():
        x = x_ref[...]
        h_ref[...] = (x * _rms_scale(x) * g_ref[...]).astype(BF16)

    o_ref[...] = _dot(h_ref[...], w_ref[...]).astype(o_ref.dtype)


def norm_proj(x, g, w, layer, out_dtype, bm=1024, bn=1024):
    m, k = x.shape
    n = w.shape[2]
    return pl.pallas_call(
        _norm_proj_body,
        grid=(m // bm, n // bn),
        in_specs=[pl.BlockSpec((bm, k), lambda i, j: (i, 0)),
                  pl.BlockSpec((1, k), lambda i, j: (0, 0)),
                  pl.BlockSpec((None, k, bn), lambda i, j: (layer, 0, j))],
        out_specs=pl.BlockSpec((bm, bn), lambda i, j: (i, j)),
        out_shape=jax.ShapeDtypeStruct((m, n), out_dtype),
        scratch_shapes=[pltpu.VMEM((bm, k), BF16)],
        compiler_params=_cparams(("parallel", "arbitrary")),
        name="norm_proj",
    )(x, g, w)


IN_SEGMENTS = (("z", 2048, P_Z), ("xbc", 3072, P_XBC), ("dt", SSD_HEADS, None), ("u", 1024, P_U),
               ("hq", 1024, P_HQ), ("hf", 1024, P_HF), ("hi", 1024, P_HI), ("hgate", 1024, P_HGATE),
               ("gates", 6144, P_GATES))
IN_DIM = sum(width for _, width, _ in IN_SEGMENTS)


def _in_weights_body(w_ref, o_ref, odt_ref):
    col = 0
    for _, width, off in IN_SEGMENTS:
        piece = w_ref[col:col + width, :].astype(BF16)
        if off is None:
            odt_ref[...] = jnp.zeros_like(odt_ref)
            odt_ref[0:width, :] = piece
        else:
            o_ref[off:off + width, :] = piece
        col += width


def in_proj_weights(w_in, lanes=256):
    wt = w_in.transpose(0, 2, 1)
    dd, n, k = wt.shape
    return pl.pallas_call(
        _in_weights_body,
        grid=(dd, k // lanes),
        in_specs=[pl.BlockSpec((None, n, lanes), lambda l, i: (l, 0, i))],
        out_specs=[pl.BlockSpec((None, P_WIDTH, lanes), lambda l, i: (l, 0, i)),
                   pl.BlockSpec((None, LANES, lanes), lambda l, i: (l, 0, i))],
        out_shape=[jax.ShapeDtypeStruct((dd, P_WIDTH, k), BF16), jax.ShapeDtypeStruct((dd, LANES, k), BF16)],
        compiler_params=_cparams(("parallel", "parallel")),
        name="in_proj_weights",
    )(wt)


def _in_proj_body(x_ref, g_ref, w_ref, wdt_ref, o_ref, odt_ref, h_ref):
    @pl.when(pl.program_id(1) == 0)
    def _():
        x = x_ref[...]
        h = (x * _rms_scale(x) * g_ref[...]).astype(BF16)
        h_ref[...] = h
        odt_ref[...] = _dot_nt(h, wdt_ref[...])

    o_ref[...] = _dot_nt(h_ref[...], w_ref[...])


def in_proj(x, g, w, wdt, layer, bm=1024, bn=1024):
    m, k = x.shape
    n = w.shape[1]
    ndt = wdt.shape[1]
    return pl.pallas_call(
        _in_proj_body,
        grid=(m // bm, n // bn),
        in_specs=[pl.BlockSpec((bm, k), lambda i, j: (i, 0)),
                  pl.BlockSpec((1, k), lambda i, j: (0, 0)),
                  pl.BlockSpec((None, bn, k), lambda i, j: (layer, j, 0)),
                  pl.BlockSpec((None, ndt, k), lambda i, j: (layer, 0, 0))],
        out_specs=[pl.BlockSpec((bm, bn), lambda i, j: (i, j)),
                   pl.BlockSpec((bm, ndt), lambda i, j: (i, 0))],
        out_shape=[jax.ShapeDtypeStruct((m, n), F32), jax.ShapeDtypeStruct((m, ndt), F32)],
        scratch_shapes=[pltpu.VMEM((bm, k), BF16)],
        compiler_params=_cparams(("parallel", "arbitrary")),
        name="in_proj",
    )(x, g, w, wdt)


def _res_mm_body(x_ref, a_ref, w_ref, o_ref):
    o_ref[...] = x_ref[...] + _dot(a_ref[...], w_ref[...])


def res_mm(x, a, w, layer, bm=1024, bn=1024):
    m, n = x.shape
    k = a.shape[1]
    return pl.pallas_call(
        _res_mm_body,
        grid=(m // bm, n // bn),
        in_specs=[pl.BlockSpec((bm, bn), lambda i, j: (i, j)),
                  pl.BlockSpec((bm, k), lambda i, j: (i, 0)),
                  pl.BlockSpec((None, k, bn), lambda i, j: (layer, 0, j))],
        out_specs=pl.BlockSpec((bm, bn), lambda i, j: (i, j)),
        out_shape=jax.ShapeDtypeStruct((m, n), F32),
        compiler_params=_cparams(("parallel", "parallel")),
        name="res_mm",
    )(x, a, w)


def _ffn_body(x_ref, g_ref, w1_ref, w3_ref, w2_ref, gf_ref, o_ref, h_ref, acc_ref, *, final_norm):
    f = pl.program_id(1)

    @pl.when(f == 0)
    def _():
        x = x_ref[...]
        h_ref[...] = (x * _rms_scale(x) * g_ref[...]).astype(BF16)
        acc_ref[...] = jnp.zeros_like(acc_ref)

    h = h_ref[...]
    a = _silu(_dot(h, w1_ref[...])) * _dot(h, w3_ref[...])
    acc_ref[...] += _dot(a.astype(BF16), w2_ref[...])

    @pl.when(f == pl.num_programs(1) - 1)
    def _():
        y = x_ref[...] + 0.5 * acc_ref[...]
        if final_norm:
            y = y * _rms_scale(y) * gf_ref[...]
        o_ref[...] = y


def ffn(x, g, w1, w3, w2, gf, layer, final_norm, bm=768, bf=512):
    m, d = x.shape
    fdim = w1.shape[2]
    return pl.pallas_call(
        functools.partial(_ffn_body, final_norm=final_norm),
        grid=(m // bm, fdim // bf),
        in_specs=[pl.BlockSpec((bm, d), lambda i, f: (i, 0)),
                  pl.BlockSpec((1, d), lambda i, f: (0, 0)),
                  pl.BlockSpec((None, d, bf), lambda i, f: (layer, 0, f)),
                  pl.BlockSpec((None, d, bf), lambda i, f: (layer, 0, f)),
                  pl.BlockSpec((None, bf, d), lambda i, f: (layer, f, 0)),
                  pl.BlockSpec((1, d), lambda i, f: (0, 0))],
        out_specs=pl.BlockSpec((bm, d), lambda i, f: (i, 0)),
        out_shape=jax.ShapeDtypeStruct((m, d), F32),
        scratch_shapes=[pltpu.VMEM((bm, d), BF16), pltpu.VMEM((bm, d), F32)],
        compiler_params=_cparams(("parallel", "arbitrary")),
        name="ffn",
    )(x, g, w1, w3, w2, gf)


def _mix_body(x_ref, ys_ref, gy_ref, o_ref, ga_ref, gb_ref, gc_ref, wssd_ref, wa_ref, wb_ref, whg_ref,
              wmix_ref, out_ref, acc_ref):
    j = pl.program_id(1)

    @pl.when(j == 0)
    def _():
        acc_ref[...] = jnp.zeros_like(acc_ref)

    gy = gy_ref[...].astype(BF16)
    y_a = _dot(ys_ref[...], wssd_ref[...])
    y_b = _dot(gy, wa_ref[...]) * jax.nn.sigmoid(_dot(gy, wb_ref[...]))
    y_c = _dot(o_ref[...], whg_ref[...])
    mix = (jax.nn.sigmoid(ga_ref[...]) * y_a + jax.nn.sigmoid(gb_ref[...]) * y_b
           + jax.nn.sigmoid(gc_ref[...]) * y_c)
    acc_ref[...] += _dot(mix.astype(BF16), wmix_ref[...])

    @pl.when(j == pl.num_programs(1) - 1)
    def _():
        out_ref[...] = x_ref[...] + acc_ref[...]


def branch_mix(x, ys, gy, o, p, wssd, wa, wb, whg, wmix, layer, bm=512, bn=512):
    m, d = x.shape
    nj = d // bn
    return pl.pallas_call(
        _mix_body,
        grid=(m // bm, nj),
        in_specs=[pl.BlockSpec((bm, d), lambda i, j: (i, 0)),
                  pl.BlockSpec((bm, ys.shape[1]), lambda i, j: (i, 0)),
                  pl.BlockSpec((bm, gy.shape[1]), lambda i, j: (i, 0)),
                  pl.BlockSpec((bm, o.shape[1]), lambda i, j: (i, 0)),
                  pl.BlockSpec((bm, bn), lambda i, j: (i, j)),
                  pl.BlockSpec((bm, bn), lambda i, j: (i, nj + j)),
                  pl.BlockSpec((bm, bn), lambda i, j: (i, 2 * nj + j)),
                  pl.BlockSpec((None, wssd.shape[1], bn), lambda i, j: (layer, 0, j)),
                  pl.BlockSpec((None, wa.shape[1], bn), lambda i, j: (layer, 0, j)),
                  pl.BlockSpec((None, wb.shape[1], bn), lambda i, j: (layer, 0, j)),
                  pl.BlockSpec((None, whg.shape[1], bn), lambda i, j: (layer, 0, j)),
                  pl.BlockSpec((None, bn, d), lambda i, j: (layer, j, 0))],
        out_specs=pl.BlockSpec((bm, d), lambda i, j: (i, 0)),
        out_shape=jax.ShapeDtypeStruct((m, d), F32),
        scratch_shapes=[pltpu.VMEM((bm, d), F32)],
        compiler_params=_cparams(("parallel", "arbitrary")),
        name="branch_mix",
    )(x, ys, gy, o, p, p, p, wssd, wa, wb, whg, wmix)


def _attend(q, k_head, v_head):
    outs = []
    for h in range(XA_HEADS):
        sl = slice(h * XA_HEAD_DIM, (h + 1) * XA_HEAD_DIM)
        s = _dot_nt(q[:, sl], k_head(h).astype(BF16)) * (XA_HEAD_DIM ** -0.5)
        s = s - jnp.max(s, axis=-1, keepdims=True)
        e = jnp.exp(s)
        p = e / jnp.sum(e, axis=-1, keepdims=True)
        outs.append(_dot(p.astype(BF16), v_head(h).astype(BF16)))
    return jnp.concatenate(outs, axis=-1)


def _xattn_prompt_body(q_ref, kv_ref, o_ref):
    k_head = lambda h: kv_ref[:, h * XA_HEAD_DIM:(h + 1) * XA_HEAD_DIM]
    v_head = lambda h: kv_ref[:, D_MODEL + h * XA_HEAD_DIM:D_MODEL + (h + 1) * XA_HEAD_DIM]
    o_ref[...] = _attend(q_ref[...], k_head, v_head).astype(BF16)


def xattn_prompt(q, kv, n_seq, seq_len, bl=512):
    nl = seq_len // bl
    return pl.pallas_call(
        _xattn_prompt_body,
        grid=(n_seq, nl),
        in_specs=[pl.BlockSpec((bl, D_MODEL), lambda b, i: (b * nl + i, 0)),
                  pl.BlockSpec((MEM_LEN, 2 * D_MODEL), lambda b, i: (b, 0))],
        out_specs=pl.BlockSpec((bl, D_MODEL), lambda b, i: (b * nl + i, 0)),
        out_shape=jax.ShapeDtypeStruct(q.shape, BF16),
        compiler_params=_cparams(("parallel", "arbitrary")),
        name="xattn_prompt",
    )(q, kv)


def _xattn_sample_body(q_ref, k_ref, v_ref, buf_ref, o_ref, *, seq_len):
    del buf_ref
    q = q_ref[...]
    r = 2 * seq_len
    nc = XA_HEAD_DIM // LANES
    piece = lambda ref, b, h, c: ref[b, pl.ds(c * XA_HEADS + h, MEM_LEN, stride=nc * XA_HEADS), :].astype(BF16)
    scores = []
    for b in range(2):
        for h in range(XA_HEADS):
            s = None
            for c in range(nc):
                lo = h * XA_HEAD_DIM + c * LANES
                d = _dot_nt(q[:, lo:lo + LANES], piece(k_ref, b, h, c))
                s = d if s is None else s + d
            scores.append(s)
    s = jnp.concatenate(scores, axis=0) * (XA_HEAD_DIM ** -0.5)
    s = s - jnp.max(s, axis=-1, keepdims=True)
    e = jnp.exp(s)
    p = (e / jnp.sum(e, axis=-1, keepdims=True)).astype(BF16)
    rows = lax.broadcasted_iota(jnp.int32, (r, LANES), 0)
    for h in range(XA_HEADS):
        for c in range(nc):
            lo = h * XA_HEAD_DIM + c * LANES
            o0 = _dot(p[h * r:(h + 1) * r], piece(v_ref, 0, h, c))
            o1 = _dot(p[(XA_HEADS + h) * r:(XA_HEADS + h + 1) * r], piece(v_ref, 1, h, c))
            o_ref[:, lo:lo + LANES] = jnp.where(rows < seq_len, o0, o1).astype(BF16)


def xattn_sample(q, k, v, buf, layer, row0, n_seq, seq_len):
    r = 2 * seq_len
    blk0 = row0 // r
    mem_spec = pl.BlockSpec((None, 2, MEM_LEN * D_MODEL // LANES, LANES), lambda i: (layer, i, 0, 0))
    return pl.pallas_call(
        functools.partial(_xattn_sample_body, seq_len=seq_len),
        grid=(n_seq // 2,),
        in_specs=[pl.BlockSpec((r, D_MODEL), lambda i: (blk0 + i, 0)), mem_spec, mem_spec,
                  pl.BlockSpec(memory_space=pl.ANY)],
        out_specs=pl.BlockSpec((r, D_MODEL), lambda i: (blk0 + i, 0)),
        out_shape=jax.ShapeDtypeStruct(buf.shape, BF16),
        input_output_aliases={3: 0},
        compiler_params=_cparams(("parallel",)),
        name="xattn_sample",
    )(q, k, v, buf)


LANES = 128
HALO = 8


def _softplus(x):
    return jnp.maximum(x, 0.0) + jnp.log1p(jnp.exp(-jnp.abs(x)))


def _pad_rows(x, rows):
    if x.shape[0] == rows:
        return x
    return jnp.concatenate([x, jnp.zeros((rows - x.shape[0], x.shape[1]), x.dtype)], axis=0)


def _tile_consts(r, t):
    i = np.arange(r)[:, None]
    j = np.arange(r)[None, :]
    same = (i // t) == (j // t)
    lt = (same & (j <= i)).astype(np.float32)
    last = (same & (j % t == t - 1)).astype(np.float32)
    return jnp.asarray(lt, BF16), jnp.asarray(last, BF16)


def _head_expand(n_heads, width, rows=LANES):
    e = np.zeros((rows, n_heads * width), np.float32)
    for h in range(n_heads):
        e[h, h * width:(h + 1) * width] = 1.0
    return jnp.asarray(e, BF16)


def _ssd_tile(xc, dt_raw, dtb, a_log, lt, last, e, e128, r, t):
    xs = xc[:, :SSD_WIDTH]
    dt = _softplus(dt_raw + dtb)
    d_a = dt * (-jnp.exp(a_log))
    a = _sel_dot(lt, d_a, 3)
    a_e = _dot_sel(a, e, 3)
    dt_e = _dot_sel(dt, e, 2)
    if t == r:
        alast_e = jnp.broadcast_to(a_e[r - 1:r, :], a_e.shape)
    else:
        alast_e = _sel_dot(last, a_e, 3)
    a_col = _dot_sel(a, e128, 3)
    a_t = _pad_rows(a, LANES).T

    row = lax.broadcasted_iota(jnp.int32, (r, LANES), 0)
    col = lax.broadcasted_iota(jnp.int32, (r, LANES), 1)
    valid = (col <= row) & (col >= (row // t) * t)
    lane_lo = col < SSD_HEAD_DIM

    xdt = _pad_rows((xs * dt_e).astype(BF16), LANES)
    hpg = SSD_HEADS // SSD_GROUPS
    ys = []
    for g in range(SSD_GROUPS):
        bg = xc[:, SSD_WIDTH + g * SSD_STATE:SSD_WIDTH + (g + 1) * SSD_STATE].astype(BF16)
        cg = xc[:, SSD_WIDTH + (SSD_GROUPS + g) * SSD_STATE:
                SSD_WIDTH + (SSD_GROUPS + g + 1) * SSD_STATE].astype(BF16)
        cb = _dot_nt(cg, _pad_rows(bg, LANES))
        for hp in range(hpg // 2):
            h0 = g * hpg + 2 * hp
            res = []
            for h in (h0, h0 + 1):
                rel = a_col[:, h * LANES:(h + 1) * LANES] - a_t[h:h + 1, :]
                dec = jnp.where(valid, jnp.exp(jnp.where(valid, rel, 0.0)), 0.0)
                res.append(_dot((cb * dec).astype(BF16), xdt[:, h0 * SSD_HEAD_DIM:(h0 + 2) * SSD_HEAD_DIM]))
            ys.append(jnp.where(lane_lo, res[0], res[1]))
    y_intra = jnp.concatenate(ys, axis=-1)
    return xs, y_intra, a_e, dt_e, alast_e


def _ssd_finish(y, xs, z, dsk, ng):
    y = y + dsk * xs
    y = y * _silu(z)
    return (y * _rms_scale(y) * ng).astype(BF16)


def _conv_silu(ext_ref, cw_ref, cb_ref, base, r):
    acc = cb_ref[...] + cw_ref[SSD_CONV - 1:SSD_CONV, :] * ext_ref[pl.ds(base, r), :]
    for k in range(1, SSD_CONV):
        acc = acc + cw_ref[SSD_CONV - 1 - k:SSD_CONV - k, :] * ext_ref[pl.ds(base - k, r), :]
    return _silu(acc)


def _ssd_prompt_body(xbc_ref, z_ref, dt_ref, cw_ref, cb_ref, dtb_ref, alog_ref, dsk_ref, ng_ref,
                     lt_ref, last_ref, e_ref, e128_ref, y_ref, sout_ref, ext_ref, st_ref, *, r):
    c = pl.program_id(1)

    @pl.when(c == 0)
    def _():
        ext_ref[0:HALO, :] = jnp.zeros((HALO, SSD_CONV_DIM), F32)
        st_ref[...] = jnp.zeros_like(st_ref)

    ext_ref[HALO:HALO + r, :] = xbc_ref[...]
    xc = _conv_silu(ext_ref, cw_ref, cb_ref, HALO, r)
    ext_ref[0:HALO, :] = xbc_ref[r - HALO:r, :]

    xs, y, a_e, dt_e, alast_e = _ssd_tile(xc, dt_ref[...], dtb_ref[...], alog_ref[...], lt_ref[...],
                                          last_ref[...], e_ref[...], e128_ref[...], r, r)
    ea_e = jnp.exp(a_e)
    xw = (xs * (dt_e * jnp.exp(alast_e - a_e))).astype(BF16)
    sdec = jnp.exp(alast_e[0:1, :])
    gw = SSD_WIDTH // SSD_GROUPS
    inter = []
    for g in range(SSD_GROUPS):
        bg = xc[:, SSD_WIDTH + g * SSD_STATE:SSD_WIDTH + (g + 1) * SSD_STATE].astype(BF16)
        cg = xc[:, SSD_WIDTH + (SSD_GROUPS + g) * SSD_STATE:
                SSD_WIDTH + (SSD_GROUPS + g + 1) * SSD_STATE].astype(BF16)
        st = st_ref[:, g * gw:(g + 1) * gw]
        inter.append(_dot(cg, st.astype(BF16)))
        st_ref[:, g * gw:(g + 1) * gw] = st * sdec[:, g * gw:(g + 1) * gw] + _dot_tn(bg, xw[:, g * gw:(g + 1) * gw])
    y = y + jnp.concatenate(inter, axis=-1) * ea_e
    y_ref[...] = _ssd_finish(y, xs, z_ref[...], dsk_ref[...], ng_ref[...])

    @pl.when(c == pl.num_programs(1) - 1)
    def _():
        sout_ref[0] = st_ref[...].T


def _ssd_consts(r, t):
    lt, last = _tile_consts(r, t)
    return lt, last, _head_expand(SSD_HEADS, SSD_HEAD_DIM), _head_expand(SSD_HEADS, LANES)


def ssd_prompt(p, dt, prm, n_seq, seq_len, r=128):
    nc = seq_len // r
    consts = _ssd_consts(r, r)
    full = lambda a: pl.BlockSpec(a.shape, lambda b, c: (0,) * a.ndim)
    small = [prm["conv_w"], prm["conv_b"], prm["dt_bias"], prm["a_log"], prm["d_skip"], prm["norm"], *consts]
    return pl.pallas_call(
        functools.partial(_ssd_prompt_body, r=r),
        grid=(n_seq, nc),
        in_specs=[pl.BlockSpec((r, SSD_CONV_DIM), lambda b, c: (b * nc + c, P_XBC // SSD_CONV_DIM)),
                  pl.BlockSpec((r, SSD_WIDTH), lambda b, c: (b * nc + c, P_Z // SSD_WIDTH)),
                  pl.BlockSpec((r, LANES), lambda b, c: (b * nc + c, 0)),
                  *[full(a) for a in small]],
        out_specs=[pl.BlockSpec((r, SSD_WIDTH), lambda b, c: (b * nc + c, 0)),
                   pl.BlockSpec((1, SSD_WIDTH, SSD_STATE), lambda b, c: (b, 0, 0))],
        out_shape=[jax.ShapeDtypeStruct((p.shape[0], SSD_WIDTH), BF16),
                   jax.ShapeDtypeStruct((n_seq, SSD_WIDTH, SSD_STATE), F32)],
        scratch_shapes=[pltpu.VMEM((HALO + r, SSD_CONV_DIM), F32), pltpu.VMEM((SSD_STATE, SSD_WIDTH), F32)],
        compiler_params=_cparams(("parallel", "arbitrary")),
        name="ssd_prompt",
    )(p, p, dt, *small)


def _ssd_sample_body(xbc_ref, z_ref, dt_ref, buf_ref, s0_ref, cw_ref, cb_ref, dtb_ref, alog_ref, dsk_ref,
                     ng_ref, lt_ref, last_ref, e_ref, e128_ref, *rest, r, t):
    y_ref, sout_ref, ext_ref = rest[-3:]
    nb = r // t
    pitch = HALO + t
    for b in range(nb):
        ext_ref[b * pitch + HALO - (SSD_CONV - 1):b * pitch + HALO, :] = buf_ref[b]
        ext_ref[b * pitch + HALO:(b + 1) * pitch, :] = xbc_ref[b * t:(b + 1) * t, :]
    xc = jnp.concatenate([_conv_silu(ext_ref, cw_ref, cb_ref, b * pitch + HALO, t) for b in range(nb)], axis=0)

    xs, y, a_e, dt_e, alast_e = _ssd_tile(xc, dt_ref[...], dtb_ref[...], alog_ref[...], lt_ref[...],
                                          last_ref[...], e_ref[...], e128_ref[...], r, t)
    ea_e = jnp.exp(a_e)
    xw = xs * (dt_e * jnp.exp(alast_e - a_e))
    sdec = jnp.exp(alast_e)
    gw = SSD_WIDTH // SSD_GROUPS
    pr = 2 * t
    prow = lax.broadcasted_iota(jnp.int32, (pr, 1), 0)
    ones = jnp.ones((pr, SSD_STATE), BF16)
    inter = []
    for g in range(SSD_GROUPS):
        bg = xc[:, SSD_WIDTH + g * SSD_STATE:SSD_WIDTH + (g + 1) * SSD_STATE].astype(BF16)
        cg = xc[:, SSD_WIDTH + (SSD_GROUPS + g) * SSD_STATE:
                SSD_WIDTH + (SSD_GROUPS + g + 1) * SSD_STATE].astype(BF16)
        cols = slice(g * gw, (g + 1) * gw)
        rows_out = []
        for q in range(nb // 2):
            rs = slice(q * pr, (q + 1) * pr)
            acc = None
            for s in range(2):
                b = 2 * q + s
                mine = (prow >= s * t) & (prow < (s + 1) * t)
                s0 = s0_ref[b, cols, :]
                yi = _dot_nt(cg[rs], s0.astype(BF16))
                acc = jnp.where(mine, yi, 0.0) if acc is None else acc + jnp.where(mine, yi, 0.0)
                upd = _dot_tn(jnp.where(mine, xw[rs, cols], 0.0).astype(BF16), bg[rs])
                lastrow = prow == (s + 1) * t - 1
                dh, dl = _split2(jnp.where(lastrow, sdec[rs, cols], 0.0))
                dcol = _dot_tn(dh, ones) + _dot_tn(dl, ones)
                sout_ref[b, cols, :] = s0 * dcol + upd
            rows_out.append(acc)
        inter.append(jnp.concatenate(rows_out, axis=0))
    y = y + jnp.concatenate(inter, axis=-1) * ea_e
    y_ref[...] = _ssd_finish(y, xs, z_ref[...], dsk_ref[...], ng_ref[...])


def ssd_sample(p, dt, conv_buf, s0, ybuf, sprev, prm, layer, row0, n_seq, seq_len, r=64):
    nb = r // seq_len
    blk0 = row0 // r
    consts = _ssd_consts(r, seq_len)
    full = lambda a: pl.BlockSpec(a.shape, lambda i: (0,) * a.ndim)
    small = [prm["conv_w"], prm["conv_b"], prm["dt_bias"], prm["a_log"], prm["d_skip"], prm["norm"], *consts]
    inplace = [ybuf] if sprev is None else [ybuf, sprev]
    n_in = 5 + len(small)
    return pl.pallas_call(
        functools.partial(_ssd_sample_body, r=r, t=seq_len),
        grid=(n_seq // nb,),
        in_specs=[pl.BlockSpec((r, SSD_CONV_DIM), lambda i: (blk0 + i, P_XBC // SSD_CONV_DIM)),
                  pl.BlockSpec((r, SSD_WIDTH), lambda i: (blk0 + i, P_Z // SSD_WIDTH)),
                  pl.BlockSpec((r, LANES), lambda i: (blk0 + i, 0)),
                  pl.BlockSpec((None, nb, SSD_CONV - 1, SSD_CONV_DIM), lambda i: (layer, i, 0, 0)),
                  pl.BlockSpec((None, nb, SSD_WIDTH, SSD_STATE), lambda i: (layer, i, 0, 0)),
                  *[full(a) for a in small],
                  *[pl.BlockSpec(memory_space=pl.ANY) for _ in inplace]],
        out_specs=[pl.BlockSpec((r, SSD_WIDTH), lambda i: (blk0 + i, 0)),
                   pl.BlockSpec((None, nb, SSD_WIDTH, SSD_STATE), lambda i: (layer, i, 0, 0))],
        out_shape=[jax.ShapeDtypeStruct(ybuf.shape, BF16),
                   jax.ShapeDtypeStruct(s0.shape, F32)],
        input_output_aliases={n_in + k: k for k in range(len(inplace))},
        scratch_shapes=[pltpu.VMEM((nb * (HALO + seq_len), SSD_CONV_DIM), F32)],
        compiler_params=_cparams(("parallel",)),
        name="ssd_sample",
    )(p, p, dt, conv_buf, s0, *small, *inplace)


S5_CH = S5_GROUPS * S5_STATE
S5_BLK = 4
S5_SSEQ = 32


def _s5_in(u, bre_ref, bim_ref):
    ub = u.astype(BF16)
    kin = S5_WIDTH // S5_BLK
    re = [_dot(ub[:, q * kin:(q + 1) * kin], bre_ref[q]) for q in range(S5_BLK)]
    im = [_dot(ub[:, q * kin:(q + 1) * kin], bim_ref[q]) for q in range(S5_BLK)]
    return jnp.concatenate(re, axis=-1), jnp.concatenate(im, axis=-1)


def _s5_out(h_re, h_im, u, cre_ref, cim_ref, dsk):
    kst = S5_CH // S5_BLK
    hr = h_re.astype(BF16)
    hi = h_im.astype(BF16)
    y = [_dot(hr[:, q * kst:(q + 1) * kst], cre_ref[q]) - _dot(hi[:, q * kst:(q + 1) * kst], cim_ref[q])
         for q in range(S5_BLK)]
    y = jnp.concatenate(y, axis=-1) + dsk * u
    return jax.nn.gelu(y).astype(BF16)


def _s5_scan_body(u_ref, h0re_ref, h0im_ref, are_ref, aim_ref, bre_ref, bim_ref, cre_ref, cim_ref, dsk_ref,
                  y_ref, hre_ref, him_ref, sre_ref, sim_ref, *, s, tc):
    c = pl.program_id(1)

    @pl.when(c == 0)
    def _():
        sre_ref[0:s, :] = h0re_ref[...]
        sim_ref[0:s, :] = h0im_ref[...]

    u = u_ref[...]
    bu_re, bu_im = _s5_in(u, bre_ref, bim_ref)
    sre_ref[s:, :] = bu_re
    sim_ref[s:, :] = bu_im
    a_re = jnp.broadcast_to(are_ref[...], (s, S5_CH))
    a_im = jnp.broadcast_to(aim_ref[...], (s, S5_CH))

    def step(t, carry):
        prev = pl.ds(pl.multiple_of(t * s, s), s)
        cur = pl.ds(pl.multiple_of((t + 1) * s, s), s)
        h_re = sre_ref[prev, :]
        h_im = sim_ref[prev, :]
        sre_ref[cur, :] = a_re * h_re - a_im * h_im + sre_ref[cur, :]
        sim_ref[cur, :] = a_re * h_im + a_im * h_re + sim_ref[cur, :]
        return carry

    lax.fori_loop(0, tc, step, 0, unroll=4)
    h_re = sre_ref[tc * s:, :]
    h_im = sim_ref[tc * s:, :]
    y_ref[...] = _s5_out(sre_ref[s:, :], sim_ref[s:, :], u, cre_ref, cim_ref, dsk_ref[...])
    sre_ref[0:s, :] = h_re
    sim_ref[0:s, :] = h_im
    hre_ref[...] = h_re
    him_ref[...] = h_im


def s5_scan(u, h0_re, h0_im, prm, n_blocks, s, n_steps, tc):
    nc = n_steps // tc
    r = tc * s
    full = lambda a: pl.BlockSpec(a.shape, lambda b, c: (0,) * a.ndim)
    small = [prm["abar_re"], prm["abar_im"], prm["b_re"], prm["b_im"], prm["c_re"], prm["c_im"], prm["d_skip"]]
    state = jax.ShapeDtypeStruct((n_blocks * s, S5_CH), F32)
    sspec = pl.BlockSpec((s, S5_CH), lambda b, c: (b, 0))
    return pl.pallas_call(
        functools.partial(_s5_scan_body, s=s, tc=tc),
        grid=(n_blocks, nc),
        in_specs=[pl.BlockSpec((r, S5_WIDTH), lambda b, c: (b * nc + c, 0)), sspec, sspec,
                  *[full(a) for a in small]],
        out_specs=[pl.BlockSpec((r, S5_WIDTH), lambda b, c: (b * nc + c, 0)), sspec, sspec],
        out_shape=[jax.ShapeDtypeStruct((u.shape[0], S5_WIDTH), BF16), state, state],
        scratch_shapes=[pltpu.VMEM((s + r, S5_CH), F32), pltpu.VMEM((s + r, S5_CH), F32)],
        compiler_params=_cparams(("parallel", "arbitrary")),
        name="s5_scan",
    )(u, h0_re, h0_im, *small)


S5_T = 16
S5_GB = LANES // S5_GROUP_SIZE
S5_CL = S5_T * LANES
S5_SL = S5_GB * S5_STATE
S5_NB = 2
S5_PITCH_PAD = 8


def _block_diag_rows(x, rows_w, cols_w):
    xt = jnp.concatenate([x] * S5_GB, axis=0)
    r = lax.broadcasted_iota(jnp.int32, xt.shape, 0) // rows_w
    c = lax.broadcasted_iota(jnp.int32, xt.shape, 1) // cols_w
    return jnp.where(r == c, xt, jnp.zeros_like(xt))


def _s5_chunk_body(u_ref, kj_ref, bcre_ref, bcim_ref, ccre_ref, ccim_ref, are_ref, aim_ref, dsk_ref,
                   y_ref, hre_ref, him_ref, vre_ref, vim_ref, sre_ref, sim_ref,
                   tp_ref, bre_ref, bim_ref, cre_ref, cim_ref, *, n_chunks):
    ks = S5_GROUP_SIZE

    @pl.when(pl.program_id(1) == 0)
    def _():
        zero = jnp.zeros((LANES, LANES), BF16)
        for tau in range(S5_T):
            piece = _block_diag_rows(kj_ref[0, tau], ks, ks)
            for s in range(S5_T - tau):
                tp_ref[s * LANES:(s + 1) * LANES, (s + tau) * LANES:(s + tau + 1) * LANES] = piece
        for s in range(1, S5_T):
            for t in range(s):
                tp_ref[s * LANES:(s + 1) * LANES, t * LANES:(t + 1) * LANES] = zero
        for s in range(S5_T):
            bre_ref[s * LANES:(s + 1) * LANES, :] = _block_diag_rows(bcre_ref[0, s], ks, S5_STATE)
            bim_ref[s * LANES:(s + 1) * LANES, :] = _block_diag_rows(bcim_ref[0, s], ks, S5_STATE)
            cre_ref[:, s * LANES:(s + 1) * LANES] = _block_diag_rows(ccre_ref[0, s], S5_STATE, ks)
            cim_ref[:, s * LANES:(s + 1) * LANES] = _block_diag_rows(ccim_ref[0, s], S5_STATE, ks)

    rows = S5_NB * n_chunks
    pitch = n_chunks + S5_PITCH_PAD
    nk = S5_SL // LANES
    u = jnp.concatenate([u_ref[pl.ds(t, rows, stride=S5_T), :] for t in range(S5_T)], axis=-1)
    ub = u.astype(BF16)
    v_re = _dot(ub, bre_ref[...])
    v_im = _dot(ub, bim_ref[...])
    for k in range(nk):
        vre_ref[k] = v_re[:, k * LANES:(k + 1) * LANES]
        vim_ref[k] = v_im[:, k * LANES:(k + 1) * LANES]
        sre_ref[k] = jnp.zeros((S5_NB * pitch, LANES), F32)
        sim_ref[k] = jnp.zeros((S5_NB * pitch, LANES), F32)
    a_re = [jnp.broadcast_to(are_ref[0, :, k * LANES:(k + 1) * LANES], (S5_NB, LANES)) for k in range(nk)]
    a_im = [jnp.broadcast_to(aim_ref[0, :, k * LANES:(k + 1) * LANES], (S5_NB, LANES)) for k in range(nk)]

    def step(c, carry):
        for k in range(nk):
            h_re = sre_ref[k, pl.ds(c, S5_NB, stride=pitch), :]
            h_im = sim_ref[k, pl.ds(c, S5_NB, stride=pitch), :]
            x_re = vre_ref[k, pl.ds(c, S5_NB, stride=n_chunks), :]
            x_im = vim_ref[k, pl.ds(c, S5_NB, stride=n_chunks), :]
            sre_ref[k, pl.ds(c + 1, S5_NB, stride=pitch), :] = a_re[k] * h_re - a_im[k] * h_im + x_re
            sim_ref[k, pl.ds(c + 1, S5_NB, stride=pitch), :] = a_re[k] * h_im + a_im[k] * h_re + x_im
        return carry

    lax.fori_loop(0, n_chunks, step, 0, unroll=2)
    starts = lambda ref: jnp.concatenate(
        [jnp.concatenate([ref[k, b * pitch:b * pitch + n_chunks, :] for b in range(S5_NB)], axis=0)
         for k in range(nk)], axis=-1)
    y = _dot(ub, tp_ref[...])
    y = y + _dot(starts(sre_ref).astype(BF16), cre_ref[...]) - _dot(starts(sim_ref).astype(BF16), cim_ref[...])
    y = jax.nn.gelu(y + dsk_ref[0] * u)
    for t in range(S5_T):
        y_ref[pl.ds(t, rows, stride=S5_T), :] = y[:, t * LANES:(t + 1) * LANES]
    final = lambda ref: jnp.concatenate(
        [jnp.concatenate([ref[k, b * pitch + n_chunks:b * pitch + n_chunks + 1, :] for b in range(S5_NB)], axis=0)
         for k in range(nk)], axis=-1)
    hre_ref[...] = final(sre_ref)
    him_ref[...] = final(sim_ref)


def s5_chunked(p, prm, n_seq, seq_len):
    n_chunks = seq_len // S5_T
    rows = S5_NB * n_chunks
    nblk = S5_GROUPS // S5_GB
    nhalf = n_seq // S5_NB
    pitch = n_chunks + S5_PITCH_PAD
    nk = S5_SL // LANES
    blk = lambda a: pl.BlockSpec((1,) + a.shape[1:], lambda j, i: (j,) + (0,) * (a.ndim - 1))
    small = [prm["kern_j"], prm["bst_re"], prm["bst_im"], prm["cst_re"], prm["cst_im"], prm["apow_re"],
             prm["apow_im"], prm["d_tiled"]]
    state = jax.ShapeDtypeStruct((nhalf, S5_NB, S5_CH), F32)
    sspec = pl.BlockSpec((None, S5_NB, S5_SL), lambda j, i: (i, 0, j))
    return pl.pallas_call(
        functools.partial(_s5_chunk_body, n_chunks=n_chunks),
        grid=(nblk, nhalf),
        in_specs=[pl.BlockSpec((S5_NB * seq_len, LANES), lambda j, i: (i, P_U // LANES + j)),
                  *[blk(a) for a in small]],
        out_specs=[pl.BlockSpec((S5_NB * seq_len, LANES), lambda j, i: (i, j)), sspec, sspec],
        out_shape=[jax.ShapeDtypeStruct((p.shape[0], S5_WIDTH), F32), state, state],
        scratch_shapes=[pltpu.VMEM((nk, rows, LANES), F32), pltpu.VMEM((nk, rows, LANES), F32),
                        pltpu.VMEM((nk, S5_NB * pitch, LANES), F32), pltpu.VMEM((nk, S5_NB * pitch, LANES), F32),
                        pltpu.VMEM((S5_CL, S5_CL), BF16), pltpu.VMEM((S5_CL, S5_SL), BF16),
                        pltpu.VMEM((S5_CL, S5_SL), BF16), pltpu.VMEM((S5_SL, S5_CL), BF16),
                        pltpu.VMEM((S5_SL, S5_CL), BF16)],
        compiler_params=_cparams(("parallel", "arbitrary")),
        name="s5_chunked",
    )(p, *small)


def _hg_consts(r, t):
    i = np.arange(r)[:, None]
    j = np.arange(r)[None, :]
    sums, upper, pair = [], [], []
    s = 1
    while s < t:
        blk_i, blk_j = i // (2 * s), j // (2 * s)
        up_i = (i % (2 * s)) >= s
        mid_i = blk_i * 2 * s + s
        m_up = up_i & (j >= mid_i) & (j <= i)
        m_lo = (~up_i) & (j > i) & (j < mid_i)
        sums.append((m_up | m_lo).astype(np.float32))
        upper.append(np.broadcast_to(up_i, (r, 1)).astype(np.float32))
        pair.append(((blk_i == blk_j) & up_i & ((j % (2 * s)) < s)).astype(np.float32))
        s *= 2
    pair.append((i == j).astype(np.float32))
    return (jnp.asarray(np.stack(sums), BF16), jnp.asarray(np.stack(upper), F32),
            jnp.asarray(np.stack(pair), F32))


def _hg_gates(hf, lb):
    logf = -_softplus(-hf) + jnp.log1p(lb * jnp.exp(-hf))
    kk = (1.0 - lb) * jax.nn.sigmoid(-hf)
    return logf, kk


def _hg_intra(q, kk, v, logf, b, sums_ref, upper_ref, pair_ref):
    r = q.shape[0]
    nlev = sums_ref.shape[0]
    lf3 = _split3(logf)
    qb = q.astype(BF16)
    kb = kk.astype(BF16)
    vb = _pad_rows(v.astype(BF16), LANES) if r < LANES else v.astype(BF16)
    scores = [None] * HG_HEADS
    for lev in range(nlev + 1):
        if lev < nlev:
            upper = upper_ref[lev] > 0.5
            half = 2 ** lev
            if half >= HALO:
                blocks = b.reshape(r // (2 * half), 2 * half, HG_WIDTH)
                ref = jnp.broadcast_to(blocks[:, half - 1:half, :], blocks.shape).reshape(r, HG_WIDTH)
                d = jnp.where(upper, b - ref, ref - b)
            else:
                m = sums_ref[lev]
                d = _dot(m, lf3[0]) + _dot(m, lf3[1]) + _dot(m, lf3[2])
            x = (jnp.where(upper, q, kk) * jnp.exp(d)).astype(BF16)
            xq, xk = x, x
        else:
            xq, xk = qb, kb
        mask = pair_ref[lev]
        for h in range(HG_HEADS):
            sl = slice(h * HG_KEY_DIM, (h + 1) * HG_KEY_DIM)
            sc = _dot_nt(xq[:, sl], xk[:, sl]) * mask
            scores[h] = sc if scores[h] is None else scores[h] + sc
    outs = [_dot(scores[h].astype(BF16), vb[:, h * HG_VAL_DIM:(h + 1) * HG_VAL_DIM]) for h in range(HG_HEADS)]
    return jnp.concatenate(outs, axis=-1)


def _hg_finish(o, hgate, ng):
    outs = []
    for h in range(HG_HEADS):
        oh = o[:, h * HG_VAL_DIM:(h + 1) * HG_VAL_DIM]
        outs.append(oh * _rms_scale(oh) * ng)
    return (jnp.concatenate(outs, axis=-1) * _silu(hgate)).astype(BF16)


def _hg_prompt_body(q_ref, f_ref, i_ref, gate_ref, lb_ref, ng_ref, lt_ref, last_ref, sums_ref, upper_ref,
                    pair_ref, o_ref, sout_ref, *, r):
    c = pl.program_id(1)

    @pl.when(c == 0)
    def _():
        sout_ref[...] = jnp.zeros_like(sout_ref)

    q = q_ref[...]
    v = i_ref[...]
    logf, kk = _hg_gates(f_ref[...], lb_ref[...])
    b = _sel_dot(lt_ref[...], logf, 3)
    o = _hg_intra(q, kk, v, logf, b, sums_ref, upper_ref, pair_ref)
    blast = b[r - 1:r, :]
    qe = (q * jnp.exp(b)).astype(BF16)
    kw = (kk * jnp.exp(blast - b)).astype(BF16)
    vb = v.astype(BF16)
    ones = jnp.ones((r, HG_VAL_DIM), BF16)
    rows = lax.broadcasted_iota(jnp.int32, (r, 1), 0)
    dh, dl = _split2(jnp.where(rows == r - 1, jnp.exp(b), 0.0))
    inter = []
    for h in range(HG_HEADS):
        sl = slice(h * HG_KEY_DIM, (h + 1) * HG_KEY_DIM)
        s = sout_ref[0, h]
        inter.append(_dot(qe[:, sl], s.astype(BF16)))
        dcol = _dot_tn(dh[:, sl], ones) + _dot_tn(dl[:, sl], ones)
        sout_ref[0, h] = s * dcol + _dot_tn(kw[:, sl], vb[:, sl])
    o = o + jnp.concatenate(inter, axis=-1)
    o_ref[...] = _hg_finish(o, gate_ref[...], ng_ref[...])


def hg_prompt(p, prm, n_seq, seq_len, r=128):
    nc = seq_len // r
    lt, last = _tile_consts(r, r)
    consts = [lt, last, *_hg_consts(r, r)]
    full = lambda a: pl.BlockSpec(a.shape, lambda b, c: (0,) * a.ndim)
    small = [prm["lb"], prm["norm"], *consts]
    col = lambda off: pl.BlockSpec((r, HG_WIDTH), lambda b, c: (b * nc + c, off // HG_WIDTH))
    return pl.pallas_call(
        functools.partial(_hg_prompt_body, r=r),
        grid=(n_seq, nc),
        in_specs=[col(P_HQ), col(P_HF), col(P_HI), col(P_HGATE), *[full(a) for a in small]],
        out_specs=[pl.BlockSpec((r, HG_WIDTH), lambda b, c: (b * nc + c, 0)),
                   pl.BlockSpec((1, HG_HEADS, HG_KEY_DIM, HG_VAL_DIM), lambda b, c: (b, 0, 0, 0))],
        out_shape=[jax.ShapeDtypeStruct((p.shape[0], HG_WIDTH), BF16),
                   jax.ShapeDtypeStruct((n_seq, HG_HEADS, HG_KEY_DIM, HG_VAL_DIM), F32)],
        compiler_params=_cparams(("parallel", "arbitrary")),
        name="hg_prompt",
    )(p, p, p, p, *small)


def _hg_sample_body(q_ref, f_ref, i_ref, gate_ref, s0_ref, lb_ref, ng_ref, lt_ref, last_ref, sums_ref,
                    upper_ref, pair_ref, *rest, r, t):
    o_ref, sout_ref = rest[-2:]
    q = q_ref[...]
    v = i_ref[...]
    logf, kk = _hg_gates(f_ref[...], lb_ref[...])
    b = _sel_dot(lt_ref[...], logf, 3)
    o = _hg_intra(q, kk, v, logf, b, sums_ref, upper_ref, pair_ref)
    blast = _sel_dot(last_ref[...], b, 3)
    qe = (q * jnp.exp(b)).astype(BF16)
    kw = kk * jnp.exp(blast - b)
    sdec = jnp.exp(blast)
    vb = v.astype(BF16)
    pr = 2 * t
    prow = lax.broadcasted_iota(jnp.int32, (pr, 1), 0)
    ones = jnp.ones((pr, HG_VAL_DIM), BF16)
    rows_out = []
    for p2 in range(r // pr):
        rs = slice(p2 * pr, (p2 + 1) * pr)
        heads = []
        for h in range(HG_HEADS):
            sl = slice(h * HG_KEY_DIM, (h + 1) * HG_KEY_DIM)
            acc = None
            for s in range(2):
                bi = 2 * p2 + s
                mine = (prow >= s * t) & (prow < (s + 1) * t)
                s0 = s0_ref[bi, h]
                oi = jnp.where(mine, _dot(qe[rs, sl], s0.astype(BF16)), 0.0)
                acc = oi if acc is None else acc + oi
                upd = _dot_tn(jnp.where(mine, kw[rs, sl], 0.0).astype(BF16), vb[rs, sl])
                dh, dl = _split2(jnp.where(prow == (s + 1) * t - 1, sdec[rs, sl], 0.0))
                dcol = _dot_tn(dh, ones) + _dot_tn(dl, ones)
                sout_ref[bi, h] = s0 * dcol + upd
            heads.append(acc)
        rows_out.append(jnp.concatenate(heads, axis=-1))
    o = o + jnp.concatenate(rows_out, axis=0)
    o_ref[...] = _hg_finish(o, gate_ref[...], ng_ref[...])


def hg_sample(p, s0, obuf, sprev, prm, layer, row0, n_seq, seq_len, r=128):
    nb = r // seq_len
    blk0 = row0 // r
    lt, last = _tile_consts(r, seq_len)
    consts = [lt, last, *_hg_consts(r, seq_len)]
    full = lambda a: pl.BlockSpec(a.shape, lambda i: (0,) * a.ndim)
    small = [prm["lb"], prm["norm"], *consts]
    col = lambda off: pl.BlockSpec((r, HG_WIDTH), lambda i: (blk0 + i, off // HG_WIDTH))
    sspec = pl.BlockSpec((None, nb, HG_HEADS, HG_KEY_DIM, HG_VAL_DIM), lambda i: (layer, i, 0, 0, 0))
    inplace = [obuf] if sprev is None else [obuf, sprev]
    n_in = 5 + len(small)
    return pl.pallas_call(
        functools.partial(_hg_sample_body, r=r, t=seq_len),
        grid=(n_seq // nb,),
        in_specs=[col(P_HQ), col(P_HF), col(P_HI), col(P_HGATE), sspec, *[full(a) for a in small],
                  *[pl.BlockSpec(memory_space=pl.ANY) for _ in inplace]],
        out_specs=[pl.BlockSpec((r, HG_WIDTH), lambda i: (blk0 + i, 0)), sspec],
        out_shape=[jax.ShapeDtypeStruct(obuf.shape, BF16), jax.ShapeDtypeStruct(s0.shape, F32)],
        input_output_aliases={n_in + k: k for k in range(len(inplace))},
        compiler_params=_cparams(("parallel",)),
        name="hg_sample",
    )(p, p, p, p, s0, *small, *inplace)


def _s5_params(a_re, a_im, log_dt, b_re, b_im, c_re, c_im, d_skip):
    dt = jnp.exp(log_dt)[:, None]
    mag = jnp.exp(a_re * dt)
    abar_re = mag * jnp.cos(a_im * dt)
    abar_im = mag * jnp.sin(a_im * dt)
    den = a_re * a_re + a_im * a_im
    nr = abar_re - 1.0
    coef_re = (nr * a_re + abar_im * a_im) / den
    coef_im = (abar_im * a_re - nr * a_im) / den
    bbar_re = coef_re[..., None] * b_re - coef_im[..., None] * b_im
    bbar_im = coef_re[..., None] * b_im + coef_im[..., None] * b_re
    gpb = S5_GROUPS // S5_BLK
    eye = jnp.eye(gpb, dtype=F32)

    def in_blocks(bbar):
        bb = bbar.reshape(S5_BLK, gpb, S5_STATE, S5_GROUP_SIZE)
        return jnp.einsum("qgnk,gh->qgkhn", bb, eye).reshape(
            S5_BLK, gpb * S5_GROUP_SIZE, gpb * S5_STATE).astype(BF16)

    def out_blocks(c):
        cc = c.reshape(S5_BLK, gpb, S5_GROUP_SIZE, S5_STATE)
        return jnp.einsum("qgkn,gh->qgnhk", cc, eye).reshape(
            S5_BLK, gpb * S5_STATE, gpb * S5_GROUP_SIZE).astype(BF16)

    prm = {
        "abar_re": abar_re.reshape(1, S5_CH), "abar_im": abar_im.reshape(1, S5_CH),
        "b_re": in_blocks(bbar_re), "b_im": in_blocks(bbar_im),
        "c_re": out_blocks(c_re), "c_im": out_blocks(c_im),
        "d_skip": d_skip.reshape(1, S5_WIDTH),
    }

    tau = jnp.arange(S5_T + 1, dtype=F32)[:, None, None]
    pmag = jnp.exp(tau * (a_re * dt))
    pw_re = pmag * jnp.cos(tau * (a_im * dt))
    pw_im = pmag * jnp.sin(tau * (a_im * dt))
    cp_re = c_re[None] * pw_re[:, :, None, :] - c_im[None] * pw_im[:, :, None, :]
    cp_im = c_re[None] * pw_im[:, :, None, :] + c_im[None] * pw_re[:, :, None, :]
    bt_re = bbar_re.transpose(0, 2, 1)[None, :, None]
    bt_im = bbar_im.transpose(0, 2, 1)[None, :, None]
    kern = jnp.sum(cp_re[:, :, :, None, :] * bt_re - cp_im[:, :, :, None, :] * bt_im, axis=-1)
    nq = S5_GROUPS // S5_GB
    ks = S5_GROUP_SIZE
    kern_j = (kern[:S5_T].reshape(S5_T, nq, S5_GB, ks, ks).transpose(1, 0, 4, 2, 3)
              .reshape(nq, S5_T, ks, LANES).astype(BF16))
    rev = pw_re[S5_T - 1 - jnp.arange(S5_T)], pw_im[S5_T - 1 - jnp.arange(S5_T)]
    bst_re = rev[0][..., None] * bbar_re[None] - rev[1][..., None] * bbar_im[None]
    bst_im = rev[0][..., None] * bbar_im[None] + rev[1][..., None] * bbar_re[None]

    def bst_blocks(b):
        return (b.reshape(S5_T, nq, S5_GB, S5_STATE, ks).transpose(1, 0, 4, 2, 3)
                .reshape(nq, S5_T, ks, S5_SL).astype(BF16))

    def cst_blocks(c):
        return (c.reshape(S5_T, nq, S5_GB, ks, S5_STATE).transpose(1, 0, 4, 2, 3)
                .reshape(nq, S5_T, S5_STATE, LANES).astype(BF16))

    prm.update({
        "kern_j": kern_j,
        "bst_re": bst_blocks(bst_re), "bst_im": bst_blocks(bst_im),
        "cst_re": cst_blocks(cp_re[1:]), "cst_im": cst_blocks(cp_im[1:]),
        "apow_re": pw_re[S5_T].reshape(nq, 1, S5_SL), "apow_im": pw_im[S5_T].reshape(nq, 1, S5_SL),
        "d_tiled": jnp.broadcast_to(d_skip.reshape(nq, 1, 1, LANES), (nq, 1, S5_T, LANES)).reshape(nq, 1, S5_CL),
    })
    return prm


def _mem_rows(cache):
    dd, b, m, h, hd = cache.shape
    c = cache.reshape(dd, b, m, h, hd // LANES, LANES).transpose(0, 1, 2, 4, 3, 5)
    return c.reshape(dd, b, m * h * (hd // LANES), LANES)


def _pad_lanes(v):
    return jnp.pad(v.reshape(1, -1), ((0, 0), (0, LANES - v.shape[-1])))


def kernel(x_prompt, x_sample, cache_mem_k, cache_mem_v, state_ssd, state_ssd_conv, state_s5_re, state_s5_im,
           state_hgrn, mem_prompt, norm_ffn1, ffn1_w1, ffn1_w3, ffn1_w2, norm_mix, w_in, ssd_conv_w, ssd_conv_b,
           ssd_dt_bias, ssd_a_log, ssd_d, ssd_norm, ssd_w_out, s5_a_re, s5_a_im, s5_log_dt, s5_b_re, s5_b_im,
           s5_c_re, s5_c_im, s5_d, s5_w_glu_a, s5_w_glu_b, hg_lower_bounds, hg_norm, hg_w_out, w_mix_out,
           norm_xa, norm_mem, xa_wq, xa_wk, xa_wv, xa_wo, norm_ffn2, ffn2_w1, ffn2_w3, ffn2_w2, norm_final):
    bp, lp, d = x_prompt.shape
    bs, ls, _ = x_sample.shape
    mp, ms = bp * lp, bs * ls
    x = jnp.concatenate([x_prompt.reshape(mp, d), x_sample.reshape(ms, d)], axis=0)
    mem = mem_prompt.reshape(bp * MEM_LEN, d)
    row = lambda v: v.reshape(1, -1)
    bf = lambda w: w.astype(BF16)

    lb_p = jax.nn.softmax(hg_lower_bounds, axis=0)
    lb_all = jnp.cumsum(lb_p, axis=0) - lb_p[0]

    ssd_s0 = state_ssd.reshape(DEPTH, bs, SSD_WIDTH, SSD_STATE)
    mem_k = _mem_rows(cache_mem_k)
    mem_v = _mem_rows(cache_mem_v)
    ssd_states = None
    hg_states = None

    f1_w1, f1_w3, f1_w2 = bf(ffn1_w1), bf(ffn1_w3), bf(ffn1_w2)
    f2_w1, f2_w3, f2_w2 = bf(ffn2_w1), bf(ffn2_w3), bf(ffn2_w2)
    w_p, w_dt = in_proj_weights(w_in)
    b_ssd, b_glu_a, b_glu_b, b_hg, b_mix = bf(ssd_w_out), bf(s5_w_glu_a), bf(s5_w_glu_b), bf(hg_w_out), bf(w_mix_out)
    b_wq, b_wo = bf(xa_wq), bf(xa_wo)
    b_wkv = bf(jnp.concatenate([xa_wk, xa_wv], axis=-1))

    outs = {k: [] for k in ("pk", "pv", "pss", "pcv", "psr", "psi", "phg", "scv", "ssr", "ssi")}
    for l in range(DEPTH):
        x = ffn(x, row(norm_ffn1[l]), f1_w1, f1_w3, f1_w2, row(norm_final), l, False)

        p, dt = in_proj(x, row(norm_mix[l]), w_p, w_dt, l)

        ssd_prm = {"conv_w": ssd_conv_w[l], "conv_b": row(ssd_conv_b[l]), "dt_bias": _pad_lanes(ssd_dt_bias[l]),
                   "a_log": _pad_lanes(ssd_a_log[l]), "d_skip": row(jnp.repeat(ssd_d[l], SSD_HEAD_DIM)),
                   "norm": row(ssd_norm[l])}
        ys, ss_p = ssd_prompt(p, dt, ssd_prm, bp, lp)
        ys, ssd_states = ssd_sample(p, dt, state_ssd_conv, ssd_s0, ys, ssd_states, ssd_prm, l, mp, bs, ls)

        s5_prm = _s5_params(s5_a_re[l], s5_a_im[l], s5_log_dt[l], s5_b_re[l], s5_b_im[l], s5_c_re[l],
                            s5_c_im[l], s5_d[l])
        gy, sr_p, si_p = s5_chunked(p, s5_prm, bp, lp)
        u_s = p[mp:, P_U:P_U + S5_WIDTH].reshape(bs // S5_SSEQ, S5_SSEQ, ls, S5_WIDTH).transpose(0, 2, 1, 3)
        u_s = u_s.reshape(ms, S5_WIDTH)
        gy_s, sr_s, si_s = s5_scan(u_s, state_s5_re[l].reshape(bs, S5_CH), state_s5_im[l].reshape(bs, S5_CH),
                                   s5_prm, bs // S5_SSEQ, S5_SSEQ, ls, ls)
        gy_s = gy_s.reshape(bs // S5_SSEQ, ls, S5_SSEQ, S5_WIDTH).transpose(0, 2, 1, 3).reshape(ms, S5_WIDTH)
        gy = lax.dynamic_update_slice(gy, gy_s.astype(F32), (mp, 0))

        hg_prm = {"lb": row(lb_all[l]), "norm": row(hg_norm[l])}
        o, hg_p = hg_prompt(p, hg_prm, bp, lp)
        o, hg_states = hg_sample(p, state_hgrn, o, hg_states, hg_prm, l, mp, bs, ls)

        x = branch_mix(x, ys, gy, o, p, b_ssd, b_glu_a, b_glu_b, b_hg, b_mix, l)

        q = norm_proj(x, row(norm_xa[l]), b_wq, l, BF16)
        kv = norm_proj(mem, row(norm_mem[l]), b_wkv, l, F32)
        at = xattn_prompt(q, kv, bp, lp)
        at = xattn_sample(q, mem_k, mem_v, at, l, mp, bs, ls)
        x = res_mm(x, at, b_wo, l)

        x = ffn(x, row(norm_ffn2[l]), f2_w1, f2_w3, f2_w2, row(norm_final), l, l == DEPTH - 1)

        tail = SSD_CONV - 1
        outs["pk"].append(kv[:, :d].reshape(bp, MEM_LEN, XA_HEADS, XA_HEAD_DIM))
        outs["pv"].append(kv[:, d:].reshape(bp, MEM_LEN, XA_HEADS, XA_HEAD_DIM))
        outs["pss"].append(ss_p.reshape(bp, SSD_HEADS, SSD_HEAD_DIM, SSD_STATE))
        outs["pcv"].append(jnp.stack([p[(b + 1) * lp - tail:(b + 1) * lp, P_XBC:P_XBC + SSD_CONV_DIM]
                                      for b in range(bp)]))
        outs["psr"].append(sr_p.reshape(bp, S5_GROUPS, S5_STATE))
        outs["psi"].append(si_p.reshape(bp, S5_GROUPS, S5_STATE))
        outs["phg"].append(hg_p)
        outs["scv"].append(p[mp:, P_XBC:P_XBC + SSD_CONV_DIM].reshape(bs, ls, SSD_CONV_DIM)[:, ls - tail:])
        outs["ssr"].append(sr_s.reshape(bs, S5_GROUPS, S5_STATE))
        outs["ssi"].append(si_s.reshape(bs, S5_GROUPS, S5_STATE))

    st = lambda k: jnp.stack(outs[k])
    return (x[:mp].reshape(bp, lp, d), x[mp:].reshape(bs, ls, d),
            st("pk"), st("pv"), st("pss"), st("pcv"), st("psr"), st("psi"), st("phg"),
            ssd_states.reshape(DEPTH, bs, SSD_HEADS, SSD_HEAD_DIM, SSD_STATE), st("scv"), st("ssr"), st("ssi"),
            hg_states)
```

```python
import functools
import math

import jax
import jax.numpy as jnp
import numpy as np
from jax import lax
from jax.experimental import pallas as pl
from jax.experimental.pallas import tpu as pltpu

F32 = jnp.float32
BF16 = jnp.bfloat16

D_MODEL = 2048
DEPTH = 2
NORM_EPS = 1e-5
SSD_HEAD_DIM = 64
SSD_HEADS = 32
SSD_GROUPS = 4
SSD_STATE = 128
SSD_CONV = 4
SSD_WIDTH = 2048
SSD_CONV_DIM = 3072
S5_WIDTH = 1024
S5_GROUP_SIZE = 16
S5_GROUPS = 64
S5_STATE = 64
HG_WIDTH = 1024
HG_HEADS = 8
HG_KEY_DIM = 128
HG_VAL_DIM = 128
MEM_LEN = 256
XA_HEADS = 4
XA_HEAD_DIM = 512
FFN_DIM = 5632

P_GATES, P_XBC, P_U, P_HQ, P_HF, P_HI, P_HGATE, P_Z = 0, 6144, 9216, 10240, 11264, 12288, 13312, 14336
P_WIDTH = 16384

V7X_VMEM_LIMIT = 56 * 1024 * 1024


def _cparams(sem, vmem=V7X_VMEM_LIMIT):
    return pltpu.CompilerParams(dimension_semantics=sem, vmem_limit_bytes=vmem)


def _rms_scale(x):
    return lax.rsqrt(jnp.mean(x * x, axis=-1, keepdims=True) + NORM_EPS)


def _sigmoid(x):
    return 0.5 * jnp.tanh(0.5 * x) + 0.5


def _silu(x):
    return x * _sigmoid(x)


def _dot(a, b):
    return jnp.dot(a, b, preferred_element_type=F32)


def _dot_nt(a, b):
    return lax.dot_general(a, b, (((1,), (1,)), ((), ())), preferred_element_type=F32)


def _dot_tn(a, b):
    return lax.dot_general(a, b, (((0,), (0,)), ((), ())), preferred_element_type=F32)


def _split2(x):
    hi = x.astype(BF16)
    lo = (x - hi.astype(F32)).astype(BF16)
    return hi, lo


def _split3(x):
    hi = x.astype(BF16)
    r = x - hi.astype(F32)
    mid = r.astype(BF16)
    lo = (r - mid.astype(F32)).astype(BF16)
    return hi, mid, lo


def _sel_dot(sel, x, parts=3):
    ps = _split3(x) if parts == 3 else _split2(x)
    out = _dot(sel, ps[0])
    for p in ps[1:]:
        out = out + _dot(sel, p)
    return out


def _dot_sel(x, sel, parts=2):
    ps = _split3(x) if parts == 3 else _split2(x)
    out = _dot(ps[0], sel)
    for p in ps[1:]:
        out = out + _dot(p, sel)
    return out


def _norm_proj_body(x_ref, g_ref, w_ref, o_ref, h_ref):
    @pl.when(pl.program_id(1) == 0)
    def _():
        x = x_ref[...]
        h_ref[...] = (x * _rms_scale(x) * g_ref[...]).astype(BF16)

    o_ref[...] = _dot(h_ref[...], w_ref[...]).astype(o_ref.dtype)


def norm_proj(x, g, w, layer, out_dtype, bm=1024, bn=1024):
    m, k = x.shape
    n = w.shape[2]
    return pl.pallas_call(
        _norm_proj_body,
        grid=(m // bm, n // bn),
        in_specs=[pl.BlockSpec((bm, k), lambda i, j: (i, 0)),
                  pl.BlockSpec((1, k), lambda i, j: (0, 0)),
                  pl.BlockSpec((None, k, bn), lambda i, j: (layer, 0, j))],
        out_specs=pl.BlockSpec((bm, bn), lambda i, j: (i, j)),
        out_shape=jax.ShapeDtypeStruct((m, n), out_dtype),
        scratch_shapes=[pltpu.VMEM((bm, k), BF16)],
        compiler_params=_cparams(("parallel", "arbitrary")),
        name="norm_proj",
    )(x, g, w)


IN_SEGMENTS = (("z", 2048, P_Z), ("xbc", 3072, P_XBC), ("dt", SSD_HEADS, None), ("u", 1024, P_U),
               ("hq", 1024, P_HQ), ("hf", 1024, P_HF), ("hi", 1024, P_HI), ("hgate", 1024, P_HGATE),
               ("gates", 6144, P_GATES))
IN_DIM = sum(width for _, width, _ in IN_SEGMENTS)


def _in_weights_body(w_ref, o_ref, odt_ref):
    col = 0
    for _, width, off in IN_SEGMENTS:
        piece = w_ref[col:col + width, :].astype(BF16)
        if off is None:
            odt_ref[...] = jnp.zeros_like(odt_ref)
            odt_ref[0:width, :] = piece
        else:
            o_ref[off:off + width, :] = piece
        col += width


def in_proj_weights(w_in, lanes=256):
    wt = w_in.transpose(0, 2, 1)
    dd, n, k = wt.shape
    return pl.pallas_call(
        _in_weights_body,
        grid=(dd, k // lanes),
        in_specs=[pl.BlockSpec((None, n, lanes), lambda l, i: (l, 0, i))],
        out_specs=[pl.BlockSpec((None, P_WIDTH, lanes), lambda l, i: (l, 0, i)),
                   pl.BlockSpec((None, LANES, lanes), lambda l, i: (l, 0, i))],
        out_shape=[jax.ShapeDtypeStruct((dd, P_WIDTH, k), BF16), jax.ShapeDtypeStruct((dd, LANES, k), BF16)],
        compiler_params=_cparams(("parallel", "parallel")),
        name="in_proj_weights",
    )(wt)


def _in_proj_body(x_ref, g_ref, w_ref, wdt_ref, o_ref, odt_ref, h_ref):
    @pl.when(pl.program_id(1) == 0)
    def _():
        x = x_ref[...]
        h = (x * _rms_scale(x) * g_ref[...]).astype(BF16)
        h_ref[...] = h
        odt_ref[...] = _dot_nt(h, wdt_ref[...])

    o_ref[...] = _dot_nt(h_ref[...], w_ref[...])


def in_proj(x, g, w, wdt, layer, bm=1024, bn=1024):
    m, k = x.shape
    n = w.shape[1]
    ndt = wdt.shape[1]
    return pl.pallas_call(
        _in_proj_body,
        grid=(m // bm, n // bn),
        in_specs=[pl.BlockSpec((bm, k), lambda i, j: (i, 0)),
                  pl.BlockSpec((1, k), lambda i, j: (0, 0)),
                  pl.BlockSpec((None, bn, k), lambda i, j: (layer, j, 0)),
                  pl.BlockSpec((None, ndt, k), lambda i, j: (layer, 0, 0))],
        out_specs=[pl.BlockSpec((bm, bn), lambda i, j: (i, j)),
                   pl.BlockSpec((bm, ndt), lambda i, j: (i, 0))],
        out_shape=[jax.ShapeDtypeStruct((m, n), F32), jax.ShapeDtypeStruct((m, ndt), F32)],
        scratch_shapes=[pltpu.VMEM((bm, k), BF16)],
        compiler_params=_cparams(("parallel", "arbitrary")),
        name="in_proj",
    )(x, g, w, wdt)


def _res_mm_body(x_ref, a_ref, w_ref, o_ref):
    o_ref[...] = x_ref[...] + _dot(a_ref[...], w_ref[...])


def res_mm(x, a, w, layer, bm=1024, bn=1024):
    m, n = x.shape
    k = a.shape[1]
    return pl.pallas_call(
        _res_mm_body,
        grid=(m // bm, n // bn),
        in_specs=[pl.BlockSpec((bm, bn), lambda i, j: (i, j)),
                  pl.BlockSpec((bm, k), lambda i, j: (i, 0)),
                  pl.BlockSpec((None, k, bn), lambda i, j: (layer, 0, j))],
        out_specs=pl.BlockSpec((bm, bn), lambda i, j: (i, j)),
        out_shape=jax.ShapeDtypeStruct((m, n), F32),
        compiler_params=_cparams(("parallel", "parallel")),
        name="res_mm",
    )(x, a, w)


def _ffn_body(x_ref, g_ref, w1_ref, w3_ref, w2_ref, gf_ref, o_ref, h_ref, acc_ref, *, final_norm):
    f = pl.program_id(1)

    @pl.when(f == 0)
    def _():
        x = x_ref[...]
        h_ref[...] = (x * _rms_scale(x) * g_ref[...]).astype(BF16)
        acc_ref[...] = jnp.zeros_like(acc_ref)

    h = h_ref[...]
    a = _silu(_dot(h, w1_ref[...])) * _dot(h, w3_ref[...])
    acc_ref[...] += _dot(a.astype(BF16), w2_ref[...])

    @pl.when(f == pl.num_programs(1) - 1)
    def _():
        y = x_ref[...] + 0.5 * acc_ref[...]
        if final_norm:
            y = y * _rms_scale(y) * gf_ref[...]
        o_ref[...] = y


def ffn(x, g, w1, w3, w2, gf, layer, final_norm, bm=768, bf=512):
    m, d = x.shape
    fdim = w1.shape[2]
    return pl.pallas_call(
        functools.partial(_ffn_body, final_norm=final_norm),
        grid=(m // bm, fdim // bf),
        in_specs=[pl.BlockSpec((bm, d), lambda i, f: (i, 0)),
                  pl.BlockSpec((1, d), lambda i, f: (0, 0)),
                  pl.BlockSpec((None, d, bf), lambda i, f: (layer, 0, f)),
                  pl.BlockSpec((None, d, bf), lambda i, f: (layer, 0, f)),
                  pl.BlockSpec((None, bf, d), lambda i, f: (layer, f, 0)),
                  pl.BlockSpec((1, d), lambda i, f: (0, 0))],
        out_specs=pl.BlockSpec((bm, d), lambda i, f: (i, 0)),
        out_shape=jax.ShapeDtypeStruct((m, d), F32),
        scratch_shapes=[pltpu.VMEM((bm, d), BF16), pltpu.VMEM((bm, d), F32)],
        compiler_params=_cparams(("parallel", "arbitrary")),
        name="ffn",
    )(x, g, w1, w3, w2, gf)


def _mix_body(x_ref, ys_ref, gy_ref, o_ref, ga_ref, gb_ref, gc_ref, wssd_ref, wa_ref, wb_ref, whg_ref,
              wmix_ref, out_ref, acc_ref):
    j = pl.program_id(1)

    @pl.when(j == 0)
    def _():
        acc_ref[...] = jnp.zeros_like(acc_ref)

    gy = gy_ref[...].astype(BF16)
    y_a = _dot(ys_ref[...], wssd_ref[...])
    y_b = _dot(gy, wa_ref[...]) * _sigmoid(_dot(gy, wb_ref[...]))
    y_c = _dot(o_ref[...], whg_ref[...])
    mix = (_sigmoid(ga_ref[...]) * y_a + _sigmoid(gb_ref[...]) * y_b
           + _sigmoid(gc_ref[...]) * y_c)
    acc_ref[...] += _dot(mix.astype(BF16), wmix_ref[...])

    @pl.when(j == pl.num_programs(1) - 1)
    def _():
        out_ref[...] = x_ref[...] + acc_ref[...]


def branch_mix(x, ys, gy, o, p, wssd, wa, wb, whg, wmix, layer, bm=512, bn=512):
    m, d = x.shape
    nj = d // bn
    return pl.pallas_call(
        _mix_body,
        grid=(m // bm, nj),
        in_specs=[pl.BlockSpec((bm, d), lambda i, j: (i, 0)),
                  pl.BlockSpec((bm, ys.shape[1]), lambda i, j: (i, 0)),
                  pl.BlockSpec((bm, gy.shape[1]), lambda i, j: (i, 0)),
                  pl.BlockSpec((bm, o.shape[1]), lambda i, j: (i, 0)),
                  pl.BlockSpec((bm, bn), lambda i, j: (i, j)),
                  pl.BlockSpec((bm, bn), lambda i, j: (i, nj + j)),
                  pl.BlockSpec((bm, bn), lambda i, j: (i, 2 * nj + j)),
                  pl.BlockSpec((None, wssd.shape[1], bn), lambda i, j: (layer, 0, j)),
                  pl.BlockSpec((None, wa.shape[1], bn), lambda i, j: (layer, 0, j)),
                  pl.BlockSpec((None, wb.shape[1], bn), lambda i, j: (layer, 0, j)),
                  pl.BlockSpec((None, whg.shape[1], bn), lambda i, j: (layer, 0, j)),
                  pl.BlockSpec((None, bn, d), lambda i, j: (layer, j, 0))],
        out_specs=pl.BlockSpec((bm, d), lambda i, j: (i, 0)),
        out_shape=jax.ShapeDtypeStruct((m, d), F32),
        scratch_shapes=[pltpu.VMEM((bm, d), F32)],
        compiler_params=_cparams(("parallel", "arbitrary")),
        name="branch_mix",
    )(x, ys, gy, o, p, p, p, wssd, wa, wb, whg, wmix)


def _attend(q, k_head, v_head):
    outs = []
    for h in range(XA_HEADS):
        sl = slice(h * XA_HEAD_DIM, (h + 1) * XA_HEAD_DIM)
        s = _dot_nt(q[:, sl], k_head(h).astype(BF16)) * (XA_HEAD_DIM ** -0.5)
        s = s - jnp.max(s, axis=-1, keepdims=True)
        e = jnp.exp(s)
        p = e / jnp.sum(e, axis=-1, keepdims=True)
        outs.append(_dot(p.astype(BF16), v_head(h).astype(BF16)))
    return jnp.concatenate(outs, axis=-1)


def _xattn_prompt_body(q_ref, kv_ref, o_ref):
    k_head = lambda h: kv_ref[:, h * XA_HEAD_DIM:(h + 1) * XA_HEAD_DIM]
    v_head = lambda h: kv_ref[:, D_MODEL + h * XA_HEAD_DIM:D_MODEL + (h + 1) * XA_HEAD_DIM]
    o_ref[...] = _attend(q_ref[...], k_head, v_head).astype(BF16)


def xattn_prompt(q, kv, n_seq, seq_len, bl=512):
    nl = seq_len // bl
    return pl.pallas_call(
        _xattn_prompt_body,
        grid=(n_seq, nl),
        in_specs=[pl.BlockSpec((bl, D_MODEL), lambda b, i: (b * nl + i, 0)),
                  pl.BlockSpec((MEM_LEN, 2 * D_MODEL), lambda b, i: (b, 0))],
        out_specs=pl.BlockSpec((bl, D_MODEL), lambda b, i: (b * nl + i, 0)),
        out_shape=jax.ShapeDtypeStruct(q.shape, BF16),
        compiler_params=_cparams(("parallel", "arbitrary")),
        name="xattn_prompt",
    )(q, kv)


def _xattn_sample_body(q_ref, k_ref, v_ref, buf_ref, o_ref, *, seq_len):
    del buf_ref
    q = q_ref[...]
    r = 2 * seq_len
    nc = XA_HEAD_DIM // LANES
    piece = lambda ref, b, h, c: ref[b, pl.ds(c * XA_HEADS + h, MEM_LEN, stride=nc * XA_HEADS), :].astype(BF16)
    scores = []
    for b in range(2):
        for h in range(XA_HEADS):
            s = None
            for c in range(nc):
                lo = h * XA_HEAD_DIM + c * LANES
                d = _dot_nt(q[:, lo:lo + LANES], piece(k_ref, b, h, c))
                s = d if s is None else s + d
            scores.append(s)
    s = jnp.concatenate(scores, axis=0) * (XA_HEAD_DIM ** -0.5)
    s = s - jnp.max(s, axis=-1, keepdims=True)
    e = jnp.exp(s)
    p = (e / jnp.sum(e, axis=-1, keepdims=True)).astype(BF16)
    rows = lax.broadcasted_iota(jnp.int32, (r, LANES), 0)
    for h in range(XA_HEADS):
        for c in range(nc):
            lo = h * XA_HEAD_DIM + c * LANES
            o0 = _dot(p[h * r:(h + 1) * r], piece(v_ref, 0, h, c))
            o1 = _dot(p[(XA_HEADS + h) * r:(XA_HEADS + h + 1) * r], piece(v_ref, 1, h, c))
            o_ref[:, lo:lo + LANES] = jnp.where(rows < seq_len, o0, o1).astype(BF16)


def xattn_sample(q, k, v, buf, layer, row0, n_seq, seq_len):
    r = 2 * seq_len
    blk0 = row0 // r
    mem_spec = pl.BlockSpec((None, 2, MEM_LEN * D_MODEL // LANES, LANES), lambda i: (layer, i, 0, 0))
    return pl.pallas_call(
        functools.partial(_xattn_sample_body, seq_len=seq_len),
        grid=(n_seq // 2,),
        in_specs=[pl.BlockSpec((r, D_MODEL), lambda i: (blk0 + i, 0)), mem_spec, mem_spec,
                  pl.BlockSpec(memory_space=pl.ANY)],
        out_specs=pl.BlockSpec((r, D_MODEL), lambda i: (blk0 + i, 0)),
        out_shape=jax.ShapeDtypeStruct(buf.shape, BF16),
        input_output_aliases={3: 0},
        compiler_params=_cparams(("parallel",)),
        name="xattn_sample",
    )(q, k, v, buf)


LANES = 128
HALO = 8


def _softplus(x):
    return jnp.maximum(x, 0.0) + jnp.log1p(jnp.exp(-jnp.abs(x)))


def _pad_rows(x, rows):
    if x.shape[0] == rows:
        return x
    return jnp.concatenate([x, jnp.zeros((rows - x.shape[0], x.shape[1]), x.dtype)], axis=0)


def _tile_consts(r, t):
    i = np.arange(r)[:, None]
    j = np.arange(r)[None, :]
    same = (i // t) == (j // t)
    lt = (same & (j <= i)).astype(np.float32)
    last = (same & (j % t == t - 1)).astype(np.float32)
    return jnp.asarray(lt, BF16), jnp.asarray(last, BF16)


def _head_expand(n_heads, width, rows=LANES):
    e = np.zeros((rows, n_heads * width), np.float32)
    for h in range(n_heads):
        e[h, h * width:(h + 1) * width] = 1.0
    return jnp.asarray(e, BF16)


def _ssd_tile(xc, dt_raw, dtb, a_log, lt, last, e, e128, r, t):
    xs = xc[:, :SSD_WIDTH]
    dt = _softplus(dt_raw + dtb)
    d_a = dt * (-jnp.exp(a_log))
    a = _sel_dot(lt, d_a, 3)
    a_e = _dot_sel(a, e, 3)
    dt_e = _dot_sel(dt, e, 2)
    if t == r:
        alast_e = jnp.broadcast_to(a_e[r - 1:r, :], a_e.shape)
    else:
        alast_e = _sel_dot(last, a_e, 3)
    a_col = _dot_sel(a, e128, 3)
    a_t = _pad_rows(a, LANES).T

    row = lax.broadcasted_iota(jnp.int32, (r, LANES), 0)
    col = lax.broadcasted_iota(jnp.int32, (r, LANES), 1)
    valid = (col <= row) & (col >= (row // t) * t)
    lane_lo = col < SSD_HEAD_DIM

    xdt = _pad_rows((xs * dt_e).astype(BF16), LANES)
    hpg = SSD_HEADS // SSD_GROUPS
    ys = []
    for g in range(SSD_GROUPS):
        bg = xc[:, SSD_WIDTH + g * SSD_STATE:SSD_WIDTH + (g + 1) * SSD_STATE].astype(BF16)
        cg = xc[:, SSD_WIDTH + (SSD_GROUPS + g) * SSD_STATE:
                SSD_WIDTH + (SSD_GROUPS + g + 1) * SSD_STATE].astype(BF16)
        cb = _dot_nt(cg, _pad_rows(bg, LANES))
        for hp in range(hpg // 2):
            h0 = g * hpg + 2 * hp
            res = []
            for h in (h0, h0 + 1):
                rel = a_col[:, h * LANES:(h + 1) * LANES] - a_t[h:h + 1, :]
                dec = jnp.where(valid, jnp.exp(jnp.where(valid, rel, 0.0)), 0.0)
                res.append(_dot((cb * dec).astype(BF16), xdt[:, h0 * SSD_HEAD_DIM:(h0 + 2) * SSD_HEAD_DIM]))
            ys.append(jnp.where(lane_lo, res[0], res[1]))
    y_intra = jnp.concatenate(ys, axis=-1)
    return xs, y_intra, a_e, dt_e, alast_e


def _ssd_finish(y, xs, z, dsk, ng):
    y = y + dsk * xs
    y = y * _silu(z)
    return (y * _rms_scale(y) * ng).astype(BF16)


def _conv_silu(ext_ref, cw_ref, cb_ref, base, r):
    acc = cb_ref[...] + cw_ref[SSD_CONV - 1:SSD_CONV, :] * ext_ref[pl.ds(base, r), :]
    for k in range(1, SSD_CONV):
        acc = acc + cw_ref[SSD_CONV - 1 - k:SSD_CONV - k, :] * ext_ref[pl.ds(base - k, r), :]
    return _silu(acc)


def _ssd_prompt_body(xbc_ref, z_ref, dt_ref, cw_ref, cb_ref, dtb_ref, alog_ref, dsk_ref, ng_ref,
                     lt_ref, last_ref, e_ref, e128_ref, y_ref, sout_ref, ext_ref, st_ref, *, r):
    c = pl.program_id(1)

    @pl.when(c == 0)
    def _():
        ext_ref[0:HALO, :] = jnp.zeros((HALO, SSD_CONV_DIM), F32)
        st_ref[...] = jnp.zeros_like(st_ref)

    ext_ref[HALO:HALO + r, :] = xbc_ref[...]
    xc = _conv_silu(ext_ref, cw_ref, cb_ref, HALO, r)
    ext_ref[0:HALO, :] = xbc_ref[r - HALO:r, :]

    xs, y, a_e, dt_e, alast_e = _ssd_tile(xc, dt_ref[...], dtb_ref[...], alog_ref[...], lt_ref[...],
                                          last_ref[...], e_ref[...], e128_ref[...], r, r)
    ea_e = jnp.exp(a_e)
    xw = (xs * (dt_e * jnp.exp(alast_e - a_e))).astype(BF16)
    sdec = jnp.exp(alast_e[0:1, :])
    gw = SSD_WIDTH // SSD_GROUPS
    inter = []
    for g in range(SSD_GROUPS):
        bg = xc[:, SSD_WIDTH + g * SSD_STATE:SSD_WIDTH + (g + 1) * SSD_STATE].astype(BF16)
        cg = xc[:, SSD_WIDTH + (SSD_GROUPS + g) * SSD_STATE:
                SSD_WIDTH + (SSD_GROUPS + g + 1) * SSD_STATE].astype(BF16)
        st = st_ref[:, g * gw:(g + 1) * gw]
        inter.append(_dot(cg, st.astype(BF16)))
        st_ref[:, g * gw:(g + 1) * gw] = st * sdec[:, g * gw:(g + 1) * gw] + _dot_tn(bg, xw[:, g * gw:(g + 1) * gw])
    y = y + jnp.concatenate(inter, axis=-1) * ea_e
    y_ref[...] = _ssd_finish(y, xs, z_ref[...], dsk_ref[...], ng_ref[...])

    @pl.when(c == pl.num_programs(1) - 1)
    def _():
        sout_ref[0] = st_ref[...].T


def _ssd_consts(r, t):
    lt, last = _tile_consts(r, t)
    return lt, last, _head_expand(SSD_HEADS, SSD_HEAD_DIM), _head_expand(SSD_HEADS, LANES)


def ssd_prompt(p, dt, prm, n_seq, seq_len, r=128):
    nc = seq_len // r
    consts = _ssd_consts(r, r)
    full = lambda a: pl.BlockSpec(a.shape, lambda b, c: (0,) * a.ndim)
    small = [prm["conv_w"], prm["conv_b"], prm["dt_bias"], prm["a_log"], prm["d_skip"], prm["norm"], *consts]
    return pl.pallas_call(
        functools.partial(_ssd_prompt_body, r=r),
        grid=(n_seq, nc),
        in_specs=[pl.BlockSpec((r, SSD_CONV_DIM), lambda b, c: (b * nc + c, P_XBC // SSD_CONV_DIM)),
                  pl.BlockSpec((r, SSD_WIDTH), lambda b, c: (b * nc + c, P_Z // SSD_WIDTH)),
                  pl.BlockSpec((r, LANES), lambda b, c: (b * nc + c, 0)),
                  *[full(a) for a in small]],
        out_specs=[pl.BlockSpec((r, SSD_WIDTH), lambda b, c: (b * nc + c, 0)),
                   pl.BlockSpec((1, SSD_WIDTH, SSD_STATE), lambda b, c: (b, 0, 0))],
        out_shape=[jax.ShapeDtypeStruct((p.shape[0], SSD_WIDTH), BF16),
                   jax.ShapeDtypeStruct((n_seq, SSD_WIDTH, SSD_STATE), F32)],
        scratch_shapes=[pltpu.VMEM((HALO + r, SSD_CONV_DIM), F32), pltpu.VMEM((SSD_STATE, SSD_WIDTH), F32)],
        compiler_params=_cparams(("parallel", "arbitrary")),
        name="ssd_prompt",
    )(p, p, dt, *small)


def _ssd_sample_body(xbc_ref, z_ref, dt_ref, buf_ref, s0_ref, cw_ref, cb_ref, dtb_ref, alog_ref, dsk_ref,
                     ng_ref, lt_ref, last_ref, e_ref, e128_ref, *rest, r, t):
    y_ref, sout_ref, ext_ref = rest[-3:]
    nb = r // t
    pitch = HALO + t
    for b in range(nb):
        ext_ref[b * pitch + HALO - (SSD_CONV - 1):b * pitch + HALO, :] = buf_ref[b]
        ext_ref[b * pitch + HALO:(b + 1) * pitch, :] = xbc_ref[b * t:(b + 1) * t, :]
    xc = jnp.concatenate([_conv_silu(ext_ref, cw_ref, cb_ref, b * pitch + HALO, t) for b in range(nb)], axis=0)

    xs, y, a_e, dt_e, alast_e = _ssd_tile(xc, dt_ref[...], dtb_ref[...], alog_ref[...], lt_ref[...],
                                          last_ref[...], e_ref[...], e128_ref[...], r, t)
    ea_e = jnp.exp(a_e)
    xw = xs * (dt_e * jnp.exp(alast_e - a_e))
    sdec = jnp.exp(alast_e)
    gw = SSD_WIDTH // SSD_GROUPS
    pr = 2 * t
    prow = lax.broadcasted_iota(jnp.int32, (pr, 1), 0)
    ones = jnp.ones((pr, SSD_STATE), BF16)
    inter = []
    for g in range(SSD_GROUPS):
        bg = xc[:, SSD_WIDTH + g * SSD_STATE:SSD_WIDTH + (g + 1) * SSD_STATE].astype(BF16)
        cg = xc[:, SSD_WIDTH + (SSD_GROUPS + g) * SSD_STATE:
                SSD_WIDTH + (SSD_GROUPS + g + 1) * SSD_STATE].astype(BF16)
        cols = slice(g * gw, (g + 1) * gw)
        rows_out = []
        for q in range(nb // 2):
            rs = slice(q * pr, (q + 1) * pr)
            acc = None
            for s in range(2):
                b = 2 * q + s
                mine = (prow >= s * t) & (prow < (s + 1) * t)
                s0 = s0_ref[b, cols, :]
                yi = _dot_nt(cg[rs], s0.astype(BF16))
                acc = jnp.where(mine, yi, 0.0) if acc is None else acc + jnp.where(mine, yi, 0.0)
                upd = _dot_tn(jnp.where(mine, xw[rs, cols], 0.0).astype(BF16), bg[rs])
                lastrow = prow == (s + 1) * t - 1
                dh, dl = _split2(jnp.where(lastrow, sdec[rs, cols], 0.0))
                dcol = _dot_tn(dh, ones) + _dot_tn(dl, ones)
                sout_ref[b, cols, :] = s0 * dcol + upd
            rows_out.append(acc)
        inter.append(jnp.concatenate(rows_out, axis=0))
    y = y + jnp.concatenate(inter, axis=-1) * ea_e
    y_ref[...] = _ssd_finish(y, xs, z_ref[...], dsk_ref[...], ng_ref[...])


def ssd_sample(p, dt, conv_buf, s0, ybuf, sprev, prm, layer, row0, n_seq, seq_len, r=64):
    nb = r // seq_len
    blk0 = row0 // r
    consts = _ssd_consts(r, seq_len)
    full = lambda a: pl.BlockSpec(a.shape, lambda i: (0,) * a.ndim)
    small = [prm["conv_w"], prm["conv_b"], prm["dt_bias"], prm["a_log"], prm["d_skip"], prm["norm"], *consts]
    inplace = [ybuf] if sprev is None else [ybuf, sprev]
    n_in = 5 + len(small)
    return pl.pallas_call(
        functools.partial(_ssd_sample_body, r=r, t=seq_len),
        grid=(n_seq // nb,),
        in_specs=[pl.BlockSpec((r, SSD_CONV_DIM), lambda i: (blk0 + i, P_XBC // SSD_CONV_DIM)),
                  pl.BlockSpec((r, SSD_WIDTH), lambda i: (blk0 + i, P_Z // SSD_WIDTH)),
                  pl.BlockSpec((r, LANES), lambda i: (blk0 + i, 0)),
                  pl.BlockSpec((None, nb, SSD_CONV - 1, SSD_CONV_DIM), lambda i: (layer, i, 0, 0)),
                  pl.BlockSpec((None, nb, SSD_WIDTH, SSD_STATE), lambda i: (layer, i, 0, 0)),
                  *[full(a) for a in small],
                  *[pl.BlockSpec(memory_space=pl.ANY) for _ in inplace]],
        out_specs=[pl.BlockSpec((r, SSD_WIDTH), lambda i: (blk0 + i, 0)),
                   pl.BlockSpec((None, nb, SSD_WIDTH, SSD_STATE), lambda i: (layer, i, 0, 0))],
        out_shape=[jax.ShapeDtypeStruct(ybuf.shape, BF16),
                   jax.ShapeDtypeStruct(s0.shape, F32)],
        input_output_aliases={n_in + k: k for k in range(len(inplace))},
        scratch_shapes=[pltpu.VMEM((nb * (HALO + seq_len), SSD_CONV_DIM), F32)],
        compiler_params=_cparams(("parallel",)),
        name="ssd_sample",
    )(p, p, dt, conv_buf, s0, *small, *inplace)


S5_CH = S5_GROUPS * S5_STATE
S5_BLK = 4
S5_SSEQ = 32


def _s5_in(u, bre_ref, bim_ref):
    ub = u.astype(BF16)
    kin = S5_WIDTH // S5_BLK
    re = [_dot(ub[:, q * kin:(q + 1) * kin], bre_ref[q]) for q in range(S5_BLK)]
    im = [_dot(ub[:, q * kin:(q + 1) * kin], bim_ref[q]) for q in range(S5_BLK)]
    return jnp.concatenate(re, axis=-1), jnp.concatenate(im, axis=-1)


def _s5_out(h_re, h_im, u, cre_ref, cim_ref, dsk):
    kst = S5_CH // S5_BLK
    hr = h_re.astype(BF16)
    hi = h_im.astype(BF16)
    y = [_dot(hr[:, q * kst:(q + 1) * kst], cre_ref[q]) - _dot(hi[:, q * kst:(q + 1) * kst], cim_ref[q])
         for q in range(S5_BLK)]
    y = jnp.concatenate(y, axis=-1) + dsk * u
    return jax.nn.gelu(y).astype(BF16)


def _s5_scan_body(u_ref, h0re_ref, h0im_ref, are_ref, aim_ref, bre_ref, bim_ref, cre_ref, cim_ref, dsk_ref,
                  y_ref, hre_ref, him_ref, sre_ref, sim_ref, *, s, tc):
    c = pl.program_id(1)

    @pl.when(c == 0)
    def _():
        sre_ref[0:s, :] = h0re_ref[...]
        sim_ref[0:s, :] = h0im_ref[...]

    u = u_ref[...]
    bu_re, bu_im = _s5_in(u, bre_ref, bim_ref)
    sre_ref[s:, :] = bu_re
    sim_ref[s:, :] = bu_im
    a_re = jnp.broadcast_to(are_ref[...], (s, S5_CH))
    a_im = jnp.broadcast_to(aim_ref[...], (s, S5_CH))

    def step(t, carry):
        prev = pl.ds(pl.multiple_of(t * s, s), s)
        cur = pl.ds(pl.multiple_of((t + 1) * s, s), s)
        h_re = sre_ref[prev, :]
        h_im = sim_ref[prev, :]
        sre_ref[cur, :] = a_re * h_re - a_im * h_im + sre_ref[cur, :]
        sim_ref[cur, :] = a_re * h_im + a_im * h_re + sim_ref[cur, :]
        return carry

    lax.fori_loop(0, tc, step, 0, unroll=4)
    h_re = sre_ref[tc * s:, :]
    h_im = sim_ref[tc * s:, :]
    y_ref[...] = _s5_out(sre_ref[s:, :], sim_ref[s:, :], u, cre_ref, cim_ref, dsk_ref[...])
    sre_ref[0:s, :] = h_re
    sim_ref[0:s, :] = h_im
    hre_ref[...] = h_re
    him_ref[...] = h_im


def s5_scan(u, h0_re, h0_im, prm, n_blocks, s, n_steps, tc):
    nc = n_steps // tc
    r = tc * s
    full = lambda a: pl.BlockSpec(a.shape, lambda b, c: (0,) * a.ndim)
    small = [prm["abar_re"], prm["abar_im"], prm["b_re"], prm["b_im"], prm["c_re"], prm["c_im"], prm["d_skip"]]
    state = jax.ShapeDtypeStruct((n_blocks * s, S5_CH), F32)
    sspec = pl.BlockSpec((s, S5_CH), lambda b, c: (b, 0))
    return pl.pallas_call(
        functools.partial(_s5_scan_body, s=s, tc=tc),
        grid=(n_blocks, nc),
        in_specs=[pl.BlockSpec((r, S5_WIDTH), lambda b, c: (b * nc + c, 0)), sspec, sspec,
                  *[full(a) for a in small]],
        out_specs=[pl.BlockSpec((r, S5_WIDTH), lambda b, c: (b * nc + c, 0)), sspec, sspec],
        out_shape=[jax.ShapeDtypeStruct((u.shape[0], S5_WIDTH), BF16), state, state],
        scratch_shapes=[pltpu.VMEM((s + r, S5_CH), F32), pltpu.VMEM((s + r, S5_CH), F32)],
        compiler_params=_cparams(("parallel", "arbitrary")),
        name="s5_scan",
    )(u, h0_re, h0_im, *small)


S5_T = 16
S5_GB = LANES // S5_GROUP_SIZE
S5_CL = S5_T * LANES
S5_SL = S5_GB * S5_STATE
S5_NB = 2
S5_PITCH_PAD = 8


def _block_diag_rows(x, rows_w, cols_w):
    xt = jnp.concatenate([x] * S5_GB, axis=0)
    r = lax.broadcasted_iota(jnp.int32, xt.shape, 0) // rows_w
    c = lax.broadcasted_iota(jnp.int32, xt.shape, 1) // cols_w
    return jnp.where(r == c, xt, jnp.zeros_like(xt))


def _s5_chunk_body(u_ref, kj_ref, bcre_ref, bcim_ref, ccre_ref, ccim_ref, are_ref, aim_ref, dsk_ref,
                   y_ref, hre_ref, him_ref, vre_ref, vim_ref, sre_ref, sim_ref,
                   tp_ref, bre_ref, bim_ref, cre_ref, cim_ref, *, n_chunks):
    ks = S5_GROUP_SIZE

    @pl.when(pl.program_id(1) == 0)
    def _():
        zero = jnp.zeros((LANES, LANES), BF16)
        for tau in range(S5_T):
            piece = _block_diag_rows(kj_ref[0, tau], ks, ks)
            for s in range(S5_T - tau):
                tp_ref[s * LANES:(s + 1) * LANES, (s + tau) * LANES:(s + tau + 1) * LANES] = piece
        for s in range(1, S5_T):
            for t in range(s):
                tp_ref[s * LANES:(s + 1) * LANES, t * LANES:(t + 1) * LANES] = zero
        for s in range(S5_T):
            bre_ref[s * LANES:(s + 1) * LANES, :] = _block_diag_rows(bcre_ref[0, s], ks, S5_STATE)
            bim_ref[s * LANES:(s + 1) * LANES, :] = _block_diag_rows(bcim_ref[0, s], ks, S5_STATE)
            cre_ref[:, s * LANES:(s + 1) * LANES] = _block_diag_rows(ccre_ref[0, s], S5_STATE, ks)
            cim_ref[:, s * LANES:(s + 1) * LANES] = _block_diag_rows(ccim_ref[0, s], S5_STATE, ks)

    rows = S5_NB * n_chunks
    pitch = n_chunks + S5_PITCH_PAD
    nk = S5_SL // LANES
    u = jnp.concatenate([u_ref[pl.ds(t, rows, stride=S5_T), :] for t in range(S5_T)], axis=-1)
    ub = u.astype(BF16)
    v_re = _dot(ub, bre_ref[...])
    v_im = _dot(ub, bim_ref[...])
    for k in range(nk):
        vre_ref[k] = v_re[:, k * LANES:(k + 1) * LANES]
        vim_ref[k] = v_im[:, k * LANES:(k + 1) * LANES]
        sre_ref[k] = jnp.zeros((S5_NB * pitch, LANES), F32)
        sim_ref[k] = jnp.zeros((S5_NB * pitch, LANES), F32)
    a_re = [jnp.broadcast_to(are_ref[0, :, k * LANES:(k + 1) * LANES], (S5_NB, LANES)) for k in range(nk)]
    a_im = [jnp.broadcast_to(aim_ref[0, :, k * LANES:(k + 1) * LANES], (S5_NB, LANES)) for k in range(nk)]

    def step(c, carry):
        for k in range(nk):
            h_re = sre_ref[k, pl.ds(c, S5_NB, stride=pitch), :]
            h_im = sim_ref[k, pl.ds(c, S5_NB, stride=pitch), :]
            x_re = vre_ref[k, pl.ds(c, S5_NB, stride=n_chunks), :]
            x_im = vim_ref[k, pl.ds(c, S5_NB, stride=n_chunks), :]
            sre_ref[k, pl.ds(c + 1, S5_NB, stride=pitch), :] = a_re[k] * h_re - a_im[k] * h_im + x_re
            sim_ref[k, pl.ds(c + 1, S5_NB, stride=pitch), :] = a_re[k] * h_im + a_im[k] * h_re + x_im
        return carry

    lax.fori_loop(0, n_chunks, step, 0, unroll=2)
    starts = lambda ref: jnp.concatenate(
        [jnp.concatenate([ref[k, b * pitch:b * pitch + n_chunks, :] for b in range(S5_NB)], axis=0)
         for k in range(nk)], axis=-1)
    y = _dot(ub, tp_ref[...])
    y = y + _dot(starts(sre_ref).astype(BF16), cre_ref[...]) - _dot(starts(sim_ref).astype(BF16), cim_ref[...])
    y = jax.nn.gelu(y + dsk_ref[0] * u)
    for t in range(S5_T):
        y_ref[pl.ds(t, rows, stride=S5_T), :] = y[:, t * LANES:(t + 1) * LANES]
    final = lambda ref: jnp.concatenate(
        [jnp.concatenate([ref[k, b * pitch + n_chunks:b * pitch + n_chunks + 1, :] for b in range(S5_NB)], axis=0)
         for k in range(nk)], axis=-1)
    hre_ref[...] = final(sre_ref)
    him_ref[...] = final(sim_ref)


def s5_chunked(p, prm, n_seq, seq_len):
    n_chunks = seq_len // S5_T
    rows = S5_NB * n_chunks
    nblk = S5_GROUPS // S5_GB
    nhalf = n_seq // S5_NB
    pitch = n_chunks + S5_PITCH_PAD
    nk = S5_SL // LANES
    blk = lambda a: pl.BlockSpec((1,) + a.shape[1:], lambda j, i: (j,) + (0,) * (a.ndim - 1))
    small = [prm["kern_j"], prm["bst_re"], prm["bst_im"], prm["cst_re"], prm["cst_im"], prm["apow_re"],
             prm["apow_im"], prm["d_tiled"]]
    state = jax.ShapeDtypeStruct((nhalf, S5_NB, S5_CH), F32)
    sspec = pl.BlockSpec((None, S5_NB, S5_SL), lambda j, i: (i, 0, j))
    return pl.pallas_call(
        functools.partial(_s5_chunk_body, n_chunks=n_chunks),
        grid=(nblk, nhalf),
        in_specs=[pl.BlockSpec((S5_NB * seq_len, LANES), lambda j, i: (i, P_U // LANES + j)),
                  *[blk(a) for a in small]],
        out_specs=[pl.BlockSpec((S5_NB * seq_len, LANES), lambda j, i: (i, j)), sspec, sspec],
        out_shape=[jax.ShapeDtypeStruct((p.shape[0], S5_WIDTH), F32), state, state],
        scratch_shapes=[pltpu.VMEM((nk, rows, LANES), F32), pltpu.VMEM((nk, rows, LANES), F32),
                        pltpu.VMEM((nk, S5_NB * pitch, LANES), F32), pltpu.VMEM((nk, S5_NB * pitch, LANES), F32),
                        pltpu.VMEM((S5_CL, S5_CL), BF16), pltpu.VMEM((S5_CL, S5_SL), BF16),
                        pltpu.VMEM((S5_CL, S5_SL), BF16), pltpu.VMEM((S5_SL, S5_CL), BF16),
                        pltpu.VMEM((S5_SL, S5_CL), BF16)],
        compiler_params=_cparams(("parallel", "arbitrary")),
        name="s5_chunked",
    )(p, *small)


def _hg_consts(r, t):
    i = np.arange(r)[:, None]
    j = np.arange(r)[None, :]
    sums, upper, pair = [], [], []
    s = 1
    while s < t:
        blk_i, blk_j = i // (2 * s), j // (2 * s)
        up_i = (i % (2 * s)) >= s
        mid_i = blk_i * 2 * s + s
        m_up = up_i & (j >= mid_i) & (j <= i)
        m_lo = (~up_i) & (j > i) & (j < mid_i)
        sums.append((m_up | m_lo).astype(np.float32))
        upper.append(np.broadcast_to(up_i, (r, 1)).astype(np.float32))
        pair.append(((blk_i == blk_j) & up_i & ((j % (2 * s)) < s)).astype(np.float32))
        s *= 2
    pair.append((i == j).astype(np.float32))
    return (jnp.asarray(np.stack(sums), BF16), jnp.asarray(np.stack(upper), F32),
            jnp.asarray(np.stack(pair), F32))


def _hg_gates(hf, lb):
    logf = -_softplus(-hf) + jnp.log1p(lb * jnp.exp(-hf))
    kk = (1.0 - lb) * _sigmoid(-hf)
    return logf, kk


def _hg_intra(q, kk, v, logf, b, sums_ref, upper_ref, pair_ref):
    r = q.shape[0]
    nlev = sums_ref.shape[0]
    lf3 = _split3(logf)
    qb = q.astype(BF16)
    kb = kk.astype(BF16)
    vb = _pad_rows(v.astype(BF16), LANES) if r < LANES else v.astype(BF16)
    scores = [None] * HG_HEADS
    for lev in range(nlev + 1):
        if lev < nlev:
            upper = upper_ref[lev] > 0.5
            half = 2 ** lev
            if half >= HALO:
                blocks = b.reshape(r // (2 * half), 2 * half, HG_WIDTH)
                ref = jnp.broadcast_to(blocks[:, half - 1:half, :], blocks.shape).reshape(r, HG_WIDTH)
                d = jnp.where(upper, b - ref, ref - b)
            else:
                m = sums_ref[lev]
                d = _dot(m, lf3[0]) + _dot(m, lf3[1]) + _dot(m, lf3[2])
            x = (jnp.where(upper, q, kk) * jnp.exp(d)).astype(BF16)
            xq, xk = x, x
        else:
            xq, xk = qb, kb
        mask = pair_ref[lev]
        for h in range(HG_HEADS):
            sl = slice(h * HG_KEY_DIM, (h + 1) * HG_KEY_DIM)
            sc = _dot_nt(xq[:, sl], xk[:, sl]) * mask
            scores[h] = sc if scores[h] is None else scores[h] + sc
    outs = [_dot(scores[h].astype(BF16), vb[:, h * HG_VAL_DIM:(h + 1) * HG_VAL_DIM]) for h in range(HG_HEADS)]
    return jnp.concatenate(outs, axis=-1)


def _hg_finish(o, hgate, ng):
    outs = []
    for h in range(HG_HEADS):
        oh = o[:, h * HG_VAL_DIM:(h + 1) * HG_VAL_DIM]
        outs.append(oh * _rms_scale(oh) * ng)
    return (jnp.concatenate(outs, axis=-1) * _silu(hgate)).astype(BF16)


def _hg_prompt_body(q_ref, f_ref, i_ref, gate_ref, lb_ref, ng_ref, lt_ref, last_ref, sums_ref, upper_ref,
                    pair_ref, o_ref, sout_ref, *, r):
    c = pl.program_id(1)

    @pl.when(c == 0)
    def _():
        sout_ref[...] = jnp.zeros_like(sout_ref)

    q = q_ref[...]
    v = i_ref[...]
    logf, kk = _hg_gates(f_ref[...], lb_ref[...])
    b = _sel_dot(lt_ref[...], logf, 3)
    o = _hg_intra(q, kk, v, logf, b, sums_ref, upper_ref, pair_ref)
    blast = b[r - 1:r, :]
    qe = (q * jnp.exp(b)).astype(BF16)
    kw = (kk * jnp.exp(blast - b)).astype(BF16)
    vb = v.astype(BF16)
    ones = jnp.ones((r, HG_VAL_DIM), BF16)
    rows = lax.broadcasted_iota(jnp.int32, (r, 1), 0)
    dh, dl = _split2(jnp.where(rows == r - 1, jnp.exp(b), 0.0))
    inter = []
    for h in range(HG_HEADS):
        sl = slice(h * HG_KEY_DIM, (h + 1) * HG_KEY_DIM)
        s = sout_ref[0, h]
        inter.append(_dot(qe[:, sl], s.astype(BF16)))
        dcol = _dot_tn(dh[:, sl], ones) + _dot_tn(dl[:, sl], ones)
        sout_ref[0, h] = s * dcol + _dot_tn(kw[:, sl], vb[:, sl])
    o = o + jnp.concatenate(inter, axis=-1)
    o_ref[...] = _hg_finish(o, gate_ref[...], ng_ref[...])


def hg_prompt(p, prm, n_seq, seq_len, r=128):
    nc = seq_len // r
    lt, last = _tile_consts(r, r)
    consts = [lt, last, *_hg_consts(r, r)]
    full = lambda a: pl.BlockSpec(a.shape, lambda b, c: (0,) * a.ndim)
    small = [prm["lb"], prm["norm"], *consts]
    col = lambda off: pl.BlockSpec((r, HG_WIDTH), lambda b, c: (b * nc + c, off // HG_WIDTH))
    return pl.pallas_call(
        functools.partial(_hg_prompt_body, r=r),
        grid=(n_seq, nc),
        in_specs=[col(P_HQ), col(P_HF), col(P_HI), col(P_HGATE), *[full(a) for a in small]],
        out_specs=[pl.BlockSpec((r, HG_WIDTH), lambda b, c: (b * nc + c, 0)),
                   pl.BlockSpec((1, HG_HEADS, HG_KEY_DIM, HG_VAL_DIM), lambda b, c: (b, 0, 0, 0))],
        out_shape=[jax.ShapeDtypeStruct((p.shape[0], HG_WIDTH), BF16),
                   jax.ShapeDtypeStruct((n_seq, HG_HEADS, HG_KEY_DIM, HG_VAL_DIM), F32)],
        compiler_params=_cparams(("parallel", "arbitrary")),
        name="hg_prompt",
    )(p, p, p, p, *small)


def _hg_sample_body(q_ref, f_ref, i_ref, gate_ref, s0_ref, lb_ref, ng_ref, lt_ref, last_ref, sums_ref,
                    upper_ref, pair_ref, *rest, r, t):
    o_ref, sout_ref = rest[-2:]
    q = q_ref[...]
    v = i_ref[...]
    logf, kk = _hg_gates(f_ref[...], lb_ref[...])
    b = _sel_dot(lt_ref[...], logf, 3)
    o = _hg_intra(q, kk, v, logf, b, sums_ref, upper_ref, pair_ref)
    blast = _sel_dot(last_ref[...], b, 3)
    qe = (q * jnp.exp(b)).astype(BF16)
    kw = kk * jnp.exp(blast - b)
    sdec = jnp.exp(blast)
    vb = v.astype(BF16)
    pr = 2 * t
    prow = lax.broadcasted_iota(jnp.int32, (pr, 1), 0)
    ones = jnp.ones((pr, HG_VAL_DIM), BF16)
    rows_out = []
    for p2 in range(r // pr):
        rs = slice(p2 * pr, (p2 + 1) * pr)
        heads = []
        for h in range(HG_HEADS):
            sl = slice(h * HG_KEY_DIM, (h + 1) * HG_KEY_DIM)
            acc = None
            for s in range(2):
                bi = 2 * p2 + s
                mine = (prow >= s * t) & (prow < (s + 1) * t)
                s0 = s0_ref[bi, h]
                oi = jnp.where(mine, _dot(qe[rs, sl], s0.astype(BF16)), 0.0)
                acc = oi if acc is None else acc + oi
                upd = _dot_tn(jnp.where(mine, kw[rs, sl], 0.0).astype(BF16), vb[rs, sl])
                dh, dl = _split2(jnp.where(prow == (s + 1) * t - 1, sdec[rs, sl], 0.0))
                dcol = _dot_tn(dh, ones) + _dot_tn(dl, ones)
                sout_ref[bi, h] = s0 * dcol + upd
            heads.append(acc)
        rows_out.append(jnp.concatenate(heads, axis=-1))
    o = o + jnp.concatenate(rows_out, axis=0)
    o_ref[...] = _hg_finish(o, gate_ref[...], ng_ref[...])


def hg_sample(p, s0, obuf, sprev, prm, layer, row0, n_seq, seq_len, r=128):
    nb = r // seq_len
    blk0 = row0 // r
    lt, last = _tile_consts(r, seq_len)
    consts = [lt, last, *_hg_consts(r, seq_len)]
    full = lambda a: pl.BlockSpec(a.shape, lambda i: (0,) * a.ndim)
    small = [prm["lb"], prm["norm"], *consts]
    col = lambda off: pl.BlockSpec((r, HG_WIDTH), lambda i: (blk0 + i, off // HG_WIDTH))
    sspec = pl.BlockSpec((None, nb, HG_HEADS, HG_KEY_DIM, HG_VAL_DIM), lambda i: (layer, i, 0, 0, 0))
    inplace = [obuf] if sprev is None else [obuf, sprev]
    n_in = 5 + len(small)
    return pl.pallas_call(
        functools.partial(_hg_sample_body, r=r, t=seq_len),
        grid=(n_seq // nb,),
        in_specs=[col(P_HQ), col(P_HF), col(P_HI), col(P_HGATE), sspec, *[full(a) for a in small],
                  *[pl.BlockSpec(memory_space=pl.ANY) for _ in inplace]],
        out_specs=[pl.BlockSpec((r, HG_WIDTH), lambda i: (blk0 + i, 0)), sspec],
        out_shape=[jax.ShapeDtypeStruct(obuf.shape, BF16), jax.ShapeDtypeStruct(s0.shape, F32)],
        input_output_aliases={n_in + k: k for k in range(len(inplace))},
        compiler_params=_cparams(("parallel",)),
        name="hg_sample",
    )(p, p, p, p, s0, *small, *inplace)


def _s5_params(a_re, a_im, log_dt, b_re, b_im, c_re, c_im, d_skip):
    dt = jnp.exp(log_dt)[:, None]
    mag = jnp.exp(a_re * dt)
    abar_re = mag * jnp.cos(a_im * dt)
    abar_im = mag * jnp.sin(a_im * dt)
    den = a_re * a_re + a_im * a_im
    nr = abar_re - 1.0
    coef_re = (nr * a_re + abar_im * a_im) / den
    coef_im = (abar_im * a_re - nr * a_im) / den
    bbar_re = coef_re[..., None] * b_re - coef_im[..., None] * b_im
    bbar_im = coef_re[..., None] * b_im + coef_im[..., None] * b_re
    gpb = S5_GROUPS // S5_BLK
    eye = jnp.eye(gpb, dtype=F32)

    def in_blocks(bbar):
        bb = bbar.reshape(S5_BLK, gpb, S5_STATE, S5_GROUP_SIZE)
        return jnp.einsum("qgnk,gh->qgkhn", bb, eye).reshape(
            S5_BLK, gpb * S5_GROUP_SIZE, gpb * S5_STATE).astype(BF16)

    def out_blocks(c):
        cc = c.reshape(S5_BLK, gpb, S5_GROUP_SIZE, S5_STATE)
        return jnp.einsum("qgkn,gh->qgnhk", cc, eye).reshape(
            S5_BLK, gpb * S5_STATE, gpb * S5_GROUP_SIZE).astype(BF16)

    prm = {
        "abar_re": abar_re.reshape(1, S5_CH), "abar_im": abar_im.reshape(1, S5_CH),
        "b_re": in_blocks(bbar_re), "b_im": in_blocks(bbar_im),
        "c_re": out_blocks(c_re), "c_im": out_blocks(c_im),
        "d_skip": d_skip.reshape(1, S5_WIDTH),
    }

    tau = jnp.arange(S5_T + 1, dtype=F32)[:, None, None]
    pmag = jnp.exp(tau * (a_re * dt))
    pw_re = pmag * jnp.cos(tau * (a_im * dt))
    pw_im = pmag * jnp.sin(tau * (a_im * dt))
    cp_re = c_re[None] * pw_re[:, :, None, :] - c_im[None] * pw_im[:, :, None, :]
    cp_im = c_re[None] * pw_im[:, :, None, :] + c_im[None] * pw_re[:, :, None, :]
    bt_re = bbar_re.transpose(0, 2, 1)[None, :, None]
    bt_im = bbar_im.transpose(0, 2, 1)[None, :, None]
    kern = jnp.sum(cp_re[:, :, :, None, :] * bt_re - cp_im[:, :, :, None, :] * bt_im, axis=-1)
    nq = S5_GROUPS // S5_GB
    ks = S5_GROUP_SIZE
    kern_j = (kern[:S5_T].reshape(S5_T, nq, S5_GB, ks, ks).transpose(1, 0, 4, 2, 3)
              .reshape(nq, S5_T, ks, LANES).astype(BF16))
    rev = pw_re[S5_T - 1 - jnp.arange(S5_T)], pw_im[S5_T - 1 - jnp.arange(S5_T)]
    bst_re = rev[0][..., None] * bbar_re[None] - rev[1][..., None] * bbar_im[None]
    bst_im = rev[0][..., None] * bbar_im[None] + rev[1][..., None] * bbar_re[None]

    def bst_blocks(b):
        return (b.reshape(S5_T, nq, S5_GB, S5_STATE, ks).transpose(1, 0, 4, 2, 3)
                .reshape(nq, S5_T, ks, S5_SL).astype(BF16))

    def cst_blocks(c):
        return (c.reshape(S5_T, nq, S5_GB, ks, S5_STATE).transpose(1, 0, 4, 2, 3)
                .reshape(nq, S5_T, S5_STATE, LANES).astype(BF16))

    prm.update({
        "kern_j": kern_j,
        "bst_re": bst_blocks(bst_re), "bst_im": bst_blocks(bst_im),
        "cst_re": cst_blocks(cp_re[1:]), "cst_im": cst_blocks(cp_im[1:]),
        "apow_re": pw_re[S5_T].reshape(nq, 1, S5_SL), "apow_im": pw_im[S5_T].reshape(nq, 1, S5_SL),
        "d_tiled": jnp.broadcast_to(d_skip.reshape(nq, 1, 1, LANES), (nq, 1, S5_T, LANES)).reshape(nq, 1, S5_CL),
    })
    return prm


def _mem_rows(cache):
    dd, b, m, h, hd = cache.shape
    c = cache.reshape(dd, b, m, h, hd // LANES, LANES).transpose(0, 1, 2, 4, 3, 5)
    return c.reshape(dd, b, m * h * (hd // LANES), LANES)


def _pad_lanes(v):
    return jnp.pad(v.reshape(1, -1), ((0, 0), (0, LANES - v.shape[-1])))


def kernel(x_prompt, x_sample, cache_mem_k, cache_mem_v, state_ssd, state_ssd_conv, state_s5_re, state_s5_im,
           state_hgrn, mem_prompt, norm_ffn1, ffn1_w1, ffn1_w3, ffn1_w2, norm_mix, w_in, ssd_conv_w, ssd_conv_b,
           ssd_dt_bias, ssd_a_log, ssd_d, ssd_norm, ssd_w_out, s5_a_re, s5_a_im, s5_log_dt, s5_b_re, s5_b_im,
           s5_c_re, s5_c_im, s5_d, s5_w_glu_a, s5_w_glu_b, hg_lower_bounds, hg_norm, hg_w_out, w_mix_out,
           norm_xa, norm_mem, xa_wq, xa_wk, xa_wv, xa_wo, norm_ffn2, ffn2_w1, ffn2_w3, ffn2_w2, norm_final):
    bp, lp, d = x_prompt.shape
    bs, ls, _ = x_sample.shape
    mp, ms = bp * lp, bs * ls
    x = jnp.concatenate([x_prompt.reshape(mp, d), x_sample.reshape(ms, d)], axis=0)
    mem = mem_prompt.reshape(bp * MEM_LEN, d)
    row = lambda v: v.reshape(1, -1)
    bf = lambda w: w.astype(BF16)

    lb_p = jax.nn.softmax(hg_lower_bounds, axis=0)
    lb_all = jnp.cumsum(lb_p, axis=0) - lb_p[0]

    ssd_s0 = state_ssd.reshape(DEPTH, bs, SSD_WIDTH, SSD_STATE)
    mem_k = _mem_rows(cache_mem_k)
    mem_v = _mem_rows(cache_mem_v)
    ssd_states = None
    hg_states = None

    f1_w1, f1_w3, f1_w2 = bf(ffn1_w1), bf(ffn1_w3), bf(ffn1_w2)
    f2_w1, f2_w3, f2_w2 = bf(ffn2_w1), bf(ffn2_w3), bf(ffn2_w2)
    w_p, w_dt = in_proj_weights(w_in)
    b_ssd, b_glu_a, b_glu_b, b_hg, b_mix = bf(ssd_w_out), bf(s5_w_glu_a), bf(s5_w_glu_b), bf(hg_w_out), bf(w_mix_out)
    b_wq, b_wo = bf(xa_wq), bf(xa_wo)
    b_wkv = bf(jnp.concatenate([xa_wk, xa_wv], axis=-1))

    outs = {k: [] for k in ("pk", "pv", "pss", "pcv", "psr", "psi", "phg", "scv", "ssr", "ssi")}
    for l in range(DEPTH):
        x = ffn(x, row(norm_ffn1[l]), f1_w1, f1_w3, f1_w2, row(norm_final), l, False)

        p, dt = in_proj(x, row(norm_mix[l]), w_p, w_dt, l)

        ssd_prm = {"conv_w": ssd_conv_w[l], "conv_b": row(ssd_conv_b[l]), "dt_bias": _pad_lanes(ssd_dt_bias[l]),
                   "a_log": _pad_lanes(ssd_a_log[l]), "d_skip": row(jnp.repeat(ssd_d[l], SSD_HEAD_DIM)),
                   "norm": row(ssd_norm[l])}
        ys, ss_p = ssd_prompt(p, dt, ssd_prm, bp, lp)
        ys, ssd_states = ssd_sample(p, dt, state_ssd_conv, ssd_s0, ys, ssd_states, ssd_prm, l, mp, bs, ls)

        s5_prm = _s5_params(s5_a_re[l], s5_a_im[l], s5_log_dt[l], s5_b_re[l], s5_b_im[l], s5_c_re[l],
                            s5_c_im[l], s5_d[l])
        gy, sr_p, si_p = s5_chunked(p, s5_prm, bp, lp)
        u_s = p[mp:, P_U:P_U + S5_WIDTH].reshape(bs // S5_SSEQ, S5_SSEQ, ls, S5_WIDTH).transpose(0, 2, 1, 3)
        u_s = u_s.reshape(ms, S5_WIDTH)
        gy_s, sr_s, si_s = s5_scan(u_s, state_s5_re[l].reshape(bs, S5_CH), state_s5_im[l].reshape(bs, S5_CH),
                                   s5_prm, bs // S5_SSEQ, S5_SSEQ, ls, ls)
        gy_s = gy_s.reshape(bs // S5_SSEQ, ls, S5_SSEQ, S5_WIDTH).transpose(0, 2, 1, 3).reshape(ms, S5_WIDTH)
        gy = lax.dynamic_update_slice(gy, gy_s.astype(F32), (mp, 0))

        hg_prm = {"lb": row(lb_all[l]), "norm": row(hg_norm[l])}
        o, hg_p = hg_prompt(p, hg_prm, bp, lp)
        o, hg_states = hg_sample(p, state_hgrn, o, hg_states, hg_prm, l, mp, bs, ls)

        x = branch_mix(x, ys, gy, o, p, b_ssd, b_glu_a, b_glu_b, b_hg, b_mix, l)

        q = norm_proj(x, row(norm_xa[l]), b_wq, l, BF16)
        kv = norm_proj(mem, row(norm_mem[l]), b_wkv, l, F32)
        at = xattn_prompt(q, kv, bp, lp)
        at = xattn_sample(q, mem_k, mem_v, at, l, mp, bs, ls)
        x = res_mm(x, at, b_wo, l)

        x = ffn(x, row(norm_ffn2[l]), f2_w1, f2_w3, f2_w2, row(norm_final), l, l == DEPTH - 1)

        tail = SSD_CONV - 1
        outs["pk"].append(kv[:, :d].reshape(bp, MEM_LEN, XA_HEADS, XA_HEAD_DIM))
        outs["pv"].append(kv[:, d:].reshape(bp, MEM_LEN, XA_HEADS, XA_HEAD_DIM))
        outs["pss"].append(ss_p.reshape(bp, SSD_HEADS, SSD_HEAD_DIM, SSD_STATE))
        outs["pcv"].append(jnp.stack([p[(b + 1) * lp - tail:(b + 1) * lp, P_XBC:P_XBC + SSD_CONV_DIM]
                                      for b in range(bp)]))
        outs["psr"].append(sr_p.reshape(bp, S5_GROUPS, S5_STATE))
        outs["psi"].append(si_p.reshape(bp, S5_GROUPS, S5_STATE))
        outs["phg"].append(hg_p)
        outs["scv"].append(p[mp:, P_XBC:P_XBC + SSD_CONV_DIM].reshape(bs, ls, SSD_CONV_DIM)[:, ls - tail:])
        outs["ssr"].append(sr_s.reshape(bs, S5_GROUPS, S5_STATE))
        outs["ssi"].append(si_s.reshape(bs, S5_GROUPS, S5_STATE))

    st = lambda k: jnp.stack(outs[k])
    return (x[:mp].reshape(bp, lp, d), x[mp:].reshape(bs, ls, d),
            st("pk"), st("pv"), st("pss"), st("pcv"), st("psr"), st("psi"), st("phg"),
            ssd_states.reshape(DEPTH, bs, SSD_HEADS, SSD_HEAD_DIM, SSD_STATE), st("scv"), st("ssr"), st("ssi"),
            hg_states)
```
